```python
import jax
import jax.numpy as jnp
from jax import lax
import numpy as np

D_MODEL = 2048
BATCH = 32
SEQ = 256
DEPTH = 2
DEC_BATCH = 8
DEC_SEQ = 4096
PAST_LEN = 512

GRID_W = 64
N_DIR = 2
N_BRANCH = 3
BRANCH_W = 1024
GLA_HEADS = 4
GLA_DK = 128
GLA_DV = 256
GLA_RANK = 16
GLA_TAU = 16.0
GLA_CHUNK = 64
MLSTM_HEADS = 4
MLSTM_DH = 256
MLSTM_CHUNK = 64
LRU_WIDTH = 1024
LRU_BLOCKS = 8
LRU_BLOCK = 128
LRU_CONV = 4
LRU_C = 8.0
D_FF = 5632
FFN_CONV = 3
EPS = 1e-6
IN_SPLITS = (
    GLA_HEADS * GLA_DK, GLA_HEADS * GLA_DK, GLA_HEADS * GLA_DV, GLA_HEADS * GLA_DV, N_DIR * GLA_RANK,
    MLSTM_HEADS * MLSTM_DH, MLSTM_HEADS * MLSTM_DH, MLSTM_HEADS * MLSTM_DH, MLSTM_HEADS * MLSTM_DH, N_DIR * 2 * MLSTM_HEADS,
    LRU_WIDTH, LRU_WIDTH,
    N_BRANCH * D_MODEL,
)
D_IN = sum(IN_SPLITS)

kernel_name = 'bidir_gla_mlstm_rglru_diffusion_step'


def _flip(t):
    return jnp.flip(t, axis=1)


def _chunks(t, size):
    b, n = t.shape[0], t.shape[1]
    return jnp.moveaxis(t.reshape((b, n // size, size) + t.shape[2:]), 1, 0)


def _unchunks(t):
    n, b, size = t.shape[0], t.shape[1], t.shape[2]
    return jnp.moveaxis(t, 0, 1).reshape((b, n * size) + t.shape[3:])


def rmsnorm(x, g):
    xf = x.astype(jnp.float32)
    y = xf * lax.rsqrt(jnp.mean(xf * xf, axis=-1, keepdims=True) + EPS)
    return (y * g.astype(jnp.float32)).astype(x.dtype)


def head_rmsnorm(x, g):
    b, t, h, d = x.shape
    y = x * lax.rsqrt(jnp.mean(x * x, axis=-1, keepdims=True) + EPS)
    return y.reshape(b, t, h * d) * g.astype(jnp.float32)


def dwconv1d(x, w, bias, left):
    k, t = w.shape[0], x.shape[1]
    xp = jnp.pad(x, ((0, 0), (left, k - 1 - left), (0, 0)))
    return sum(xp[:, j:j + t] * w[j] for j in range(k)) + bias


def dwconv_grid(x, w, bias):
    b, t, ch = x.shape
    rows = t // GRID_W
    p = FFN_CONV // 2
    xg = jnp.pad(x.reshape(b, rows, GRID_W, ch), ((0, 0), (p, p), (p, p), (0, 0)))
    y = sum(xg[:, dy:dy + rows, dx:dx + GRID_W] * w[dy, dx] for dy in range(FFN_CONV) for dx in range(FFN_CONV))
    return (y + bias).reshape(b, t, ch)


def gla_scan(q, k, v, log_a, s0):
    causal = jnp.tril(jnp.ones((GLA_CHUNK, GLA_CHUNK), dtype=bool))

    def step(s, inp):
        qc, kc, vc, lc = inp
        cum = jnp.cumsum(lc, axis=1)
        last = cum[:, -1]
        q_in = qc * jnp.exp(cum)
        k_in = kc * jnp.exp(-cum)
        att = jnp.where(causal, jnp.einsum('bihd,bjhd->bhij', q_in, k_in), 0.0)
        o = jnp.einsum('bhij,bjhv->bihv', att, vc) + jnp.einsum('bihd,bhdv->bihv', q_in, s)
        k_end = kc * jnp.exp(last[:, None] - cum)
        s = jnp.exp(last)[..., None] * s + jnp.einsum('bjhd,bjhv->bhdv', k_end, vc)
        return s, o

    s_end, o = lax.scan(step, s0.astype(jnp.float32), tuple(_chunks(t, GLA_CHUNK) for t in (q, k, v, log_a)))
    return _unchunks(o), s_end


def mlstm_scan(q, k, v, log_f, log_i, c0, n0, m0):
    causal = jnp.tril(jnp.ones((MLSTM_CHUNK, MLSTM_CHUNK), dtype=bool))

    def step(carry, inp):
        cmat, nvec, m = carry
        qc, kc, vc, fc, ic = inp
        fcum = jnp.moveaxis(jnp.cumsum(fc, axis=1), 1, -1)
        ig = jnp.moveaxis(ic, 1, -1)
        dlog = jnp.where(causal, fcum[..., :, None] - fcum[..., None, :] + ig[..., None, :], -jnp.inf)
        prev = fcum + m[..., None]
        mj = jnp.maximum(prev, jnp.max(dlog, axis=-1))
        w = jnp.exp(dlog - mj[..., None])
        wp = jnp.moveaxis(jnp.exp(prev - mj), -1, 1)[..., None]
        s = jnp.einsum('bjhd,bihd->bhji', qc, kc) * w
        qp = qc * wp
        num = jnp.einsum('bhji,bihv->bjhv', s, vc) + jnp.einsum('bjhd,bhdv->bjhv', qp, cmat)
        den = jnp.moveaxis(jnp.sum(s, axis=-1), -1, 1) + jnp.einsum('bjhd,bhd->bjh', qp, nvec)
        floor = jnp.moveaxis(jnp.exp(-mj), -1, 1)
        h = num / jnp.maximum(jnp.abs(den), floor)[..., None]
        m_new = mj[..., -1]
        wl = jnp.exp(fcum[..., -1:] - fcum + ig - m_new[..., None])
        decay = jnp.exp(fcum[..., -1] + m - m_new)
        kw = kc * jnp.moveaxis(wl, -1, 1)[..., None]
        cmat = decay[..., None, None] * cmat + jnp.einsum('bihd,bihv->bhdv', kw, vc)
        nvec = decay[..., None] * nvec + jnp.sum(kw, axis=1)
        return (cmat, nvec, m_new), h

    f32 = jnp.float32
    init = (c0.astype(f32), n0.astype(f32), m0.astype(f32))
    (c_end, n_end, m_end), h = lax.scan(step, init, tuple(_chunks(t, MLSTM_CHUNK) for t in (q, k, v, log_f, log_i)))
    return _unchunks(h), c_end, n_end, m_end


def rglru_dir(x, gate_w, gate_b, lam, h0):
    b, t, w = x.shape
    f32 = jnp.float32
    xb = x.reshape(b, t, LRU_BLOCKS, LRU_BLOCK)
    pre = jnp.einsum('btnk,gnkj->gbtnj', xb, gate_w.astype(f32)).reshape(2, b, t, w)
    gates = jax.nn.sigmoid(pre + gate_b.astype(f32)[:, None, None, :])
    log_a = -LRU_C * gates[0] * jax.nn.softplus(-lam.astype(f32))
    a = jnp.exp(log_a)
    inp = jnp.sqrt(-jnp.expm1(2.0 * log_a)) * (gates[1] * x)
    inp = inp.at[:, 0].add(a[:, 0] * h0.astype(f32))

    def combine(left, right):
        return left[0] * right[0], right[0] * left[1] + right[1]

    _, h = lax.associative_scan(combine, (a, inp), axis=1)
    return h, h[:, -1]


def token_mix(u, init, w_in, gla_w_alpha, gla_b_alpha, gla_norm_g, mlstm_b_if, mlstm_norm_g,
              lru_conv_w, lru_conv_b, lru_gate_w, lru_gate_b, lru_lambda, w_branch, b_merge, w_out):
    b, t, _ = u.shape
    f32 = jnp.float32
    s_gla0, c_ml0, n_ml0, m_ml0, h_lru0 = init
    z = u @ w_in
    (qa, ka, va, ga, ra, qb, kb, vb, ob, gb, xr, yr, mg) = jnp.split(z, np.cumsum(IN_SPLITS)[:-1].tolist(), axis=-1)

    qa = qa.reshape(b, t, GLA_HEADS, GLA_DK).astype(f32) * GLA_DK ** -0.5
    ka = ka.reshape(b, t, GLA_HEADS, GLA_DK).astype(f32)
    va = va.reshape(b, t, GLA_HEADS, GLA_DV).astype(f32)
    la = jax.nn.log_sigmoid(jnp.einsum('btnr,nrk->btnk', ra.reshape(b, t, N_DIR, GLA_RANK).astype(f32),
                                       gla_w_alpha.astype(f32)) + gla_b_alpha.astype(f32)) / GLA_TAU
    la = la.reshape(b, t, N_DIR, GLA_HEADS, GLA_DK)
    oa_f, sa_f = gla_scan(qa, ka, va, la[:, :, 0], s_gla0[:, 0])
    oa_r, sa_r = gla_scan(_flip(qa), _flip(ka), _flip(va), _flip(la[:, :, 1]), s_gla0[:, 1])
    y_a = head_rmsnorm(oa_f + _flip(oa_r), gla_norm_g) * jax.nn.silu(ga.astype(f32))

    qb = qb.reshape(b, t, MLSTM_HEADS, MLSTM_DH).astype(f32)
    kb = kb.reshape(b, t, MLSTM_HEADS, MLSTM_DH).astype(f32) * MLSTM_DH ** -0.5
    vb = vb.reshape(b, t, MLSTM_HEADS, MLSTM_DH).astype(f32)
    gif = gb.reshape(b, t, N_DIR, 2, MLSTM_HEADS).astype(f32) + mlstm_b_if.astype(f32)
    log_i = gif[:, :, :, 0]
    log_f = jax.nn.log_sigmoid(gif[:, :, :, 1])
    hb_f, cb_f, nb_f, mb_f = mlstm_scan(qb, kb, vb, log_f[:, :, 0], log_i[:, :, 0], c_ml0[:, 0], n_ml0[:, 0], m_ml0[:, 0])
    hb_r, cb_r, nb_r, mb_r = mlstm_scan(_flip(qb), _flip(kb), _flip(vb), _flip(log_f[:, :, 1]), _flip(log_i[:, :, 1]),
                                        c_ml0[:, 1], n_ml0[:, 1], m_ml0[:, 1])
    y_b = jax.nn.sigmoid(ob.astype(f32)) * head_rmsnorm(hb_f + _flip(hb_r), mlstm_norm_g)

    xr = dwconv1d(xr.astype(f32), lru_conv_w.astype(f32), lru_conv_b.astype(f32), LRU_CONV // 2)
    hc_f, hc_f_end = rglru_dir(xr, lru_gate_w[0], lru_gate_b[0], lru_lambda[0], h_lru0[:, 0])
    hc_r, hc_r_end = rglru_dir(_flip(xr), lru_gate_w[1], lru_gate_b[1], lru_lambda[1], h_lru0[:, 1])
    y_c = (hc_f + _flip(hc_r)) * jax.nn.gelu(yr.astype(f32))

    g = jax.nn.sigmoid(mg.reshape(b, t, N_BRANCH, D_MODEL).astype(f32) + b_merge.reshape(N_BRANCH, D_MODEL).astype(f32))
    merged = sum(g[:, :, n] * (y @ w_branch[n]) for n, y in enumerate((y_a, y_b, y_c)))
    out = merged.astype(u.dtype) @ w_out
    new_state = (jnp.stack([sa_f, sa_r], axis=1), jnp.stack([cb_f, cb_r], axis=1), jnp.stack([nb_f, nb_r], axis=1),
                 jnp.stack([mb_f, mb_r], axis=1), jnp.stack([hc_f_end, hc_r_end], axis=1))
    return out, new_state


def conv_ffn(v, w_up, conv_w, conv_b, w_down, on_grid):
    hg, hu = jnp.split(v @ w_up, 2, axis=-1)
    if on_grid:
        hg = dwconv_grid(hg, conv_w, conv_b)
    else:
        hg = dwconv1d(hg, conv_w[FFN_CONV // 2], conv_b, FFN_CONV // 2)
    return (jax.nn.silu(hg) * hu) @ w_down


def block(x, mod, on_grid, init, lp):
    (n1, n2, w_in, gwa, gba, gng, mbif, mng, lcw, lcb, lgw, lgb, llam, wbr, bmg, wout, fup, fcw, fcb, fdown) = lp
    sh1, sc1, g1, sh2, sc2, g2 = jnp.split(mod, 6, axis=-1)
    u = rmsnorm(x, n1) * (1.0 + sc1) + sh1
    mix, new_state = token_mix(u, init, w_in, gwa, gba, gng, mbif, mng, lcw, lcb, lgw, lgb, llam, wbr, bmg, wout)
    x = x + g1 * mix
    v = rmsnorm(x, n2) * (1.0 + sc2) + sh2
    x = x + g2 * conv_ffn(v, fup, fcw, fcb, fdown, on_grid)
    return x, new_state


def zero_context_state(b):
    f32 = jnp.float32
    return (jnp.zeros((b, N_DIR, GLA_HEADS, GLA_DK, GLA_DV), f32),
            jnp.zeros((b, N_DIR, MLSTM_HEADS, MLSTM_DH, MLSTM_DH), f32),
            jnp.zeros((b, N_DIR, MLSTM_HEADS, MLSTM_DH), f32),
            jnp.zeros((b, N_DIR, MLSTM_HEADS), f32),
            jnp.zeros((b, N_DIR, LRU_WIDTH), f32))


def setup_inputs(seed: int = 0) -> dict:
    key = jax.random.key(seed)
    keys = iter(jax.random.split(key, 40))
    f32 = jnp.float32
    d = D_MODEL

    def nrm(shape, scale):
        return jax.random.normal(next(keys), shape, f32) * scale

    a_c = jax.random.uniform(next(keys), (DEPTH, N_DIR, LRU_WIDTH), f32, 0.9, 0.999)
    a = a_c ** (1.0 / LRU_C)
    lru_lambda = jnp.log(a) - jnp.log1p(-a)
    return {
        'x_prompt': nrm((BATCH, SEQ, d), 1.0),
        'x_sample': nrm((DEC_BATCH, DEC_SEQ, d), 1.0),
        'state_gla': nrm((DEC_BATCH, DEPTH, N_DIR, GLA_HEADS, GLA_DK, GLA_DV), 1.0),
        'state_mlstm_c': nrm((DEC_BATCH, DEPTH, N_DIR, MLSTM_HEADS, MLSTM_DH, MLSTM_DH), 0.1),
        'state_mlstm_n': nrm((DEC_BATCH, DEPTH, N_DIR, MLSTM_HEADS, MLSTM_DH), 0.2),
        'state_mlstm_m': nrm((DEC_BATCH, DEPTH, N_DIR, MLSTM_HEADS), 1.0),
        'state_rglru': nrm((DEC_BATCH, DEPTH, N_DIR, LRU_WIDTH), 0.5),
        'c': nrm((DEC_BATCH, d), 1.0),
        'c_ctx': nrm((d,), 1.0),
        'norm1_g': 1.0 + nrm((DEPTH, d), 0.02),
        'norm2_g': 1.0 + nrm((DEPTH, d), 0.02),
        'w_mod': nrm((DEPTH, d, 6 * d), 0.5 * d ** -0.5),
        'b_mod': nrm((DEPTH, 6 * d), 0.01),
        'w_in': nrm((DEPTH, d, D_IN), d ** -0.5),
        'gla_w_alpha': nrm((DEPTH, N_DIR, GLA_RANK, GLA_HEADS * GLA_DK), GLA_RANK ** -0.5),
        'gla_b_alpha': nrm((DEPTH, N_DIR, GLA_HEADS * GLA_DK), 0.1),
        'gla_norm_g': 1.0 + nrm((DEPTH, GLA_HEADS * GLA_DV), 0.02),
        'mlstm_b_if': nrm((DEPTH, N_DIR, 2, MLSTM_HEADS), 0.1) + jnp.array([0.0, 3.0], f32)[:, None],
        'mlstm_norm_g': 1.0 + nrm((DEPTH, MLSTM_HEADS * MLSTM_DH), 0.02),
        'lru_conv_w': nrm((DEPTH, LRU_CONV, LRU_WIDTH), LRU_CONV ** -0.5),
        'lru_conv_b': nrm((DEPTH, LRU_WIDTH), 0.01),
        'lru_gate_w': nrm((DEPTH, N_DIR, 2, LRU_BLOCKS, LRU_BLOCK, LRU_BLOCK), LRU_BLOCK ** -0.5),
        'lru_gate_b': nrm((DEPTH, N_DIR, 2, LRU_WIDTH), 0.01),
        'lru_lambda': lru_lambda,
        'w_branch': nrm((DEPTH, N_BRANCH, BRANCH_W, d), BRANCH_W ** -0.5),
        'b_merge': nrm((DEPTH, N_BRANCH * d), 0.01),
        'w_out': nrm((DEPTH, d, d), d ** -0.5),
        'ffn_w_up': nrm((DEPTH, d, 2 * D_FF), d ** -0.5),
        'ffn_conv_w': nrm((DEPTH, FFN_CONV, FFN_CONV, D_FF), 1.0 / FFN_CONV),
        'ffn_conv_b': nrm((DEPTH, D_FF), 0.01),
        'ffn_w_down': nrm((DEPTH, D_FF, d), D_FF ** -0.5),
        'norm_f_g': 1.0 + nrm((d,), 0.02),
    }


def reference(x_prompt, x_sample, state_gla, state_mlstm_c, state_mlstm_n, state_mlstm_m, state_rglru, c, c_ctx,
              norm1_g, norm2_g, w_mod, b_mod, w_in, gla_w_alpha, gla_b_alpha, gla_norm_g, mlstm_b_if, mlstm_norm_g,
              lru_conv_w, lru_conv_b, lru_gate_w, lru_gate_b, lru_lambda, w_branch, b_merge, w_out,
              ffn_w_up, ffn_conv_w, ffn_conv_b, ffn_w_down, norm_f_g):
    xp, xs = x_prompt, x_sample
    ctx_init = zero_context_state(x_prompt.shape[0])
    new_gla, new_mc, new_mn, new_mm, new_lru = [], [], [], [], []
    for l in range(DEPTH):
        lp = (norm1_g[l], norm2_g[l], w_in[l], gla_w_alpha[l], gla_b_alpha[l], gla_norm_g[l], mlstm_b_if[l],
              mlstm_norm_g[l], lru_conv_w[l], lru_conv_b[l], lru_gate_w[l], lru_gate_b[l], lru_lambda[l],
              w_branch[l], b_merge[l], w_out[l], ffn_w_up[l], ffn_conv_w[l], ffn_conv_b[l], ffn_w_down[l])
        mod_ctx = (jax.nn.silu(c_ctx) @ w_mod[l] + b_mod[l])[None, None, :]
        mod_lat = (jax.nn.silu(c) @ w_mod[l] + b_mod[l])[:, None, :]
        xp, st = block(xp, mod_ctx, False, ctx_init, lp)
        new_gla.append(st[0])
        new_mc.append(st[1])
        new_mn.append(st[2])
        new_mm.append(st[3])
        new_lru.append(st[4])
        lat_init = (state_gla[:, l], state_mlstm_c[:, l], state_mlstm_n[:, l], state_mlstm_m[:, l], state_rglru[:, l])
        xs, _ = block(xs, mod_lat, True, lat_init, lp)
    y_prompt = rmsnorm(xp, norm_f_g)
    y_sample = rmsnorm(xs, norm_f_g)
    new_state_gla = jnp.stack(new_gla, axis=1)
    new_state_mlstm_c = jnp.stack(new_mc, axis=1)
    new_state_mlstm_n = jnp.stack(new_mn, axis=1)
    new_state_mlstm_m = jnp.stack(new_mm, axis=1)
    new_state_rglru = jnp.stack(new_lru, axis=1)
    return (y_prompt, y_sample, new_state_gla, new_state_mlstm_c, new_state_mlstm_n, new_state_mlstm_m, new_state_rglru)
```

```python
import functools
from typing import NamedTuple

import numpy as np
import jax
import jax.numpy as jnp
from jax import lax
from jax.experimental import pallas as pl
from jax.experimental.pallas import tpu as pltpu

F32 = jnp.float32
BF16 = jnp.bfloat16

D_MODEL = 2048
DEPTH = 2
GRID_W = 64
N_DIR = 2
N_BRANCH = 3
BRANCH_W = 1024
GLA_HEADS = 4
GLA_DK = 128
GLA_DV = 256
GLA_RANK = 16
GLA_TAU = 16.0
MLSTM_HEADS = 4
MLSTM_DH = 256
CHUNK = 64
LRU_WIDTH = 1024
LRU_BLOCKS = 8
LRU_BLOCK = 128
LRU_CONV = 4
LRU_C = 8.0
D_FF = 5632
FFN_CONV = 3
EPS = 1e-6

_SRC = dict(qa=0, ka=512, va=1024, ga=2048, ra=3072, qb=3104, kb=4128, vb=5152, ob=6176, gb=7200,
            xr=7216, yr=8240, mg=9264, end=15408)
Z_MG, Z_VA, Z_GA, Z_QB, Z_KB, Z_VB, Z_OB, Z_XR, Z_YR, Z_QA, Z_KA, Z_SM = (
    0, 6144, 7168, 8192, 9216, 10240, 11264, 12288, 13312, 14336, 14848, 15360)
SM_W = 128
SM_RA, SM_GB = 0, 32
NZ = 15872
MOD_ROWS = 16

TB = 256
NCH = TB // CHUNK
VMEM_LIMIT = 48 * 1024 * 1024


class Cfg(NamedTuple):
    bc: int
    tc: int
    bl: int
    tl: int

    @property
    def m_ctx(self):
        return self.bc * self.tc

    @property
    def m(self):
        return self.bc * self.tc + self.bl * self.tl


def _params(sem):
    return pltpu.CompilerParams(dimension_semantics=sem, vmem_limit_bytes=VMEM_LIMIT)


def _softplus(x):
    return jnp.maximum(x, 0.0) + jnp.log1p(jnp.exp(-jnp.abs(x)))


def _log_sigmoid(x):
    return -_softplus(-x)


def _silu(x):
    return x * jax.nn.sigmoid(x)


def _gelu_tanh(x):
    return 0.5 * x * (1.0 + jnp.tanh(np.sqrt(2.0 / np.pi).astype(np.float32) * (x + 0.044715 * (x * x * x))))


def _mod_row(cfg, row):
    return jnp.where(row < cfg.m_ctx, 0, 1 + jnp.maximum(row - cfg.m_ctx, 0) // cfg.tl)


def _mod_kernel(c_ref, w_ref, b_ref, o_ref):
    c = c_ref[...]
    a = _silu(c).astype(BF16)
    o_ref[0] = jnp.dot(a, w_ref[0].astype(BF16), preferred_element_type=F32) + b_ref[0]


def _modulation(c_all, w_mod, b_mod):
    depth, d, n = w_mod.shape
    tn = 512
    return pl.pallas_call(
        _mod_kernel,
        grid=(depth, n // tn),
        in_specs=[pl.BlockSpec((MOD_ROWS, d), lambda l, j: (0, 0)),
                  pl.BlockSpec((1, d, tn), lambda l, j: (l, 0, j)),
                  pl.BlockSpec((1, 1, tn), lambda l, j: (l, 0, j))],
        out_specs=pl.BlockSpec((1, MOD_ROWS, tn), lambda l, j: (l, 0, j)),
        out_shape=jax.ShapeDtypeStruct((depth, MOD_ROWS, n), F32),
        compiler_params=_params(("parallel", "parallel")),
        name="modulation",
    )(c_all, w_mod, b_mod.reshape(depth, 1, n))


def _norm_matmul_kernel(x_ref, g_ref, sc_ref, sh_ref, w_ref, o_ref, u_ref):
    @pl.when(pl.program_id(1) == 0)
    def _():
        x = x_ref[...]
        y = x * lax.rsqrt(jnp.mean(x * x, axis=-1, keepdims=True) + EPS) * g_ref[...]
        u_ref[...] = (y * (1.0 + sc_ref[0]) + sh_ref[0]).astype(BF16)

    o_ref[...] = jnp.dot(u_ref[...], w_ref[...], preferred_element_type=F32)


def _norm_matmul(cfg, x, gain, sc, sh, w, tm, tn):
    m, d = x.shape
    n = w.shape[1]
    mod_map = lambda i, j: (_mod_row(cfg, i * tm), 0, 0)
    return pl.pallas_call(
        _norm_matmul_kernel,
        grid=(m // tm, n // tn),
        in_specs=[pl.BlockSpec((tm, d), lambda i, j: (i, 0)),
                  pl.BlockSpec((1, d), lambda i, j: (0, 0)),
                  pl.BlockSpec((1, 1, d), mod_map),
                  pl.BlockSpec((1, 1, d), mod_map),
                  pl.BlockSpec((d, tn), lambda i, j: (0, j))],
        out_specs=pl.BlockSpec((tm, tn), lambda i, j: (i, j)),
        out_shape=jax.ShapeDtypeStruct((m, n), F32),
        scratch_shapes=[pltpu.VMEM((tm, d), BF16)],
        compiler_params=_params(("parallel", "arbitrary")),
        name="norm_matmul",
    )(x, gain, sc, sh, w)


def _seq_pos(cfg, i):
    kc, kl = cfg.tc // TB, cfg.tl // TB
    nc = cfg.bc * kc
    is_ctx = i < nc
    il = jnp.maximum(i - nc, 0)
    j = jnp.where(is_ctx, i % kc, il % kl)
    k = jnp.where(is_ctx, kc, kl)
    s = jnp.where(is_ctx, i // kc, il // kl)
    return is_ctx, s, j, k


def _row_block(cfg, reverse, i):
    kc, kl = cfg.tc // TB, cfg.tl // TB
    is_ctx, s, j, k = _seq_pos(cfg, i)
    jj = (k - 1 - j) if reverse else j
    return jnp.where(is_ctx, s * kc + jj, cfg.bc * kc + s * kl + jj)


def _lat_seq(cfg, i):
    is_ctx, s, _, _ = _seq_pos(cfg, i)
    return jnp.where(is_ctx, 0, jnp.minimum(s, cfg.bl - 1))


def _ctx_seq(cfg, i):
    is_ctx, s, _, _ = _seq_pos(cfg, i)
    return jnp.where(is_ctx, s, cfg.bc - 1)


def _tri(reverse):
    row = lax.broadcasted_iota(jnp.int32, (CHUNK, CHUNK), 0)
    col = lax.broadcasted_iota(jnp.int32, (CHUNK, CHUNK), 1)
    return (row <= col) if reverse else (row >= col)


def _head_norm(x, g):
    return x * lax.rsqrt(jnp.mean(x * x, axis=-1, keepdims=True) + EPS) * g


_NT = (((1,), (1,)), ((), ()))
_TN = (((0,), (0,)), ((), ()))


def _gla_kernel(*refs, cfg, reverse):
    if reverse:
        (q_ref, k_ref, v_ref, sm_ref, wal_ref, bal_ref, s0_ref, ga_ref, of_ref, gn_ref,
         out_ref, so_ref, st_ref, la_ref) = refs
    else:
        (q_ref, k_ref, v_ref, sm_ref, wal_ref, bal_ref, s0_ref, out_ref, so_ref, st_ref, la_ref) = refs
    is_ctx, _, j, k = _seq_pos(cfg, pl.program_id(0))

    @pl.when(jnp.logical_and(j == 0, is_ctx))
    def _():
        st_ref[...] = jnp.zeros_like(st_ref)

    @pl.when(jnp.logical_and(j == 0, jnp.logical_not(is_ctx)))
    def _():
        st_ref[...] = s0_ref[0]

    d = 1 if reverse else 0
    ra =sm_ref[:, SM_RA + d * GLA_RANK:SM_RA + (d + 1) * GLA_RANK].astype(BF16)
    pre = jnp.dot(ra, wal_ref[...].astype(BF16), preferred_element_type=F32) + bal_ref[...]
    la_ref[...] = _log_sigmoid(pre) * (1.0 / GLA_TAU)

    tri = _tri(reverse)
    tri_f = tri.astype(F32)
    last = 0 if reverse else CHUNK - 1
    for c in (range(NCH - 1, -1, -1) if reverse else range(NCH)):
        rows = slice(c * CHUNK, (c + 1) * CHUNK)
        cum = jnp.dot(tri_f, la_ref[rows, :], precision=lax.Precision.HIGHEST, preferred_element_type=F32)
        tot = cum[last:last + 1, :]
        e_q = jnp.exp(cum)
        e_k = jnp.exp(-cum)
        e_end = jnp.exp(tot - cum)
        e_tot = jnp.exp(tot)
        for h in range(GLA_HEADS):
            kc = slice(h * GLA_DK, (h + 1) * GLA_DK)
            vc = slice(h * GLA_DV, (h + 1) * GLA_DV)
            qh = q_ref[rows, kc] * (GLA_DK ** -0.5)
            kh = k_ref[rows, kc]
            vh = v_ref[rows, vc].astype(BF16)
            q_in = (qh * e_q[:, kc]).astype(BF16)
            k_in = (kh * e_k[:, kc]).astype(BF16)
            att = lax.dot_general(q_in, k_in, _NT, preferred_element_type=F32)
            att = jnp.where(tri, att, 0.0).astype(BF16)
            st = st_ref[h]
            o = (jnp.dot(att, vh, preferred_element_type=F32)
                 + lax.dot_general(q_in, st.astype(BF16), _NT, preferred_element_type=F32))
            k_end = (kh * e_end[:, kc]).astype(BF16)
            st_ref[h] = e_tot[:, kc] * st + lax.dot_general(vh, k_end, _TN, preferred_element_type=F32)
            if reverse:
                y = _head_norm(of_ref[rows, vc] + o, gn_ref[:, vc]) * _silu(ga_ref[rows, vc])
                out_ref[rows, vc] = y.astype(out_ref.dtype)
            else:
                out_ref[rows, vc] = o

    @pl.when(jnp.logical_and(is_ctx, j == k - 1))
    def _():
        so_ref[0] = st_ref[...]


def _gla(cfg, z, wal, bal, s0t, reverse, o_fwd=None, gnorm=None):
    m = cfg.m
    nblk = m // TB
    rb = functools.partial(_row_block, cfg, reverse)
    hk, hv = GLA_HEADS * GLA_DK, GLA_HEADS * GLA_DV
    in_specs = [pl.BlockSpec((TB, hk), lambda i: (rb(i), Z_QA // hk)),
                pl.BlockSpec((TB, hk), lambda i: (rb(i), Z_KA // hk)),
                pl.BlockSpec((TB, hv), lambda i: (rb(i), Z_VA // hv)),
                pl.BlockSpec((TB, SM_W), lambda i: (rb(i), Z_SM // SM_W)),
                pl.BlockSpec((GLA_RANK, hk), lambda i: (0, 0)),
                pl.BlockSpec((1, hk), lambda i: (0, 0)),
                pl.BlockSpec((1, GLA_HEADS, GLA_DV, GLA_DK), lambda i: (_lat_seq(cfg, i), 0, 0, 0))]
    args = [z, z, z, z, wal, bal, s0t]
    if reverse:
        in_specs += [pl.BlockSpec((TB, hv), lambda i: (rb(i), Z_GA // hv)),
                     pl.BlockSpec((TB, hv), lambda i: (rb(i), 0)),
                     pl.BlockSpec((1, hv), lambda i: (0, 0))]
        args += [z, o_fwd, gnorm]
    return pl.pallas_call(
        functools.partial(_gla_kernel, cfg=cfg, reverse=reverse),
        grid=(nblk,),
        in_specs=in_specs,
        out_specs=[pl.BlockSpec((TB, hv), lambda i: (rb(i), 0)),
                   pl.BlockSpec((1, GLA_HEADS, GLA_DV, GLA_DK), lambda i: (_ctx_seq(cfg, i), 0, 0, 0))],
        out_shape=[jax.ShapeDtypeStruct((m, hv), BF16 if reverse else F32),
                   jax.ShapeDtypeStruct((cfg.bc, GLA_HEADS, GLA_DV, GLA_DK), F32)],
        scratch_shapes=[pltpu.VMEM((GLA_HEADS, GLA_DV, GLA_DK), F32), pltpu.VMEM((TB, hk), F32)],
        compiler_params=_params(("arbitrary",)),
        name="gla_rev" if reverse else "gla_fwd",
    )(*args)


def _mlstm_kernel(*refs, cfg, reverse):
    if reverse:
        (q_ref, k_ref, v_ref, sm_ref, gr_ref, bc_ref, br_ref, c0_ref, n0_ref, m0_ref, ob_ref, hf_ref, gn_ref,
         out_ref, co_ref, no_ref, mo_ref, c_ref, n_ref, m_ref) = refs
    else:
        (q_ref, k_ref, v_ref, sm_ref, gr_ref, bc_ref, br_ref, c0_ref, n0_ref, m0_ref,
         out_ref, co_ref, no_ref, mo_ref, c_ref, n_ref, m_ref) = refs
    is_ctx, _, j, k = _seq_pos(cfg, pl.program_id(0))

    @pl.when(jnp.logical_and(j == 0, is_ctx))
    def _():
        c_ref[...] = jnp.zeros_like(c_ref)
        n_ref[...] = jnp.zeros_like(n_ref)
        m_ref[...] = jnp.zeros_like(m_ref)

    @pl.when(jnp.logical_and(j == 0, jnp.logical_not(is_ctx)))
    def _():
        c_ref[...] = c0_ref[0]
        n_ref[...] = n0_ref[0]
        m_ref[...] = m0_ref[0]

    d = 1 if reverse else 0
    tri = _tri(reverse)
    tri_f = tri.astype(F32)
    tri_tf = _tri(not reverse).astype(F32)
    last = 0 if reverse else CHUNK - 1
    ng = 2 * MLSTM_HEADS
    for c in (range(NCH - 1, -1, -1) if reverse else range(NCH)):
        rows = slice(c * CHUNK, (c + 1) * CHUNK)
        g_col = sm_ref[rows, SM_GB:SM_GB + 2 * ng] + bc_ref[...]
        g_row = gr_ref[0, c] + br_ref[...]
        fcum_col = jnp.dot(tri_f, _log_sigmoid(g_col), precision=lax.Precision.HIGHEST, preferred_element_type=F32)
        fcum_row = jnp.dot(_log_sigmoid(g_row), tri_tf, precision=lax.Precision.HIGHEST, preferred_element_type=F32)
        for h in range(MLSTM_HEADS):
            ii, fi = d * ng + h, d * ng + MLSTM_HEADS + h
            hc = slice(h * MLSTM_DH, (h + 1) * MLSTM_DH)
            f_col = fcum_col[:, fi:fi + 1]
            f_row = fcum_row[fi:fi + 1, :]
            i_col = g_col[:, ii:ii + 1]
            i_row = g_row[ii:ii + 1, :]
            m_prev = m_ref[h][:, 0:1]
            dlog = jnp.where(tri, f_col - f_row + i_row, -jnp.inf)
            prev = f_col + m_prev
            mj = jnp.maximum(prev, jnp.max(dlog, axis=-1, keepdims=True))
            w = jnp.exp(dlog - mj)
            wp = jnp.exp(prev - mj)
            qh = q_ref[rows, hc]
            kh = k_ref[rows, hc] * (MLSTM_DH ** -0.5)
            vh = v_ref[rows, hc].astype(BF16)
            s = lax.dot_general(qh.astype(BF16), kh.astype(BF16), _NT, preferred_element_type=F32) * w
            qp = qh * wp
            cm = c_ref[h]
            nv = n_ref[h]
            num = (jnp.dot(s.astype(BF16), vh, preferred_element_type=F32)
                   + jnp.dot(qp.astype(BF16), cm.astype(BF16), preferred_element_type=F32))
            den = jnp.sum(s, axis=-1, keepdims=True) + jnp.sum(qp * nv, axis=-1, keepdims=True)
            hh = num / jnp.maximum(jnp.abs(den), jnp.exp(-mj))
            m_new = mj[last:last + 1, :]
            tot = f_col[last:last + 1, :]
            wl = jnp.exp(tot - f_col + i_col - m_new)
            decay = jnp.exp(tot + m_prev - m_new)
            kw = kh * wl
            c_ref[h] = decay * cm + lax.dot_general(kw.astype(BF16), vh, _TN, preferred_element_type=F32)
            n_ref[h] = decay * nv + jnp.sum(kw, axis=0, keepdims=True)
            m_ref[h] = jnp.broadcast_to(m_new, (1, 128))
            if reverse:
                y = jax.nn.sigmoid(ob_ref[rows, hc]) * _head_norm(hf_ref[rows, hc] + hh, gn_ref[:, hc])
                out_ref[rows, hc] = y.astype(out_ref.dtype)
            else:
                out_ref[rows, hc] = hh

    @pl.when(jnp.logical_and(is_ctx, j == k - 1))
    def _():
        co_ref[0] = c_ref[...]
        no_ref[0] = n_ref[...]
        mo_ref[0] = m_ref[...]


def _mlstm(cfg, z, g_rows, b_col, b_row, c0, n0, m0, reverse, h_fwd=None, gnorm=None):
    m = cfg.m
    nblk = m // TB
    rb = functools.partial(_row_block, cfg, reverse)
    hd = MLSTM_HEADS * MLSTM_DH
    lat = lambda i: (_lat_seq(cfg, i), 0, 0, 0)
    ctx = lambda i: (_ctx_seq(cfg, i), 0, 0, 0)
    in_specs = [pl.BlockSpec((TB, hd), lambda i: (rb(i), Z_QB // hd)),
                pl.BlockSpec((TB, hd), lambda i: (rb(i), Z_KB // hd)),
                pl.BlockSpec((TB, hd), lambda i: (rb(i), Z_VB // hd)),
                pl.BlockSpec((TB, SM_W), lambda i: (rb(i), Z_SM // SM_W)),
                pl.BlockSpec((1, NCH, 4 * MLSTM_HEADS, CHUNK), lambda i: (rb(i), 0, 0, 0)),
                pl.BlockSpec((1, 4 * MLSTM_HEADS), lambda i: (0, 0)),
                pl.BlockSpec((4 * MLSTM_HEADS, 1), lambda i: (0, 0)),
                pl.BlockSpec((1, MLSTM_HEADS, MLSTM_DH, MLSTM_DH), lat),
                pl.BlockSpec((1, MLSTM_HEADS, 1, MLSTM_DH), lat),
                pl.BlockSpec((1, MLSTM_HEADS, 1, 128), lat)]
    args = [z, z, z, z, g_rows, b_col, b_row, c0, n0, m0]
    if reverse:
        in_specs += [pl.BlockSpec((TB, hd), lambda i: (rb(i), Z_OB // hd)),
                     pl.BlockSpec((TB, hd), lambda i: (rb(i), 0)),
                     pl.BlockSpec((1, hd), lambda i: (0, 0))]
        args += [z, h_fwd, gnorm]
    return pl.pallas_call(
        functools.partial(_mlstm_kernel, cfg=cfg, reverse=reverse),
        grid=(nblk,),
        in_specs=in_specs,
        out_specs=[pl.BlockSpec((TB, hd), lambda i: (rb(i), 0)),
                   pl.BlockSpec((1, MLSTM_HEADS, MLSTM_DH, MLSTM_DH), ctx),
                   pl.BlockSpec((1, MLSTM_HEADS, 1, MLSTM_DH), ctx),
                   pl.BlockSpec((1, MLSTM_HEADS, 1, 128), ctx)],
        out_shape=[jax.ShapeDtypeStruct((m, hd), BF16 if reverse else F32),
                   jax.ShapeDtypeStruct((cfg.bc, MLSTM_HEADS, MLSTM_DH, MLSTM_DH), F32),
                   jax.ShapeDtypeStruct((cfg.bc, MLSTM_HEADS, 1, MLSTM_DH), F32),
                   jax.ShapeDtypeStruct((cfg.bc, MLSTM_HEADS, 1, 128), F32)],
        scratch_shapes=[pltpu.VMEM((MLSTM_HEADS, MLSTM_DH, MLSTM_DH), F32),
                        pltpu.VMEM((MLSTM_HEADS, 1, MLSTM_DH), F32),
                        pltpu.VMEM((MLSTM_HEADS, 1, 128), F32)],
        compiler_params=_params(("arbitrary",)),
        name="mlstm_rev" if reverse else "mlstm_fwd",
    )(*args)


LRU_ROWS = 256
LRU_PAD = 8


def _rglru_kernel(*refs, t, aliased):
    if aliased:
        refs = refs[1:]
    (xr_ref, yr_ref, cw_ref, cb_ref, gw_ref, gb_ref, lam_ref, h0_ref, y_ref, he_ref,
     pad_ref, a_ref, b_ref, hf_ref, hr_ref) = refs
    nsteps = t // LRU_ROWS
    zeros = jnp.zeros((LRU_PAD, LRU_BLOCK), F32)
    pad_ref[0:LRU_PAD, :] = zeros
    pad_ref[LRU_PAD + t:2 * LRU_PAD + t, :] = zeros

    def fill(ci, _):
        r0 = pl.multiple_of(ci * LRU_ROWS, LRU_ROWS)
        pad_ref[pl.ds(r0 + LRU_PAD, LRU_ROWS), :] = xr_ref[pl.ds(r0, LRU_ROWS), :]
        return 0

    lax.fori_loop(0, nsteps, fill, 0)

    sp = _softplus(-lam_ref[0])
    left = LRU_CONV // 2

    def gates(ci, _):
        r0 = pl.multiple_of(ci * LRU_ROWS, LRU_ROWS)
        xc = cb_ref[...]
        for tap in range(LRU_CONV):
            xc = xc + cw_ref[tap:tap + 1, :] * pad_ref[pl.ds(r0 + LRU_PAD - left + tap, LRU_ROWS), :]
        pre = jnp.dot(xc.astype(BF16), gw_ref[0], preferred_element_type=F32) + gb_ref[0]
        for d in range(N_DIR):
            o = d * 2 * LRU_BLOCK
            r = jax.nn.sigmoid(pre[:, o:o + LRU_BLOCK])
            ig = jax.nn.sigmoid(pre[:, o + LRU_BLOCK:o + 2 * LRU_BLOCK])
            log_a = (-LRU_C * r) * sp[:, d * LRU_BLOCK:(d + 1) * LRU_BLOCK]
            a = jnp.exp(log_a)
            a_ref[d, pl.ds(r0, LRU_ROWS), :] = a
            b_ref[d, pl.ds(r0, LRU_ROWS), :] = jnp.sqrt(1.0 - a * a) * (ig * xc)
        return 0

    lax.fori_loop(0, nsteps, gates, 0)

    ngrp = t // 8
    row = lax.broadcasted_iota(jnp.int32, (8, LRU_BLOCK), 0)

    def scan(g, carry):
        cf, cr = carry
        rf = pl.multiple_of(g * 8, 8)
        rr = pl.multiple_of((ngrp - 1 - g) * 8, 8)
        a = a_ref[0, pl.ds(rf, 8), :]
        b = b_ref[0, pl.ds(rf, 8), :]
        for s in (1, 2, 4):
            keep = row >= s
            b = a * jnp.where(keep, pltpu.roll(b, s, axis=0), 0.0) + b
            a = a * jnp.where(keep, pltpu.roll(a, s, axis=0), 1.0)
        hf = a * cf + b
        hf_ref[pl.ds(rf, 8), :] = hf
        cf = jnp.broadcast_to(hf[7:8, :], (8, LRU_BLOCK))
        a = a_ref[1, pl.ds(rr, 8), :]
        b = b_ref[1, pl.ds(rr, 8), :]
        for s in (1, 2, 4):
            keep = row < 8 - s
            b = a * jnp.where(keep, pltpu.roll(b, 8 - s, axis=0), 0.0) + b
            a = a * jnp.where(keep, pltpu.roll(a, 8 - s, axis=0), 1.0)
        hr = a * cr + b
        hr_ref[pl.ds(rr, 8), :] = hr
        cr = jnp.broadcast_to(hr[0:1, :], (8, LRU_BLOCK))
        return cf, cr

    cf0 = jnp.broadcast_to(h0_ref[0, 0:1, :], (8, LRU_BLOCK))
    cr0 = jnp.broadcast_to(h0_ref[0, 1:2, :], (8, LRU_BLOCK))
    cf, cr = lax.fori_loop(0, ngrp, scan, (cf0, cr0), unroll=2)
    he_ref[0, 0:1, :] = cf[0:1, :]
    he_ref[0, 1:2, :] = cr[0:1, :]

    def finish(ci, _):
        r0 = pl.multiple_of(ci * LRU_ROWS, LRU_ROWS)
        y = (hf_ref[pl.ds(r0, LRU_ROWS), :] + hr_ref[pl.ds(r0, LRU_ROWS), :]) * _gelu_tanh(yr_ref[pl.ds(r0, LRU_ROWS), :])
        y_ref[pl.ds(r0, LRU_ROWS), :] = y.astype(y_ref.dtype)
        return 0

    lax.fori_loop(0, nsteps, finish, 0)


def _rglru(cfg, z, cw, cb, gw, gb, lam, h0, nseq, t, row0, y_prev=None):
    m = cfg.m
    assert row0 % t == 0
    sb = row0 // t
    aliased = y_prev is not None
    in_specs = [pl.BlockSpec((t, LRU_BLOCK), lambda b, n: (sb + b, Z_XR // LRU_BLOCK + n)),
                pl.BlockSpec((t, LRU_BLOCK), lambda b, n: (sb + b, Z_YR // LRU_BLOCK + n)),
                pl.BlockSpec((LRU_CONV, LRU_BLOCK), lambda b, n: (0, n)),
                pl.BlockSpec((1, LRU_BLOCK), lambda b, n: (0, n)),
                pl.BlockSpec((1, LRU_BLOCK, 4 * LRU_BLOCK), lambda b, n: (n, 0, 0)),
                pl.BlockSpec((1, 1, 4 * LRU_BLOCK), lambda b, n: (n, 0, 0)),
                pl.BlockSpec((1, 1, 2 * LRU_BLOCK), lambda b, n: (n, 0, 0)),
                pl.BlockSpec((1, N_DIR, LRU_BLOCK), lambda b, n: (b, 0, n))]
    args = [z, z, cw, cb, gw, gb, lam, h0]
    if aliased:
        in_specs = [pl.BlockSpec(memory_space=pl.ANY)] + in_specs
        args = [y_prev] + args
    return pl.pallas_call(
        functools.partial(_rglru_kernel, t=t, aliased=aliased),
        grid=(nseq, LRU_BLOCKS),
        in_specs=in_specs,
        out_specs=[pl.BlockSpec((t, LRU_BLOCK), lambda b, n: (sb + b, n)),
                   pl.BlockSpec((1, N_DIR, LRU_BLOCK), lambda b, n: (b, 0, n))],
        out_shape=[jax.ShapeDtypeStruct((m, LRU_WIDTH), BF16),
                   jax.ShapeDtypeStruct((nseq, N_DIR, LRU_WIDTH), F32)],
        scratch_shapes=[pltpu.VMEM((t + 2 * LRU_PAD, LRU_BLOCK), F32),
                        pltpu.VMEM((N_DIR, t, LRU_BLOCK), F32),
                        pltpu.VMEM((N_DIR, t, LRU_BLOCK), F32),
                        pltpu.VMEM((t, LRU_BLOCK), F32),
                        pltpu.VMEM((t, LRU_BLOCK), F32)],
        input_output_aliases={0: 0} if aliased else {},
        compiler_params=_params(("parallel", "parallel")),
        name="rglru",
    )(*args)


def _merge_kernel(ya_ref, yb_ref, yc_ref, w_ref, ma_ref, mb_ref, mc_ref, bm_ref, o_ref):
    acc = None
    for n, (y_ref, mg_ref) in enumerate(((ya_ref, ma_ref), (yb_ref, mb_ref), (yc_ref, mc_ref))):
        g = jax.nn.sigmoid(mg_ref[...] + bm_ref[n])
        term = g * jnp.dot(y_ref[...], w_ref[n], preferred_element_type=F32)
        acc = term if acc is None else acc + term
    o_ref[...] = acc.astype(o_ref.dtype)


def _merge(cfg, ya, yb, yc, wbr, z, bm, tm, tn):
    m = cfg.m
    nj = D_MODEL // tn
    y_spec = pl.BlockSpec((tm, BRANCH_W), lambda j, i: (i, 0))
    mg_spec = lambda n: pl.BlockSpec((tm, tn), lambda j, i: (i, Z_MG // tn + n * nj + j))
    return pl.pallas_call(
        _merge_kernel,
        grid=(nj, m // tm),
        in_specs=[y_spec, y_spec, y_spec,
                  pl.BlockSpec((N_BRANCH, BRANCH_W, tn), lambda j, i: (0, 0, j)),
                  mg_spec(0), mg_spec(1), mg_spec(2),
                  pl.BlockSpec((N_BRANCH, 1, tn), lambda j, i: (0, 0, j))],
        out_specs=pl.BlockSpec((tm, tn), lambda j, i: (i, j)),
        out_shape=jax.ShapeDtypeStruct((m, D_MODEL), BF16),
        compiler_params=_params(("parallel", "parallel")),
        name="merge",
    )(ya, yb, yc, wbr, z, z, z, bm)


def _matmul_res_kernel(a_ref, w_ref, x_ref, g_ref, o_ref):
    o_ref[...] = x_ref[...] + g_ref[0] * jnp.dot(a_ref[...], w_ref[...], preferred_element_type=F32)


def _matmul_res(cfg, a, w, x, gate, tm, tn):
    m, kdim = a.shape
    n = w.shape[1]
    return pl.pallas_call(
        _matmul_res_kernel,
        grid=(n // tn, m // tm),
        in_specs=[pl.BlockSpec((tm, kdim), lambda j, i: (i, 0)),
                  pl.BlockSpec((kdim, tn), lambda j, i: (0, j)),
                  pl.BlockSpec((tm, tn), lambda j, i: (i, j)),
                  pl.BlockSpec((1, 1, tn), lambda j, i: (_mod_row(cfg, i * tm), 0, j))],
        out_specs=pl.BlockSpec((tm, tn), lambda j, i: (i, j)),
        out_shape=jax.ShapeDtypeStruct((m, n), F32),
        compiler_params=_params(("parallel", "parallel")),
        name="matmul_res",
    )(a, w, x, gate)


FFN_ROWS = 128
FFN_PAD = 72
FFN_CT = 128


def _conv_act_kernel(*refs, t, taps, aliased):
    if aliased:
        refs = refs[1:]
    hg_ref, hu_ref, w_ref, b_ref, o_ref, pad_ref = refs
    nsteps = t // FFN_ROWS
    zeros = jnp.zeros((FFN_PAD, FFN_CT), F32)
    pad_ref[0:FFN_PAD, :] = zeros
    pad_ref[FFN_PAD + t:2 * FFN_PAD + t, :] = zeros

    def fill(ci, _):
        r0 = pl.multiple_of(ci * FFN_ROWS, FFN_ROWS)
        pad_ref[pl.ds(r0 + FFN_PAD, FFN_ROWS), :] = hg_ref[pl.ds(r0, FFN_ROWS), :]
        return 0

    lax.fori_loop(0, nsteps, fill, 0)
    col = lax.broadcasted_iota(jnp.int32, (FFN_ROWS, FFN_CT), 0) % GRID_W

    def step(ci, _):
        r0 = pl.multiple_of(ci * FFN_ROWS, FFN_ROWS)
        acc = b_ref[...]
        for shift, widx, dx in taps:
            xs = pad_ref[pl.ds(r0 + FFN_PAD + shift, FFN_ROWS), :]
            if dx < 0:
                xs = jnp.where(col >= -dx, xs, 0.0)
            elif dx > 0:
                xs = jnp.where(col < GRID_W - dx, xs, 0.0)
            acc = acc + xs * w_ref[widx:widx + 1, :]
        o_ref[pl.ds(r0, FFN_ROWS), :] = (_silu(acc) * hu_ref[pl.ds(r0, FFN_ROWS), :]).astype(o_ref.dtype)
        return 0

    lax.fori_loop(0, nsteps, step, 0)


def _conv_act(cfg, h, w9, bias, nseq, t, row0, on_grid, prev=None):
    m = cfg.m
    assert row0 % t == 0 and t % FFN_ROWS == 0
    sb = row0 // t
    p = FFN_CONV // 2
    if on_grid:
        taps = tuple(((dy - p) * GRID_W + (dx - p), dy * FFN_CONV + dx, dx - p)
                     for dy in range(FFN_CONV) for dx in range(FFN_CONV))
    else:
        taps = tuple((dx - p, p * FFN_CONV + dx, 0) for dx in range(FFN_CONV))
    aliased = prev is not None
    nct = D_FF // FFN_CT
    in_specs = [pl.BlockSpec((t, FFN_CT), lambda b, j: (sb + b, j)),
                pl.BlockSpec((t, FFN_CT), lambda b, j: (sb + b, nct + j)),
                pl.BlockSpec((FFN_CONV * FFN_CONV, FFN_CT), lambda b, j: (0, j)),
                pl.BlockSpec((1, FFN_CT), lambda b, j: (0, j))]
    args = [h, h, w9, bias]
    if aliased:
        in_specs = [pl.BlockSpec(memory_space=pl.ANY)] + in_specs
        args = [prev] + args
    return pl.pallas_call(
        functools.partial(_conv_act_kernel, t=t, taps=taps, aliased=aliased),
        grid=(nseq, nct),
        in_specs=in_specs,
        out_specs=pl.BlockSpec((t, FFN_CT), lambda b, j: (sb + b, j)),
        out_shape=jax.ShapeDtypeStruct((m, D_FF), BF16),
        scratch_shapes=[pltpu.VMEM((t + 2 * FFN_PAD, FFN_CT), F32)],
        input_output_aliases={0: 0} if aliased else {},
        compiler_params=_params(("parallel", "parallel")),
        name="conv_act",
    )(*args)


def _rmsnorm_kernel(x_ref, g_ref, o_ref):
    x = x_ref[...]
    o_ref[...] = x * lax.rsqrt(jnp.mean(x * x, axis=-1, keepdims=True) + EPS) * g_ref[...]


def _final_norm(x, gain, row0, rows, tm):
    d = x.shape[1]
    assert row0 % tm == 0 and rows % tm == 0
    return pl.pallas_call(
        _rmsnorm_kernel,
        grid=(rows // tm,),
        in_specs=[pl.BlockSpec((tm, d), lambda i: (row0 // tm + i, 0)),
                  pl.BlockSpec((1, d), lambda i: (0, 0))],
        out_specs=pl.BlockSpec((tm, d), lambda i: (i, 0)),
        out_shape=jax.ShapeDtypeStruct((rows, d), F32),
        compiler_params=_params(("parallel",)),
        name="final_norm",
    )(x, gain)


def _pack_w_in(w):
    s = _SRC
    cols = [w[:, s["mg"]:s["end"]], w[:, s["va"]:s["ga"]], w[:, s["ga"]:s["ra"]], w[:, s["qb"]:s["kb"]],
            w[:, s["kb"]:s["vb"]], w[:, s["vb"]:s["ob"]], w[:, s["ob"]:s["gb"]], w[:, s["xr"]:s["yr"]],
            w[:, s["yr"]:s["mg"]], w[:, s["qa"]:s["ka"]], w[:, s["ka"]:s["va"]], w[:, s["ra"]:s["qb"]],
            w[:, s["gb"]:s["xr"]]]
    used = sum(c.shape[1] for c in cols)
    cols.append(jnp.zeros((w.shape[0], NZ - used), w.dtype))
    return jnp.concatenate(cols, axis=1).astype(BF16)


def _pack_gate_w(gw, gb, lam):
    nb, blk = LRU_BLOCKS, LRU_BLOCK
    gw_p = jnp.transpose(gw, (2, 3, 0, 1, 4)).reshape(nb, blk, 4 * blk).astype(BF16)
    gb_p = jnp.transpose(gb.reshape(2, 2, nb, blk), (2, 0, 1, 3)).reshape(nb, 1, 4 * blk)
    lam_p = jnp.transpose(lam.reshape(2, nb, blk), (1, 0, 2)).reshape(nb, 1, 2 * blk)
    return gw_p, gb_p, lam_p


def _layer(cfg, x, mod, lp, states):
    (n1, n2, w_in, gwa, gba, gng, mbif, mng, lcw, lcb, lgw, lgb, llam, wbr, bmg, wout, fup, fcw, fcb, fdown) = lp
    s_gla, s_mc, s_mn, s_mm, s_lru = states
    m = cfg.m
    sh1, sc1, g1, sh2, sc2, g2 = (mod[:, k * D_MODEL:(k + 1) * D_MODEL].reshape(MOD_ROWS, 1, D_MODEL) for k in range(6))

    z = _norm_matmul(cfg, x, n1.reshape(1, -1), sc1, sh1, _pack_w_in(w_in), 512, 512)

    s0t = jnp.swapaxes(s_gla, -1, -2)
    o_f, sa_f = _gla(cfg, z, gwa[0], gba[0].reshape(1, -1), s0t[:, 0], False)
    y_a, sa_r = _gla(cfg, z, gwa[1], gba[1].reshape(1, -1), s0t[:, 1], True, o_f, gng.reshape(1, -1))
    new_gla = jnp.swapaxes(jnp.stack([sa_f, sa_r], axis=1), -1, -2)

    gb = z[:, Z_SM + SM_GB:Z_SM + SM_GB + 4 * MLSTM_HEADS]
    g_rows = jnp.swapaxes(gb.reshape(m // TB, NCH, CHUNK, 4 * MLSTM_HEADS), -1, -2)
    b_col = mbif.reshape(1, -1)
    b_row = mbif.reshape(-1, 1)
    n0 = s_mn[:, :, :, None, :]
    m0 = jnp.broadcast_to(s_mm[:, :, :, None, None], s_mm.shape + (1, 128))
    h_f, cb_f, nb_f, mb_f = _mlstm(cfg, z, g_rows, b_col, b_row, s_mc[:, 0], n0[:, 0], m0[:, 0], False)
    y_b, cb_r, nb_r, mb_r = _mlstm(cfg, z, g_rows, b_col, b_row, s_mc[:, 1], n0[:, 1], m0[:, 1], True,
                                   h_f, mng.reshape(1, -1))
    new_mc = jnp.stack([cb_f, cb_r], axis=1)
    new_mn = jnp.stack([nb_f[:, :, 0], nb_r[:, :, 0]], axis=1)
    new_mm = jnp.stack([mb_f[:, :, 0, 0], mb_r[:, :, 0, 0]], axis=1)

    gw_p, gb_p, lam_p = _pack_gate_w(lgw, lgb, llam)
    cbias = lcb.reshape(1, -1)
    y_c, new_lru = _rglru(cfg, z, lcw, cbias, gw_p, gb_p, lam_p, jnp.zeros((cfg.bc, N_DIR, LRU_WIDTH), F32),
                          cfg.bc, cfg.tc, 0)
    y_c, _ = _rglru(cfg, z, lcw, cbias, gw_p, gb_p, lam_p, s_lru, cfg.bl, cfg.tl, cfg.m_ctx, y_prev=y_c)

    merged = _merge(cfg, y_a, y_b, y_c, wbr.astype(BF16), z, bmg.reshape(N_BRANCH, 1, D_MODEL), 512, 512)
    x = _matmul_res(cfg, merged, wout.astype(BF16), x, g1, 512, 1024)

    h = _norm_matmul(cfg, x, n2.reshape(1, -1), sc2, sh2, fup.astype(BF16), 512, 512)
    w9 = fcw.reshape(FFN_CONV * FFN_CONV, D_FF)
    fbias = fcb.reshape(1, -1)
    act = _conv_act(cfg, h, w9, fbias, cfg.bc, cfg.tc, 0, False)
    act = _conv_act(cfg, h, w9, fbias, cfg.bl, cfg.tl, cfg.m_ctx, True, prev=act)
    x = _matmul_res(cfg, act, fdown.astype(BF16), x, g2, 512, 512)
    return x, (new_gla, new_mc, new_mn, new_mm, new_lru)


def kernel(x_prompt, x_sample, state_gla, state_mlstm_c, state_mlstm_n, state_mlstm_m, state_rglru, c, c_ctx, norm1_g, norm2_g, w_mod, b_mod, w_in, gla_w_alpha, gla_b_alpha, gla_norm_g, mlstm_b_if, mlstm_norm_g, lru_conv_w, lru_conv_b, lru_gate_w, lru_gate_b, lru_lambda, w_branch, b_merge, w_out, ffn_w_up, ffn_conv_w, ffn_conv_b, ffn_w_down, norm_f_g):
    bc, tc, d = x_prompt.shape
    bl, tl, _ = x_sample.shape
    cfg = Cfg(bc, tc, bl, tl)
    assert tc % TB == 0 and tl % TB == 0 and cfg.m_ctx % tl == 0 and 1 + bl <= MOD_ROWS
    depth = w_in.shape[0]

    x = jnp.concatenate([x_prompt.reshape(bc * tc, d), x_sample.reshape(bl * tl, d)], axis=0)
    c_all = jnp.concatenate([c_ctx[None, :], c, jnp.zeros((MOD_ROWS - 1 - bl, d), F32)], axis=0)
    mod = _modulation(c_all, w_mod, b_mod)

    new = []
    for l in range(depth):
        lp = (norm1_g[l], norm2_g[l], w_in[l], gla_w_alpha[l], gla_b_alpha[l], gla_norm_g[l], mlstm_b_if[l],
              mlstm_norm_g[l], lru_conv_w[l], lru_conv_b[l], lru_gate_w[l], lru_gate_b[l], lru_lambda[l],
              w_branch[l], b_merge[l], w_out[l], ffn_w_up[l], ffn_conv_w[l], ffn_conv_b[l], ffn_w_down[l])
        states = (state_gla[:, l], state_mlstm_c[:, l], state_mlstm_n[:, l], state_mlstm_m[:, l], state_rglru[:, l])
        x, st = _layer(cfg, x, mod[l], lp, states)
        new.append(st)

    gain = norm_f_g.reshape(1, -1)
    y_prompt = _final_norm(x, gain, 0, cfg.m_ctx, 512).reshape(bc, tc, d)
    y_sample = _final_norm(x, gain, cfg.m_ctx, bl * tl, 512).reshape(bl, tl, d)
    stacked = tuple(jnp.stack([new[l][k] for l in range(depth)], axis=1) for k in range(5))
    return (y_prompt, y_sample) + stacked
```

```python
import functools
from typing import NamedTuple

import numpy as np
import jax
import jax.numpy as jnp
from jax import lax
from jax.experimental import pallas as pl
from jax.experimental.pallas import tpu as pltpu

F32 = jnp.float32
BF16 = jnp.bfloat16

D_MODEL = 2048
DEPTH = 2
GRID_W = 64
N_DIR = 2
N_BRANCH = 3
BRANCH_W = 1024
GLA_HEADS = 4
GLA_DK = 128
GLA_DV = 256
GLA_RANK = 16
GLA_TAU = 16.0
MLSTM_HEADS = 4
MLSTM_DH = 256
CHUNK = 64
LRU_WIDTH = 1024
LRU_BLOCKS = 8
LRU_BLOCK = 128
LRU_CONV = 4
LRU_C = 8.0
D_FF = 5632
FFN_CONV = 3
EPS = 1e-6

_SRC = dict(qa=0, ka=512, va=1024, ga=2048, ra=3072, qb=3104, kb=4128, vb=5152, ob=6176, gb=7200,
            xr=7216, yr=8240, mg=9264, end=15408)
Z_MG, Z_VA, Z_GA, Z_QB, Z_KB, Z_VB, Z_OB, Z_XR, Z_YR, Z_QA, Z_KA, Z_SM = (
    0, 6144, 7168, 8192, 9216, 10240, 11264, 12288, 13312, 14336, 14848, 15360)
SM_W = 128
SM_RA, SM_GB = 0, 32
NZ = 15872
MOD_ROWS = 16

PROJ_TM = 1024
NORM_SLAB = 32
TB = 256
NCH = TB // CHUNK
VMEM_LIMIT = 48 * 1024 * 1024


class Cfg(NamedTuple):
    bc: int
    tc: int
    bl: int
    tl: int

    @property
    def m_ctx(self):
        return self.bc * self.tc

    @property
    def m(self):
        return self.bc * self.tc + self.bl * self.tl


def _params(sem):
    return pltpu.CompilerParams(dimension_semantics=sem, vmem_limit_bytes=VMEM_LIMIT)


def _softplus(x):
    return jnp.maximum(x, 0.0) + jnp.log1p(jnp.exp(-jnp.abs(x)))


def _log_sigmoid(x):
    return -_softplus(-x)


def _silu(x):
    return x * jax.nn.sigmoid(x)


def _gelu_tanh(x):
    return 0.5 * x * (1.0 + jnp.tanh(np.sqrt(2.0 / np.pi).astype(np.float32) * (x + 0.044715 * (x * x * x))))


def _mod_row(cfg, row):
    return jnp.where(row < cfg.m_ctx, 0, 1 + jnp.maximum(row - cfg.m_ctx, 0) // cfg.tl)


def _mod_kernel(c_ref, w_ref, b_ref, o_ref):
    c = c_ref[...]
    a = _silu(c).astype(BF16)
    o_ref[0] = jnp.dot(a, w_ref[0].astype(BF16), preferred_element_type=F32) + b_ref[0]


def _modulation(c_all, w_mod, b_mod):
    depth, d, n = w_mod.shape
    tn = 512
    return pl.pallas_call(
        _mod_kernel,
        grid=(depth, n // tn),
        in_specs=[pl.BlockSpec((MOD_ROWS, d), lambda l, j: (0, 0)),
                  pl.BlockSpec((1, d, tn), lambda l, j: (l, 0, j)),
                  pl.BlockSpec((1, 1, tn), lambda l, j: (l, 0, j))],
        out_specs=pl.BlockSpec((1, MOD_ROWS, tn), lambda l, j: (l, 0, j)),
        out_shape=jax.ShapeDtypeStruct((depth, MOD_ROWS, n), F32),
        compiler_params=_params(("parallel", "parallel")),
        name="modulation",
    )(c_all, w_mod, b_mod.reshape(depth, 1, n))


def _norm_matmul_kernel(x_ref, g_ref, sc_ref, sh_ref, w_ref, o_ref, u_ref):
    @pl.when(pl.program_id(1) == 0)
    def _():
        def slab(si, _):
            r0 = pl.multiple_of(si * NORM_SLAB, NORM_SLAB)
            x = x_ref[pl.ds(r0, NORM_SLAB), :]
            y = x * lax.rsqrt(jnp.mean(x * x, axis=-1, keepdims=True) + EPS) * g_ref[...]
            u_ref[pl.ds(r0, NORM_SLAB), :] = (y * (1.0 + sc_ref[0]) + sh_ref[0]).astype(BF16)
            return 0

        lax.fori_loop(0, x_ref.shape[0] // NORM_SLAB, slab, 0)

    o_ref[...] = jnp.dot(u_ref[...], w_ref[...], preferred_element_type=F32)


def _norm_matmul(cfg, x, gain, sc, sh, w, tm, tn):
    m, d = x.shape
    n = w.shape[1]
    mod_map = lambda i, j: (_mod_row(cfg, i * tm), 0, 0)
    return pl.pallas_call(
        _norm_matmul_kernel,
        grid=(m // tm, n // tn),
        in_specs=[pl.BlockSpec((tm, d), lambda i, j: (i, 0)),
                  pl.BlockSpec((1, d), lambda i, j: (0, 0)),
                  pl.BlockSpec((1, 1, d), mod_map),
                  pl.BlockSpec((1, 1, d), mod_map),
                  pl.BlockSpec((d, tn), lambda i, j: (0, j))],
        out_specs=pl.BlockSpec((tm, tn), lambda i, j: (i, j)),
        out_shape=jax.ShapeDtypeStruct((m, n), F32),
        scratch_shapes=[pltpu.VMEM((tm, d), BF16)],
        compiler_params=_params(("parallel", "arbitrary")),
        name="norm_matmul",
    )(x, gain, sc, sh, w)


def _seq_pos(cfg, i):
    kc, kl = cfg.tc // TB, cfg.tl // TB
    nc = cfg.bc * kc
    is_ctx = i < nc
    il = jnp.maximum(i - nc, 0)
    j = jnp.where(is_ctx, i % kc, il % kl)
    k = jnp.where(is_ctx, kc, kl)
    s = jnp.where(is_ctx, i // kc, il // kl)
    return is_ctx, s, j, k


def _row_block(cfg, reverse, i):
    kc, kl = cfg.tc // TB, cfg.tl // TB
    is_ctx, s, j, k = _seq_pos(cfg, i)
    jj = (k - 1 - j) if reverse else j
    return jnp.where(is_ctx, s * kc + jj, cfg.bc * kc + s * kl + jj)


def _lat_seq(cfg, i):
    is_ctx, s, _, _ = _seq_pos(cfg, i)
    return jnp.where(is_ctx, 0, jnp.minimum(s, cfg.bl - 1))


def _ctx_seq(cfg, i):
    is_ctx, s, _, _ = _seq_pos(cfg, i)
    return jnp.where(is_ctx, s, cfg.bc - 1)


def _tri(reverse, n=CHUNK):
    row = lax.broadcasted_iota(jnp.int32, (n, n), 0)
    col = lax.broadcasted_iota(jnp.int32, (n, n), 1)
    return (row <= col) if reverse else (row >= col)


def _head_norm(x, g):
    return x * lax.rsqrt(jnp.mean(x * x, axis=-1, keepdims=True) + EPS) * g


_NT = (((1,), (1,)), ((), ()))
_TN = (((0,), (0,)), ((), ()))


def _gla_kernel(*refs, cfg, reverse):
    if reverse:
        (q_ref, k_ref, v_ref, sm_ref, wal_ref, bal_ref, s0_ref, ga_ref, of_ref, gn_ref,
         out_ref, so_ref, st_ref, la_ref) = refs
    else:
        (q_ref, k_ref, v_ref, sm_ref, wal_ref, bal_ref, s0_ref, out_ref, so_ref, st_ref, la_ref) = refs
    is_ctx, _, j, k = _seq_pos(cfg, pl.program_id(0))

    @pl.when(jnp.logical_and(j == 0, is_ctx))
    def _():
        st_ref[...] = jnp.zeros_like(st_ref)

    @pl.when(jnp.logical_and(j == 0, jnp.logical_not(is_ctx)))
    def _():
        st_ref[...] = s0_ref[0]

    d = 1 if reverse else 0
    ra =sm_ref[:, SM_RA + d * GLA_RANK:SM_RA + (d + 1) * GLA_RANK].astype(BF16)
    pre = jnp.dot(ra, wal_ref[...].astype(BF16), preferred_element_type=F32) + bal_ref[...]
    la_ref[...] = _log_sigmoid(pre) * (1.0 / GLA_TAU)

    tri = _tri(reverse)
    tri_f = tri.astype(F32)
    last = 0 if reverse else CHUNK - 1
    for c in (range(NCH - 1, -1, -1) if reverse else range(NCH)):
        rows = slice(c * CHUNK, (c + 1) * CHUNK)
        cum = jnp.dot(tri_f, la_ref[rows, :], precision=lax.Precision.HIGHEST, preferred_element_type=F32)
        tot = cum[last:last + 1, :]
        e_q = jnp.exp(cum)
        e_k = jnp.exp(-cum)
        e_end = jnp.exp(tot - cum)
        e_tot = jnp.exp(tot)
        for h in range(GLA_HEADS):
            kc = slice(h * GLA_DK, (h + 1) * GLA_DK)
            vc = slice(h * GLA_DV, (h + 1) * GLA_DV)
            qh = q_ref[rows, kc] * (GLA_DK ** -0.5)
            kh = k_ref[rows, kc]
            vh = v_ref[rows, vc].astype(BF16)
            q_in = (qh * e_q[:, kc]).astype(BF16)
            k_in = (kh * e_k[:, kc]).astype(BF16)
            att = lax.dot_general(q_in, k_in, _NT, preferred_element_type=F32)
            att = jnp.where(tri, att, 0.0).astype(BF16)
            st = st_ref[h]
            o = (jnp.dot(att, vh, preferred_element_type=F32)
                 + lax.dot_general(q_in, st.astype(BF16), _NT, preferred_element_type=F32))
            k_end = (kh * e_end[:, kc]).astype(BF16)
            st_ref[h] = e_tot[:, kc] * st + lax.dot_general(vh, k_end, _TN, preferred_element_type=F32)
            if reverse:
                y = _head_norm(of_ref[rows, vc] + o, gn_ref[:, vc]) * _silu(ga_ref[rows, vc])
                out_ref[rows, vc] = y.astype(out_ref.dtype)
            else:
                out_ref[rows, vc] = o

    @pl.when(jnp.logical_and(is_ctx, j == k - 1))
    def _():
        so_ref[0] = st_ref[...]


def _gla(cfg, z, wal, bal, s0t, reverse, o_fwd=None, gnorm=None):
    m = cfg.m
    nblk = m // TB
    rb = functools.partial(_row_block, cfg, reverse)
    hk, hv = GLA_HEADS * GLA_DK, GLA_HEADS * GLA_DV
    in_specs = [pl.BlockSpec((TB, hk), lambda i: (rb(i), Z_QA // hk)),
                pl.BlockSpec((TB, hk), lambda i: (rb(i), Z_KA // hk)),
                pl.BlockSpec((TB, hv), lambda i: (rb(i), Z_VA // hv)),
                pl.BlockSpec((TB, SM_W), lambda i: (rb(i), Z_SM // SM_W)),
                pl.BlockSpec((GLA_RANK, hk), lambda i: (0, 0)),
                pl.BlockSpec((1, hk), lambda i: (0, 0)),
                pl.BlockSpec((1, GLA_HEADS, GLA_DV, GLA_DK), lambda i: (_lat_seq(cfg, i), 0, 0, 0))]
    args = [z, z, z, z, wal, bal, s0t]
    if reverse:
        in_specs += [pl.BlockSpec((TB, hv), lambda i: (rb(i), Z_GA // hv)),
                     pl.BlockSpec((TB, hv), lambda i: (rb(i), 0)),
                     pl.BlockSpec((1, hv), lambda i: (0, 0))]
        args += [z, o_fwd, gnorm]
    return pl.pallas_call(
        functools.partial(_gla_kernel, cfg=cfg, reverse=reverse),
        grid=(nblk,),
        in_specs=in_specs,
        out_specs=[pl.BlockSpec((TB, hv), lambda i: (rb(i), 0)),
                   pl.BlockSpec((1, GLA_HEADS, GLA_DV, GLA_DK), lambda i: (_ctx_seq(cfg, i), 0, 0, 0))],
        out_shape=[jax.ShapeDtypeStruct((m, hv), BF16 if reverse else F32),
                   jax.ShapeDtypeStruct((cfg.bc, GLA_HEADS, GLA_DV, GLA_DK), F32)],
        scratch_shapes=[pltpu.VMEM((GLA_HEADS, GLA_DV, GLA_DK), F32), pltpu.VMEM((TB, hk), F32)],
        compiler_params=_params(("arbitrary",)),
        name="gla_rev" if reverse else "gla_fwd",
    )(*args)


MCH = 128
NMC = TB // MCH


def _mlstm_kernel(*refs, cfg, reverse):
    if reverse:
        (q_ref, k_ref, v_ref, kt_ref, sm_ref, gr_ref, bc_ref, br_ref, c0_ref, n0_ref, m0_ref, ob_ref, hf_ref, gn_ref,
         out_ref, co_ref, no_ref, mo_ref, c_ref, n_ref, m_ref) = refs
    else:
        (q_ref, k_ref, v_ref, kt_ref, sm_ref, gr_ref, bc_ref, br_ref, c0_ref, n0_ref, m0_ref,
         out_ref, co_ref, no_ref, mo_ref, c_ref, n_ref, m_ref) = refs
    is_ctx, _, j, k = _seq_pos(cfg, pl.program_id(0))

    @pl.when(jnp.logical_and(j == 0, is_ctx))
    def _():
        c_ref[...] = jnp.zeros_like(c_ref)
        n_ref[...] = jnp.zeros_like(n_ref)
        m_ref[...] = jnp.zeros_like(m_ref)

    @pl.when(jnp.logical_and(j == 0, jnp.logical_not(is_ctx)))
    def _():
        c_ref[...] = c0_ref[0]
        n_ref[...] = n0_ref[0]
        m_ref[...] = m0_ref[0]

    d = 1 if reverse else 0
    tri = _tri(reverse, MCH)
    tri_f = tri.astype(F32)
    tri_tf = _tri(not reverse, MCH).astype(F32)
    last = 0 if reverse else MCH - 1
    ng = 2 * MLSTM_HEADS
    kscale = MLSTM_DH ** -0.5
    for c in (range(NMC - 1, -1, -1) if reverse else range(NMC)):
        rows = slice(c * MCH, (c + 1) * MCH)
        g_col = sm_ref[rows, SM_GB:SM_GB + 2 * ng] + bc_ref[...]
        g_row = gr_ref[:, rows] + br_ref[...]
        fcum_col = jnp.dot(tri_f, _log_sigmoid(g_col), precision=lax.Precision.HIGHEST, preferred_element_type=F32)
        fcum_row = jnp.dot(_log_sigmoid(g_row), tri_tf, precision=lax.Precision.HIGHEST, preferred_element_type=F32)
        for h in range(MLSTM_HEADS):
            ii, fi = d * ng + h, d * ng + MLSTM_HEADS + h
            hc = slice(h * MLSTM_DH, (h + 1) * MLSTM_DH)
            f_col = fcum_col[:, fi:fi + 1]
            f_row = fcum_row[fi:fi + 1, :]
            i_col = g_col[:, ii:ii + 1]
            i_row = g_row[ii:ii + 1, :]
            m_prev = m_ref[h][:, 0:1]
            dlog = jnp.where(tri, f_col - f_row + i_row, -jnp.inf)
            prev = f_col + m_prev
            mj = jnp.maximum(prev, jnp.max(dlog, axis=-1, keepdims=True))
            w = jnp.exp(dlog - mj)
            wp = jnp.exp(prev - mj)
            qh = q_ref[rows, hc]
            kh = k_ref[rows, hc] * kscale
            vh = v_ref[rows, hc].astype(BF16)
            s = lax.dot_general(qh.astype(BF16), kh.astype(BF16), _NT, preferred_element_type=F32) * w
            qp = qh * wp
            cm = c_ref[h]
            nv = n_ref[h]
            num = (jnp.dot(s.astype(BF16), vh, preferred_element_type=F32)
                   + jnp.dot(qp.astype(BF16), cm.astype(BF16), preferred_element_type=F32))
            den = jnp.sum(s, axis=-1, keepdims=True) + jnp.sum(qp * nv, axis=-1, keepdims=True)
            hh = num / jnp.maximum(jnp.abs(den), jnp.exp(-mj))
            m_new = mj[last:last + 1, :]
            tot = f_col[last:last + 1, :]
            wl_col = jnp.exp(tot - f_col + i_col - m_new)
            wl_row = jnp.exp(tot - f_row + i_row - m_new)
            decay = jnp.exp(tot + m_prev - m_new)
            kw_t = (kt_ref[hc, rows] * kscale) * wl_row
            c_ref[h] = decay * cm + jnp.dot(kw_t.astype(BF16), vh, preferred_element_type=F32)
            n_ref[h] = decay * nv + jnp.sum(kh * wl_col, axis=0, keepdims=True)
            m_ref[h] = jnp.broadcast_to(m_new, (1, 128))
            if reverse:
                y = jax.nn.sigmoid(ob_ref[rows, hc]) * _head_norm(hf_ref[rows, hc] + hh, gn_ref[:, hc])
                out_ref[rows, hc] = y.astype(out_ref.dtype)
            else:
                out_ref[rows, hc] = hh

    @pl.when(jnp.logical_and(is_ctx, j == k - 1))
    def _():
        co_ref[0] = c_ref[...]
        no_ref[0] = n_ref[...]
        mo_ref[0] = m_ref[...]


def _mlstm(cfg, z, k_t, g_rows, b_col, b_row, c0, n0, m0, reverse, h_fwd=None, gnorm=None):
    m = cfg.m
    nblk = m // TB
    rb = functools.partial(_row_block, cfg, reverse)
    hd = MLSTM_HEADS * MLSTM_DH
    lat = lambda i: (_lat_seq(cfg, i), 0, 0, 0)
    ctx = lambda i: (_ctx_seq(cfg, i), 0, 0, 0)
    in_specs = [pl.BlockSpec((TB, hd), lambda i: (rb(i), Z_QB // hd)),
                pl.BlockSpec((TB, hd), lambda i: (rb(i), Z_KB // hd)),
                pl.BlockSpec((TB, hd), lambda i: (rb(i), Z_VB // hd)),
                pl.BlockSpec((hd, TB), lambda i: (0, rb(i))),
                pl.BlockSpec((TB, SM_W), lambda i: (rb(i), Z_SM // SM_W)),
                pl.BlockSpec((4 * MLSTM_HEADS, TB), lambda i: (0, rb(i))),
                pl.BlockSpec((1, 4 * MLSTM_HEADS), lambda i: (0, 0)),
                pl.BlockSpec((4 * MLSTM_HEADS, 1), lambda i: (0, 0)),
                pl.BlockSpec((1, MLSTM_HEADS, MLSTM_DH, MLSTM_DH), lat),
                pl.BlockSpec((1, MLSTM_HEADS, 1, MLSTM_DH), lat),
                pl.BlockSpec((1, MLSTM_HEADS, 1, 128), lat)]
    args = [z, z, z, k_t, z, g_rows, b_col, b_row, c0, n0, m0]
    if reverse:
        in_specs += [pl.BlockSpec((TB, hd), lambda i: (rb(i), Z_OB // hd)),
                     pl.BlockSpec((TB, hd), lambda i: (rb(i), 0)),
                     pl.BlockSpec((1, hd), lambda i: (0, 0))]
        args += [z, h_fwd, gnorm]
    return pl.pallas_call(
        functools.partial(_mlstm_kernel, cfg=cfg, reverse=reverse),
        grid=(nblk,),
        in_specs=in_specs,
        out_specs=[pl.BlockSpec((TB, hd), lambda i: (rb(i), 0)),
                   pl.BlockSpec((1, MLSTM_HEADS, MLSTM_DH, MLSTM_DH), ctx),
                   pl.BlockSpec((1, MLSTM_HEADS, 1, MLSTM_DH), ctx),
                   pl.BlockSpec((1, MLSTM_HEADS, 1, 128), ctx)],
        out_shape=[jax.ShapeDtypeStruct((m, hd), BF16 if reverse else F32),
                   jax.ShapeDtypeStruct((cfg.bc, MLSTM_HEADS, MLSTM_DH, MLSTM_DH), F32),
                   jax.ShapeDtypeStruct((cfg.bc, MLSTM_HEADS, 1, MLSTM_DH), F32),
                   jax.ShapeDtypeStruct((cfg.bc, MLSTM_HEADS, 1, 128), F32)],
        scratch_shapes=[pltpu.VMEM((MLSTM_HEADS, MLSTM_DH, MLSTM_DH), F32),
                        pltpu.VMEM((MLSTM_HEADS, 1, MLSTM_DH), F32),
                        pltpu.VMEM((MLSTM_HEADS, 1, 128), F32)],
        compiler_params=_params(("arbitrary",)),
        name="mlstm_rev" if reverse else "mlstm_fwd",
    )(*args)


LRU_ROWS = 256
LRU_PAD = 8


def _rglru_kernel(*refs, t, aliased):
    if aliased:
        refs = refs[1:]
    (xr_ref, yr_ref, cw_ref, cb_ref, gw_ref, gb_ref, lam_ref, h0_ref, y_ref, he_ref,
     pad_ref, a_ref, b_ref, hf_ref, hr_ref) = refs
    nsteps = t // LRU_ROWS
    zeros = jnp.zeros((LRU_PAD, LRU_BLOCK), F32)
    pad_ref[0:LRU_PAD, :] = zeros
    pad_ref[LRU_PAD + t:2 * LRU_PAD + t, :] = zeros

    def fill(ci, _):
        r0 = pl.multiple_of(ci * LRU_ROWS, LRU_ROWS)
        pad_ref[pl.ds(r0 + LRU_PAD, LRU_ROWS), :] = xr_ref[pl.ds(r0, LRU_ROWS), :]
        return 0

    lax.fori_loop(0, nsteps, fill, 0)

    sp = _softplus(-lam_ref[0])
    left = LRU_CONV // 2

    def gates(ci, _):
        r0 = pl.multiple_of(ci * LRU_ROWS, LRU_ROWS)
        xc = cb_ref[...]
        for tap in range(LRU_CONV):
            xc = xc + cw_ref[tap:tap + 1, :] * pad_ref[pl.ds(r0 + LRU_PAD - left + tap, LRU_ROWS), :]
        pre = jnp.dot(xc.astype(BF16), gw_ref[0], preferred_element_type=F32) + gb_ref[0]
        for d in range(N_DIR):
            o = d * 2 * LRU_BLOCK
            r = jax.nn.sigmoid(pre[:, o:o + LRU_BLOCK])
            ig = jax.nn.sigmoid(pre[:, o + LRU_BLOCK:o + 2 * LRU_BLOCK])
            log_a = (-LRU_C * r) * sp[:, d * LRU_BLOCK:(d + 1) * LRU_BLOCK]
            a = jnp.exp(log_a)
            a_ref[d, pl.ds(r0, LRU_ROWS), :] = a
            b_ref[d, pl.ds(r0, LRU_ROWS), :] = jnp.sqrt(1.0 - a * a) * (ig * xc)
        return 0

    lax.fori_loop(0, nsteps, gates, 0)

    ngrp = t // 8
    row = lax.broadcasted_iota(jnp.int32, (8, LRU_BLOCK), 0)

    def scan(g, carry):
        cf, cr = carry
        rf = pl.multiple_of(g * 8, 8)
        rr = pl.multiple_of((ngrp - 1 - g) * 8, 8)
        a = a_ref[0, pl.ds(rf, 8), :]
        b = b_ref[0, pl.ds(rf, 8), :]
        for s in (1, 2, 4):
            keep = row >= s
            b = a * jnp.where(keep, pltpu.roll(b, s, axis=0), 0.0) + b
            a = a * jnp.where(keep, pltpu.roll(a, s, axis=0), 1.0)
        hf = a * cf + b
        hf_ref[pl.ds(rf, 8), :] = hf
        cf = jnp.broadcast_to(hf[7:8, :], (8, LRU_BLOCK))
        a = a_ref[1, pl.ds(rr, 8), :]
        b = b_ref[1, pl.ds(rr, 8), :]
        for s in (1, 2, 4):
            keep = row < 8 - s
            b = a * jnp.where(keep, pltpu.roll(b, 8 - s, axis=0), 0.0) + b
            a = a * jnp.where(keep, pltpu.roll(a, 8 - s, axis=0), 1.0)
        hr = a * cr + b
        hr_ref[pl.ds(rr, 8), :] = hr
        cr = jnp.broadcast_to(hr[0:1, :], (8, LRU_BLOCK))
        return cf, cr

    cf0 = jnp.broadcast_to(h0_ref[0, 0:1, :], (8, LRU_BLOCK))
    cr0 = jnp.broadcast_to(h0_ref[0, 1:2, :], (8, LRU_BLOCK))
    cf, cr = lax.fori_loop(0, ngrp, scan, (cf0, cr0), unroll=2)
    he_ref[0, 0:1, :] = cf[0:1, :]
    he_ref[0, 1:2, :] = cr[0:1, :]

    def finish(ci, _):
        r0 = pl.multiple_of(ci * LRU_ROWS, LRU_ROWS)
        y = (hf_ref[pl.ds(r0, LRU_ROWS), :] + hr_ref[pl.ds(r0, LRU_ROWS), :]) * _gelu_tanh(yr_ref[pl.ds(r0, LRU_ROWS), :])
        y_ref[pl.ds(r0, LRU_ROWS), :] = y.astype(y_ref.dtype)
        return 0

    lax.fori_loop(0, nsteps, finish, 0)


def _rglru(cfg, z, cw, cb, gw, gb, lam, h0, nseq, t, row0, y_prev=None):
    m = cfg.m
    assert row0 % t == 0
    sb = row0 // t
    aliased = y_prev is not None
    in_specs = [pl.BlockSpec((t, LRU_BLOCK), lambda b, n: (sb + b, Z_XR // LRU_BLOCK + n)),
                pl.BlockSpec((t, LRU_BLOCK), lambda b, n: (sb + b, Z_YR // LRU_BLOCK + n)),
                pl.BlockSpec((LRU_CONV, LRU_BLOCK), lambda b, n: (0, n)),
                pl.BlockSpec((1, LRU_BLOCK), lambda b, n: (0, n)),
                pl.BlockSpec((1, LRU_BLOCK, 4 * LRU_BLOCK), lambda b, n: (n, 0, 0)),
                pl.BlockSpec((1, 1, 4 * LRU_BLOCK), lambda b, n: (n, 0, 0)),
                pl.BlockSpec((1, 1, 2 * LRU_BLOCK), lambda b, n: (n, 0, 0)),
                pl.BlockSpec((1, N_DIR, LRU_BLOCK), lambda b, n: (b, 0, n))]
    args = [z, z, cw, cb, gw, gb, lam, h0]
    if aliased:
        in_specs = [pl.BlockSpec(memory_space=pl.ANY)] + in_specs
        args = [y_prev] + args
    return pl.pallas_call(
        functools.partial(_rglru_kernel, t=t, aliased=aliased),
        grid=(nseq, LRU_BLOCKS),
        in_specs=in_specs,
        out_specs=[pl.BlockSpec((t, LRU_BLOCK), lambda b, n: (sb + b, n)),
                   pl.BlockSpec((1, N_DIR, LRU_BLOCK), lambda b, n: (b, 0, n))],
        out_shape=[jax.ShapeDtypeStruct((m, LRU_WIDTH), BF16),
                   jax.ShapeDtypeStruct((nseq, N_DIR, LRU_WIDTH), F32)],
        scratch_shapes=[pltpu.VMEM((t + 2 * LRU_PAD, LRU_BLOCK), F32),
                        pltpu.VMEM((N_DIR, t, LRU_BLOCK), F32),
                        pltpu.VMEM((N_DIR, t, LRU_BLOCK), F32),
                        pltpu.VMEM((t, LRU_BLOCK), F32),
                        pltpu.VMEM((t, LRU_BLOCK), F32)],
        input_output_aliases={0: 0} if aliased else {},
        compiler_params=_params(("parallel", "parallel")),
        name="rglru",
    )(*args)


def _merge_kernel(ya_ref, yb_ref, yc_ref, w_ref, ma_ref, mb_ref, mc_ref, bm_ref, o_ref):
    acc = None
    for n, (y_ref, mg_ref) in enumerate(((ya_ref, ma_ref), (yb_ref, mb_ref), (yc_ref, mc_ref))):
        g = jax.nn.sigmoid(mg_ref[...] + bm_ref[n])
        term = g * jnp.dot(y_ref[...], w_ref[n], preferred_element_type=F32)
        acc = term if acc is None else acc + term
    o_ref[...] = acc.astype(o_ref.dtype)


def _merge(cfg, ya, yb, yc, wbr, z, bm, tm, tn):
    m = cfg.m
    nj = D_MODEL // tn
    y_spec = pl.BlockSpec((tm, BRANCH_W), lambda j, i: (i, 0))
    mg_spec = lambda n: pl.BlockSpec((tm, tn), lambda j, i: (i, Z_MG // tn + n * nj + j))
    return pl.pallas_call(
        _merge_kernel,
        grid=(nj, m // tm),
        in_specs=[y_spec, y_spec, y_spec,
                  pl.BlockSpec((N_BRANCH, BRANCH_W, tn), lambda j, i: (0, 0, j)),
                  mg_spec(0), mg_spec(1), mg_spec(2),
                  pl.BlockSpec((N_BRANCH, 1, tn), lambda j, i: (0, 0, j))],
        out_specs=pl.BlockSpec((tm, tn), lambda j, i: (i, j)),
        out_shape=jax.ShapeDtypeStruct((m, D_MODEL), BF16),
        compiler_params=_params(("parallel", "parallel")),
        name="merge",
    )(ya, yb, yc, wbr, z, z, z, bm)


def _matmul_res_kernel(a_ref, w_ref, x_ref, g_ref, o_ref):
    o_ref[...] = x_ref[...] + g_ref[0] * jnp.dot(a_ref[...], w_ref[...], preferred_element_type=F32)


def _matmul_res(cfg, a, w, x, gate, tm, tn):
    m, kdim = a.shape
    n = w.shape[1]
    return pl.pallas_call(
        _matmul_res_kernel,
        grid=(n // tn, m // tm),
        in_specs=[pl.BlockSpec((tm, kdim), lambda j, i: (i, 0)),
                  pl.BlockSpec((kdim, tn), lambda j, i: (0, j)),
                  pl.BlockSpec((tm, tn), lambda j, i: (i, j)),
                  pl.BlockSpec((1, 1, tn), lambda j, i: (_mod_row(cfg, i * tm), 0, j))],
        out_specs=pl.BlockSpec((tm, tn), lambda j, i: (i, j)),
        out_shape=jax.ShapeDtypeStruct((m, n), F32),
        compiler_params=_params(("parallel", "parallel")),
        name="matmul_res",
    )(a, w, x, gate)


FFN_PAD = 72
FFN_STEP_ELEMS = 128 * 128


def _conv_act_kernel(*refs, t, taps, ct, aliased):
    if aliased:
        refs = refs[1:]
    hg_ref, hu_ref, w_ref, b_ref, o_ref, pad_ref = refs
    rows = FFN_STEP_ELEMS // ct
    nsteps = t // rows
    zeros = jnp.zeros((FFN_PAD, ct), F32)
    pad_ref[0:FFN_PAD, :] = zeros
    pad_ref[FFN_PAD + t:2 * FFN_PAD + t, :] = zeros

    def fill(ci, _):
        r0 = pl.multiple_of(ci * rows, rows)
        pad_ref[pl.ds(r0 + FFN_PAD, rows), :] = hg_ref[pl.ds(r0, rows), :]
        return 0

    lax.fori_loop(0, nsteps, fill, 0)
    col = lax.broadcasted_iota(jnp.int32, (rows, ct), 0) % GRID_W

    def step(ci, _):
        r0 = ci * rows if isinstance(ci, int) else pl.multiple_of(ci * rows, rows)
        acc = b_ref[...]
        for shift, widx, dx in taps:
            xs = pad_ref[pl.ds(r0 + FFN_PAD + shift, rows), :]
            if dx < 0:
                xs = jnp.where(col >= -dx, xs, 0.0)
            elif dx > 0:
                xs = jnp.where(col < GRID_W - dx, xs, 0.0)
            acc = acc + xs * w_ref[widx:widx + 1, :]
        o_ref[pl.ds(r0, rows), :] = (_silu(acc) * hu_ref[pl.ds(r0, rows), :]).astype(o_ref.dtype)
        return 0

    if ct > 128:
        for ci in range(nsteps):
            step(ci, 0)
    else:
        lax.fori_loop(0, nsteps, step, 0)


def _conv_act(cfg, h, w9, bias, nseq, t, row0, on_grid, ct, prev=None):
    m = cfg.m
    rows = FFN_STEP_ELEMS // ct
    assert row0 % t == 0 and t % rows == 0 and D_FF % ct == 0 and rows % 16 == 0
    assert not on_grid or rows % GRID_W == 0
    sb = row0 // t
    p = FFN_CONV // 2
    if on_grid:
        taps = tuple(((dy - p) * GRID_W + (dx - p), dy * FFN_CONV + dx, dx - p)
                     for dy in range(FFN_CONV) for dx in range(FFN_CONV))
    else:
        taps = tuple((dx - p, p * FFN_CONV + dx, 0) for dx in range(FFN_CONV))
    aliased = prev is not None
    nct = D_FF // ct
    in_specs = [pl.BlockSpec((t, ct), lambda b, j: (sb + b, j)),
                pl.BlockSpec((t, ct), lambda b, j: (sb + b, nct + j)),
                pl.BlockSpec((FFN_CONV * FFN_CONV, ct), lambda b, j: (0, j)),
                pl.BlockSpec((1, ct), lambda b, j: (0, j))]
    args = [h, h, w9, bias]
    if aliased:
        in_specs = [pl.BlockSpec(memory_space=pl.ANY)] + in_specs
        args = [prev] + args
    return pl.pallas_call(
        functools.partial(_conv_act_kernel, t=t, taps=taps, ct=ct, aliased=aliased),
        grid=(nseq, nct),
        in_specs=in_specs,
        out_specs=pl.BlockSpec((t, ct), lambda b, j: (sb + b, j)),
        out_shape=jax.ShapeDtypeStruct((m, D_FF), BF16),
        scratch_shapes=[pltpu.VMEM((t + 2 * FFN_PAD, ct), F32)],
        input_output_aliases={0: 0} if aliased else {},
        compiler_params=_params(("parallel", "parallel")),
        name="conv_act",
    )(*args)


def _rmsnorm_kernel(x_ref, g_ref, o_ref):
    x = x_ref[...]
    o_ref[...] = x * lax.rsqrt(jnp.mean(x * x, axis=-1, keepdims=True) + EPS) * g_ref[...]


def _final_norm(x, gain, row0, rows, tm):
    d = x.shape[1]
    assert row0 % tm == 0 and rows % tm == 0
    return pl.pallas_call(
        _rmsnorm_kernel,
        grid=(rows // tm,),
        in_specs=[pl.BlockSpec((tm, d), lambda i: (row0 // tm + i, 0)),
                  pl.BlockSpec((1, d), lambda i: (0, 0))],
        out_specs=pl.BlockSpec((tm, d), lambda i: (i, 0)),
        out_shape=jax.ShapeDtypeStruct((rows, d), F32),
        compiler_params=_params(("parallel",)),
        name="final_norm",
    )(x, gain)


def _pack_w_in(w):
    s = _SRC
    cols = [w[:, s["mg"]:s["end"]], w[:, s["va"]:s["ga"]], w[:, s["ga"]:s["ra"]], w[:, s["qb"]:s["kb"]],
            w[:, s["kb"]:s["vb"]], w[:, s["vb"]:s["ob"]], w[:, s["ob"]:s["gb"]], w[:, s["xr"]:s["yr"]],
            w[:, s["yr"]:s["mg"]], w[:, s["qa"]:s["ka"]], w[:, s["ka"]:s["va"]], w[:, s["ra"]:s["qb"]],
            w[:, s["gb"]:s["xr"]]]
    used = sum(c.shape[1] for c in cols)
    cols.append(jnp.zeros((w.shape[0], NZ - used), w.dtype))
    return jnp.concatenate(cols, axis=1).astype(BF16)


def _pack_gate_w(gw, gb, lam):
    nb, blk = LRU_BLOCKS, LRU_BLOCK
    gw_p = jnp.transpose(gw, (2, 3, 0, 1, 4)).reshape(nb, blk, 4 * blk).astype(BF16)
    gb_p = jnp.transpose(gb.reshape(2, 2, nb, blk), (2, 0, 1, 3)).reshape(nb, 1, 4 * blk)
    lam_p = jnp.transpose(lam.reshape(2, nb, blk), (1, 0, 2)).reshape(nb, 1, 2 * blk)
    return gw_p, gb_p, lam_p


def _layer(cfg, x, mod, lp, states):
    (n1, n2, w_in, gwa, gba, gng, mbif, mng, lcw, lcb, lgw, lgb, llam, wbr, bmg, wout, fup, fcw, fcb, fdown) = lp
    s_gla, s_mc, s_mn, s_mm, s_lru = states
    m = cfg.m
    sh1, sc1, g1, sh2, sc2, g2 = (mod[:, k * D_MODEL:(k + 1) * D_MODEL].reshape(MOD_ROWS, 1, D_MODEL) for k in range(6))

    proj_tm = PROJ_TM if (cfg.m_ctx % PROJ_TM == 0 and cfg.tl % PROJ_TM == 0) else 512
    z = _norm_matmul(cfg, x, n1.reshape(1, -1), sc1, sh1, _pack_w_in(w_in), proj_tm, 512)

    s0t = jnp.swapaxes(s_gla, -1, -2)
    o_f, sa_f = _gla(cfg, z, gwa[0], gba[0].reshape(1, -1), s0t[:, 0], False)
    y_a, sa_r = _gla(cfg, z, gwa[1], gba[1].reshape(1, -1), s0t[:, 1], True, o_f, gng.reshape(1, -1))
    new_gla = jnp.swapaxes(jnp.stack([sa_f, sa_r], axis=1), -1, -2)

    g_rows = z[:, Z_SM + SM_GB:Z_SM + SM_GB + 4 * MLSTM_HEADS].T
    k_t = z[:, Z_KB:Z_KB + MLSTM_HEADS * MLSTM_DH].T
    b_col = mbif.reshape(1, -1)
    b_row = mbif.reshape(-1, 1)
    n0 = s_mn[:, :, :, None, :]
    m0 = jnp.broadcast_to(s_mm[:, :, :, None, None], s_mm.shape + (1, 128))
    h_f, cb_f, nb_f, mb_f = _mlstm(cfg, z, k_t, g_rows, b_col, b_row, s_mc[:, 0], n0[:, 0], m0[:, 0], False)
    y_b, cb_r, nb_r, mb_r = _mlstm(cfg, z, k_t, g_rows, b_col, b_row, s_mc[:, 1], n0[:, 1], m0[:, 1], True,
                                   h_f, mng.reshape(1, -1))
    new_mc = jnp.stack([cb_f, cb_r], axis=1)
    new_mn = jnp.stack([nb_f[:, :, 0], nb_r[:, :, 0]], axis=1)
    new_mm = jnp.stack([mb_f[:, :, 0, 0], mb_r[:, :, 0, 0]], axis=1)

    gw_p, gb_p, lam_p = _pack_gate_w(lgw, lgb, llam)
    cbias = lcb.reshape(1, -1)
    y_c, new_lru = _rglru(cfg, z, lcw, cbias, gw_p, gb_p, lam_p, jnp.zeros((cfg.bc, N_DIR, LRU_WIDTH), F32),
                          cfg.bc, cfg.tc, 0)
    y_c, _ = _rglru(cfg, z, lcw, cbias, gw_p, gb_p, lam_p, s_lru, cfg.bl, cfg.tl, cfg.m_ctx, y_prev=y_c)

    merged = _merge(cfg, y_a, y_b, y_c, wbr.astype(BF16), z, bmg.reshape(N_BRANCH, 1, D_MODEL), 512, 1024)
    x = _matmul_res(cfg, merged, wout.astype(BF16), x, g1, 512, 1024)

    h = _norm_matmul(cfg, x, n2.reshape(1, -1), sc2, sh2, fup.astype(BF16), proj_tm, 512)
    w9 = fcw.reshape(FFN_CONV * FFN_CONV, D_FF)
    fbias = fcb.reshape(1, -1)
    act = _conv_act(cfg, h, w9, fbias, cfg.bc, cfg.tc, 0, False, 512)
    act = _conv_act(cfg, h, w9, fbias, cfg.bl, cfg.tl, cfg.m_ctx, True, 128, prev=act)
    x = _matmul_res(cfg, act, fdown.astype(BF16), x, g2, 512, 512)
    return x, (new_gla, new_mc, new_mn, new_mm, new_lru)


def kernel(x_prompt, x_sample, state_gla, state_mlstm_c, state_mlstm_n, state_mlstm_m, state_rglru, c, c_ctx, norm1_g, norm2_g, w_mod, b_mod, w_in, gla_w_alpha, gla_b_alpha, gla_norm_g, mlstm_b_if, mlstm_norm_g, lru_conv_w, lru_conv_b, lru_gate_w, lru_gate_b, lru_lambda, w_branch, b_merge, w_out, ffn_w_up, ffn_conv_w, ffn_conv_b, ffn_w_down, norm_f_g):
    bc, tc, d = x_prompt.shape
    bl, tl, _ = x_sample.shape
    cfg = Cfg(bc, tc, bl, tl)
    assert tc % TB == 0 and tl % TB == 0 and cfg.m_ctx % tl == 0 and 1 + bl <= MOD_ROWS
    depth = w_in.shape[0]

    x = jnp.concatenate([x_prompt.reshape(bc * tc, d), x_sample.reshape(bl * tl, d)], axis=0)
    c_all = jnp.concatenate([c_ctx[None, :], c, jnp.zeros((MOD_ROWS - 1 - bl, d), F32)], axis=0)
    mod = _modulation(c_all, w_mod, b_mod)

    new = []
    for l in range(depth):
        lp = (norm1_g[l], norm2_g[l], w_in[l], gla_w_alpha[l], gla_b_alpha[l], gla_norm_g[l], mlstm_b_if[l],
              mlstm_norm_g[l], lru_conv_w[l], lru_conv_b[l], lru_gate_w[l], lru_gate_b[l], lru_lambda[l],
              w_branch[l], b_merge[l], w_out[l], ffn_w_up[l], ffn_conv_w[l], ffn_conv_b[l], ffn_w_down[l])
        states = (state_gla[:, l], state_mlstm_c[:, l], state_mlstm_n[:, l], state_mlstm_m[:, l], state_rglru[:, l])
        x, st = _layer(cfg, x, mod[l], lp, states)
        new.append(st)

    gain = norm_f_g.reshape(1, -1)
    y_prompt = _final_norm(x, gain, 0, cfg.m_ctx, 512).reshape(bc, tc, d)
    y_sample = _final_norm(x, gain, cfg.m_ctx, bl * tl, 512).reshape(bl, tl, d)
    stacked = tuple(jnp.stack([new[l][k] for l in range(depth)], axis=1) for k in range(5))
    return (y_prompt, y_sample) + stacked
```

```python
import functools
from typing import NamedTuple

import numpy as np
import jax
import jax.numpy as jnp
from jax import lax
from jax.experimental import pallas as pl
from jax.experimental.pallas import tpu as pltpu

F32 = jnp.float32
BF16 = jnp.bfloat16

D_MODEL = 2048
DEPTH = 2
GRID_W = 64
N_DIR = 2
N_BRANCH = 3
BRANCH_W = 1024
GLA_HEADS = 4
GLA_DK = 128
GLA_DV = 256
GLA_RANK = 16
GLA_TAU = 16.0
MLSTM_HEADS = 4
MLSTM_DH = 256
CHUNK = 64
LRU_WIDTH = 1024
LRU_BLOCKS = 8
LRU_BLOCK = 128
LRU_CONV = 4
LRU_C = 8.0
D_FF = 5632
FFN_CONV = 3
EPS = 1e-6

_SRC = dict(qa=0, ka=512, va=1024, ga=2048, ra=3072, qb=3104, kb=4128, vb=5152, ob=6176, gb=7200,
            xr=7216, yr=8240, mg=9264, end=15408)
Z_MG, Z_VA, Z_GA, Z_QB, Z_KB, Z_VB, Z_OB, Z_XR, Z_YR, Z_QA, Z_KA, Z_SM = (
    0, 6144, 7168, 8192, 9216, 10240, 11264, 12288, 13312, 14336, 14848, 15360)
SM_W = 128
SM_RA, SM_GB = 0, 32
NZ = 15872
MOD_ROWS = 16

PROJ_TM = 1024
NORM_SLAB = 32
PROJ_SLAB = 64
TB = 256
NCH = TB // CHUNK
VMEM_LIMIT = 48 * 1024 * 1024


class Cfg(NamedTuple):
    bc: int
    tc: int
    bl: int
    tl: int

    @property
    def m_ctx(self):
        return self.bc * self.tc

    @property
    def m(self):
        return self.bc * self.tc + self.bl * self.tl


def _params(sem):
    return pltpu.CompilerParams(dimension_semantics=sem, vmem_limit_bytes=VMEM_LIMIT)


def _softplus(x):
    return jnp.maximum(x, 0.0) + jnp.log1p(jnp.exp(-jnp.abs(x)))


def _log_sigmoid(x):
    return -_softplus(-x)


def _silu(x):
    return x * jax.nn.sigmoid(x)


def _gelu_tanh(x):
    return 0.5 * x * (1.0 + jnp.tanh(np.sqrt(2.0 / np.pi).astype(np.float32) * (x + 0.044715 * (x * x * x))))


def _mod_row(cfg, row):
    return jnp.where(row < cfg.m_ctx, 0, 1 + jnp.maximum(row - cfg.m_ctx, 0) // cfg.tl)


def _mod_kernel(c_ref, w_ref, b_ref, o_ref):
    c = c_ref[...]
    a = _silu(c).astype(BF16)
    o_ref[0] = jnp.dot(a, w_ref[0].astype(BF16), preferred_element_type=F32) + b_ref[0]


def _modulation(c_all, w_mod, b_mod):
    depth, d, n = w_mod.shape
    tn = 512
    return pl.pallas_call(
        _mod_kernel,
        grid=(depth, n // tn),
        in_specs=[pl.BlockSpec((MOD_ROWS, d), lambda l, j: (0, 0)),
                  pl.BlockSpec((1, d, tn), lambda l, j: (l, 0, j)),
                  pl.BlockSpec((1, 1, tn), lambda l, j: (l, 0, j))],
        out_specs=pl.BlockSpec((1, MOD_ROWS, tn), lambda l, j: (l, 0, j)),
        out_shape=jax.ShapeDtypeStruct((depth, MOD_ROWS, n), F32),
        compiler_params=_params(("parallel", "parallel")),
        name="modulation",
    )(c_all, w_mod, b_mod.reshape(depth, 1, n))


def _norm_rows(x_ref, g_ref, sc_ref, sh_ref, u_ref, r0, nrows):
    x = x_ref[pl.ds(r0, nrows), :]
    y = x * lax.rsqrt(jnp.mean(x * x, axis=-1, keepdims=True) + EPS) * g_ref[...]
    u_ref[pl.ds(r0, nrows), :] = (y * (1.0 + sc_ref[0]) + sh_ref[0]).astype(BF16)


def _norm_matmul_kernel(x0_ref, xn_ref, g_ref, sc0_ref, sh0_ref, scn_ref, shn_ref, w_ref, o_ref, ua_ref, ub_ref, *, slab):
    i, j = pl.program_id(0), pl.program_id(1)
    tm = xn_ref.shape[0]

    @pl.when(jnp.logical_and(i == 0, j == 0))
    def _():
        def first(si, _):
            _norm_rows(x0_ref, g_ref, sc0_ref, sh0_ref, ua_ref, pl.multiple_of(si * NORM_SLAB, NORM_SLAB), NORM_SLAB)
            return 0

        lax.fori_loop(0, tm // NORM_SLAB, first, 0)

    r0 = pl.multiple_of(jnp.minimum(j, tm // slab - 1) * slab, slab)

    @pl.when(i % 2 == 0)
    def _():
        _norm_rows(xn_ref, g_ref, scn_ref, shn_ref, ub_ref, r0, slab)
        o_ref[...] = jnp.dot(ua_ref[...], w_ref[...], preferred_element_type=F32)

    @pl.when(i % 2 == 1)
    def _():
        _norm_rows(xn_ref, g_ref, scn_ref, shn_ref, ua_ref, r0, slab)
        o_ref[...] = jnp.dot(ub_ref[...], w_ref[...], preferred_element_type=F32)


def _norm_matmul(cfg, x, gain, sc, sh, w, tm, tn):
    m, d = x.shape
    n = w.shape[1]
    slab = PROJ_SLAB
    assert tm % slab == 0 and n // tn >= tm // slab
    nxt = lambda i: jnp.minimum(i + 1, m // tm - 1)
    mod0 = lambda i, j: (_mod_row(cfg, 0), 0, 0)
    modn = lambda i, j: (_mod_row(cfg, nxt(i) * tm), 0, 0)
    return pl.pallas_call(
        functools.partial(_norm_matmul_kernel, slab=slab),
        grid=(m // tm, n // tn),
        in_specs=[pl.BlockSpec((tm, d), lambda i, j: (0, 0), pipeline_mode=pl.Buffered(1)),
                  pl.BlockSpec((tm, d), lambda i, j: (nxt(i), 0)),
                  pl.BlockSpec((1, d), lambda i, j: (0, 0)),
                  pl.BlockSpec((1, 1, d), mod0),
                  pl.BlockSpec((1, 1, d), mod0),
                  pl.BlockSpec((1, 1, d), modn),
                  pl.BlockSpec((1, 1, d), modn),
                  pl.BlockSpec((d, tn), lambda i, j: (0, j))],
        out_specs=pl.BlockSpec((tm, tn), lambda i, j: (i, j)),
        out_shape=jax.ShapeDtypeStruct((m, n), F32),
        scratch_shapes=[pltpu.VMEM((tm, d), BF16), pltpu.VMEM((tm, d), BF16)],
        compiler_params=_params(("arbitrary", "arbitrary")),
        name="norm_matmul",
    )(x, x, gain, sc, sh, sc, sh, w)


def _seq_pos(cfg, i):
    kc, kl = cfg.tc // TB, cfg.tl // TB
    nc = cfg.bc * kc
    is_ctx = i < nc
    il = jnp.maximum(i - nc, 0)
    j = jnp.where(is_ctx, i % kc, il % kl)
    k = jnp.where(is_ctx, kc, kl)
    s = jnp.where(is_ctx, i // kc, il // kl)
    return is_ctx, s, j, k


def _row_block(cfg, reverse, i):
    kc, kl = cfg.tc // TB, cfg.tl // TB
    is_ctx, s, j, k = _seq_pos(cfg, i)
    jj = (k - 1 - j) if reverse else j
    return jnp.where(is_ctx, s * kc + jj, cfg.bc * kc + s * kl + jj)


def _lat_seq(cfg, i):
    is_ctx, s, _, _ = _seq_pos(cfg, i)
    return jnp.where(is_ctx, 0, jnp.minimum(s, cfg.bl - 1))


def _ctx_seq(cfg, i):
    is_ctx, s, _, _ = _seq_pos(cfg, i)
    return jnp.where(is_ctx, s, cfg.bc - 1)


def _tri(reverse, n=CHUNK):
    row = lax.broadcasted_iota(jnp.int32, (n, n), 0)
    col = lax.broadcasted_iota(jnp.int32, (n, n), 1)
    return (row <= col) if reverse else (row >= col)


def _head_norm(x, g):
    return x * lax.rsqrt(jnp.mean(x * x, axis=-1, keepdims=True) + EPS) * g


_NT = (((1,), (1,)), ((), ()))
_TN = (((0,), (0,)), ((), ()))


def _gla_kernel(*refs, cfg, reverse):
    if reverse:
        (q_ref, k_ref, v_ref, sm_ref, wal_ref, bal_ref, s0_ref, ga_ref, of_ref, gn_ref,
         out_ref, so_ref, st_ref, la_ref) = refs
    else:
        (q_ref, k_ref, v_ref, sm_ref, wal_ref, bal_ref, s0_ref, out_ref, so_ref, st_ref, la_ref) = refs
    is_ctx, _, j, k = _seq_pos(cfg, pl.program_id(0))

    @pl.when(jnp.logical_and(j == 0, is_ctx))
    def _():
        st_ref[...] = jnp.zeros_like(st_ref)

    @pl.when(jnp.logical_and(j == 0, jnp.logical_not(is_ctx)))
    def _():
        st_ref[...] = s0_ref[0]

    d = 1 if reverse else 0
    ra =sm_ref[:, SM_RA + d * GLA_RANK:SM_RA + (d + 1) * GLA_RANK].astype(BF16)
    pre = jnp.dot(ra, wal_ref[...].astype(BF16), preferred_element_type=F32) + bal_ref[...]
    la_ref[...] = _log_sigmoid(pre) * (1.0 / GLA_TAU)

    tri = _tri(reverse)
    tri_f = tri.astype(F32)
    last = 0 if reverse else CHUNK - 1
    for c in (range(NCH - 1, -1, -1) if reverse else range(NCH)):
        rows = slice(c * CHUNK, (c + 1) * CHUNK)
        cum = jnp.dot(tri_f, la_ref[rows, :], precision=lax.Precision.HIGHEST, preferred_element_type=F32)
        tot = cum[last:last + 1, :]
        e_q = jnp.exp(cum)
        e_k = jnp.exp(-cum)
        e_end = jnp.exp(tot - cum)
        e_tot = jnp.exp(tot)
        for h in range(GLA_HEADS):
            kc = slice(h * GLA_DK, (h + 1) * GLA_DK)
            vc = slice(h * GLA_DV, (h + 1) * GLA_DV)
            qh = q_ref[rows, kc] * (GLA_DK ** -0.5)
            kh = k_ref[rows, kc]
            vh = v_ref[rows, vc].astype(BF16)
            q_in = (qh * e_q[:, kc]).astype(BF16)
            k_in = (kh * e_k[:, kc]).astype(BF16)
            att = lax.dot_general(q_in, k_in, _NT, preferred_element_type=F32)
            att = jnp.where(tri, att, 0.0).astype(BF16)
            st = st_ref[h]
            o = (jnp.dot(att, vh, preferred_element_type=F32)
                 + lax.dot_general(q_in, st.astype(BF16), _NT, preferred_element_type=F32))
            k_end = (kh * e_end[:, kc]).astype(BF16)
            st_ref[h] = e_tot[:, kc] * st + lax.dot_general(vh, k_end, _TN, preferred_element_type=F32)
            if reverse:
                y = _head_norm(of_ref[rows, vc] + o, gn_ref[:, vc]) * _silu(ga_ref[rows, vc])
                out_ref[rows, vc] = y.astype(out_ref.dtype)
            else:
                out_ref[rows, vc] = o

    @pl.when(jnp.logical_and(is_ctx, j == k - 1))
    def _():
        so_ref[0] = st_ref[...]


def _gla(cfg, z, wal, bal, s0t, reverse, o_fwd=None, gnorm=None):
    m = cfg.m
    nblk = m // TB
    rb = functools.partial(_row_block, cfg, reverse)
    hk, hv = GLA_HEADS * GLA_DK, GLA_HEADS * GLA_DV
    in_specs = [pl.BlockSpec((TB, hk), lambda i: (rb(i), Z_QA // hk)),
                pl.BlockSpec((TB, hk), lambda i: (rb(i), Z_KA // hk)),
                pl.BlockSpec((TB, hv), lambda i: (rb(i), Z_VA // hv)),
                pl.BlockSpec((TB, SM_W), lambda i: (rb(i), Z_SM // SM_W)),
                pl.BlockSpec((GLA_RANK, hk), lambda i: (0, 0)),
                pl.BlockSpec((1, hk), lambda i: (0, 0)),
                pl.BlockSpec((1, GLA_HEADS, GLA_DV, GLA_DK), lambda i: (_lat_seq(cfg, i), 0, 0, 0))]
    args = [z, z, z, z, wal, bal, s0t]
    if reverse:
        in_specs += [pl.BlockSpec((TB, hv), lambda i: (rb(i), Z_GA // hv)),
                     pl.BlockSpec((TB, hv), lambda i: (rb(i), 0)),
                     pl.BlockSpec((1, hv), lambda i: (0, 0))]
        args += [z, o_fwd, gnorm]
    return pl.pallas_call(
        functools.partial(_gla_kernel, cfg=cfg, reverse=reverse),
        grid=(nblk,),
        in_specs=in_specs,
        out_specs=[pl.BlockSpec((TB, hv), lambda i: (rb(i), 0)),
                   pl.BlockSpec((1, GLA_HEADS, GLA_DV, GLA_DK), lambda i: (_ctx_seq(cfg, i), 0, 0, 0))],
        out_shape=[jax.ShapeDtypeStruct((m, hv), BF16 if reverse else F32),
                   jax.ShapeDtypeStruct((cfg.bc, GLA_HEADS, GLA_DV, GLA_DK), F32)],
        scratch_shapes=[pltpu.VMEM((GLA_HEADS, GLA_DV, GLA_DK), F32), pltpu.VMEM((TB, hk), F32)],
        compiler_params=_params(("arbitrary",)),
        name="gla_rev" if reverse else "gla_fwd",
    )(*args)


MCH = 128
NMC = TB // MCH


def _mlstm_kernel(*refs, cfg, reverse):
    if reverse:
        (q_ref, k_ref, v_ref, kt_ref, sm_ref, gr_ref, bc_ref, br_ref, c0_ref, n0_ref, m0_ref, ob_ref, hf_ref, gn_ref,
         out_ref, co_ref, no_ref, mo_ref, c_ref, n_ref, m_ref) = refs
    else:
        (q_ref, k_ref, v_ref, kt_ref, sm_ref, gr_ref, bc_ref, br_ref, c0_ref, n0_ref, m0_ref,
         out_ref, co_ref, no_ref, mo_ref, c_ref, n_ref, m_ref) = refs
    is_ctx, _, j, k = _seq_pos(cfg, pl.program_id(0))

    @pl.when(jnp.logical_and(j == 0, is_ctx))
    def _():
        c_ref[...] = jnp.zeros_like(c_ref)
        n_ref[...] = jnp.zeros_like(n_ref)
        m_ref[...] = jnp.zeros_like(m_ref)

    @pl.when(jnp.logical_and(j == 0, jnp.logical_not(is_ctx)))
    def _():
        c_ref[...] = c0_ref[0]
        n_ref[...] = n0_ref[0]
        m_ref[...] = m0_ref[0]

    d = 1 if reverse else 0
    tri = _tri(reverse, MCH)
    tri_f = tri.astype(F32)
    tri_tf = _tri(not reverse, MCH).astype(F32)
    last = 0 if reverse else MCH - 1
    ng = 2 * MLSTM_HEADS
    kscale = MLSTM_DH ** -0.5
    for c in (range(NMC - 1, -1, -1) if reverse else range(NMC)):
        rows = slice(c * MCH, (c + 1) * MCH)
        g_col = sm_ref[rows, SM_GB:SM_GB + 2 * ng] + bc_ref[...]
        g_row = gr_ref[:, rows] + br_ref[...]
        fcum_col = jnp.dot(tri_f, _log_sigmoid(g_col), precision=lax.Precision.HIGHEST, preferred_element_type=F32)
        fcum_row = jnp.dot(_log_sigmoid(g_row), tri_tf, precision=lax.Precision.HIGHEST, preferred_element_type=F32)
        for h in range(MLSTM_HEADS):
            ii, fi = d * ng + h, d * ng + MLSTM_HEADS + h
            hc = slice(h * MLSTM_DH, (h + 1) * MLSTM_DH)
            f_col = fcum_col[:, fi:fi + 1]
            f_row = fcum_row[fi:fi + 1, :]
            i_col = g_col[:, ii:ii + 1]
            i_row = g_row[ii:ii + 1, :]
            m_prev = m_ref[h][:, 0:1]
            dlog = jnp.where(tri, f_col - f_row + i_row, -jnp.inf)
            prev = f_col + m_prev
            mj = jnp.maximum(prev, jnp.max(dlog, axis=-1, keepdims=True))
            w = jnp.exp(dlog - mj)
            wp = jnp.exp(prev - mj)
            qh = q_ref[rows, hc]
            kh = k_ref[rows, hc] * kscale
            vh = v_ref[rows, hc].astype(BF16)
            s = lax.dot_general(qh.astype(BF16), kh.astype(BF16), _NT, preferred_element_type=F32) * w
            qp = qh * wp
            cm = c_ref[h]
            nv = n_ref[h]
            num = (jnp.dot(s.astype(BF16), vh, preferred_element_type=F32)
                   + jnp.dot(qp.astype(BF16), cm.astype(BF16), preferred_element_type=F32))
            den = jnp.sum(s, axis=-1, keepdims=True) + jnp.sum(qp * nv, axis=-1, keepdims=True)
            hh = num / jnp.maximum(jnp.abs(den), jnp.exp(-mj))
            m_new = mj[last:last + 1, :]
            tot = f_col[last:last + 1, :]
            wl_col = jnp.exp(tot - f_col + i_col - m_new)
            wl_row = jnp.exp(tot - f_row + i_row - m_new)
            decay = jnp.exp(tot + m_prev - m_new)
            kw_t = (kt_ref[hc, rows] * kscale) * wl_row
            c_ref[h] = decay * cm + jnp.dot(kw_t.astype(BF16), vh, preferred_element_type=F32)
            n_ref[h] = decay * nv + jnp.sum(kh * wl_col, axis=0, keepdims=True)
            m_ref[h] = jnp.broadcast_to(m_new, (1, 128))
            if reverse:
                y = jax.nn.sigmoid(ob_ref[rows, hc]) * _head_norm(hf_ref[rows, hc] + hh, gn_ref[:, hc])
                out_ref[rows, hc] = y.astype(out_ref.dtype)
            else:
                out_ref[rows, hc] = hh

    @pl.when(jnp.logical_and(is_ctx, j == k - 1))
    def _():
        co_ref[0] = c_ref[...]
        no_ref[0] = n_ref[...]
        mo_ref[0] = m_ref[...]


def _mlstm(cfg, z, k_t, g_rows, b_col, b_row, c0, n0, m0, reverse, h_fwd=None, gnorm=None):
    m = cfg.m
    nblk = m // TB
    rb = functools.partial(_row_block, cfg, reverse)
    hd = MLSTM_HEADS * MLSTM_DH
    lat = lambda i: (_lat_seq(cfg, i), 0, 0, 0)
    ctx = lambda i: (_ctx_seq(cfg, i), 0, 0, 0)
    in_specs = [pl.BlockSpec((TB, hd), lambda i: (rb(i), Z_QB // hd)),
                pl.BlockSpec((TB, hd), lambda i: (rb(i), Z_KB // hd)),
                pl.BlockSpec((TB, hd), lambda i: (rb(i), Z_VB // hd)),
                pl.BlockSpec((hd, TB), lambda i: (0, rb(i))),
                pl.BlockSpec((TB, SM_W), lambda i: (rb(i), Z_SM // SM_W)),
                pl.BlockSpec((4 * MLSTM_HEADS, TB), lambda i: (0, rb(i))),
                pl.BlockSpec((1, 4 * MLSTM_HEADS), lambda i: (0, 0)),
                pl.BlockSpec((4 * MLSTM_HEADS, 1), lambda i: (0, 0)),
                pl.BlockSpec((1, MLSTM_HEADS, MLSTM_DH, MLSTM_DH), lat),
                pl.BlockSpec((1, MLSTM_HEADS, 1, MLSTM_DH), lat),
                pl.BlockSpec((1, MLSTM_HEADS, 1, 128), lat)]
    args = [z, z, z, k_t, z, g_rows, b_col, b_row, c0, n0, m0]
    if reverse:
        in_specs += [pl.BlockSpec((TB, hd), lambda i: (rb(i), Z_OB // hd)),
                     pl.BlockSpec((TB, hd), lambda i: (rb(i), 0)),
                     pl.BlockSpec((1, hd), lambda i: (0, 0))]
        args += [z, h_fwd, gnorm]
    return pl.pallas_call(
        functools.partial(_mlstm_kernel, cfg=cfg, reverse=reverse),
        grid=(nblk,),
        in_specs=in_specs,
        out_specs=[pl.BlockSpec((TB, hd), lambda i: (rb(i), 0)),
                   pl.BlockSpec((1, MLSTM_HEADS, MLSTM_DH, MLSTM_DH), ctx),
                   pl.BlockSpec((1, MLSTM_HEADS, 1, MLSTM_DH), ctx),
                   pl.BlockSpec((1, MLSTM_HEADS, 1, 128), ctx)],
        out_shape=[jax.ShapeDtypeStruct((m, hd), BF16 if reverse else F32),
                   jax.ShapeDtypeStruct((cfg.bc, MLSTM_HEADS, MLSTM_DH, MLSTM_DH), F32),
                   jax.ShapeDtypeStruct((cfg.bc, MLSTM_HEADS, 1, MLSTM_DH), F32),
                   jax.ShapeDtypeStruct((cfg.bc, MLSTM_HEADS, 1, 128), F32)],
        scratch_shapes=[pltpu.VMEM((MLSTM_HEADS, MLSTM_DH, MLSTM_DH), F32),
                        pltpu.VMEM((MLSTM_HEADS, 1, MLSTM_DH), F32),
                        pltpu.VMEM((MLSTM_HEADS, 1, 128), F32)],
        compiler_params=_params(("arbitrary",)),
        name="mlstm_rev" if reverse else "mlstm_fwd",
    )(*args)


LRU_ROWS = 256
LRU_PAD = 8


def _rglru_kernel(*refs, t, aliased):
    if aliased:
        refs = refs[1:]
    (xr_ref, yr_ref, cw_ref, cb_ref, gw_ref, gb_ref, lam_ref, h0_ref, y_ref, he_ref,
     pad_ref, a_ref, b_ref, hf_ref, hr_ref) = refs
    nsteps = t // LRU_ROWS
    zeros = jnp.zeros((LRU_PAD, LRU_BLOCK), F32)
    pad_ref[0:LRU_PAD, :] = zeros
    pad_ref[LRU_PAD + t:2 * LRU_PAD + t, :] = zeros

    def fill(ci, _):
        r0 = pl.multiple_of(ci * LRU_ROWS, LRU_ROWS)
        pad_ref[pl.ds(r0 + LRU_PAD, LRU_ROWS), :] = xr_ref[pl.ds(r0, LRU_ROWS), :]
        return 0

    lax.fori_loop(0, nsteps, fill, 0)

    sp = _softplus(-lam_ref[0])
    left = LRU_CONV // 2

    def gates(ci, _):
        r0 = pl.multiple_of(ci * LRU_ROWS, LRU_ROWS)
        xc = cb_ref[...]
        for tap in range(LRU_CONV):
            xc = xc + cw_ref[tap:tap + 1, :] * pad_ref[pl.ds(r0 + LRU_PAD - left + tap, LRU_ROWS), :]
        pre = jnp.dot(xc.astype(BF16), gw_ref[0], preferred_element_type=F32) + gb_ref[0]
        for d in range(N_DIR):
            o = d * 2 * LRU_BLOCK
            r = jax.nn.sigmoid(pre[:, o:o + LRU_BLOCK])
            ig = jax.nn.sigmoid(pre[:, o + LRU_BLOCK:o + 2 * LRU_BLOCK])
            log_a = (-LRU_C * r) * sp[:, d * LRU_BLOCK:(d + 1) * LRU_BLOCK]
            a = jnp.exp(log_a)
            a_ref[d, pl.ds(r0, LRU_ROWS), :] = a
            b_ref[d, pl.ds(r0, LRU_ROWS), :] = jnp.sqrt(1.0 - a * a) * (ig * xc)
        return 0

    lax.fori_loop(0, nsteps, gates, 0)

    ngrp = t // 8
    row = lax.broadcasted_iota(jnp.int32, (8, LRU_BLOCK), 0)

    def scan(g, carry):
        cf, cr = carry
        rf = pl.multiple_of(g * 8, 8)
        rr = pl.multiple_of((ngrp - 1 - g) * 8, 8)
        a = a_ref[0, pl.ds(rf, 8), :]
        b = b_ref[0, pl.ds(rf, 8), :]
        for s in (1, 2, 4):
            keep = row >= s
            b = a * jnp.where(keep, pltpu.roll(b, s, axis=0), 0.0) + b
            a = a * jnp.where(keep, pltpu.roll(a, s, axis=0), 1.0)
        hf = a * cf + b
        hf_ref[pl.ds(rf, 8), :] = hf
        cf = jnp.broadcast_to(hf[7:8, :], (8, LRU_BLOCK))
        a = a_ref[1, pl.ds(rr, 8), :]
        b = b_ref[1, pl.ds(rr, 8), :]
        for s in (1, 2, 4):
            keep = row < 8 - s
            b = a * jnp.where(keep, pltpu.roll(b, 8 - s, axis=0), 0.0) + b
            a = a * jnp.where(keep, pltpu.roll(a, 8 - s, axis=0), 1.0)
        hr = a * cr + b
        hr_ref[pl.ds(rr, 8), :] = hr
        cr = jnp.broadcast_to(hr[0:1, :], (8, LRU_BLOCK))
        return cf, cr

    cf0 = jnp.broadcast_to(h0_ref[0, 0:1, :], (8, LRU_BLOCK))
    cr0 = jnp.broadcast_to(h0_ref[0, 1:2, :], (8, LRU_BLOCK))
    cf, cr = lax.fori_loop(0, ngrp, scan, (cf0, cr0), unroll=2)
    he_ref[0, 0:1, :] = cf[0:1, :]
    he_ref[0, 1:2, :] = cr[0:1, :]

    def finish(ci, _):
        r0 = pl.multiple_of(ci * LRU_ROWS, LRU_ROWS)
        y = (hf_ref[pl.ds(r0, LRU_ROWS), :] + hr_ref[pl.ds(r0, LRU_ROWS), :]) * _gelu_tanh(yr_ref[pl.ds(r0, LRU_ROWS), :])
        y_ref[pl.ds(r0, LRU_ROWS), :] = y.astype(y_ref.dtype)
        return 0

    lax.fori_loop(0, nsteps, finish, 0)


def _rglru(cfg, z, cw, cb, gw, gb, lam, h0, nseq, t, row0, y_prev=None):
    m = cfg.m
    assert row0 % t == 0
    sb = row0 // t
    aliased = y_prev is not None
    in_specs = [pl.BlockSpec((t, LRU_BLOCK), lambda b, n: (sb + b, Z_XR // LRU_BLOCK + n)),
                pl.BlockSpec((t, LRU_BLOCK), lambda b, n: (sb + b, Z_YR // LRU_BLOCK + n)),
                pl.BlockSpec((LRU_CONV, LRU_BLOCK), lambda b, n: (0, n)),
                pl.BlockSpec((1, LRU_BLOCK), lambda b, n: (0, n)),
                pl.BlockSpec((1, LRU_BLOCK, 4 * LRU_BLOCK), lambda b, n: (n, 0, 0)),
                pl.BlockSpec((1, 1, 4 * LRU_BLOCK), lambda b, n: (n, 0, 0)),
                pl.BlockSpec((1, 1, 2 * LRU_BLOCK), lambda b, n: (n, 0, 0)),
                pl.BlockSpec((1, N_DIR, LRU_BLOCK), lambda b, n: (b, 0, n))]
    args = [z, z, cw, cb, gw, gb, lam, h0]
    if aliased:
        in_specs = [pl.BlockSpec(memory_space=pl.ANY)] + in_specs
        args = [y_prev] + args
    return pl.pallas_call(
        functools.partial(_rglru_kernel, t=t, aliased=aliased),
        grid=(nseq, LRU_BLOCKS),
        in_specs=in_specs,
        out_specs=[pl.BlockSpec((t, LRU_BLOCK), lambda b, n: (sb + b, n)),
                   pl.BlockSpec((1, N_DIR, LRU_BLOCK), lambda b, n: (b, 0, n))],
        out_shape=[jax.ShapeDtypeStruct((m, LRU_WIDTH), BF16),
                   jax.ShapeDtypeStruct((nseq, N_DIR, LRU_WIDTH), F32)],
        scratch_shapes=[pltpu.VMEM((t + 2 * LRU_PAD, LRU_BLOCK), F32),
                        pltpu.VMEM((N_DIR, t, LRU_BLOCK), F32),
                        pltpu.VMEM((N_DIR, t, LRU_BLOCK), F32),
                        pltpu.VMEM((t, LRU_BLOCK), F32),
                        pltpu.VMEM((t, LRU_BLOCK), F32)],
        input_output_aliases={0: 0} if aliased else {},
        compiler_params=_params(("parallel", "parallel")),
        name="rglru",
    )(*args)


def _merge_kernel(ya_ref, yb_ref, yc_ref, w_ref, ma_ref, mb_ref, mc_ref, bm_ref, o_ref):
    acc = None
    for n, (y_ref, mg_ref) in enumerate(((ya_ref, ma_ref), (yb_ref, mb_ref), (yc_ref, mc_ref))):
        g = jax.nn.sigmoid(mg_ref[...] + bm_ref[n])
        term = g * jnp.dot(y_ref[...], w_ref[n], preferred_element_type=F32)
        acc = term if acc is None else acc + term
    o_ref[...] = acc.astype(o_ref.dtype)


def _merge(cfg, ya, yb, yc, wbr, z, bm, tm, tn):
    m = cfg.m
    nj = D_MODEL // tn
    y_spec = pl.BlockSpec((tm, BRANCH_W), lambda j, i: (i, 0))
    mg_spec = lambda n: pl.BlockSpec((tm, tn), lambda j, i: (i, Z_MG // tn + n * nj + j))
    return pl.pallas_call(
        _merge_kernel,
        grid=(nj, m // tm),
        in_specs=[y_spec, y_spec, y_spec,
                  pl.BlockSpec((N_BRANCH, BRANCH_W, tn), lambda j, i: (0, 0, j)),
                  mg_spec(0), mg_spec(1), mg_spec(2),
                  pl.BlockSpec((N_BRANCH, 1, tn), lambda j, i: (0, 0, j))],
        out_specs=pl.BlockSpec((tm, tn), lambda j, i: (i, j)),
        out_shape=jax.ShapeDtypeStruct((m, D_MODEL), BF16),
        compiler_params=_params(("parallel", "parallel")),
        name="merge",
    )(ya, yb, yc, wbr, z, z, z, bm)


def _matmul_res_kernel(a_ref, w_ref, x_ref, g_ref, o_ref):
    o_ref[...] = x_ref[...] + g_ref[0] * jnp.dot(a_ref[...], w_ref[...], preferred_element_type=F32)


def _matmul_res(cfg, a, w, x, gate, tm, tn):
    m, kdim = a.shape
    n = w.shape[1]
    return pl.pallas_call(
        _matmul_res_kernel,
        grid=(n // tn, m // tm),
        in_specs=[pl.BlockSpec((tm, kdim), lambda j, i: (i, 0)),
                  pl.BlockSpec((kdim, tn), lambda j, i: (0, j)),
                  pl.BlockSpec((tm, tn), lambda j, i: (i, j)),
                  pl.BlockSpec((1, 1, tn), lambda j, i: (_mod_row(cfg, i * tm), 0, j))],
        out_specs=pl.BlockSpec((tm, tn), lambda j, i: (i, j)),
        out_shape=jax.ShapeDtypeStruct((m, n), F32),
        compiler_params=_params(("parallel", "parallel")),
        name="matmul_res",
    )(a, w, x, gate)


FFN_PAD = 72
FFN_STEP_ELEMS = 128 * 128


def _conv_act_kernel(*refs, t, taps, ct, aliased):
    if aliased:
        refs = refs[1:]
    hg_ref, hu_ref, w_ref, b_ref, o_ref, pad_ref = refs
    rows = FFN_STEP_ELEMS // ct
    nsteps = t // rows
    zeros = jnp.zeros((FFN_PAD, ct), F32)
    pad_ref[0:FFN_PAD, :] = zeros
    pad_ref[FFN_PAD + t:2 * FFN_PAD + t, :] = zeros

    def fill(ci, _):
        r0 = pl.multiple_of(ci * rows, rows)
        pad_ref[pl.ds(r0 + FFN_PAD, rows), :] = hg_ref[pl.ds(r0, rows), :]
        return 0

    lax.fori_loop(0, nsteps, fill, 0)
    col = lax.broadcasted_iota(jnp.int32, (rows, ct), 0) % GRID_W

    def step(ci, _):
        r0 = ci * rows if isinstance(ci, int) else pl.multiple_of(ci * rows, rows)
        acc = b_ref[...]
        for shift, widx, dx in taps:
            xs = pad_ref[pl.ds(r0 + FFN_PAD + shift, rows), :]
            if dx < 0:
                xs = jnp.where(col >= -dx, xs, 0.0)
            elif dx > 0:
                xs = jnp.where(col < GRID_W - dx, xs, 0.0)
            acc = acc + xs * w_ref[widx:widx + 1, :]
        o_ref[pl.ds(r0, rows), :] = (_silu(acc) * hu_ref[pl.ds(r0, rows), :]).astype(o_ref.dtype)
        return 0

    if ct > 128:
        for ci in range(nsteps):
            step(ci, 0)
    else:
        lax.fori_loop(0, nsteps, step, 0)


def _conv_act(cfg, h, w9, bias, nseq, t, row0, on_grid, ct, prev=None):
    m = cfg.m
    rows = FFN_STEP_ELEMS // ct
    assert row0 % t == 0 and t % rows == 0 and D_FF % ct == 0 and rows % 16 == 0
    assert not on_grid or rows % GRID_W == 0
    sb = row0 // t
    p = FFN_CONV // 2
    if on_grid:
        taps = tuple(((dy - p) * GRID_W + (dx - p), dy * FFN_CONV + dx, dx - p)
                     for dy in range(FFN_CONV) for dx in range(FFN_CONV))
    else:
        taps = tuple((dx - p, p * FFN_CONV + dx, 0) for dx in range(FFN_CONV))
    aliased = prev is not None
    nct = D_FF // ct
    in_specs = [pl.BlockSpec((t, ct), lambda b, j: (sb + b, j)),
                pl.BlockSpec((t, ct), lambda b, j: (sb + b, nct + j)),
                pl.BlockSpec((FFN_CONV * FFN_CONV, ct), lambda b, j: (0, j)),
                pl.BlockSpec((1, ct), lambda b, j: (0, j))]
    args = [h, h, w9, bias]
    if aliased:
        in_specs = [pl.BlockSpec(memory_space=pl.ANY)] + in_specs
        args = [prev] + args
    return pl.pallas_call(
        functools.partial(_conv_act_kernel, t=t, taps=taps, ct=ct, aliased=aliased),
        grid=(nseq, nct),
        in_specs=in_specs,
        out_specs=pl.BlockSpec((t, ct), lambda b, j: (sb + b, j)),
        out_shape=jax.ShapeDtypeStruct((m, D_FF), BF16),
        scratch_shapes=[pltpu.VMEM((t + 2 * FFN_PAD, ct), F32)],
        input_output_aliases={0: 0} if aliased else {},
        compiler_params=_params(("parallel", "parallel")),
        name="conv_act",
    )(*args)


FD_TM = 512
FD_TK = 512
FD_HALO = GRID_W
FD_TOP = 8
FD_VMEM = 56 * 1024 * 1024


def _ffn_down_kernel(*refs, cfg, on_grid, seq_len, final_norm, aliased):
    refs = list(refs[1:] if aliased else refs)
    hg_ref, hu_ref = refs[:2]
    refs = refs[2:]
    if on_grid:
        hp_ref, hn_ref = refs[:2]
        refs = refs[2:]
    w9_ref, cb_ref, wd_ref, x_ref, g_ref = refs[:5]
    refs = refs[5:]
    if final_norm:
        nf_ref = refs[0]
        refs = refs[1:]
    o_ref, pad_ref, cur_ref, nxt_ref = refs
    i, k = pl.program_id(0), pl.program_id(1)
    nk = pl.num_programs(1) - 1
    tm, tk = FD_TM, FD_TK
    nslab = tk // 128
    base = FD_TOP + FD_HALO

    @pl.when(k == 0)
    def _():
        cur_ref[...] = jnp.zeros_like(cur_ref)
        o_ref[...] = jnp.zeros_like(o_ref)
        edge = jnp.zeros((nslab, FD_TOP, 128), F32)
        pad_ref[:, 0:FD_TOP, :] = edge
        pad_ref[:, base + tm + FD_HALO:base + tm + FD_HALO + FD_TOP, :] = edge
        if not on_grid:
            halo = jnp.zeros((nslab, FD_HALO, 128), F32)
            pad_ref[:, FD_TOP:base, :] = halo
            pad_ref[:, base + tm:base + tm + FD_HALO, :] = halo

    if on_grid:
        tiles_per_seq = seq_len // tm
        first = i % tiles_per_seq == 0
        last = i % tiles_per_seq == tiles_per_seq - 1

        @pl.when(first)
        def _():
            pad_ref[:, FD_TOP:base, :] = jnp.zeros((nslab, FD_HALO, 128), F32)

        @pl.when(jnp.logical_not(first))
        def _():
            for s in range(nslab):
                pad_ref[s, FD_TOP:base, :] = hp_ref[:, s * 128:(s + 1) * 128]

        @pl.when(last)
        def _():
            pad_ref[:, base + tm:base + tm + FD_HALO, :] = jnp.zeros((nslab, FD_HALO, 128), F32)

        @pl.when(jnp.logical_not(last))
        def _():
            for s in range(nslab):
                pad_ref[s, base + tm:base + tm + FD_HALO, :] = hn_ref[:, s * 128:(s + 1) * 128]

    for s in range(nslab):
        pad_ref[s, base:base + tm, :] = hg_ref[:, s * 128:(s + 1) * 128]

    p = FFN_CONV // 2
    if on_grid:
        taps = tuple(((dy - p) * GRID_W + (dx - p), dy * FFN_CONV + dx, dx - p)
                     for dy in range(FFN_CONV) for dx in range(FFN_CONV))
        period = GRID_W
    else:
        taps = tuple((dx - p, p * FFN_CONV + dx, dx - p) for dx in range(FFN_CONV))
        period = seq_len
    kk = pl.multiple_of(jnp.maximum(k - 1, 0) * tk, tk)
    o_ref[...] += jnp.dot(cur_ref[...], wd_ref[pl.ds(kk, tk), :], preferred_element_type=F32)

    rows = 128
    w9 = w9_ref[0]
    cb = cb_ref[0]
    for r in range(tm // rows):
        pos = (lax.broadcasted_iota(jnp.int32, (rows, 128), 0) + r * rows) % period
        for s in range(nslab):
            lanes = slice(s * 128, (s + 1) * 128)
            acc = cb[:, lanes]
            for shift, widx, dx in taps:
                xs = pad_ref[s, base + r * rows + shift:base + (r + 1) * rows + shift, :]
                if dx < 0:
                    xs = jnp.where(pos >= -dx, xs, 0.0)
                elif dx > 0:
                    xs = jnp.where(pos < period - dx, xs, 0.0)
                acc = acc + xs * w9[widx:widx + 1, lanes]
            act = _silu(acc) * hu_ref[r * rows:(r + 1) * rows, lanes]
            nxt_ref[r * rows:(r + 1) * rows, lanes] = act.astype(BF16)

    cur_ref[...] = nxt_ref[...]

    @pl.when(k == nk)
    def _():
        def slab(si, _):
            r0 = pl.multiple_of(si * NORM_SLAB, NORM_SLAB)
            res = x_ref[pl.ds(r0, NORM_SLAB), :] + g_ref[0] * o_ref[pl.ds(r0, NORM_SLAB), :]
            if final_norm:
                res = res * lax.rsqrt(jnp.mean(res * res, axis=-1, keepdims=True) + EPS) * nf_ref[...]
            o_ref[pl.ds(r0, NORM_SLAB), :] = res
            return 0

        lax.fori_loop(0, tm // NORM_SLAB, slab, 0)


def _ffn_down(cfg, h, w9, bias, wd, x, gate, nseq, t, row0, on_grid, prev=None, norm_gain=None):
    m, d = x.shape
    tm, tk = FD_TM, FD_TK
    rows = nseq * t
    nk = D_FF // tk
    assert D_FF % tk == 0 and row0 % tm == 0 and rows % tm == 0
    assert (t % tm == 0 and tm % GRID_W == 0) if on_grid else tm % t == 0
    final_norm = norm_gain is not None
    aliased = prev is not None
    assert not (final_norm and aliased)
    rt0 = row0 // tm
    kc = lambda k: jnp.minimum(k, nk - 1)
    hpb = tm // FD_HALO
    in_specs = [pl.BlockSpec((tm, tk), lambda i, k: (rt0 + i, kc(k))),
                pl.BlockSpec((tm, tk), lambda i, k: (rt0 + i, nk + kc(k)))]
    args = [h, h]
    if on_grid:
        in_specs += [pl.BlockSpec((FD_HALO, tk), lambda i, k: (jnp.maximum((rt0 + i) * hpb - 1, 0), kc(k))),
                     pl.BlockSpec((FD_HALO, tk), lambda i, k: (jnp.minimum((rt0 + i + 1) * hpb, m // FD_HALO - 1), kc(k)))]
        args += [h, h]
    w9c = jnp.transpose(w9.reshape(FFN_CONV * FFN_CONV, nk, tk), (1, 0, 2))
    in_specs += [pl.BlockSpec((1, FFN_CONV * FFN_CONV, tk), lambda i, k: (kc(k), 0, 0)),
                 pl.BlockSpec((1, 1, tk), lambda i, k: (kc(k), 0, 0)),
                 pl.BlockSpec((D_FF, d), lambda i, k: (0, 0), pipeline_mode=pl.Buffered(1)),
                 pl.BlockSpec((tm, d), lambda i, k: (rt0 + i, 0)),
                 pl.BlockSpec((1, 1, d), lambda i, k: (_mod_row(cfg, (rt0 + i) * tm), 0, 0))]
    args += [w9c, bias.reshape(nk, 1, tk), wd, x, gate]
    if final_norm:
        in_specs.append(pl.BlockSpec((1, d), lambda i, k: (0, 0)))
        args.append(norm_gain)
    if aliased:
        in_specs = [pl.BlockSpec(memory_space=pl.ANY)] + in_specs
        args = [prev] + args
    out_rows, out_rt0 = (rows, 0) if final_norm else (m, rt0)
    return pl.pallas_call(
        functools.partial(_ffn_down_kernel, cfg=cfg, on_grid=on_grid, seq_len=t, final_norm=final_norm, aliased=aliased),
        grid=(rows // tm, nk + 1),
        in_specs=in_specs,
        out_specs=pl.BlockSpec((tm, d), lambda i, k: (out_rt0 + i, 0)),
        out_shape=jax.ShapeDtypeStruct((out_rows, d), F32),
        scratch_shapes=[pltpu.VMEM((tk // 128, tm + 2 * (FD_HALO + FD_TOP), 128), F32),
                        pltpu.VMEM((tm, tk), BF16),
                        pltpu.VMEM((tm, tk), BF16)],
        input_output_aliases={0: 0} if aliased else {},
        compiler_params=pltpu.CompilerParams(dimension_semantics=("parallel", "arbitrary"), vmem_limit_bytes=FD_VMEM),
        name="ffn_down",
    )(*args)


def _rmsnorm_kernel(x_ref, g_ref, o_ref):
    x = x_ref[...]
    o_ref[...] = x * lax.rsqrt(jnp.mean(x * x, axis=-1, keepdims=True) + EPS) * g_ref[...]


def _final_norm(x, gain, row0, rows, tm):
    d = x.shape[1]
    assert row0 % tm == 0 and rows % tm == 0
    return pl.pallas_call(
        _rmsnorm_kernel,
        grid=(rows // tm,),
        in_specs=[pl.BlockSpec((tm, d), lambda i: (row0 // tm + i, 0)),
                  pl.BlockSpec((1, d), lambda i: (0, 0))],
        out_specs=pl.BlockSpec((tm, d), lambda i: (i, 0)),
        out_shape=jax.ShapeDtypeStruct((rows, d), F32),
        compiler_params=_params(("parallel",)),
        name="final_norm",
    )(x, gain)


def _pack_w_in(w):
    s = _SRC
    cols = [w[:, s["mg"]:s["end"]], w[:, s["va"]:s["ga"]], w[:, s["ga"]:s["ra"]], w[:, s["qb"]:s["kb"]],
            w[:, s["kb"]:s["vb"]], w[:, s["vb"]:s["ob"]], w[:, s["ob"]:s["gb"]], w[:, s["xr"]:s["yr"]],
            w[:, s["yr"]:s["mg"]], w[:, s["qa"]:s["ka"]], w[:, s["ka"]:s["va"]], w[:, s["ra"]:s["qb"]],
            w[:, s["gb"]:s["xr"]]]
    used = sum(c.shape[1] for c in cols)
    cols.append(jnp.zeros((w.shape[0], NZ - used), w.dtype))
    return jnp.concatenate(cols, axis=1).astype(BF16)


def _pack_gate_w(gw, gb, lam):
    nb, blk = LRU_BLOCKS, LRU_BLOCK
    gw_p = jnp.transpose(gw, (2, 3, 0, 1, 4)).reshape(nb, blk, 4 * blk).astype(BF16)
    gb_p = jnp.transpose(gb.reshape(2, 2, nb, blk), (2, 0, 1, 3)).reshape(nb, 1, 4 * blk)
    lam_p = jnp.transpose(lam.reshape(2, nb, blk), (1, 0, 2)).reshape(nb, 1, 2 * blk)
    return gw_p, gb_p, lam_p


def _layer(cfg, x, mod, lp, states, final_gain=None):
    (n1, n2, w_in, gwa, gba, gng, mbif, mng, lcw, lcb, lgw, lgb, llam, wbr, bmg, wout, fup, fcw, fcb, fdown) = lp
    s_gla, s_mc, s_mn, s_mm, s_lru = states
    m = cfg.m
    sh1, sc1, g1, sh2, sc2, g2 = (mod[:, k * D_MODEL:(k + 1) * D_MODEL].reshape(MOD_ROWS, 1, D_MODEL) for k in range(6))

    proj_tm = PROJ_TM if (cfg.m_ctx % PROJ_TM == 0 and cfg.tl % PROJ_TM == 0) else 512
    z = _norm_matmul(cfg, x, n1.reshape(1, -1), sc1, sh1, _pack_w_in(w_in), proj_tm, 512)

    s0t = jnp.swapaxes(s_gla, -1, -2)
    o_f, sa_f = _gla(cfg, z, gwa[0], gba[0].reshape(1, -1), s0t[:, 0], False)
    y_a, sa_r = _gla(cfg, z, gwa[1], gba[1].reshape(1, -1), s0t[:, 1], True, o_f, gng.reshape(1, -1))
    new_gla = jnp.swapaxes(jnp.stack([sa_f, sa_r], axis=1), -1, -2)

    g_rows = z[:, Z_SM + SM_GB:Z_SM + SM_GB + 4 * MLSTM_HEADS].T
    k_t = z[:, Z_KB:Z_KB + MLSTM_HEADS * MLSTM_DH].T
    b_col = mbif.reshape(1, -1)
    b_row = mbif.reshape(-1, 1)
    n0 = s_mn[:, :, :, None, :]
    m0 = jnp.broadcast_to(s_mm[:, :, :, None, None], s_mm.shape + (1, 128))
    h_f, cb_f, nb_f, mb_f = _mlstm(cfg, z, k_t, g_rows, b_col, b_row, s_mc[:, 0], n0[:, 0], m0[:, 0], False)
    y_b, cb_r, nb_r, mb_r = _mlstm(cfg, z, k_t, g_rows, b_col, b_row, s_mc[:, 1], n0[:, 1], m0[:, 1], True,
                                   h_f, mng.reshape(1, -1))
    new_mc = jnp.stack([cb_f, cb_r], axis=1)
    new_mn = jnp.stack([nb_f[:, :, 0], nb_r[:, :, 0]], axis=1)
    new_mm = jnp.stack([mb_f[:, :, 0, 0], mb_r[:, :, 0, 0]], axis=1)

    gw_p, gb_p, lam_p = _pack_gate_w(lgw, lgb, llam)
    cbias = lcb.reshape(1, -1)
    y_c, new_lru = _rglru(cfg, z, lcw, cbias, gw_p, gb_p, lam_p, jnp.zeros((cfg.bc, N_DIR, LRU_WIDTH), F32),
                          cfg.bc, cfg.tc, 0)
    y_c, _ = _rglru(cfg, z, lcw, cbias, gw_p, gb_p, lam_p, s_lru, cfg.bl, cfg.tl, cfg.m_ctx, y_prev=y_c)

    merged = _merge(cfg, y_a, y_b, y_c, wbr.astype(BF16), z, bmg.reshape(N_BRANCH, 1, D_MODEL), 512, 1024)
    x = _matmul_res(cfg, merged, wout.astype(BF16), x, g1, 512, 1024)

    h = _norm_matmul(cfg, x, n2.reshape(1, -1), sc2, sh2, fup.astype(BF16), proj_tm, 512)
    w9 = fcw.reshape(FFN_CONV * FFN_CONV, D_FF)
    wd = fdown.astype(BF16)
    if final_gain is None:
        x_ctx = _ffn_down(cfg, h, w9, fcb, wd, x, g2, cfg.bc, cfg.tc, 0, False)
        x = _ffn_down(cfg, h, w9, fcb, wd, x, g2, cfg.bl, cfg.tl, cfg.m_ctx, True, prev=x_ctx)
    else:
        x = (_ffn_down(cfg, h, w9, fcb, wd, x, g2, cfg.bc, cfg.tc, 0, False, norm_gain=final_gain),
             _ffn_down(cfg, h, w9, fcb, wd, x, g2, cfg.bl, cfg.tl, cfg.m_ctx, True, norm_gain=final_gain))
    return x, (new_gla, new_mc, new_mn, new_mm, new_lru)


def kernel(x_prompt, x_sample, state_gla, state_mlstm_c, state_mlstm_n, state_mlstm_m, state_rglru, c, c_ctx, norm1_g, norm2_g, w_mod, b_mod, w_in, gla_w_alpha, gla_b_alpha, gla_norm_g, mlstm_b_if, mlstm_norm_g, lru_conv_w, lru_conv_b, lru_gate_w, lru_gate_b, lru_lambda, w_branch, b_merge, w_out, ffn_w_up, ffn_conv_w, ffn_conv_b, ffn_w_down, norm_f_g):
    bc, tc, d = x_prompt.shape
    bl, tl, _ = x_sample.shape
    cfg = Cfg(bc, tc, bl, tl)
    assert tc % TB == 0 and tl % TB == 0 and cfg.m_ctx % tl == 0 and 1 + bl <= MOD_ROWS
    depth = w_in.shape[0]

    x = jnp.concatenate([x_prompt.reshape(bc * tc, d), x_sample.reshape(bl * tl, d)], axis=0)
    c_all = jnp.concatenate([c_ctx[None, :], c, jnp.zeros((MOD_ROWS - 1 - bl, d), F32)], axis=0)
    mod = _modulation(c_all, w_mod, b_mod)

    new = []
    for l in range(depth):
        lp = (norm1_g[l], norm2_g[l], w_in[l], gla_w_alpha[l], gla_b_alpha[l], gla_norm_g[l], mlstm_b_if[l],
              mlstm_norm_g[l], lru_conv_w[l], lru_conv_b[l], lru_gate_w[l], lru_gate_b[l], lru_lambda[l],
              w_branch[l], b_merge[l], w_out[l], ffn_w_up[l], ffn_conv_w[l], ffn_conv_b[l], ffn_w_down[l])
        states = (state_gla[:, l], state_mlstm_c[:, l], state_mlstm_n[:, l], state_mlstm_m[:, l], state_rglru[:, l])
        x, st = _layer(cfg, x, mod[l], lp, states, norm_f_g.reshape(1, -1) if l == depth - 1 else None)
        new.append(st)

    y_prompt = x[0].reshape(bc, tc, d)
    y_sample = x[1].reshape(bl, tl, d)
    stacked = tuple(jnp.stack([new[l][k] for l in range(depth)], axis=1) for k in range(5))
    return (y_prompt, y_sample) + stacked
```

```python
import functools
from typing import NamedTuple

import numpy as np
import jax
import jax.numpy as jnp
from jax import lax
from jax.experimental import pallas as pl
from jax.experimental.pallas import tpu as pltpu

F32 = jnp.float32
BF16 = jnp.bfloat16

D_MODEL = 2048
DEPTH = 2
GRID_W = 64
N_DIR = 2
N_BRANCH = 3
BRANCH_W = 1024
GLA_HEADS = 4
GLA_DK = 128
GLA_DV = 256
GLA_RANK = 16
GLA_TAU = 16.0
MLSTM_HEADS = 4
MLSTM_DH = 256
CHUNK = 64
LRU_WIDTH = 1024
LRU_BLOCKS = 8
LRU_BLOCK = 128
LRU_CONV = 4
LRU_C = 8.0
D_FF = 5632
FFN_CONV = 3
EPS = 1e-6

_SRC = dict(qa=0, ka=512, va=1024, ga=2048, ra=3072, qb=3104, kb=4128, vb=5152, ob=6176, gb=7200,
            xr=7216, yr=8240, mg=9264, end=15408)
Z_MG, Z_VA, Z_GA, Z_QB, Z_KB, Z_VB, Z_OB, Z_XR, Z_YR, Z_QA, Z_KA, Z_SM = (
    0, 6144, 7168, 8192, 9216, 10240, 11264, 12288, 13312, 14336, 14848, 15360)
SM_W = 128
SM_RA, SM_GB = 0, 32
NZ = 15872
MOD_ROWS = 16

PROJ_TM = 1024
NORM_SLAB = 32
PROJ_SLAB = 64
TB = 256
NCH = TB // CHUNK
VMEM_LIMIT = 48 * 1024 * 1024


class Cfg(NamedTuple):
    bc: int
    tc: int
    bl: int
    tl: int

    @property
    def m_ctx(self):
        return self.bc * self.tc

    @property
    def m(self):
        return self.bc * self.tc + self.bl * self.tl


def _params(sem):
    return pltpu.CompilerParams(dimension_semantics=sem, vmem_limit_bytes=VMEM_LIMIT)


def _softplus(x):
    return jnp.maximum(x, 0.0) + jnp.log1p(jnp.exp(-jnp.abs(x)))


def _log_sigmoid(x):
    return -_softplus(-x)


def _silu(x):
    return x * jax.nn.sigmoid(x)


def _gelu_tanh(x):
    return 0.5 * x * (1.0 + jnp.tanh(np.sqrt(2.0 / np.pi).astype(np.float32) * (x + 0.044715 * (x * x * x))))


def _mod_row(cfg, row):
    return jnp.where(row < cfg.m_ctx, 0, 1 + jnp.maximum(row - cfg.m_ctx, 0) // cfg.tl)


def _mod_kernel(c_ref, w_ref, b_ref, o_ref):
    c = c_ref[...]
    a = _silu(c).astype(BF16)
    o_ref[0] = jnp.dot(a, w_ref[0].astype(BF16), preferred_element_type=F32) + b_ref[0]


def _modulation(c_all, w_mod, b_mod):
    depth, d, n = w_mod.shape
    tn = 512
    return pl.pallas_call(
        _mod_kernel,
        grid=(depth, n // tn),
        in_specs=[pl.BlockSpec((MOD_ROWS, d), lambda l, j: (0, 0)),
                  pl.BlockSpec((1, d, tn), lambda l, j: (l, 0, j)),
                  pl.BlockSpec((1, 1, tn), lambda l, j: (l, 0, j))],
        out_specs=pl.BlockSpec((1, MOD_ROWS, tn), lambda l, j: (l, 0, j)),
        out_shape=jax.ShapeDtypeStruct((depth, MOD_ROWS, n), F32),
        compiler_params=_params(("parallel", "parallel")),
        name="modulation",
    )(c_all, w_mod, b_mod.reshape(depth, 1, n))


def _norm_rows(x_ref, g_ref, sc_ref, sh_ref, u_ref, r0, nrows):
    x = x_ref[pl.ds(r0, nrows), :]
    y = x * lax.rsqrt(jnp.mean(x * x, axis=-1, keepdims=True) + EPS) * g_ref[...]
    u_ref[pl.ds(r0, nrows), :] = (y * (1.0 + sc_ref[0]) + sh_ref[0]).astype(BF16)


def _norm_matmul_kernel(x0_ref, xn_ref, g_ref, sc0_ref, sh0_ref, scn_ref, shn_ref, w_ref, o_ref, ua_ref, ub_ref, *, slab):
    i, j = pl.program_id(0), pl.program_id(1)
    tm = xn_ref.shape[0]

    @pl.when(jnp.logical_and(i == 0, j == 0))
    def _():
        def first(si, _):
            _norm_rows(x0_ref, g_ref, sc0_ref, sh0_ref, ua_ref, pl.multiple_of(si * NORM_SLAB, NORM_SLAB), NORM_SLAB)
            return 0

        lax.fori_loop(0, tm // NORM_SLAB, first, 0)

    r0 = pl.multiple_of(jnp.minimum(j, tm // slab - 1) * slab, slab)

    @pl.when(i % 2 == 0)
    def _():
        _norm_rows(xn_ref, g_ref, scn_ref, shn_ref, ub_ref, r0, slab)
        o_ref[...] = jnp.dot(ua_ref[...], w_ref[...], preferred_element_type=F32)

    @pl.when(i % 2 == 1)
    def _():
        _norm_rows(xn_ref, g_ref, scn_ref, shn_ref, ua_ref, r0, slab)
        o_ref[...] = jnp.dot(ub_ref[...], w_ref[...], preferred_element_type=F32)


def _norm_matmul(cfg, x, gain, sc, sh, w, tm, tn):
    m, d = x.shape
    n = w.shape[1]
    slab = PROJ_SLAB
    assert tm % slab == 0 and n // tn >= tm // slab
    nxt = lambda i: jnp.minimum(i + 1, m // tm - 1)
    mod0 = lambda i, j: (_mod_row(cfg, 0), 0, 0)
    modn = lambda i, j: (_mod_row(cfg, nxt(i) * tm), 0, 0)
    return pl.pallas_call(
        functools.partial(_norm_matmul_kernel, slab=slab),
        grid=(m // tm, n // tn),
        in_specs=[pl.BlockSpec((tm, d), lambda i, j: (0, 0), pipeline_mode=pl.Buffered(1)),
                  pl.BlockSpec((tm, d), lambda i, j: (nxt(i), 0)),
                  pl.BlockSpec((1, d), lambda i, j: (0, 0)),
                  pl.BlockSpec((1, 1, d), mod0),
                  pl.BlockSpec((1, 1, d), mod0),
                  pl.BlockSpec((1, 1, d), modn),
                  pl.BlockSpec((1, 1, d), modn),
                  pl.BlockSpec((d, tn), lambda i, j: (0, j))],
        out_specs=pl.BlockSpec((tm, tn), lambda i, j: (i, j)),
        out_shape=jax.ShapeDtypeStruct((m, n), F32),
        scratch_shapes=[pltpu.VMEM((tm, d), BF16), pltpu.VMEM((tm, d), BF16)],
        compiler_params=_params(("arbitrary", "arbitrary")),
        name="norm_matmul",
    )(x, x, gain, sc, sh, sc, sh, w)


NSLOT = 2


def _pair_pos(cfg, i):
    kc, kl = cfg.tc // TB, cfg.tl // TB
    nc = (cfg.bc // NSLOT) * kc
    is_ctx = i < nc
    il = jnp.maximum(i - nc, 0)
    j = jnp.where(is_ctx, i % kc, il % kl)
    k = jnp.where(is_ctx, kc, kl)
    s = jnp.where(is_ctx, i // kc, il // kl)
    return is_ctx, s, j, k


def _pair_row_block(cfg, reverse, slot, i):
    kc, kl = cfg.tc // TB, cfg.tl // TB
    is_ctx, s, j, k = _pair_pos(cfg, i)
    jj = (k - 1 - j) if reverse else j
    return jnp.where(is_ctx, (s + slot * (cfg.bc // NSLOT)) * kc + jj,
                     cfg.bc * kc + (s + slot * (cfg.bl // NSLOT)) * kl + jj)


def _pair_local_block(cfg, reverse, i):
    kc, kl = cfg.tc // TB, cfg.tl // TB
    is_ctx, s, j, k = _pair_pos(cfg, i)
    jj = (k - 1 - j) if reverse else j
    return jnp.where(is_ctx, s * kc + jj, (cfg.bc // NSLOT) * kc + s * kl + jj)


def _pair_lat_seq(cfg, slot, i):
    is_ctx, s, _, _ = _pair_pos(cfg, i)
    return jnp.where(is_ctx, 0, jnp.minimum(s, cfg.bl // NSLOT - 1)) + slot * (cfg.bl // NSLOT)


def _pair_ctx_seq(cfg, i):
    is_ctx, s, _, _ = _pair_pos(cfg, i)
    return jnp.where(is_ctx, s, cfg.bc // NSLOT - 1)


def _stitch(cfg, parts):
    hc = cfg.m_ctx // NSLOT
    return jnp.concatenate([p[:hc] for p in parts] + [p[hc:] for p in parts], axis=0)


def _tri(reverse, n=CHUNK):
    row = lax.broadcasted_iota(jnp.int32, (n, n), 0)
    col = lax.broadcasted_iota(jnp.int32, (n, n), 1)
    return (row <= col) if reverse else (row >= col)


def _head_norm(x, g):
    return x * lax.rsqrt(jnp.mean(x * x, axis=-1, keepdims=True) + EPS) * g


_NT = (((1,), (1,)), ((), ()))
_TN = (((0,), (0,)), ((), ()))


GLA_LOCKSTEP = 2


def _gla_kernel(*refs, cfg, reverse):
    it = iter(refs)
    blocks = [[next(it) for _ in range(4)] for _ in range(NSLOT)]
    wal_ref, bal_ref = next(it), next(it)
    init = [next(it) for _ in range(NSLOT)]
    if reverse:
        epi = [[next(it) for _ in range(2)] for _ in range(NSLOT)]
        gn_ref = next(it)
    outs = [next(it) for _ in range(NSLOT)]
    fin = [next(it) for _ in range(NSLOT)]
    st_ref, la_ref = next(it), next(it)
    is_ctx, _, j, k = _pair_pos(cfg, pl.program_id(0))
    nh = GLA_HEADS

    @pl.when(jnp.logical_and(j == 0, is_ctx))
    def _():
        st_ref[...] = jnp.zeros_like(st_ref)

    @pl.when(jnp.logical_and(j == 0, jnp.logical_not(is_ctx)))
    def _():
        for b in range(NSLOT):
            st_ref[b * nh:(b + 1) * nh] = init[b][0]

    d = 1 if reverse else 0
    slots = range(NSLOT)
    both = lambda f: [f(b) for b in slots]
    wal = wal_ref[...].astype(BF16)
    for b in slots:
        ra = blocks[b][3][:, SM_RA + d * GLA_RANK:SM_RA + (d + 1) * GLA_RANK].astype(BF16)
        pre = jnp.dot(ra, wal, preferred_element_type=F32) + bal_ref[...]
        la_ref[b] = _log_sigmoid(pre) * (1.0 / GLA_TAU)

    tri = _tri(reverse)
    tri_f = tri.astype(F32)
    last = 0 if reverse else CHUNK - 1
    qscale = GLA_DK ** -0.5
    for c in (range(NCH - 1, -1, -1) if reverse else range(NCH)):
        rows = slice(c * CHUNK, (c + 1) * CHUNK)
        cum = both(lambda b: jnp.dot(tri_f, la_ref[b, rows, :], precision=lax.Precision.HIGHEST,
                                     preferred_element_type=F32))
        tot = both(lambda b: cum[b][last:last + 1, :])
        e_q = both(lambda b: jnp.exp(cum[b]))
        e_k = both(lambda b: jnp.exp(-cum[b]))
        e_end = both(lambda b: jnp.exp(tot[b] - cum[b]))
        e_tot = both(lambda b: jnp.exp(tot[b]))
        for h0 in range(0, GLA_HEADS, GLA_LOCKSTEP):
            units = [(b, h0 + dh) for dh in range(GLA_LOCKSTEP) for b in slots]
            each = lambda f: [f(u, b, h) for u, (b, h) in enumerate(units)]
            kc = [slice(h * GLA_DK, (h + 1) * GLA_DK) for _, h in units]
            vc = [slice(h * GLA_DV, (h + 1) * GLA_DV) for _, h in units]
            sh = [b * nh + h for b, h in units]
            qh = each(lambda u, b, h: blocks[b][0][rows, kc[u]] * qscale)
            kh = each(lambda u, b, h: blocks[b][1][rows, kc[u]])
            vh = each(lambda u, b, h: blocks[b][2][rows, vc[u]].astype(BF16))
            q_in = each(lambda u, b, h: (qh[u] * e_q[b][:, kc[u]]).astype(BF16))
            k_in = each(lambda u, b, h: (kh[u] * e_k[b][:, kc[u]]).astype(BF16))
            att = each(lambda u, b, h: lax.dot_general(q_in[u], k_in[u], _NT, preferred_element_type=F32))
            att = each(lambda u, b, h: jnp.where(tri, att[u], 0.0).astype(BF16))
            st = each(lambda u, b, h: st_ref[sh[u]])
            o = each(lambda u, b, h: jnp.dot(att[u], vh[u], preferred_element_type=F32)
                     + lax.dot_general(q_in[u], st[u].astype(BF16), _NT, preferred_element_type=F32))
            k_end = each(lambda u, b, h: (kh[u] * e_end[b][:, kc[u]]).astype(BF16))
            upd = each(lambda u, b, h: lax.dot_general(vh[u], k_end[u], _TN, preferred_element_type=F32))
            for u, (b, h) in enumerate(units):
                st_ref[sh[u]] = e_tot[b][:, kc[u]] * st[u] + upd[u]
            for u, (b, h) in enumerate(units):
                if reverse:
                    ga_ref, of_ref = epi[b]
                    y = _head_norm(of_ref[rows, vc[u]] + o[u], gn_ref[:, vc[u]]) * _silu(ga_ref[rows, vc[u]])
                    outs[b][rows, vc[u]] = y.astype(outs[b].dtype)
                else:
                    outs[b][rows, vc[u]] = o[u]

    @pl.when(jnp.logical_and(is_ctx, j == k - 1))
    def _():
        for b in range(NSLOT):
            fin[b][0] = st_ref[b * nh:(b + 1) * nh]


def _gla(cfg, z, wal, bal, s0t, reverse, o_fwd=None, gnorm=None):
    m = cfg.m
    nh = GLA_HEADS
    hk, hv = nh * GLA_DK, nh * GLA_DV
    lb = functools.partial(_pair_local_block, cfg, reverse)
    in_specs, args = [], []
    for b in range(NSLOT):
        rb = functools.partial(_pair_row_block, cfg, reverse, b)
        in_specs += [pl.BlockSpec((TB, hk), lambda i, rb=rb: (rb(i), Z_QA // hk)),
                     pl.BlockSpec((TB, hk), lambda i, rb=rb: (rb(i), Z_KA // hk)),
                     pl.BlockSpec((TB, hv), lambda i, rb=rb: (rb(i), Z_VA // hv)),
                     pl.BlockSpec((TB, SM_W), lambda i, rb=rb: (rb(i), Z_SM // SM_W))]
        args += [z, z, z, z]
    in_specs += [pl.BlockSpec((GLA_RANK, hk), lambda i: (0, 0)), pl.BlockSpec((1, hk), lambda i: (0, 0))]
    args += [wal, bal]
    for b in range(NSLOT):
        in_specs.append(pl.BlockSpec((1, nh, GLA_DV, GLA_DK), lambda i, b=b: (_pair_lat_seq(cfg, b, i), 0, 0, 0)))
        args.append(s0t)
    if reverse:
        for b in range(NSLOT):
            rb = functools.partial(_pair_row_block, cfg, reverse, b)
            in_specs += [pl.BlockSpec((TB, hv), lambda i, rb=rb: (rb(i), Z_GA // hv)),
                         pl.BlockSpec((TB, hv), lambda i: (lb(i), 0))]
            args += [z, o_fwd[b]]
        in_specs.append(pl.BlockSpec((1, hv), lambda i: (0, 0)))
        args.append(gnorm)
    half = cfg.bc // NSLOT
    out_specs = ([pl.BlockSpec((TB, hv), lambda i: (lb(i), 0)) for _ in range(NSLOT)]
                 + [pl.BlockSpec((1, nh, GLA_DV, GLA_DK), lambda i: (_pair_ctx_seq(cfg, i), 0, 0, 0)) for _ in range(NSLOT)])
    out_shape = ([jax.ShapeDtypeStruct((m // NSLOT, hv), BF16 if reverse else F32) for _ in range(NSLOT)]
                 + [jax.ShapeDtypeStruct((half, nh, GLA_DV, GLA_DK), F32) for _ in range(NSLOT)])
    res = pl.pallas_call(
        functools.partial(_gla_kernel, cfg=cfg, reverse=reverse),
        grid=(m // TB // NSLOT,),
        in_specs=in_specs,
        out_specs=out_specs,
        out_shape=out_shape,
        scratch_shapes=[pltpu.VMEM((NSLOT * nh, GLA_DV, GLA_DK), F32), pltpu.VMEM((NSLOT, TB, hk), F32)],
        compiler_params=_params(("arbitrary",)),
        name="gla_rev" if reverse else "gla_fwd",
    )(*args)
    return tuple(res[:NSLOT]), jnp.concatenate(res[NSLOT:], axis=0)


MCH = 128
NMC = TB // MCH
MLSTM_LOCKSTEP = 2


def _mlstm_kernel(*refs, cfg, reverse):
    it = iter(refs)
    blocks = [[next(it) for _ in range(6)] for _ in range(NSLOT)]
    bc_ref, br_ref = next(it), next(it)
    init = [[next(it) for _ in range(3)] for _ in range(NSLOT)]
    if reverse:
        epi = [[next(it) for _ in range(2)] for _ in range(NSLOT)]
        gn_ref = next(it)
    outs = [next(it) for _ in range(NSLOT)]
    fin = [[next(it) for _ in range(3)] for _ in range(NSLOT)]
    c_ref, n_ref, m_ref = next(it), next(it), next(it)
    is_ctx, _, j, k = _pair_pos(cfg, pl.program_id(0))
    nh = MLSTM_HEADS

    @pl.when(jnp.logical_and(j == 0, is_ctx))
    def _():
        c_ref[...] = jnp.zeros_like(c_ref)
        n_ref[...] = jnp.zeros_like(n_ref)
        m_ref[...] = jnp.zeros_like(m_ref)

    @pl.when(jnp.logical_and(j == 0, jnp.logical_not(is_ctx)))
    def _():
        for b in range(NSLOT):
            c_ref[b * nh:(b + 1) * nh] = init[b][0][0]
            n_ref[b * nh:(b + 1) * nh] = init[b][1][0]
            m_ref[b * nh:(b + 1) * nh] = init[b][2][0]

    d = 1 if reverse else 0
    tri = _tri(reverse, MCH)
    tri_f = tri.astype(F32)
    tri_tf = _tri(not reverse, MCH).astype(F32)
    last = 0 if reverse else MCH - 1
    ng = 2 * MLSTM_HEADS
    kscale = MLSTM_DH ** -0.5
    slots = range(NSLOT)
    both = lambda f: [f(b) for b in slots]
    hp = lax.Precision.HIGHEST
    for c in (range(NMC - 1, -1, -1) if reverse else range(NMC)):
        rows = slice(c * MCH, (c + 1) * MCH)
        g_col = both(lambda b: blocks[b][4][rows, SM_GB:SM_GB + 2 * ng] + bc_ref[...])
        g_row = both(lambda b: blocks[b][5][:, rows] + br_ref[...])
        fcum_col = both(lambda b: jnp.dot(tri_f, _log_sigmoid(g_col[b]), precision=hp, preferred_element_type=F32))
        fcum_row = both(lambda b: jnp.dot(_log_sigmoid(g_row[b]), tri_tf, precision=hp, preferred_element_type=F32))
        for h0 in range(0, MLSTM_HEADS, MLSTM_LOCKSTEP):
            units = [(b, h0 + dh) for dh in range(MLSTM_LOCKSTEP) for b in slots]
            each = lambda f: [f(u, b, h) for u, (b, h) in enumerate(units)]
            ii = [d * ng + h for _, h in units]
            fi = [d * ng + MLSTM_HEADS + h for _, h in units]
            hc = [slice(h * MLSTM_DH, (h + 1) * MLSTM_DH) for _, h in units]
            sh = [b * nh + h for b, h in units]
            f_col = each(lambda u, b, h: fcum_col[b][:, fi[u]:fi[u] + 1])
            f_row = each(lambda u, b, h: fcum_row[b][fi[u]:fi[u] + 1, :])
            i_col = each(lambda u, b, h: g_col[b][:, ii[u]:ii[u] + 1])
            i_row = each(lambda u, b, h: g_row[b][ii[u]:ii[u] + 1, :])
            m_prev = each(lambda u, b, h: m_ref[sh[u]][:, 0:1])
            dlog = each(lambda u, b, h: jnp.where(tri, f_col[u] - f_row[u] + i_row[u], -jnp.inf))
            prev = each(lambda u, b, h: f_col[u] + m_prev[u])
            mj = each(lambda u, b, h: jnp.maximum(prev[u], jnp.max(dlog[u], axis=-1, keepdims=True)))
            w = each(lambda u, b, h: jnp.exp(dlog[u] - mj[u]))
            wp = each(lambda u, b, h: jnp.exp(prev[u] - mj[u]))
            qh = each(lambda u, b, h: blocks[b][0][rows, hc[u]])
            kh = each(lambda u, b, h: blocks[b][1][rows, hc[u]] * kscale)
            vh = each(lambda u, b, h: blocks[b][2][rows, hc[u]].astype(BF16))
            s = each(lambda u, b, h: lax.dot_general(qh[u].astype(BF16), kh[u].astype(BF16), _NT,
                                                     preferred_element_type=F32) * w[u])
            qp = each(lambda u, b, h: qh[u] * wp[u])
            cm = each(lambda u, b, h: c_ref[sh[u]])
            nv = each(lambda u, b, h: n_ref[sh[u]])
            num = each(lambda u, b, h: jnp.dot(s[u].astype(BF16), vh[u], preferred_element_type=F32)
                       + jnp.dot(qp[u].astype(BF16), cm[u].astype(BF16), preferred_element_type=F32))
            den = each(lambda u, b, h: jnp.sum(s[u], axis=-1, keepdims=True)
                       + jnp.sum(qp[u] * nv[u], axis=-1, keepdims=True))
            hh = each(lambda u, b, h: num[u] / jnp.maximum(jnp.abs(den[u]), jnp.exp(-mj[u])))
            m_new = each(lambda u, b, h: mj[u][last:last + 1, :])
            tot = each(lambda u, b, h: f_col[u][last:last + 1, :])
            wl_col = each(lambda u, b, h: jnp.exp(tot[u] - f_col[u] + i_col[u] - m_new[u]))
            wl_row = each(lambda u, b, h: jnp.exp(tot[u] - f_row[u] + i_row[u] - m_new[u]))
            decay = each(lambda u, b, h: jnp.exp(tot[u] + m_prev[u] - m_new[u]))
            kw_t = each(lambda u, b, h: (blocks[b][3][hc[u], rows] * kscale) * wl_row[u])
            upd = each(lambda u, b, h: jnp.dot(kw_t[u].astype(BF16), vh[u], preferred_element_type=F32))
            for u, (b, h) in enumerate(units):
                c_ref[sh[u]] = decay[u] * cm[u] + upd[u]
                n_ref[sh[u]] = decay[u] * nv[u] + jnp.sum(kh[u] * wl_col[u], axis=0, keepdims=True)
                m_ref[sh[u]] = jnp.broadcast_to(m_new[u], (1, 128))
            for u, (b, h) in enumerate(units):
                if reverse:
                    ob_ref, hf_ref = epi[b]
                    y = jax.nn.sigmoid(ob_ref[rows, hc[u]]) * _head_norm(hf_ref[rows, hc[u]] + hh[u], gn_ref[:, hc[u]])
                    outs[b][rows, hc[u]] = y.astype(outs[b].dtype)
                else:
                    outs[b][rows, hc[u]] = hh[u]

    @pl.when(jnp.logical_and(is_ctx, j == k - 1))
    def _():
        for b in range(NSLOT):
            fin[b][0][0] = c_ref[b * nh:(b + 1) * nh]
            fin[b][1][0] = n_ref[b * nh:(b + 1) * nh]
            fin[b][2][0] = m_ref[b * nh:(b + 1) * nh]


def _mlstm(cfg, z, k_t, g_rows, b_col, b_row, c0, n0, m0, reverse, h_fwd=None, gnorm=None):
    m = cfg.m
    hd = MLSTM_HEADS * MLSTM_DH
    nh = MLSTM_HEADS
    lb = functools.partial(_pair_local_block, cfg, reverse)
    ctx = lambda i: (_pair_ctx_seq(cfg, i), 0, 0, 0)
    in_specs, args = [], []
    for b in range(NSLOT):
        rb = functools.partial(_pair_row_block, cfg, reverse, b)
        in_specs += [pl.BlockSpec((TB, hd), lambda i, rb=rb: (rb(i), Z_QB // hd)),
                     pl.BlockSpec((TB, hd), lambda i, rb=rb: (rb(i), Z_KB // hd)),
                     pl.BlockSpec((TB, hd), lambda i, rb=rb: (rb(i), Z_VB // hd)),
                     pl.BlockSpec((hd, TB), lambda i, rb=rb: (0, rb(i))),
                     pl.BlockSpec((TB, SM_W), lambda i, rb=rb: (rb(i), Z_SM // SM_W)),
                     pl.BlockSpec((4 * nh, TB), lambda i, rb=rb: (0, rb(i)))]
        args += [z, z, z, k_t, z, g_rows]
    in_specs += [pl.BlockSpec((1, 4 * nh), lambda i: (0, 0)), pl.BlockSpec((4 * nh, 1), lambda i: (0, 0))]
    args += [b_col, b_row]
    for b in range(NSLOT):
        lat = lambda i, b=b: (_pair_lat_seq(cfg, b, i), 0, 0, 0)
        in_specs += [pl.BlockSpec((1, nh, MLSTM_DH, MLSTM_DH), lat),
                     pl.BlockSpec((1, nh, 1, MLSTM_DH), lat),
                     pl.BlockSpec((1, nh, 1, 128), lat)]
        args += [c0, n0, m0]
    if reverse:
        for b in range(NSLOT):
            rb = functools.partial(_pair_row_block, cfg, reverse, b)
            in_specs += [pl.BlockSpec((TB, hd), lambda i, rb=rb: (rb(i), Z_OB // hd)),
                         pl.BlockSpec((TB, hd), lambda i: (lb(i), 0))]
            args += [z, h_fwd[b]]
        in_specs.append(pl.BlockSpec((1, hd), lambda i: (0, 0)))
        args.append(gnorm)
    half = cfg.bc // NSLOT
    out_specs = [pl.BlockSpec((TB, hd), lambda i: (lb(i), 0)) for _ in range(NSLOT)]
    out_shape = [jax.ShapeDtypeStruct((m // NSLOT, hd), BF16 if reverse else F32) for _ in range(NSLOT)]
    for _ in range(NSLOT):
        out_specs += [pl.BlockSpec((1, nh, MLSTM_DH, MLSTM_DH), ctx),
                      pl.BlockSpec((1, nh, 1, MLSTM_DH), ctx),
                      pl.BlockSpec((1, nh, 1, 128), ctx)]
        out_shape += [jax.ShapeDtypeStruct((half, nh, MLSTM_DH, MLSTM_DH), F32),
                      jax.ShapeDtypeStruct((half, nh, 1, MLSTM_DH), F32),
                      jax.ShapeDtypeStruct((half, nh, 1, 128), F32)]
    res = pl.pallas_call(
        functools.partial(_mlstm_kernel, cfg=cfg, reverse=reverse),
        grid=(m // TB // NSLOT,),
        in_specs=in_specs,
        out_specs=out_specs,
        out_shape=out_shape,
        scratch_shapes=[pltpu.VMEM((NSLOT * nh, MLSTM_DH, MLSTM_DH), F32),
                        pltpu.VMEM((NSLOT * nh, 1, MLSTM_DH), F32),
                        pltpu.VMEM((NSLOT * nh, 1, 128), F32)],
        compiler_params=_params(("arbitrary",)),
        name="mlstm_rev" if reverse else "mlstm_fwd",
    )(*args)
    outs, fin = res[:NSLOT], res[NSLOT:]
    states = tuple(jnp.concatenate([fin[3 * b + t] for b in range(NSLOT)], axis=0) for t in range(3))
    return (tuple(outs),) + states


LRU_ROWS = 256
LRU_PAD = 8


def _rglru_kernel(*refs, t, aliased):
    if aliased:
        refs = refs[1:]
    (xr_ref, yr_ref, cw_ref, cb_ref, gw_ref, gb_ref, lam_ref, h0_ref, y_ref, he_ref,
     pad_ref, a_ref, b_ref, hf_ref, hr_ref) = refs
    nsteps = t // LRU_ROWS
    zeros = jnp.zeros((LRU_PAD, LRU_BLOCK), F32)
    pad_ref[0:LRU_PAD, :] = zeros
    pad_ref[LRU_PAD + t:2 * LRU_PAD + t, :] = zeros

    def fill(ci, _):
        r0 = pl.multiple_of(ci * LRU_ROWS, LRU_ROWS)
        pad_ref[pl.ds(r0 + LRU_PAD, LRU_ROWS), :] = xr_ref[pl.ds(r0, LRU_ROWS), :]
        return 0

    lax.fori_loop(0, nsteps, fill, 0)

    sp = _softplus(-lam_ref[0])
    left = LRU_CONV // 2

    def gates(ci, _):
        r0 = pl.multiple_of(ci * LRU_ROWS, LRU_ROWS)
        xc = cb_ref[...]
        for tap in range(LRU_CONV):
            xc = xc + cw_ref[tap:tap + 1, :] * pad_ref[pl.ds(r0 + LRU_PAD - left + tap, LRU_ROWS), :]
        pre = jnp.dot(xc.astype(BF16), gw_ref[0], preferred_element_type=F32) + gb_ref[0]
        for d in range(N_DIR):
            o = d * 2 * LRU_BLOCK
            r = jax.nn.sigmoid(pre[:, o:o + LRU_BLOCK])
            ig = jax.nn.sigmoid(pre[:, o + LRU_BLOCK:o + 2 * LRU_BLOCK])
            log_a = (-LRU_C * r) * sp[:, d * LRU_BLOCK:(d + 1) * LRU_BLOCK]
            a = jnp.exp(log_a)
            a_ref[d, pl.ds(r0, LRU_ROWS), :] = a
            b_ref[d, pl.ds(r0, LRU_ROWS), :] = jnp.sqrt(1.0 - a * a) * (ig * xc)
        return 0

    lax.fori_loop(0, nsteps, gates, 0)

    ngrp = t // 8
    row = lax.broadcasted_iota(jnp.int32, (8, LRU_BLOCK), 0)

    def scan(g, carry):
        cf, cr = carry
        rf = pl.multiple_of(g * 8, 8)
        rr = pl.multiple_of((ngrp - 1 - g) * 8, 8)
        a = a_ref[0, pl.ds(rf, 8), :]
        b = b_ref[0, pl.ds(rf, 8), :]
        for s in (1, 2, 4):
            keep = row >= s
            b = a * jnp.where(keep, pltpu.roll(b, s, axis=0), 0.0) + b
            a = a * jnp.where(keep, pltpu.roll(a, s, axis=0), 1.0)
        hf = a * cf + b
        hf_ref[pl.ds(rf, 8), :] = hf
        cf = jnp.broadcast_to(hf[7:8, :], (8, LRU_BLOCK))
        a = a_ref[1, pl.ds(rr, 8), :]
        b = b_ref[1, pl.ds(rr, 8), :]
        for s in (1, 2, 4):
            keep = row < 8 - s
            b = a * jnp.where(keep, pltpu.roll(b, 8 - s, axis=0), 0.0) + b
            a = a * jnp.where(keep, pltpu.roll(a, 8 - s, axis=0), 1.0)
        hr = a * cr + b
        hr_ref[pl.ds(rr, 8), :] = hr
        cr = jnp.broadcast_to(hr[0:1, :], (8, LRU_BLOCK))
        return cf, cr

    cf0 = jnp.broadcast_to(h0_ref[0, 0:1, :], (8, LRU_BLOCK))
    cr0 = jnp.broadcast_to(h0_ref[0, 1:2, :], (8, LRU_BLOCK))
    cf, cr = lax.fori_loop(0, ngrp, scan, (cf0, cr0), unroll=2)
    he_ref[0, 0:1, :] = cf[0:1, :]
    he_ref[0, 1:2, :] = cr[0:1, :]

    def finish(ci, _):
        r0 = pl.multiple_of(ci * LRU_ROWS, LRU_ROWS)
        y = (hf_ref[pl.ds(r0, LRU_ROWS), :] + hr_ref[pl.ds(r0, LRU_ROWS), :]) * _gelu_tanh(yr_ref[pl.ds(r0, LRU_ROWS), :])
        y_ref[pl.ds(r0, LRU_ROWS), :] = y.astype(y_ref.dtype)
        return 0

    lax.fori_loop(0, nsteps, finish, 0)


def _rglru(cfg, z, cw, cb, gw, gb, lam, h0, nseq, t, row0, y_prev=None):
    m = cfg.m
    assert row0 % t == 0
    sb = row0 // t
    aliased = y_prev is not None
    in_specs = [pl.BlockSpec((t, LRU_BLOCK), lambda b, n: (sb + b, Z_XR // LRU_BLOCK + n)),
                pl.BlockSpec((t, LRU_BLOCK), lambda b, n: (sb + b, Z_YR // LRU_BLOCK + n)),
                pl.BlockSpec((LRU_CONV, LRU_BLOCK), lambda b, n: (0, n)),
                pl.BlockSpec((1, LRU_BLOCK), lambda b, n: (0, n)),
                pl.BlockSpec((1, LRU_BLOCK, 4 * LRU_BLOCK), lambda b, n: (n, 0, 0)),
                pl.BlockSpec((1, 1, 4 * LRU_BLOCK), lambda b, n: (n, 0, 0)),
                pl.BlockSpec((1, 1, 2 * LRU_BLOCK), lambda b, n: (n, 0, 0)),
                pl.BlockSpec((1, N_DIR, LRU_BLOCK), lambda b, n: (b, 0, n))]
    args = [z, z, cw, cb, gw, gb, lam, h0]
    if aliased:
        in_specs = [pl.BlockSpec(memory_space=pl.ANY)] + in_specs
        args = [y_prev] + args
    return pl.pallas_call(
        functools.partial(_rglru_kernel, t=t, aliased=aliased),
        grid=(nseq, LRU_BLOCKS),
        in_specs=in_specs,
        out_specs=[pl.BlockSpec((t, LRU_BLOCK), lambda b, n: (sb + b, n)),
                   pl.BlockSpec((1, N_DIR, LRU_BLOCK), lambda b, n: (b, 0, n))],
        out_shape=[jax.ShapeDtypeStruct((m, LRU_WIDTH), BF16),
                   jax.ShapeDtypeStruct((nseq, N_DIR, LRU_WIDTH), F32)],
        scratch_shapes=[pltpu.VMEM((t + 2 * LRU_PAD, LRU_BLOCK), F32),
                        pltpu.VMEM((N_DIR, t, LRU_BLOCK), F32),
                        pltpu.VMEM((N_DIR, t, LRU_BLOCK), F32),
                        pltpu.VMEM((t, LRU_BLOCK), F32),
                        pltpu.VMEM((t, LRU_BLOCK), F32)],
        input_output_aliases={0: 0} if aliased else {},
        compiler_params=_params(("parallel", "parallel")),
        name="rglru",
    )(*args)


def _merge_kernel(ya_ref, yb_ref, yc_ref, w_ref, ma_ref, mb_ref, mc_ref, bm_ref, o_ref):
    acc = None
    for n, (y_ref, mg_ref) in enumerate(((ya_ref, ma_ref), (yb_ref, mb_ref), (yc_ref, mc_ref))):
        g = jax.nn.sigmoid(mg_ref[...] + bm_ref[n])
        term = g * jnp.dot(y_ref[...], w_ref[n], preferred_element_type=F32)
        acc = term if acc is None else acc + term
    o_ref[...] = acc.astype(o_ref.dtype)


def _merge(cfg, ya, yb, yc, wbr, z, bm, tm, tn):
    m = cfg.m
    nj = D_MODEL // tn
    y_spec = pl.BlockSpec((tm, BRANCH_W), lambda j, i: (i, 0))
    mg_spec = lambda n: pl.BlockSpec((tm, tn), lambda j, i: (i, Z_MG // tn + n * nj + j))
    return pl.pallas_call(
        _merge_kernel,
        grid=(nj, m // tm),
        in_specs=[y_spec, y_spec, y_spec,
                  pl.BlockSpec((N_BRANCH, BRANCH_W, tn), lambda j, i: (0, 0, j)),
                  mg_spec(0), mg_spec(1), mg_spec(2),
                  pl.BlockSpec((N_BRANCH, 1, tn), lambda j, i: (0, 0, j))],
        out_specs=pl.BlockSpec((tm, tn), lambda j, i: (i, j)),
        out_shape=jax.ShapeDtypeStruct((m, D_MODEL), BF16),
        compiler_params=_params(("parallel", "parallel")),
        name="merge",
    )(ya, yb, yc, wbr, z, z, z, bm)


def _matmul_res_kernel(a_ref, w_ref, x_ref, g_ref, o_ref):
    o_ref[...] = x_ref[...] + g_ref[0] * jnp.dot(a_ref[...], w_ref[...], preferred_element_type=F32)


def _matmul_res(cfg, a, w, x, gate, tm, tn):
    m, kdim = a.shape
    n = w.shape[1]
    return pl.pallas_call(
        _matmul_res_kernel,
        grid=(n // tn, m // tm),
        in_specs=[pl.BlockSpec((tm, kdim), lambda j, i: (i, 0)),
                  pl.BlockSpec((kdim, tn), lambda j, i: (0, j)),
                  pl.BlockSpec((tm, tn), lambda j, i: (i, j)),
                  pl.BlockSpec((1, 1, tn), lambda j, i: (_mod_row(cfg, i * tm), 0, j))],
        out_specs=pl.BlockSpec((tm, tn), lambda j, i: (i, j)),
        out_shape=jax.ShapeDtypeStruct((m, n), F32),
        compiler_params=_params(("parallel", "parallel")),
        name="matmul_res",
    )(a, w, x, gate)


FFN_PAD = 72
FFN_STEP_ELEMS = 128 * 128


def _conv_act_kernel(*refs, t, taps, ct, aliased):
    if aliased:
        refs = refs[1:]
    hg_ref, hu_ref, w_ref, b_ref, o_ref, pad_ref = refs
    rows = FFN_STEP_ELEMS // ct
    nsteps = t // rows
    zeros = jnp.zeros((FFN_PAD, ct), F32)
    pad_ref[0:FFN_PAD, :] = zeros
    pad_ref[FFN_PAD + t:2 * FFN_PAD + t, :] = zeros

    def fill(ci, _):
        r0 = pl.multiple_of(ci * rows, rows)
        pad_ref[pl.ds(r0 + FFN_PAD, rows), :] = hg_ref[pl.ds(r0, rows), :]
        return 0

    lax.fori_loop(0, nsteps, fill, 0)
    col = lax.broadcasted_iota(jnp.int32, (rows, ct), 0) % GRID_W

    def step(ci, _):
        r0 = ci * rows if isinstance(ci, int) else pl.multiple_of(ci * rows, rows)
        acc = b_ref[...]
        for shift, widx, dx in taps:
            xs = pad_ref[pl.ds(r0 + FFN_PAD + shift, rows), :]
            if dx < 0:
                xs = jnp.where(col >= -dx, xs, 0.0)
            elif dx > 0:
                xs = jnp.where(col < GRID_W - dx, xs, 0.0)
            acc = acc + xs * w_ref[widx:widx + 1, :]
        o_ref[pl.ds(r0, rows), :] = (_silu(acc) * hu_ref[pl.ds(r0, rows), :]).astype(o_ref.dtype)
        return 0

    if ct > 128:
        for ci in range(nsteps):
            step(ci, 0)
    else:
        lax.fori_loop(0, nsteps, step, 0)


def _conv_act(cfg, h, w9, bias, nseq, t, row0, on_grid, ct, prev=None):
    m = cfg.m
    rows = FFN_STEP_ELEMS // ct
    assert row0 % t == 0 and t % rows == 0 and D_FF % ct == 0 and rows % 16 == 0
    assert not on_grid or rows % GRID_W == 0
    sb = row0 // t
    p = FFN_CONV // 2
    if on_grid:
        taps = tuple(((dy - p) * GRID_W + (dx - p), dy * FFN_CONV + dx, dx - p)
                     for dy in range(FFN_CONV) for dx in range(FFN_CONV))
    else:
        taps = tuple((dx - p, p * FFN_CONV + dx, 0) for dx in range(FFN_CONV))
    aliased = prev is not None
    nct = D_FF // ct
    in_specs = [pl.BlockSpec((t, ct), lambda b, j: (sb + b, j)),
                pl.BlockSpec((t, ct), lambda b, j: (sb + b, nct + j)),
                pl.BlockSpec((FFN_CONV * FFN_CONV, ct), lambda b, j: (0, j)),
                pl.BlockSpec((1, ct), lambda b, j: (0, j))]
    args = [h, h, w9, bias]
    if aliased:
        in_specs = [pl.BlockSpec(memory_space=pl.ANY)] + in_specs
        args = [prev] + args
    return pl.pallas_call(
        functools.partial(_conv_act_kernel, t=t, taps=taps, ct=ct, aliased=aliased),
        grid=(nseq, nct),
        in_specs=in_specs,
        out_specs=pl.BlockSpec((t, ct), lambda b, j: (sb + b, j)),
        out_shape=jax.ShapeDtypeStruct((m, D_FF), BF16),
        scratch_shapes=[pltpu.VMEM((t + 2 * FFN_PAD, ct), F32)],
        input_output_aliases={0: 0} if aliased else {},
        compiler_params=_params(("parallel", "parallel")),
        name="conv_act",
    )(*args)


FD_TM = 512
FD_TK = 512
FD_HALO = GRID_W
FD_TOP = 8
FD_VMEM = 56 * 1024 * 1024


def _ffn_down_kernel(*refs, cfg, on_grid, seq_len, final_norm, aliased):
    refs = list(refs[1:] if aliased else refs)
    hg_ref, hu_ref = refs[:2]
    refs = refs[2:]
    if on_grid:
        hp_ref, hn_ref = refs[:2]
        refs = refs[2:]
    w9_ref, cb_ref, wd_ref, x_ref, g_ref = refs[:5]
    refs = refs[5:]
    if final_norm:
        nf_ref = refs[0]
        refs = refs[1:]
    o_ref, pad_ref, even_ref, odd_ref = refs
    i, k = pl.program_id(0), pl.program_id(1)
    nk = pl.num_programs(1) - 1
    tm, tk = FD_TM, FD_TK
    nslab = tk // 128
    base = FD_TOP + FD_HALO

    @pl.when(k == 0)
    def _():
        odd_ref[...] = jnp.zeros_like(odd_ref)
        o_ref[...] = jnp.zeros_like(o_ref)
        edge = jnp.zeros((nslab, FD_TOP, 128), F32)
        pad_ref[:, 0:FD_TOP, :] = edge
        pad_ref[:, base + tm + FD_HALO:base + tm + FD_HALO + FD_TOP, :] = edge
        if not on_grid:
            halo = jnp.zeros((nslab, FD_HALO, 128), F32)
            pad_ref[:, FD_TOP:base, :] = halo
            pad_ref[:, base + tm:base + tm + FD_HALO, :] = halo

    if on_grid:
        tiles_per_seq = seq_len // tm
        first = i % tiles_per_seq == 0
        last = i % tiles_per_seq == tiles_per_seq - 1

        @pl.when(first)
        def _():
            pad_ref[:, FD_TOP:base, :] = jnp.zeros((nslab, FD_HALO, 128), F32)

        @pl.when(jnp.logical_not(first))
        def _():
            for s in range(nslab):
                pad_ref[s, FD_TOP:base, :] = hp_ref[:, s * 128:(s + 1) * 128]

        @pl.when(last)
        def _():
            pad_ref[:, base + tm:base + tm + FD_HALO, :] = jnp.zeros((nslab, FD_HALO, 128), F32)

        @pl.when(jnp.logical_not(last))
        def _():
            for s in range(nslab):
                pad_ref[s, base + tm:base + tm + FD_HALO, :] = hn_ref[:, s * 128:(s + 1) * 128]

    for s in range(nslab):
        pad_ref[s, base:base + tm, :] = hg_ref[:, s * 128:(s + 1) * 128]

    p = FFN_CONV // 2
    if on_grid:
        taps = tuple(((dy - p) * GRID_W + (dx - p), dy * FFN_CONV + dx, dx - p)
                     for dy in range(FFN_CONV) for dx in range(FFN_CONV))
        period = GRID_W
    else:
        taps = tuple((dx - p, p * FFN_CONV + dx, dx - p) for dx in range(FFN_CONV))
        period = seq_len
    kk = pl.multiple_of(jnp.maximum(k - 1, 0) * tk, tk)
    d_out = o_ref.shape[1]
    col_tile = 256
    n_col = d_out // col_tile
    rows = 64
    w9 = w9_ref[0]
    cb = cb_ref[0]
    conv_blocks = [(r, s) for r in range(tm // rows) for s in range(nslab)]
    per_tile = -(-len(conv_blocks) // n_col)

    def step(cur_ref, nxt_ref):
        for t in range(n_col):
            cols = slice(t * col_tile, (t + 1) * col_tile)
            o_ref[:, cols] += jnp.dot(cur_ref[...], wd_ref[pl.ds(kk, tk), cols], preferred_element_type=F32)
            for r, s in conv_blocks[t * per_tile:(t + 1) * per_tile]:
                pos = (lax.broadcasted_iota(jnp.int32, (rows, 128), 0) + r * rows) % period
                lanes = slice(s * 128, (s + 1) * 128)
                acc = cb[:, lanes]
                for shift, widx, dx in taps:
                    xs = pad_ref[s, base + r * rows + shift:base + (r + 1) * rows + shift, :]
                    if dx < 0:
                        xs = jnp.where(pos >= -dx, xs, 0.0)
                    elif dx > 0:
                        xs = jnp.where(pos < period - dx, xs, 0.0)
                    acc = acc + xs * w9[widx:widx + 1, lanes]
                act = _silu(acc) * hu_ref[r * rows:(r + 1) * rows, lanes]
                nxt_ref[r * rows:(r + 1) * rows, lanes] = act.astype(BF16)

    @pl.when(k % 2 == 0)
    def _():
        step(odd_ref, even_ref)

    @pl.when(k % 2 == 1)
    def _():
        step(even_ref, odd_ref)

    @pl.when(k == nk)
    def _():
        def slab(si, _):
            r0 = pl.multiple_of(si * NORM_SLAB, NORM_SLAB)
            res = x_ref[pl.ds(r0, NORM_SLAB), :] + g_ref[0] * o_ref[pl.ds(r0, NORM_SLAB), :]
            if final_norm:
                res = res * lax.rsqrt(jnp.mean(res * res, axis=-1, keepdims=True) + EPS) * nf_ref[...]
            o_ref[pl.ds(r0, NORM_SLAB), :] = res
            return 0

        lax.fori_loop(0, tm // NORM_SLAB, slab, 0)


def _ffn_down(cfg, h, w9, bias, wd, x, gate, nseq, t, row0, on_grid, prev=None, norm_gain=None):
    m, d = x.shape
    tm, tk = FD_TM, FD_TK
    rows = nseq * t
    nk = D_FF // tk
    assert D_FF % tk == 0 and row0 % tm == 0 and rows % tm == 0
    assert (t % tm == 0 and tm % GRID_W == 0) if on_grid else tm % t == 0
    final_norm = norm_gain is not None
    aliased = prev is not None
    assert not (final_norm and aliased)
    rt0 = row0 // tm
    kc = lambda k: jnp.minimum(k, nk - 1)
    hpb = tm // FD_HALO
    in_specs = [pl.BlockSpec((tm, tk), lambda i, k: (rt0 + i, kc(k))),
                pl.BlockSpec((tm, tk), lambda i, k: (rt0 + i, nk + kc(k)))]
    args = [h, h]
    if on_grid:
        in_specs += [pl.BlockSpec((FD_HALO, tk), lambda i, k: (jnp.maximum((rt0 + i) * hpb - 1, 0), kc(k))),
                     pl.BlockSpec((FD_HALO, tk), lambda i, k: (jnp.minimum((rt0 + i + 1) * hpb, m // FD_HALO - 1), kc(k)))]
        args += [h, h]
    w9c = jnp.transpose(w9.reshape(FFN_CONV * FFN_CONV, nk, tk), (1, 0, 2))
    in_specs += [pl.BlockSpec((1, FFN_CONV * FFN_CONV, tk), lambda i, k: (kc(k), 0, 0)),
                 pl.BlockSpec((1, 1, tk), lambda i, k: (kc(k), 0, 0)),
                 pl.BlockSpec((D_FF, d), lambda i, k: (0, 0), pipeline_mode=pl.Buffered(1)),
                 pl.BlockSpec((tm, d), lambda i, k: (rt0 + i, 0)),
                 pl.BlockSpec((1, 1, d), lambda i, k: (_mod_row(cfg, (rt0 + i) * tm), 0, 0))]
    args += [w9c, bias.reshape(nk, 1, tk), wd, x, gate]
    if final_norm:
        in_specs.append(pl.BlockSpec((1, d), lambda i, k: (0, 0)))
        args.append(norm_gain)
    if aliased:
        in_specs = [pl.BlockSpec(memory_space=pl.ANY)] + in_specs
        args = [prev] + args
    out_rows, out_rt0 = (rows, 0) if final_norm else (m, rt0)
    return pl.pallas_call(
        functools.partial(_ffn_down_kernel, cfg=cfg, on_grid=on_grid, seq_len=t, final_norm=final_norm, aliased=aliased),
        grid=(rows // tm, nk + 1),
        in_specs=in_specs,
        out_specs=pl.BlockSpec((tm, d), lambda i, k: (out_rt0 + i, 0)),
        out_shape=jax.ShapeDtypeStruct((out_rows, d), F32),
        scratch_shapes=[pltpu.VMEM((tk // 128, tm + 2 * (FD_HALO + FD_TOP), 128), F32),
                        pltpu.VMEM((tm, tk), BF16),
                        pltpu.VMEM((tm, tk), BF16)],
        input_output_aliases={0: 0} if aliased else {},
        compiler_params=pltpu.CompilerParams(dimension_semantics=("parallel", "arbitrary"), vmem_limit_bytes=FD_VMEM),
        name="ffn_down",
    )(*args)


def _rmsnorm_kernel(x_ref, g_ref, o_ref):
    x = x_ref[...]
    o_ref[...] = x * lax.rsqrt(jnp.mean(x * x, axis=-1, keepdims=True) + EPS) * g_ref[...]


def _final_norm(x, gain, row0, rows, tm):
    d = x.shape[1]
    assert row0 % tm == 0 and rows % tm == 0
    return pl.pallas_call(
        _rmsnorm_kernel,
        grid=(rows // tm,),
        in_specs=[pl.BlockSpec((tm, d), lambda i: (row0 // tm + i, 0)),
                  pl.BlockSpec((1, d), lambda i: (0, 0))],
        out_specs=pl.BlockSpec((tm, d), lambda i: (i, 0)),
        out_shape=jax.ShapeDtypeStruct((rows, d), F32),
        compiler_params=_params(("parallel",)),
        name="final_norm",
    )(x, gain)


def _pack_w_in(w):
    s = _SRC
    cols = [w[:, s["mg"]:s["end"]], w[:, s["va"]:s["ga"]], w[:, s["ga"]:s["ra"]], w[:, s["qb"]:s["kb"]],
            w[:, s["kb"]:s["vb"]], w[:, s["vb"]:s["ob"]], w[:, s["ob"]:s["gb"]], w[:, s["xr"]:s["yr"]],
            w[:, s["yr"]:s["mg"]], w[:, s["qa"]:s["ka"]], w[:, s["ka"]:s["va"]], w[:, s["ra"]:s["qb"]],
            w[:, s["gb"]:s["xr"]]]
    used = sum(c.shape[1] for c in cols)
    cols.append(jnp.zeros((w.shape[0], NZ - used), w.dtype))
    return jnp.concatenate(cols, axis=1).astype(BF16)


def _pack_gate_w(gw, gb, lam):
    nb, blk = LRU_BLOCKS, LRU_BLOCK
    gw_p = jnp.transpose(gw, (2, 3, 0, 1, 4)).reshape(nb, blk, 4 * blk).astype(BF16)
    gb_p = jnp.transpose(gb.reshape(2, 2, nb, blk), (2, 0, 1, 3)).reshape(nb, 1, 4 * blk)
    lam_p = jnp.transpose(lam.reshape(2, nb, blk), (1, 0, 2)).reshape(nb, 1, 2 * blk)
    return gw_p, gb_p, lam_p


def _layer(cfg, x, mod, lp, states, final_gain=None):
    (n1, n2, w_in, gwa, gba, gng, mbif, mng, lcw, lcb, lgw, lgb, llam, wbr, bmg, wout, fup, fcw, fcb, fdown) = lp
    s_gla, s_mc, s_mn, s_mm, s_lru = states
    m = cfg.m
    sh1, sc1, g1, sh2, sc2, g2 = (mod[:, k * D_MODEL:(k + 1) * D_MODEL].reshape(MOD_ROWS, 1, D_MODEL) for k in range(6))

    proj_tm = PROJ_TM if (cfg.m_ctx % PROJ_TM == 0 and cfg.tl % PROJ_TM == 0) else 512
    z = _norm_matmul(cfg, x, n1.reshape(1, -1), sc1, sh1, _pack_w_in(w_in), proj_tm, 512)

    s0t = jnp.swapaxes(s_gla, -1, -2)
    o_f, sa_f = _gla(cfg, z, gwa[0], gba[0].reshape(1, -1), s0t[:, 0], False)
    y_a, sa_r = _gla(cfg, z, gwa[1], gba[1].reshape(1, -1), s0t[:, 1], True, o_f, gng.reshape(1, -1))
    y_a = _stitch(cfg, y_a)
    new_gla = jnp.swapaxes(jnp.stack([sa_f, sa_r], axis=1), -1, -2)

    g_rows = z[:, Z_SM + SM_GB:Z_SM + SM_GB + 4 * MLSTM_HEADS].T
    k_t = z[:, Z_KB:Z_KB + MLSTM_HEADS * MLSTM_DH].T
    b_col = mbif.reshape(1, -1)
    b_row = mbif.reshape(-1, 1)
    n0 = s_mn[:, :, :, None, :]
    m0 = jnp.broadcast_to(s_mm[:, :, :, None, None], s_mm.shape + (1, 128))
    h_f, cb_f, nb_f, mb_f = _mlstm(cfg, z, k_t, g_rows, b_col, b_row, s_mc[:, 0], n0[:, 0], m0[:, 0], False)
    y_b, cb_r, nb_r, mb_r = _mlstm(cfg, z, k_t, g_rows, b_col, b_row, s_mc[:, 1], n0[:, 1], m0[:, 1], True,
                                   h_f, mng.reshape(1, -1))
    y_b = _stitch(cfg, y_b)
    new_mc = jnp.stack([cb_f, cb_r], axis=1)
    new_mn = jnp.stack([nb_f[:, :, 0], nb_r[:, :, 0]], axis=1)
    new_mm = jnp.stack([mb_f[:, :, 0, 0], mb_r[:, :, 0, 0]], axis=1)

    gw_p, gb_p, lam_p = _pack_gate_w(lgw, lgb, llam)
    cbias = lcb.reshape(1, -1)
    y_c, new_lru = _rglru(cfg, z, lcw, cbias, gw_p, gb_p, lam_p, jnp.zeros((cfg.bc, N_DIR, LRU_WIDTH), F32),
                          cfg.bc, cfg.tc, 0)
    y_c, _ = _rglru(cfg, z, lcw, cbias, gw_p, gb_p, lam_p, s_lru, cfg.bl, cfg.tl, cfg.m_ctx, y_prev=y_c)

    merged = _merge(cfg, y_a, y_b, y_c, wbr.astype(BF16), z, bmg.reshape(N_BRANCH, 1, D_MODEL), 512, 1024)
    x = _matmul_res(cfg, merged, wout.astype(BF16), x, g1, 512, 1024)

    h = _norm_matmul(cfg, x, n2.reshape(1, -1), sc2, sh2, fup.astype(BF16), proj_tm, 512)
    w9 = fcw.reshape(FFN_CONV * FFN_CONV, D_FF)
    wd = fdown.astype(BF16)
    if final_gain is None:
        x_ctx = _ffn_down(cfg, h, w9, fcb, wd, x, g2, cfg.bc, cfg.tc, 0, False)
        x = _ffn_down(cfg, h, w9, fcb, wd, x, g2, cfg.bl, cfg.tl, cfg.m_ctx, True, prev=x_ctx)
    else:
        x = (_ffn_down(cfg, h, w9, fcb, wd, x, g2, cfg.bc, cfg.tc, 0, False, norm_gain=final_gain),
             _ffn_down(cfg, h, w9, fcb, wd, x, g2, cfg.bl, cfg.tl, cfg.m_ctx, True, norm_gain=final_gain))
    return x, (new_gla, new_mc, new_mn, new_mm, new_lru)


def kernel(x_prompt, x_sample, state_gla, state_mlstm_c, state_mlstm_n, state_mlstm_m, state_rglru, c, c_ctx, norm1_g, norm2_g, w_mod, b_mod, w_in, gla_w_alpha, gla_b_alpha, gla_norm_g, mlstm_b_if, mlstm_norm_g, lru_conv_w, lru_conv_b, lru_gate_w, lru_gate_b, lru_lambda, w_branch, b_merge, w_out, ffn_w_up, ffn_conv_w, ffn_conv_b, ffn_w_down, norm_f_g):
    bc, tc, d = x_prompt.shape
    bl, tl, _ = x_sample.shape
    cfg = Cfg(bc, tc, bl, tl)
    assert tc % TB == 0 and tl % TB == 0 and cfg.m_ctx % tl == 0 and 1 + bl <= MOD_ROWS
    assert bc % NSLOT == 0 and bl % NSLOT == 0
    depth = w_in.shape[0]

    x = jnp.concatenate([x_prompt.reshape(bc * tc, d), x_sample.reshape(bl * tl, d)], axis=0)
    c_all = jnp.concatenate([c_ctx[None, :], c, jnp.zeros((MOD_ROWS - 1 - bl, d), F32)], axis=0)
    mod = _modulation(c_all, w_mod, b_mod)

    new = []
    for l in range(depth):
        lp = (norm1_g[l], norm2_g[l], w_in[l], gla_w_alpha[l], gla_b_alpha[l], gla_norm_g[l], mlstm_b_if[l],
              mlstm_norm_g[l], lru_conv_w[l], lru_conv_b[l], lru_gate_w[l], lru_gate_b[l], lru_lambda[l],
              w_branch[l], b_merge[l], w_out[l], ffn_w_up[l], ffn_conv_w[l], ffn_conv_b[l], ffn_w_down[l])
        states = (state_gla[:, l], state_mlstm_c[:, l], state_mlstm_n[:, l], state_mlstm_m[:, l], state_rglru[:, l])
        x, st = _layer(cfg, x, mod[l], lp, states, norm_f_g.reshape(1, -1) if l == depth - 1 else None)
        new.append(st)

    y_prompt = x[0].reshape(bc, tc, d)
    y_sample = x[1].reshape(bl, tl, d)
    stacked = tuple(jnp.stack([new[l][k] for l in range(depth)], axis=1) for k in range(5))
    return (y_prompt, y_sample) + stacked
```

```python
import functools
from typing import NamedTuple

import numpy as np
import jax
import jax.numpy as jnp
from jax import lax
from jax.experimental import pallas as pl
from jax.experimental.pallas import tpu as pltpu

F32 = jnp.float32
BF16 = jnp.bfloat16

D_MODEL = 2048
DEPTH = 2
GRID_W = 64
N_DIR = 2
N_BRANCH = 3
BRANCH_W = 1024
GLA_HEADS = 4
GLA_DK = 128
GLA_DV = 256
GLA_RANK = 16
GLA_TAU = 16.0
MLSTM_HEADS = 4
MLSTM_DH = 256
CHUNK = 64
LRU_WIDTH = 1024
LRU_BLOCKS = 8
LRU_BLOCK = 128
LRU_CONV = 4
LRU_C = 8.0
D_FF = 5632
FFN_CONV = 3
EPS = 1e-6

_SRC = dict(qa=0, ka=512, va=1024, ga=2048, ra=3072, qb=3104, kb=4128, vb=5152, ob=6176, gb=7200,
            xr=7216, yr=8240, mg=9264, end=15408)
Z_MG, Z_VA, Z_GA, Z_QB, Z_KB, Z_VB, Z_OB, Z_XR, Z_YR, Z_QA, Z_KA, Z_SM = (
    0, 6144, 7168, 8192, 9216, 10240, 11264, 12288, 13312, 14336, 14848, 15360)
SM_W = 128
SM_RA, SM_GB = 0, 32
NZ = 15872
MOD_ROWS = 16

PROJ_TM = 1024
NORM_SLAB = 32
PROJ_SLAB = 64
TB = 256
NCH = TB // CHUNK
VMEM_LIMIT = 48 * 1024 * 1024


class Cfg(NamedTuple):
    bc: int
    tc: int
    bl: int
    tl: int

    @property
    def m_ctx(self):
        return self.bc * self.tc

    @property
    def m(self):
        return self.bc * self.tc + self.bl * self.tl


def _params(sem):
    return pltpu.CompilerParams(dimension_semantics=sem, vmem_limit_bytes=VMEM_LIMIT)


def _softplus(x):
    return jnp.maximum(x, 0.0) + jnp.log1p(jnp.exp(-jnp.abs(x)))


def _log_sigmoid(x):
    return -_softplus(-x)


def _silu(x):
    return x * jax.nn.sigmoid(x)


def _gelu_tanh(x):
    return 0.5 * x * (1.0 + jnp.tanh(np.sqrt(2.0 / np.pi).astype(np.float32) * (x + 0.044715 * (x * x * x))))


def _mod_row(cfg, row):
    return jnp.where(row < cfg.m_ctx, 0, 1 + jnp.maximum(row - cfg.m_ctx, 0) // cfg.tl)


def _mod_kernel(c_ref, w_ref, b_ref, o_ref):
    c = c_ref[...]
    a = _silu(c).astype(BF16)
    o_ref[0] = jnp.dot(a, w_ref[0].astype(BF16), preferred_element_type=F32) + b_ref[0]


def _modulation(c_all, w_mod, b_mod):
    depth, d, n = w_mod.shape
    tn = 512
    return pl.pallas_call(
        _mod_kernel,
        grid=(depth, n // tn),
        in_specs=[pl.BlockSpec((MOD_ROWS, d), lambda l, j: (0, 0)),
                  pl.BlockSpec((1, d, tn), lambda l, j: (l, 0, j)),
                  pl.BlockSpec((1, 1, tn), lambda l, j: (l, 0, j))],
        out_specs=pl.BlockSpec((1, MOD_ROWS, tn), lambda l, j: (l, 0, j)),
        out_shape=jax.ShapeDtypeStruct((depth, MOD_ROWS, n), F32),
        compiler_params=_params(("parallel", "parallel")),
        name="modulation",
    )(c_all, w_mod, b_mod.reshape(depth, 1, n))


def _norm_rows(x_ref, g_ref, sc_ref, sh_ref, u_ref, r0, nrows):
    x = x_ref[pl.ds(r0, nrows), :]
    y = x * lax.rsqrt(jnp.mean(x * x, axis=-1, keepdims=True) + EPS) * g_ref[...]
    u_ref[pl.ds(r0, nrows), :] = (y * (1.0 + sc_ref[0]) + sh_ref[0]).astype(BF16)


def _norm_matmul_kernel(x0_ref, xn_ref, g_ref, sc0_ref, sh0_ref, scn_ref, shn_ref, w_ref, o_ref, ua_ref, ub_ref, *, slab):
    i, j = pl.program_id(0), pl.program_id(1)
    tm = xn_ref.shape[0]

    @pl.when(jnp.logical_and(i == 0, j == 0))
    def _():
        def first(si, _):
            _norm_rows(x0_ref, g_ref, sc0_ref, sh0_ref, ua_ref, pl.multiple_of(si * NORM_SLAB, NORM_SLAB), NORM_SLAB)
            return 0

        lax.fori_loop(0, tm // NORM_SLAB, first, 0)

    r0 = pl.multiple_of(jnp.minimum(j, tm // slab - 1) * slab, slab)

    @pl.when(i % 2 == 0)
    def _():
        _norm_rows(xn_ref, g_ref, scn_ref, shn_ref, ub_ref, r0, slab)
        o_ref[...] = jnp.dot(ua_ref[...], w_ref[...], preferred_element_type=F32)

    @pl.when(i % 2 == 1)
    def _():
        _norm_rows(xn_ref, g_ref, scn_ref, shn_ref, ua_ref, r0, slab)
        o_ref[...] = jnp.dot(ub_ref[...], w_ref[...], preferred_element_type=F32)


def _norm_matmul(cfg, x, gain, sc, sh, w, tm, tn):
    m, d = x.shape
    n = w.shape[1]
    slab = PROJ_SLAB
    assert tm % slab == 0 and n // tn >= tm // slab
    nxt = lambda i: jnp.minimum(i + 1, m // tm - 1)
    mod0 = lambda i, j: (_mod_row(cfg, 0), 0, 0)
    modn = lambda i, j: (_mod_row(cfg, nxt(i) * tm), 0, 0)
    return pl.pallas_call(
        functools.partial(_norm_matmul_kernel, slab=slab),
        grid=(m // tm, n // tn),
        in_specs=[pl.BlockSpec((tm, d), lambda i, j: (0, 0), pipeline_mode=pl.Buffered(1)),
                  pl.BlockSpec((tm, d), lambda i, j: (nxt(i), 0)),
                  pl.BlockSpec((1, d), lambda i, j: (0, 0)),
                  pl.BlockSpec((1, 1, d), mod0),
                  pl.BlockSpec((1, 1, d), mod0),
                  pl.BlockSpec((1, 1, d), modn),
                  pl.BlockSpec((1, 1, d), modn),
                  pl.BlockSpec((d, tn), lambda i, j: (0, j))],
        out_specs=pl.BlockSpec((tm, tn), lambda i, j: (i, j)),
        out_shape=jax.ShapeDtypeStruct((m, n), F32),
        scratch_shapes=[pltpu.VMEM((tm, d), BF16), pltpu.VMEM((tm, d), BF16)],
        compiler_params=_params(("arbitrary", "arbitrary")),
        name="norm_matmul",
    )(x, x, gain, sc, sh, sc, sh, w)


NSLOT = 2


def _pair_pos(cfg, i):
    kc, kl = cfg.tc // TB, cfg.tl // TB
    nc = (cfg.bc // NSLOT) * kc
    is_ctx = i < nc
    il = jnp.maximum(i - nc, 0)
    j = jnp.where(is_ctx, i % kc, il % kl)
    k = jnp.where(is_ctx, kc, kl)
    s = jnp.where(is_ctx, i // kc, il // kl)
    return is_ctx, s, j, k


def _pair_row_block(cfg, reverse, slot, i):
    kc, kl = cfg.tc // TB, cfg.tl // TB
    is_ctx, s, j, k = _pair_pos(cfg, i)
    jj = (k - 1 - j) if reverse else j
    return jnp.where(is_ctx, (s + slot * (cfg.bc // NSLOT)) * kc + jj,
                     cfg.bc * kc + (s + slot * (cfg.bl // NSLOT)) * kl + jj)


def _pair_local_block(cfg, reverse, i):
    kc, kl = cfg.tc // TB, cfg.tl // TB
    is_ctx, s, j, k = _pair_pos(cfg, i)
    jj = (k - 1 - j) if reverse else j
    return jnp.where(is_ctx, s * kc + jj, (cfg.bc // NSLOT) * kc + s * kl + jj)


def _pair_lat_seq(cfg, i):
    is_ctx, s, _, _ = _pair_pos(cfg, i)
    return jnp.where(is_ctx, 0, jnp.minimum(s, cfg.bl // NSLOT - 1))


def _pair_ctx_seq(cfg, i):
    is_ctx, s, _, _ = _pair_pos(cfg, i)
    return jnp.where(is_ctx, s, cfg.bc // NSLOT - 1)


def _slot_of_tile(cfg, tm, i):
    nc, nl = cfg.m_ctx // tm // NSLOT, (cfg.m - cfg.m_ctx) // tm // NSLOT
    il = jnp.maximum(i - NSLOT * nc, 0)
    is_ctx = i < NSLOT * nc
    return jnp.where(is_ctx, i // nc, il // nl), jnp.where(is_ctx, i % nc, nc + il % nl)


def _tri(reverse, n=CHUNK):
    row = lax.broadcasted_iota(jnp.int32, (n, n), 0)
    col = lax.broadcasted_iota(jnp.int32, (n, n), 1)
    return (row <= col) if reverse else (row >= col)


def _head_norm(x, g):
    return x * lax.rsqrt(jnp.mean(x * x, axis=-1, keepdims=True) + EPS) * g


_NT = (((1,), (1,)), ((), ()))
_TN = (((0,), (0,)), ((), ()))


GLA_LOCKSTEP = 2


def _gla_kernel(*refs, cfg, reverse):
    it = iter(refs)
    blocks = [[next(it) for _ in range(4)] for _ in range(NSLOT)]
    wal_ref, bal_ref = next(it), next(it)
    init_ref = next(it)
    init = [init_ref.at[b] for b in range(NSLOT)]
    if reverse:
        gates = [next(it) for _ in range(NSLOT)]
        of_ref, gn_ref = next(it), next(it)
        epi = [[gates[b], of_ref.at[b]] for b in range(NSLOT)]
    out_ref, fin_ref = next(it), next(it)
    outs = [out_ref.at[b] for b in range(NSLOT)]
    fin = [fin_ref.at[b] for b in range(NSLOT)]
    st_ref, la_ref = next(it), next(it)
    is_ctx, _, j, k = _pair_pos(cfg, pl.program_id(0))
    nh = GLA_HEADS

    @pl.when(jnp.logical_and(j == 0, is_ctx))
    def _():
        st_ref[...] = jnp.zeros_like(st_ref)

    @pl.when(jnp.logical_and(j == 0, jnp.logical_not(is_ctx)))
    def _():
        for b in range(NSLOT):
            st_ref[b * nh:(b + 1) * nh] = init[b][0]

    d = 1 if reverse else 0
    slots = range(NSLOT)
    both = lambda f: [f(b) for b in slots]
    wal = wal_ref[...].astype(BF16)
    for b in slots:
        ra = blocks[b][3][:, SM_RA + d * GLA_RANK:SM_RA + (d + 1) * GLA_RANK].astype(BF16)
        pre = jnp.dot(ra, wal, preferred_element_type=F32) + bal_ref[...]
        la_ref[b] = _log_sigmoid(pre) * (1.0 / GLA_TAU)

    tri = _tri(reverse)
    tri_f = tri.astype(F32)
    last = 0 if reverse else CHUNK - 1
    qscale = GLA_DK ** -0.5
    for c in (range(NCH - 1, -1, -1) if reverse else range(NCH)):
        rows = slice(c * CHUNK, (c + 1) * CHUNK)
        cum = both(lambda b: jnp.dot(tri_f, la_ref[b, rows, :], precision=lax.Precision.HIGHEST,
                                     preferred_element_type=F32))
        tot = both(lambda b: cum[b][last:last + 1, :])
        e_q = both(lambda b: jnp.exp(cum[b]))
        e_k = both(lambda b: jnp.exp(-cum[b]))
        e_end = both(lambda b: jnp.exp(tot[b] - cum[b]))
        e_tot = both(lambda b: jnp.exp(tot[b]))
        for h0 in range(0, GLA_HEADS, GLA_LOCKSTEP):
            units = [(b, h0 + dh) for dh in range(GLA_LOCKSTEP) for b in slots]
            each = lambda f: [f(u, b, h) for u, (b, h) in enumerate(units)]
            kc = [slice(h * GLA_DK, (h + 1) * GLA_DK) for _, h in units]
            vc = [slice(h * GLA_DV, (h + 1) * GLA_DV) for _, h in units]
            sh = [b * nh + h for b, h in units]
            qh = each(lambda u, b, h: blocks[b][0][rows, kc[u]] * qscale)
            kh = each(lambda u, b, h: blocks[b][1][rows, kc[u]])
            vh = each(lambda u, b, h: blocks[b][2][rows, vc[u]].astype(BF16))
            q_in = each(lambda u, b, h: (qh[u] * e_q[b][:, kc[u]]).astype(BF16))
            k_in = each(lambda u, b, h: (kh[u] * e_k[b][:, kc[u]]).astype(BF16))
            att = each(lambda u, b, h: lax.dot_general(q_in[u], k_in[u], _NT, preferred_element_type=F32))
            att = each(lambda u, b, h: jnp.where(tri, att[u], 0.0).astype(BF16))
            st = each(lambda u, b, h: st_ref[sh[u]])
            o = each(lambda u, b, h: jnp.dot(att[u], vh[u], preferred_element_type=F32)
                     + lax.dot_general(q_in[u], st[u].astype(BF16), _NT, preferred_element_type=F32))
            k_end = each(lambda u, b, h: (kh[u] * e_end[b][:, kc[u]]).astype(BF16))
            upd = each(lambda u, b, h: lax.dot_general(vh[u], k_end[u], _TN, preferred_element_type=F32))
            for u, (b, h) in enumerate(units):
                st_ref[sh[u]] = e_tot[b][:, kc[u]] * st[u] + upd[u]
            for u, (b, h) in enumerate(units):
                if reverse:
                    ga_ref, of_ref = epi[b]
                    y = _head_norm(of_ref[rows, vc[u]] + o[u], gn_ref[:, vc[u]]) * _silu(ga_ref[rows, vc[u]])
                    outs[b][rows, vc[u]] = y.astype(outs[b].dtype)
                else:
                    outs[b][rows, vc[u]] = o[u]

    @pl.when(jnp.logical_and(is_ctx, j == k - 1))
    def _():
        for b in range(NSLOT):
            fin[b][0] = st_ref[b * nh:(b + 1) * nh]


def _gla(cfg, z, wal, bal, s0t, reverse, o_fwd=None, gnorm=None):
    m = cfg.m
    nh = GLA_HEADS
    hk, hv = nh * GLA_DK, nh * GLA_DV
    lb = functools.partial(_pair_local_block, cfg, reverse)
    in_specs, args = [], []
    for b in range(NSLOT):
        rb = functools.partial(_pair_row_block, cfg, reverse, b)
        in_specs += [pl.BlockSpec((TB, hk), lambda i, rb=rb: (rb(i), Z_QA // hk)),
                     pl.BlockSpec((TB, hk), lambda i, rb=rb: (rb(i), Z_KA // hk)),
                     pl.BlockSpec((TB, hv), lambda i, rb=rb: (rb(i), Z_VA // hv)),
                     pl.BlockSpec((TB, SM_W), lambda i, rb=rb: (rb(i), Z_SM // SM_W))]
        args += [z, z, z, z]
    in_specs += [pl.BlockSpec((GLA_RANK, hk), lambda i: (0, 0)), pl.BlockSpec((1, hk), lambda i: (0, 0))]
    args += [wal, bal]
    half_c, half_l = cfg.bc // NSLOT, cfg.bl // NSLOT
    in_specs.append(pl.BlockSpec((NSLOT, 1, nh, GLA_DV, GLA_DK), lambda i: (0, _pair_lat_seq(cfg, i), 0, 0, 0)))
    args.append(s0t.reshape(NSLOT, half_l, nh, GLA_DV, GLA_DK))
    if reverse:
        for b in range(NSLOT):
            rb = functools.partial(_pair_row_block, cfg, reverse, b)
            in_specs.append(pl.BlockSpec((TB, hv), lambda i, rb=rb: (rb(i), Z_GA // hv)))
            args.append(z)
        in_specs += [pl.BlockSpec((NSLOT, TB, hv), lambda i: (0, lb(i), 0)), pl.BlockSpec((1, hv), lambda i: (0, 0))]
        args += [o_fwd, gnorm]
    out, fin = pl.pallas_call(
        functools.partial(_gla_kernel, cfg=cfg, reverse=reverse),
        grid=(m // TB // NSLOT,),
        in_specs=in_specs,
        out_specs=[pl.BlockSpec((NSLOT, TB, hv), lambda i: (0, lb(i), 0)),
                   pl.BlockSpec((NSLOT, 1, nh, GLA_DV, GLA_DK), lambda i: (0, _pair_ctx_seq(cfg, i), 0, 0, 0))],
        out_shape=[jax.ShapeDtypeStruct((NSLOT, m // NSLOT, hv), BF16 if reverse else F32),
                   jax.ShapeDtypeStruct((NSLOT, half_c, nh, GLA_DV, GLA_DK), F32)],
        scratch_shapes=[pltpu.VMEM((NSLOT * nh, GLA_DV, GLA_DK), F32), pltpu.VMEM((NSLOT, TB, hk), F32)],
        compiler_params=_params(("arbitrary",)),
        name="gla_rev" if reverse else "gla_fwd",
    )(*args)
    return out, fin.reshape(cfg.bc, nh, GLA_DV, GLA_DK)


MCH = 128
NMC = TB // MCH
MLSTM_LOCKSTEP = 2


def _mlstm_kernel(*refs, cfg, reverse):
    it = iter(refs)
    blocks = [[next(it) for _ in range(6)] for _ in range(NSLOT)]
    bc_ref, br_ref = next(it), next(it)
    init_refs = [next(it) for _ in range(3)]
    init = [[r.at[b] for r in init_refs] for b in range(NSLOT)]
    if reverse:
        gates = [next(it) for _ in range(NSLOT)]
        hf_all, gn_ref = next(it), next(it)
        epi = [[gates[b], hf_all.at[b]] for b in range(NSLOT)]
    out_ref = next(it)
    outs = [out_ref.at[b] for b in range(NSLOT)]
    fin_refs = [next(it) for _ in range(3)]
    fin = [[r.at[b] for r in fin_refs] for b in range(NSLOT)]
    c_ref, n_ref, m_ref = next(it), next(it), next(it)
    is_ctx, _, j, k = _pair_pos(cfg, pl.program_id(0))
    nh = MLSTM_HEADS

    @pl.when(jnp.logical_and(j == 0, is_ctx))
    def _():
        c_ref[...] = jnp.zeros_like(c_ref)
        n_ref[...] = jnp.zeros_like(n_ref)
        m_ref[...] = jnp.zeros_like(m_ref)

    @pl.when(jnp.logical_and(j == 0, jnp.logical_not(is_ctx)))
    def _():
        for b in range(NSLOT):
            c_ref[b * nh:(b + 1) * nh] = init[b][0][0]
            n_ref[b * nh:(b + 1) * nh] = init[b][1][0]
            m_ref[b * nh:(b + 1) * nh] = init[b][2][0]

    d = 1 if reverse else 0
    tri = _tri(reverse, MCH)
    tri_f = tri.astype(F32)
    tri_tf = _tri(not reverse, MCH).astype(F32)
    last = 0 if reverse else MCH - 1
    ng = 2 * MLSTM_HEADS
    kscale = MLSTM_DH ** -0.5
    slots = range(NSLOT)
    both = lambda f: [f(b) for b in slots]
    hp = lax.Precision.HIGHEST
    for c in (range(NMC - 1, -1, -1) if reverse else range(NMC)):
        rows = slice(c * MCH, (c + 1) * MCH)
        g_col = both(lambda b: blocks[b][4][rows, SM_GB:SM_GB + 2 * ng] + bc_ref[...])
        g_row = both(lambda b: blocks[b][5][:, rows] + br_ref[...])
        fcum_col = both(lambda b: jnp.dot(tri_f, _log_sigmoid(g_col[b]), precision=hp, preferred_element_type=F32))
        fcum_row = both(lambda b: jnp.dot(_log_sigmoid(g_row[b]), tri_tf, precision=hp, preferred_element_type=F32))
        for h0 in range(0, MLSTM_HEADS, MLSTM_LOCKSTEP):
            units = [(b, h0 + dh) for dh in range(MLSTM_LOCKSTEP) for b in slots]
            each = lambda f: [f(u, b, h) for u, (b, h) in enumerate(units)]
            ii = [d * ng + h for _, h in units]
            fi = [d * ng + MLSTM_HEADS + h for _, h in units]
            hc = [slice(h * MLSTM_DH, (h + 1) * MLSTM_DH) for _, h in units]
            sh = [b * nh + h for b, h in units]
            f_col = each(lambda u, b, h: fcum_col[b][:, fi[u]:fi[u] + 1])
            f_row = each(lambda u, b, h: fcum_row[b][fi[u]:fi[u] + 1, :])
            i_col = each(lambda u, b, h: g_col[b][:, ii[u]:ii[u] + 1])
            i_row = each(lambda u, b, h: g_row[b][ii[u]:ii[u] + 1, :])
            m_prev = each(lambda u, b, h: m_ref[sh[u]][:, 0:1])
            dlog = each(lambda u, b, h: jnp.where(tri, f_col[u] - f_row[u] + i_row[u], -jnp.inf))
            prev = each(lambda u, b, h: f_col[u] + m_prev[u])
            mj = each(lambda u, b, h: jnp.maximum(prev[u], jnp.max(dlog[u], axis=-1, keepdims=True)))
            w = each(lambda u, b, h: jnp.exp(dlog[u] - mj[u]))
            wp = each(lambda u, b, h: jnp.exp(prev[u] - mj[u]))
            qh = each(lambda u, b, h: blocks[b][0][rows, hc[u]])
            kh = each(lambda u, b, h: blocks[b][1][rows, hc[u]] * kscale)
            vh = each(lambda u, b, h: blocks[b][2][rows, hc[u]].astype(BF16))
            s = each(lambda u, b, h: lax.dot_general(qh[u].astype(BF16), kh[u].astype(BF16), _NT,
                                                     preferred_element_type=F32) * w[u])
            qp = each(lambda u, b, h: qh[u] * wp[u])
            cm = each(lambda u, b, h: c_ref[sh[u]])
            nv = each(lambda u, b, h: n_ref[sh[u]])
            num = each(lambda u, b, h: jnp.dot(s[u].astype(BF16), vh[u], preferred_element_type=F32)
                       + jnp.dot(qp[u].astype(BF16), cm[u].astype(BF16), preferred_element_type=F32))
            den = each(lambda u, b, h: jnp.sum(s[u], axis=-1, keepdims=True)
                       + jnp.sum(qp[u] * nv[u], axis=-1, keepdims=True))
            hh = each(lambda u, b, h: num[u] / jnp.maximum(jnp.abs(den[u]), jnp.exp(-mj[u])))
            m_new = each(lambda u, b, h: mj[u][last:last + 1, :])
            tot = each(lambda u, b, h: f_col[u][last:last + 1, :])
            wl_col = each(lambda u, b, h: jnp.exp(tot[u] - f_col[u] + i_col[u] - m_new[u]))
            wl_row = each(lambda u, b, h: jnp.exp(tot[u] - f_row[u] + i_row[u] - m_new[u]))
            decay = each(lambda u, b, h: jnp.exp(tot[u] + m_prev[u] - m_new[u]))
            kw_t = each(lambda u, b, h: (blocks[b][3][hc[u], rows] * kscale) * wl_row[u])
            upd = each(lambda u, b, h: jnp.dot(kw_t[u].astype(BF16), vh[u], preferred_element_type=F32))
            for u, (b, h) in enumerate(units):
                c_ref[sh[u]] = decay[u] * cm[u] + upd[u]
                n_ref[sh[u]] = decay[u] * nv[u] + jnp.sum(kh[u] * wl_col[u], axis=0, keepdims=True)
                m_ref[sh[u]] = jnp.broadcast_to(m_new[u], (1, 128))
            for u, (b, h) in enumerate(units):
                if reverse:
                    ob_ref, hf_ref = epi[b]
                    y = jax.nn.sigmoid(ob_ref[rows, hc[u]]) * _head_norm(hf_ref[rows, hc[u]] + hh[u], gn_ref[:, hc[u]])
                    outs[b][rows, hc[u]] = y.astype(outs[b].dtype)
                else:
                    outs[b][rows, hc[u]] = hh[u]

    @pl.when(jnp.logical_and(is_ctx, j == k - 1))
    def _():
        for b in range(NSLOT):
            fin[b][0][0] = c_ref[b * nh:(b + 1) * nh]
            fin[b][1][0] = n_ref[b * nh:(b + 1) * nh]
            fin[b][2][0] = m_ref[b * nh:(b + 1) * nh]


def _mlstm(cfg, z, k_t, g_rows, b_col, b_row, c0, n0, m0, reverse, h_fwd=None, gnorm=None):
    m = cfg.m
    hd = MLSTM_HEADS * MLSTM_DH
    nh = MLSTM_HEADS
    lb = functools.partial(_pair_local_block, cfg, reverse)
    ctx = lambda i: (0, _pair_ctx_seq(cfg, i), 0, 0, 0)
    in_specs, args = [], []
    for b in range(NSLOT):
        rb = functools.partial(_pair_row_block, cfg, reverse, b)
        in_specs += [pl.BlockSpec((TB, hd), lambda i, rb=rb: (rb(i), Z_QB // hd)),
                     pl.BlockSpec((TB, hd), lambda i, rb=rb: (rb(i), Z_KB // hd)),
                     pl.BlockSpec((TB, hd), lambda i, rb=rb: (rb(i), Z_VB // hd)),
                     pl.BlockSpec((hd, TB), lambda i, rb=rb: (0, rb(i))),
                     pl.BlockSpec((TB, SM_W), lambda i, rb=rb: (rb(i), Z_SM // SM_W)),
                     pl.BlockSpec((4 * nh, TB), lambda i, rb=rb: (0, rb(i)))]
        args += [z, z, z, k_t, z, g_rows]
    in_specs += [pl.BlockSpec((1, 4 * nh), lambda i: (0, 0)), pl.BlockSpec((4 * nh, 1), lambda i: (0, 0))]
    args += [b_col, b_row]
    half_c, half_l = cfg.bc // NSLOT, cfg.bl // NSLOT
    lat = lambda i: (0, _pair_lat_seq(cfg, i), 0, 0, 0)
    state_dims = ((nh, MLSTM_DH, MLSTM_DH), (nh, 1, MLSTM_DH), (nh, 1, 128))
    in_specs += [pl.BlockSpec((NSLOT, 1) + sd, lat) for sd in state_dims]
    args += [s.reshape((NSLOT, half_l) + sd) for s, sd in zip((c0, n0, m0), state_dims)]
    if reverse:
        for b in range(NSLOT):
            rb = functools.partial(_pair_row_block, cfg, reverse, b)
            in_specs.append(pl.BlockSpec((TB, hd), lambda i, rb=rb: (rb(i), Z_OB // hd)))
            args.append(z)
        in_specs += [pl.BlockSpec((NSLOT, TB, hd), lambda i: (0, lb(i), 0)), pl.BlockSpec((1, hd), lambda i: (0, 0))]
        args += [h_fwd, gnorm]
    res = pl.pallas_call(
        functools.partial(_mlstm_kernel, cfg=cfg, reverse=reverse),
        grid=(m // TB // NSLOT,),
        in_specs=in_specs,
        out_specs=[pl.BlockSpec((NSLOT, TB, hd), lambda i: (0, lb(i), 0))]
                  + [pl.BlockSpec((NSLOT, 1) + sd, ctx) for sd in state_dims],
        out_shape=[jax.ShapeDtypeStruct((NSLOT, m // NSLOT, hd), BF16 if reverse else F32)]
                  + [jax.ShapeDtypeStruct((NSLOT, half_c) + sd, F32) for sd in state_dims],
        scratch_shapes=[pltpu.VMEM((NSLOT * nh, MLSTM_DH, MLSTM_DH), F32),
                        pltpu.VMEM((NSLOT * nh, 1, MLSTM_DH), F32),
                        pltpu.VMEM((NSLOT * nh, 1, 128), F32)],
        compiler_params=_params(("arbitrary",)),
        name="mlstm_rev" if reverse else "mlstm_fwd",
    )(*args)
    return (res[0],) + tuple(r.reshape((cfg.bc,) + sd) for r, sd in zip(res[1:], state_dims))


LRU_ROWS = 256
LRU_PAD = 8


def _rglru_kernel(*refs, t, aliased):
    if aliased:
        refs = refs[1:]
    (xr_ref, yr_ref, cw_ref, cb_ref, gw_ref, gb_ref, lam_ref, h0_ref, y_ref, he_ref,
     pad_ref, a_ref, b_ref, hf_ref, hr_ref) = refs
    nsteps = t // LRU_ROWS
    zeros = jnp.zeros((LRU_PAD, LRU_BLOCK), F32)
    pad_ref[0:LRU_PAD, :] = zeros
    pad_ref[LRU_PAD + t:2 * LRU_PAD + t, :] = zeros

    def fill(ci, _):
        r0 = pl.multiple_of(ci * LRU_ROWS, LRU_ROWS)
        pad_ref[pl.ds(r0 + LRU_PAD, LRU_ROWS), :] = xr_ref[pl.ds(r0, LRU_ROWS), :]
        return 0

    lax.fori_loop(0, nsteps, fill, 0)

    sp = _softplus(-lam_ref[0])
    left = LRU_CONV // 2

    def gates(ci, _):
        r0 = pl.multiple_of(ci * LRU_ROWS, LRU_ROWS)
        xc = cb_ref[...]
        for tap in range(LRU_CONV):
            xc = xc + cw_ref[tap:tap + 1, :] * pad_ref[pl.ds(r0 + LRU_PAD - left + tap, LRU_ROWS), :]
        pre = jnp.dot(xc.astype(BF16), gw_ref[0], preferred_element_type=F32) + gb_ref[0]
        for d in range(N_DIR):
            o = d * 2 * LRU_BLOCK
            r = jax.nn.sigmoid(pre[:, o:o + LRU_BLOCK])
            ig = jax.nn.sigmoid(pre[:, o + LRU_BLOCK:o + 2 * LRU_BLOCK])
            log_a = (-LRU_C * r) * sp[:, d * LRU_BLOCK:(d + 1) * LRU_BLOCK]
            a = jnp.exp(log_a)
            a_ref[d, pl.ds(r0, LRU_ROWS), :] = a
            b_ref[d, pl.ds(r0, LRU_ROWS), :] = jnp.sqrt(1.0 - a * a) * (ig * xc)
        return 0

    lax.fori_loop(0, nsteps, gates, 0)

    ngrp = t // 8
    row = lax.broadcasted_iota(jnp.int32, (8, LRU_BLOCK), 0)

    def scan(g, carry):
        starts = (pl.multiple_of(g * 8, 8), pl.multiple_of((ngrp - 1 - g) * 8, 8))
        both = lambda f: [f(d) for d in range(N_DIR)]
        a = both(lambda d: a_ref[d, pl.ds(starts[d], 8), :])
        b = both(lambda d: b_ref[d, pl.ds(starts[d], 8), :])
        for s in (1, 2, 4):
            keep = (row >= s, row < 8 - s)
            shift = (s, 8 - s)
            b = both(lambda d: a[d] * jnp.where(keep[d], pltpu.roll(b[d], shift[d], axis=0), 0.0) + b[d])
            a = both(lambda d: a[d] * jnp.where(keep[d], pltpu.roll(a[d], shift[d], axis=0), 1.0))
        h = both(lambda d: a[d] * carry[d] + b[d])
        hf_ref[pl.ds(starts[0], 8), :] = h[0]
        hr_ref[pl.ds(starts[1], 8), :] = h[1]
        return (jnp.broadcast_to(h[0][7:8, :], (8, LRU_BLOCK)), jnp.broadcast_to(h[1][0:1, :], (8, LRU_BLOCK)))

    cf0 = jnp.broadcast_to(h0_ref[0, 0:1, :], (8, LRU_BLOCK))
    cr0 = jnp.broadcast_to(h0_ref[0, 1:2, :], (8, LRU_BLOCK))
    cf, cr = lax.fori_loop(0, ngrp, scan, (cf0, cr0), unroll=2)
    he_ref[0, 0:1, :] = cf[0:1, :]
    he_ref[0, 1:2, :] = cr[0:1, :]

    def finish(ci, _):
        r0 = pl.multiple_of(ci * LRU_ROWS, LRU_ROWS)
        y = (hf_ref[pl.ds(r0, LRU_ROWS), :] + hr_ref[pl.ds(r0, LRU_ROWS), :]) * _gelu_tanh(yr_ref[pl.ds(r0, LRU_ROWS), :])
        y_ref[pl.ds(r0, LRU_ROWS), :] = y.astype(y_ref.dtype)
        return 0

    lax.fori_loop(0, nsteps, finish, 0)


def _rglru(cfg, z, cw, cb, gw, gb, lam, h0, nseq, t, row0, y_prev=None):
    m = cfg.m
    assert row0 % t == 0
    sb = row0 // t
    aliased = y_prev is not None
    in_specs = [pl.BlockSpec((t, LRU_BLOCK), lambda b, n: (sb + b, Z_XR // LRU_BLOCK + n)),
                pl.BlockSpec((t, LRU_BLOCK), lambda b, n: (sb + b, Z_YR // LRU_BLOCK + n)),
                pl.BlockSpec((LRU_CONV, LRU_BLOCK), lambda b, n: (0, n)),
                pl.BlockSpec((1, LRU_BLOCK), lambda b, n: (0, n)),
                pl.BlockSpec((1, LRU_BLOCK, 4 * LRU_BLOCK), lambda b, n: (n, 0, 0)),
                pl.BlockSpec((1, 1, 4 * LRU_BLOCK), lambda b, n: (n, 0, 0)),
                pl.BlockSpec((1, 1, 2 * LRU_BLOCK), lambda b, n: (n, 0, 0)),
                pl.BlockSpec((1, N_DIR, LRU_BLOCK), lambda b, n: (b, 0, n))]
    args = [z, z, cw, cb, gw, gb, lam, h0]
    if aliased:
        in_specs = [pl.BlockSpec(memory_space=pl.ANY)] + in_specs
        args = [y_prev] + args
    return pl.pallas_call(
        functools.partial(_rglru_kernel, t=t, aliased=aliased),
        grid=(nseq, LRU_BLOCKS),
        in_specs=in_specs,
        out_specs=[pl.BlockSpec((t, LRU_BLOCK), lambda b, n: (sb + b, n)),
                   pl.BlockSpec((1, N_DIR, LRU_BLOCK), lambda b, n: (b, 0, n))],
        out_shape=[jax.ShapeDtypeStruct((m, LRU_WIDTH), BF16),
                   jax.ShapeDtypeStruct((nseq, N_DIR, LRU_WIDTH), F32)],
        scratch_shapes=[pltpu.VMEM((t + 2 * LRU_PAD, LRU_BLOCK), F32),
                        pltpu.VMEM((N_DIR, t, LRU_BLOCK), F32),
                        pltpu.VMEM((N_DIR, t, LRU_BLOCK), F32),
                        pltpu.VMEM((t, LRU_BLOCK), F32),
                        pltpu.VMEM((t, LRU_BLOCK), F32)],
        input_output_aliases={0: 0} if aliased else {},
        compiler_params=_params(("parallel", "parallel")),
        name="rglru",
    )(*args)


def _merge_kernel(ya_ref, yb_ref, yc_ref, w_ref, ma_ref, mb_ref, mc_ref, bm_ref, o_ref):
    acc = None
    for n, (y_ref, mg_ref) in enumerate(((ya_ref, ma_ref), (yb_ref, mb_ref), (yc_ref, mc_ref))):
        g = jax.nn.sigmoid(mg_ref[...] + bm_ref[n])
        term = g * jnp.dot(y_ref[...], w_ref[n], preferred_element_type=F32)
        acc = term if acc is None else acc + term
    o_ref[...] = acc.astype(o_ref.dtype)


def _merge(cfg, ya, yb, yc, wbr, z, bm, tm, tn):
    m = cfg.m
    nj = D_MODEL // tn
    assert (cfg.m_ctx // NSLOT) % tm == 0 and ((m - cfg.m_ctx) // NSLOT) % tm == 0
    slot_spec = pl.BlockSpec((None, tm, BRANCH_W), lambda j, i: _slot_of_tile(cfg, tm, i) + (0,))
    y_spec = pl.BlockSpec((tm, BRANCH_W), lambda j, i: (i, 0))
    mg_spec = lambda n: pl.BlockSpec((tm, tn), lambda j, i: (i, Z_MG // tn + n * nj + j))
    return pl.pallas_call(
        _merge_kernel,
        grid=(nj, m // tm),
        in_specs=[slot_spec, slot_spec, y_spec,
                  pl.BlockSpec((N_BRANCH, BRANCH_W, tn), lambda j, i: (0, 0, j)),
                  mg_spec(0), mg_spec(1), mg_spec(2),
                  pl.BlockSpec((N_BRANCH, 1, tn), lambda j, i: (0, 0, j))],
        out_specs=pl.BlockSpec((tm, tn), lambda j, i: (i, j)),
        out_shape=jax.ShapeDtypeStruct((m, D_MODEL), BF16),
        compiler_params=_params(("parallel", "parallel")),
        name="merge",
    )(ya, yb, yc, wbr, z, z, z, bm)


def _matmul_res_kernel(a_ref, w_ref, x_ref, g_ref, o_ref):
    o_ref[...] = x_ref[...] + g_ref[0] * jnp.dot(a_ref[...], w_ref[...], preferred_element_type=F32)


def _matmul_res(cfg, a, w, x, gate, tm, tn):
    m, kdim = a.shape
    n = w.shape[1]
    return pl.pallas_call(
        _matmul_res_kernel,
        grid=(n // tn, m // tm),
        in_specs=[pl.BlockSpec((tm, kdim), lambda j, i: (i, 0)),
                  pl.BlockSpec((kdim, tn), lambda j, i: (0, j)),
                  pl.BlockSpec((tm, tn), lambda j, i: (i, j)),
                  pl.BlockSpec((1, 1, tn), lambda j, i: (_mod_row(cfg, i * tm), 0, j))],
        out_specs=pl.BlockSpec((tm, tn), lambda j, i: (i, j)),
        out_shape=jax.ShapeDtypeStruct((m, n), F32),
        compiler_params=_params(("parallel", "parallel")),
        name="matmul_res",
    )(a, w, x, gate)


FFN_PAD = 72
FFN_STEP_ELEMS = 128 * 128


def _conv_act_kernel(*refs, t, taps, ct, aliased):
    if aliased:
        refs = refs[1:]
    hg_ref, hu_ref, w_ref, b_ref, o_ref, pad_ref = refs
    rows = FFN_STEP_ELEMS // ct
    nsteps = t // rows
    zeros = jnp.zeros((FFN_PAD, ct), F32)
    pad_ref[0:FFN_PAD, :] = zeros
    pad_ref[FFN_PAD + t:2 * FFN_PAD + t, :] = zeros

    def fill(ci, _):
        r0 = pl.multiple_of(ci * rows, rows)
        pad_ref[pl.ds(r0 + FFN_PAD, rows), :] = hg_ref[pl.ds(r0, rows), :]
        return 0

    lax.fori_loop(0, nsteps, fill, 0)
    col = lax.broadcasted_iota(jnp.int32, (rows, ct), 0) % GRID_W

    def step(ci, _):
        r0 = ci * rows if isinstance(ci, int) else pl.multiple_of(ci * rows, rows)
        acc = b_ref[...]
        for shift, widx, dx in taps:
            xs = pad_ref[pl.ds(r0 + FFN_PAD + shift, rows), :]
            if dx < 0:
                xs = jnp.where(col >= -dx, xs, 0.0)
            elif dx > 0:
                xs = jnp.where(col < GRID_W - dx, xs, 0.0)
            acc = acc + xs * w_ref[widx:widx + 1, :]
        o_ref[pl.ds(r0, rows), :] = (_silu(acc) * hu_ref[pl.ds(r0, rows), :]).astype(o_ref.dtype)
        return 0

    if ct > 128:
        for ci in range(nsteps):
            step(ci, 0)
    else:
        lax.fori_loop(0, nsteps, step, 0)


def _conv_act(cfg, h, w9, bias, nseq, t, row0, on_grid, ct, prev=None):
    m = cfg.m
    rows = FFN_STEP_ELEMS // ct
    assert row0 % t == 0 and t % rows == 0 and D_FF % ct == 0 and rows % 16 == 0
    assert not on_grid or rows % GRID_W == 0
    sb = row0 // t
    p = FFN_CONV // 2
    if on_grid:
        taps = tuple(((dy - p) * GRID_W + (dx - p), dy * FFN_CONV + dx, dx - p)
                     for dy in range(FFN_CONV) for dx in range(FFN_CONV))
    else:
        taps = tuple((dx - p, p * FFN_CONV + dx, 0) for dx in range(FFN_CONV))
    aliased = prev is not None
    nct = D_FF // ct
    in_specs = [pl.BlockSpec((t, ct), lambda b, j: (sb + b, j)),
                pl.BlockSpec((t, ct), lambda b, j: (sb + b, nct + j)),
                pl.BlockSpec((FFN_CONV * FFN_CONV, ct), lambda b, j: (0, j)),
                pl.BlockSpec((1, ct), lambda b, j: (0, j))]
    args = [h, h, w9, bias]
    if aliased:
        in_specs = [pl.BlockSpec(memory_space=pl.ANY)] + in_specs
        args = [prev] + args
    return pl.pallas_call(
        functools.partial(_conv_act_kernel, t=t, taps=taps, ct=ct, aliased=aliased),
        grid=(nseq, nct),
        in_specs=in_specs,
        out_specs=pl.BlockSpec((t, ct), lambda b, j: (sb + b, j)),
        out_shape=jax.ShapeDtypeStruct((m, D_FF), BF16),
        scratch_shapes=[pltpu.VMEM((t + 2 * FFN_PAD, ct), F32)],
        input_output_aliases={0: 0} if aliased else {},
        compiler_params=_params(("parallel", "parallel")),
        name="conv_act",
    )(*args)


FD_TM = 512
FD_TK = 512
FD_HALO = GRID_W
FD_TOP = 8
FD_VMEM = 56 * 1024 * 1024


def _ffn_down_kernel(*refs, cfg, on_grid, seq_len, final_norm, aliased):
    refs = list(refs[1:] if aliased else refs)
    hg_ref, hu_ref = refs[:2]
    refs = refs[2:]
    if on_grid:
        hp_ref, hn_ref = refs[:2]
        refs = refs[2:]
    w9_ref, cb_ref, wd_ref, x_ref, g_ref = refs[:5]
    refs = refs[5:]
    if final_norm:
        nf_ref = refs[0]
        refs = refs[1:]
    o_ref, pad_ref, even_ref, odd_ref = refs
    i, k = pl.program_id(0), pl.program_id(1)
    nk = pl.num_programs(1) - 1
    tm, tk = FD_TM, FD_TK
    nslab = tk // 128
    base = FD_TOP + FD_HALO

    @pl.when(k == 0)
    def _():
        odd_ref[...] = jnp.zeros_like(odd_ref)
        o_ref[...] = jnp.zeros_like(o_ref)
        edge = jnp.zeros((nslab, FD_TOP, 128), F32)
        pad_ref[:, 0:FD_TOP, :] = edge
        pad_ref[:, base + tm + FD_HALO:base + tm + FD_HALO + FD_TOP, :] = edge
        if not on_grid:
            halo = jnp.zeros((nslab, FD_HALO, 128), F32)
            pad_ref[:, FD_TOP:base, :] = halo
            pad_ref[:, base + tm:base + tm + FD_HALO, :] = halo

    if on_grid:
        tiles_per_seq = seq_len // tm
        first = i % tiles_per_seq == 0
        last = i % tiles_per_seq == tiles_per_seq - 1

        @pl.when(first)
        def _():
            pad_ref[:, FD_TOP:base, :] = jnp.zeros((nslab, FD_HALO, 128), F32)

        @pl.when(jnp.logical_not(first))
        def _():
            for s in range(nslab):
                pad_ref[s, FD_TOP:base, :] = hp_ref[:, s * 128:(s + 1) * 128]

        @pl.when(last)
        def _():
            pad_ref[:, base + tm:base + tm + FD_HALO, :] = jnp.zeros((nslab, FD_HALO, 128), F32)

        @pl.when(jnp.logical_not(last))
        def _():
            for s in range(nslab):
                pad_ref[s, base + tm:base + tm + FD_HALO, :] = hn_ref[:, s * 128:(s + 1) * 128]

    for s in range(nslab):
        pad_ref[s, base:base + tm, :] = hg_ref[:, s * 128:(s + 1) * 128]

    p = FFN_CONV // 2
    if on_grid:
        taps = tuple(((dy - p) * GRID_W + (dx - p), dy * FFN_CONV + dx, dx - p)
                     for dy in range(FFN_CONV) for dx in range(FFN_CONV))
        period = GRID_W
    else:
        taps = tuple((dx - p, p * FFN_CONV + dx, dx - p) for dx in range(FFN_CONV))
        period = seq_len
    kk = pl.multiple_of(jnp.maximum(k - 1, 0) * tk, tk)
    d_out = o_ref.shape[1]
    col_tile = 256
    n_col = d_out // col_tile
    rows = 64
    w9 = w9_ref[0]
    cb = cb_ref[0]
    conv_blocks = [(r, s) for r in range(tm // rows) for s in range(nslab)]
    per_tile = -(-len(conv_blocks) // n_col)

    def step(cur_ref, nxt_ref):
        for t in range(n_col):
            cols = slice(t * col_tile, (t + 1) * col_tile)
            o_ref[:, cols] += jnp.dot(cur_ref[...], wd_ref[pl.ds(kk, tk), cols], preferred_element_type=F32)
            for r, s in conv_blocks[t * per_tile:(t + 1) * per_tile]:
                pos = (lax.broadcasted_iota(jnp.int32, (rows, 128), 0) + r * rows) % period
                lanes = slice(s * 128, (s + 1) * 128)
                acc = cb[:, lanes]
                for shift, widx, dx in taps:
                    xs = pad_ref[s, base + r * rows + shift:base + (r + 1) * rows + shift, :]
                    if dx < 0:
                        xs = jnp.where(pos >= -dx, xs, 0.0)
                    elif dx > 0:
                        xs = jnp.where(pos < period - dx, xs, 0.0)
                    acc = acc + xs * w9[widx:widx + 1, lanes]
                act = _silu(acc) * hu_ref[r * rows:(r + 1) * rows, lanes]
                nxt_ref[r * rows:(r + 1) * rows, lanes] = act.astype(BF16)

    @pl.when(k % 2 == 0)
    def _():
        step(odd_ref, even_ref)

    @pl.when(k % 2 == 1)
    def _():
        step(even_ref, odd_ref)

    @pl.when(k == nk)
    def _():
        def slab(si, _):
            r0 = pl.multiple_of(si * NORM_SLAB, NORM_SLAB)
            res = x_ref[pl.ds(r0, NORM_SLAB), :] + g_ref[0] * o_ref[pl.ds(r0, NORM_SLAB), :]
            if final_norm:
                res = res * lax.rsqrt(jnp.mean(res * res, axis=-1, keepdims=True) + EPS) * nf_ref[...]
            o_ref[pl.ds(r0, NORM_SLAB), :] = res
            return 0

        lax.fori_loop(0, tm // NORM_SLAB, slab, 0)


def _ffn_down(cfg, h, w9, bias, wd, x, gate, nseq, t, row0, on_grid, prev=None, norm_gain=None):
    m, d = x.shape
    tm, tk = FD_TM, FD_TK
    rows = nseq * t
    nk = D_FF // tk
    assert D_FF % tk == 0 and row0 % tm == 0 and rows % tm == 0
    assert (t % tm == 0 and tm % GRID_W == 0) if on_grid else tm % t == 0
    final_norm = norm_gain is not None
    aliased = prev is not None
    assert not (final_norm and aliased)
    rt0 = row0 // tm
    kc = lambda k: jnp.minimum(k, nk - 1)
    hpb = tm // FD_HALO
    in_specs = [pl.BlockSpec((tm, tk), lambda i, k: (rt0 + i, kc(k))),
                pl.BlockSpec((tm, tk), lambda i, k: (rt0 + i, nk + kc(k)))]
    args = [h, h]
    if on_grid:
        in_specs += [pl.BlockSpec((FD_HALO, tk), lambda i, k: (jnp.maximum((rt0 + i) * hpb - 1, 0), kc(k))),
                     pl.BlockSpec((FD_HALO, tk), lambda i, k: (jnp.minimum((rt0 + i + 1) * hpb, m // FD_HALO - 1), kc(k)))]
        args += [h, h]
    w9c = jnp.transpose(w9.reshape(FFN_CONV * FFN_CONV, nk, tk), (1, 0, 2))
    in_specs += [pl.BlockSpec((1, FFN_CONV * FFN_CONV, tk), lambda i, k: (kc(k), 0, 0)),
                 pl.BlockSpec((1, 1, tk), lambda i, k: (kc(k), 0, 0)),
                 pl.BlockSpec((D_FF, d), lambda i, k: (0, 0), pipeline_mode=pl.Buffered(1)),
                 pl.BlockSpec((tm, d), lambda i, k: (rt0 + i, 0)),
                 pl.BlockSpec((1, 1, d), lambda i, k: (_mod_row(cfg, (rt0 + i) * tm), 0, 0))]
    args += [w9c, bias.reshape(nk, 1, tk), wd, x, gate]
    if final_norm:
        in_specs.append(pl.BlockSpec((1, d), lambda i, k: (0, 0)))
        args.append(norm_gain)
    if aliased:
        in_specs = [pl.BlockSpec(memory_space=pl.ANY)] + in_specs
        args = [prev] + args
    out_rows, out_rt0 = (rows, 0) if final_norm else (m, rt0)
    return pl.pallas_call(
        functools.partial(_ffn_down_kernel, cfg=cfg, on_grid=on_grid, seq_len=t, final_norm=final_norm, aliased=aliased),
        grid=(rows // tm, nk + 1),
        in_specs=in_specs,
        out_specs=pl.BlockSpec((tm, d), lambda i, k: (out_rt0 + i, 0)),
        out_shape=jax.ShapeDtypeStruct((out_rows, d), F32),
        scratch_shapes=[pltpu.VMEM((tk // 128, tm + 2 * (FD_HALO + FD_TOP), 128), F32),
                        pltpu.VMEM((tm, tk), BF16),
                        pltpu.VMEM((tm, tk), BF16)],
        input_output_aliases={0: 0} if aliased else {},
        compiler_params=pltpu.CompilerParams(dimension_semantics=("parallel", "arbitrary"), vmem_limit_bytes=FD_VMEM),
        name="ffn_down",
    )(*args)


def _rmsnorm_kernel(x_ref, g_ref, o_ref):
    x = x_ref[...]
    o_ref[...] = x * lax.rsqrt(jnp.mean(x * x, axis=-1, keepdims=True) + EPS) * g_ref[...]


def _final_norm(x, gain, row0, rows, tm):
    d = x.shape[1]
    assert row0 % tm == 0 and rows % tm == 0
    return pl.pallas_call(
        _rmsnorm_kernel,
        grid=(rows // tm,),
        in_specs=[pl.BlockSpec((tm, d), lambda i: (row0 // tm + i, 0)),
                  pl.BlockSpec((1, d), lambda i: (0, 0))],
        out_specs=pl.BlockSpec((tm, d), lambda i: (i, 0)),
        out_shape=jax.ShapeDtypeStruct((rows, d), F32),
        compiler_params=_params(("parallel",)),
        name="final_norm",
    )(x, gain)


def _pack_w_in(w):
    s = _SRC
    cols = [w[:, s["mg"]:s["end"]], w[:, s["va"]:s["ga"]], w[:, s["ga"]:s["ra"]], w[:, s["qb"]:s["kb"]],
            w[:, s["kb"]:s["vb"]], w[:, s["vb"]:s["ob"]], w[:, s["ob"]:s["gb"]], w[:, s["xr"]:s["yr"]],
            w[:, s["yr"]:s["mg"]], w[:, s["qa"]:s["ka"]], w[:, s["ka"]:s["va"]], w[:, s["ra"]:s["qb"]],
            w[:, s["gb"]:s["xr"]]]
    used = sum(c.shape[1] for c in cols)
    cols.append(jnp.zeros((w.shape[0], NZ - used), w.dtype))
    return jnp.concatenate(cols, axis=1).astype(BF16)


def _pack_gate_w(gw, gb, lam):
    nb, blk = LRU_BLOCKS, LRU_BLOCK
    gw_p = jnp.transpose(gw, (2, 3, 0, 1, 4)).reshape(nb, blk, 4 * blk).astype(BF16)
    gb_p = jnp.transpose(gb.reshape(2, 2, nb, blk), (2, 0, 1, 3)).reshape(nb, 1, 4 * blk)
    lam_p = jnp.transpose(lam.reshape(2, nb, blk), (1, 0, 2)).reshape(nb, 1, 2 * blk)
    return gw_p, gb_p, lam_p


def _layer(cfg, x, mod, lp, states, final_gain=None):
    (n1, n2, w_in, gwa, gba, gng, mbif, mng, lcw, lcb, lgw, lgb, llam, wbr, bmg, wout, fup, fcw, fcb, fdown) = lp
    s_gla, s_mc, s_mn, s_mm, s_lru = states
    m = cfg.m
    sh1, sc1, g1, sh2, sc2, g2 = (mod[:, k * D_MODEL:(k + 1) * D_MODEL].reshape(MOD_ROWS, 1, D_MODEL) for k in range(6))

    proj_tm = PROJ_TM if (cfg.m_ctx % PROJ_TM == 0 and cfg.tl % PROJ_TM == 0) else 512
    z = _norm_matmul(cfg, x, n1.reshape(1, -1), sc1, sh1, _pack_w_in(w_in), proj_tm, 512)

    s0t = jnp.swapaxes(s_gla, -1, -2)
    o_f, sa_f = _gla(cfg, z, gwa[0], gba[0].reshape(1, -1), s0t[:, 0], False)
    y_a, sa_r = _gla(cfg, z, gwa[1], gba[1].reshape(1, -1), s0t[:, 1], True, o_f, gng.reshape(1, -1))
    new_gla =jnp.swapaxes(jnp.stack([sa_f, sa_r], axis=1), -1, -2)

    g_rows = z[:, Z_SM + SM_GB:Z_SM + SM_GB + 4 * MLSTM_HEADS].T
    k_t = z[:, Z_KB:Z_KB + MLSTM_HEADS * MLSTM_DH].T
    b_col = mbif.reshape(1, -1)
    b_row = mbif.reshape(-1, 1)
    n0 = s_mn[:, :, :, None, :]
    m0 = jnp.broadcast_to(s_mm[:, :, :, None, None], s_mm.shape + (1, 128))
    h_f, cb_f, nb_f, mb_f = _mlstm(cfg, z, k_t, g_rows, b_col, b_row, s_mc[:, 0], n0[:, 0], m0[:, 0], False)
    y_b, cb_r, nb_r, mb_r = _mlstm(cfg, z, k_t, g_rows, b_col, b_row, s_mc[:, 1], n0[:, 1], m0[:, 1], True,
                                   h_f, mng.reshape(1, -1))
    new_mc =jnp.stack([cb_f, cb_r], axis=1)
    new_mn = jnp.stack([nb_f[:, :, 0], nb_r[:, :, 0]], axis=1)
    new_mm = jnp.stack([mb_f[:, :, 0, 0], mb_r[:, :, 0, 0]], axis=1)

    gw_p, gb_p, lam_p = _pack_gate_w(lgw, lgb, llam)
    cbias = lcb.reshape(1, -1)
    y_c, new_lru = _rglru(cfg, z, lcw, cbias, gw_p, gb_p, lam_p, jnp.zeros((cfg.bc, N_DIR, LRU_WIDTH), F32),
                          cfg.bc, cfg.tc, 0)
    y_c, _ = _rglru(cfg, z, lcw, cbias, gw_p, gb_p, lam_p, s_lru, cfg.bl, cfg.tl, cfg.m_ctx, y_prev=y_c)

    merged = _merge(cfg, y_a, y_b, y_c, wbr.astype(BF16), z, bmg.reshape(N_BRANCH, 1, D_MODEL), 512, 1024)
    x = _matmul_res(cfg, merged, wout.astype(BF16), x, g1, 512, 1024)

    h = _norm_matmul(cfg, x, n2.reshape(1, -1), sc2, sh2, fup.astype(BF16), proj_tm, 512)
    w9 = fcw.reshape(FFN_CONV * FFN_CONV, D_FF)
    wd = fdown.astype(BF16)
    if final_gain is None:
        x_ctx = _ffn_down(cfg, h, w9, fcb, wd, x, g2, cfg.bc, cfg.tc, 0, False)
        x = _ffn_down(cfg, h, w9, fcb, wd, x, g2, cfg.bl, cfg.tl, cfg.m_ctx, True, prev=x_ctx)
    else:
        x = (_ffn_down(cfg, h, w9, fcb, wd, x, g2, cfg.bc, cfg.tc, 0, False, norm_gain=final_gain),
             _ffn_down(cfg, h, w9, fcb, wd, x, g2, cfg.bl, cfg.tl, cfg.m_ctx, True, norm_gain=final_gain))
    return x, (new_gla, new_mc, new_mn, new_mm, new_lru)


def kernel(x_prompt, x_sample, state_gla, state_mlstm_c, state_mlstm_n, state_mlstm_m, state_rglru, c, c_ctx, norm1_g, norm2_g, w_mod, b_mod, w_in, gla_w_alpha, gla_b_alpha, gla_norm_g, mlstm_b_if, mlstm_norm_g, lru_conv_w, lru_conv_b, lru_gate_w, lru_gate_b, lru_lambda, w_branch, b_merge, w_out, ffn_w_up, ffn_conv_w, ffn_conv_b, ffn_w_down, norm_f_g):
    bc, tc, d = x_prompt.shape
    bl, tl, _ = x_sample.shape
    cfg = Cfg(bc, tc, bl, tl)
    assert tc % TB == 0 and tl % TB == 0 and cfg.m_ctx % tl == 0 and 1 + bl <= MOD_ROWS
    assert bc % NSLOT == 0 and bl % NSLOT == 0
    depth = w_in.shape[0]

    x = jnp.concatenate([x_prompt.reshape(bc * tc, d), x_sample.reshape(bl * tl, d)], axis=0)
    c_all = jnp.concatenate([c_ctx[None, :], c, jnp.zeros((MOD_ROWS - 1 - bl, d), F32)], axis=0)
    mod = _modulation(c_all, w_mod, b_mod)

    new = []
    for l in range(depth):
        lp = (norm1_g[l], norm2_g[l], w_in[l], gla_w_alpha[l], gla_b_alpha[l], gla_norm_g[l], mlstm_b_if[l],
              mlstm_norm_g[l], lru_conv_w[l], lru_conv_b[l], lru_gate_w[l], lru_gate_b[l], lru_lambda[l],
              w_branch[l], b_merge[l], w_out[l], ffn_w_up[l], ffn_conv_w[l], ffn_conv_b[l], ffn_w_down[l])
        states = (state_gla[:, l], state_mlstm_c[:, l], state_mlstm_n[:, l], state_mlstm_m[:, l], state_rglru[:, l])
        x, st = _layer(cfg, x, mod[l], lp, states, norm_f_g.reshape(1, -1) if l == depth - 1 else None)
        new.append(st)

    y_prompt = x[0].reshape(bc, tc, d)
    y_sample = x[1].reshape(bl, tl, d)
    stacked = tuple(jnp.stack([new[l][k] for l in range(depth)], axis=1) for k in range(5))
    return (y_prompt, y_sample) + stacked
```

```python
import functools
from typing import NamedTuple

import numpy as np
import jax
import jax.numpy as jnp
from jax import lax
from jax.experimental import pallas as pl
from jax.experimental.pallas import tpu as pltpu

F32 = jnp.float32
BF16 = jnp.bfloat16

D_MODEL = 2048
DEPTH = 2
GRID_W = 64
N_DIR = 2
N_BRANCH = 3
BRANCH_W = 1024
GLA_HEADS = 4
GLA_DK = 128
GLA_DV = 256
GLA_RANK = 16
GLA_TAU = 16.0
MLSTM_HEADS = 4
MLSTM_DH = 256
CHUNK = 64
LRU_WIDTH = 1024
LRU_BLOCKS = 8
LRU_BLOCK = 128
LRU_CONV = 4
LRU_C = 8.0
D_FF = 5632
FFN_CONV = 3
EPS = 1e-6

_SRC = dict(qa=0, ka=512, va=1024, ga=2048, ra=3072, qb=3104, kb=4128, vb=5152, ob=6176, gb=7200,
            xr=7216, yr=8240, mg=9264, end=15408)
Z_MG, Z_VA, Z_GA, Z_QB, Z_KB, Z_VB, Z_OB, Z_XR, Z_YR, Z_QA, Z_KA, Z_SM = (
    0, 6144, 7168, 8192, 9216, 10240, 11264, 12288, 13312, 14336, 14848, 15360)
SM_W = 128
SM_RA, SM_GB = 0, 32
NZ = 15872
MOD_ROWS = 16

PROJ_TM = 1024
NORM_SLAB = 32
PROJ_SLAB = 64
TB = 256
NCH = TB // CHUNK
VMEM_LIMIT = 48 * 1024 * 1024


class Cfg(NamedTuple):
    bc: int
    tc: int
    bl: int
    tl: int

    @property
    def m_ctx(self):
        return self.bc * self.tc

    @property
    def m(self):
        return self.bc * self.tc + self.bl * self.tl


def _params(sem):
    return pltpu.CompilerParams(dimension_semantics=sem, vmem_limit_bytes=VMEM_LIMIT)


def _softplus(x):
    return jnp.maximum(x, 0.0) + jnp.log1p(jnp.exp(-jnp.abs(x)))


def _log_sigmoid(x):
    return -_softplus(-x)


def _silu(x):
    return x * jax.nn.sigmoid(x)


def _gelu_tanh(x):
    return 0.5 * x * (1.0 + jnp.tanh(np.sqrt(2.0 / np.pi).astype(np.float32) * (x + 0.044715 * (x * x * x))))


def _mod_row(cfg, row):
    return jnp.where(row < cfg.m_ctx, 0, 1 + jnp.maximum(row - cfg.m_ctx, 0) // cfg.tl)


def _mod_kernel(c_ref, w_ref, b_ref, o_ref):
    c = c_ref[...]
    a = _silu(c).astype(BF16)
    o_ref[0] = jnp.dot(a, w_ref[0].astype(BF16), preferred_element_type=F32) + b_ref[0]


def _modulation(c_all, w_mod, b_mod):
    depth, d, n = w_mod.shape
    tn = 512
    return pl.pallas_call(
        _mod_kernel,
        grid=(depth, n // tn),
        in_specs=[pl.BlockSpec((MOD_ROWS, d), lambda l, j: (0, 0)),
                  pl.BlockSpec((1, d, tn), lambda l, j: (l, 0, j)),
                  pl.BlockSpec((1, 1, tn), lambda l, j: (l, 0, j))],
        out_specs=pl.BlockSpec((1, MOD_ROWS, tn), lambda l, j: (l, 0, j)),
        out_shape=jax.ShapeDtypeStruct((depth, MOD_ROWS, n), F32),
        compiler_params=_params(("parallel", "parallel")),
        name="modulation",
    )(c_all, w_mod, b_mod.reshape(depth, 1, n))


def _norm_rows(x_ref, g_ref, sc_ref, sh_ref, u_ref, r0, nrows):
    x = x_ref[pl.ds(r0, nrows), :]
    y = x * lax.rsqrt(jnp.mean(x * x, axis=-1, keepdims=True) + EPS) * g_ref[...]
    u_ref[pl.ds(r0, nrows), :] = (y * (1.0 + sc_ref[0]) + sh_ref[0]).astype(BF16)


def _norm_matmul_kernel(x0_ref, xn_ref, g_ref, sc0_ref, sh0_ref, scn_ref, shn_ref, w_ref, o_ref, ua_ref, ub_ref, *, slab):
    i, j = pl.program_id(0), pl.program_id(1)
    tm = xn_ref.shape[0]

    @pl.when(jnp.logical_and(i == 0, j == 0))
    def _():
        def first(si, _):
            _norm_rows(x0_ref, g_ref, sc0_ref, sh0_ref, ua_ref, pl.multiple_of(si * NORM_SLAB, NORM_SLAB), NORM_SLAB)
            return 0

        lax.fori_loop(0, tm // NORM_SLAB, first, 0)

    r0 = pl.multiple_of(jnp.minimum(j, tm // slab - 1) * slab, slab)

    @pl.when(i % 2 == 0)
    def _():
        _norm_rows(xn_ref, g_ref, scn_ref, shn_ref, ub_ref, r0, slab)
        o_ref[...] = jnp.dot(ua_ref[...], w_ref[...], preferred_element_type=F32)

    @pl.when(i % 2 == 1)
    def _():
        _norm_rows(xn_ref, g_ref, scn_ref, shn_ref, ua_ref, r0, slab)
        o_ref[...] = jnp.dot(ub_ref[...], w_ref[...], preferred_element_type=F32)


def _norm_matmul(cfg, x, gain, sc, sh, w, tm, tn):
    m, d = x.shape
    n = w.shape[1]
    slab = PROJ_SLAB
    assert tm % slab == 0 and n // tn >= tm // slab
    nxt = lambda i: jnp.minimum(i + 1, m // tm - 1)
    mod0 = lambda i, j: (_mod_row(cfg, 0), 0, 0)
    modn = lambda i, j: (_mod_row(cfg, nxt(i) * tm), 0, 0)
    return pl.pallas_call(
        functools.partial(_norm_matmul_kernel, slab=slab),
        grid=(m // tm, n // tn),
        in_specs=[pl.BlockSpec((tm, d), lambda i, j: (0, 0), pipeline_mode=pl.Buffered(1)),
                  pl.BlockSpec((tm, d), lambda i, j: (nxt(i), 0)),
                  pl.BlockSpec((1, d), lambda i, j: (0, 0)),
                  pl.BlockSpec((1, 1, d), mod0),
                  pl.BlockSpec((1, 1, d), mod0),
                  pl.BlockSpec((1, 1, d), modn),
                  pl.BlockSpec((1, 1, d), modn),
                  pl.BlockSpec((d, tn), lambda i, j: (0, j))],
        out_specs=pl.BlockSpec((tm, tn), lambda i, j: (i, j)),
        out_shape=jax.ShapeDtypeStruct((m, n), F32),
        scratch_shapes=[pltpu.VMEM((tm, d), BF16), pltpu.VMEM((tm, d), BF16)],
        compiler_params=_params(("arbitrary", "arbitrary")),
        name="norm_matmul",
    )(x, x, gain, sc, sh, sc, sh, w)


NSLOT = 2


def _pair_pos(cfg, i):
    kc, kl = cfg.tc // TB, cfg.tl // TB
    nc = (cfg.bc // NSLOT) * kc
    is_ctx = i < nc
    il = jnp.maximum(i - nc, 0)
    j = jnp.where(is_ctx, i % kc, il % kl)
    k = jnp.where(is_ctx, kc, kl)
    s = jnp.where(is_ctx, i // kc, il // kl)
    return is_ctx, s, j, k


def _pair_row_block(cfg, reverse, slot, i):
    kc, kl = cfg.tc // TB, cfg.tl // TB
    is_ctx, s, j, k = _pair_pos(cfg, i)
    jj = (k - 1 - j) if reverse else j
    return jnp.where(is_ctx, (s + slot * (cfg.bc // NSLOT)) * kc + jj,
                     cfg.bc * kc + (s + slot * (cfg.bl // NSLOT)) * kl + jj)


def _pair_local_block(cfg, reverse, i):
    kc, kl = cfg.tc // TB, cfg.tl // TB
    is_ctx, s, j, k = _pair_pos(cfg, i)
    jj = (k - 1 - j) if reverse else j
    return jnp.where(is_ctx, s * kc + jj, (cfg.bc // NSLOT) * kc + s * kl + jj)


def _pair_lat_seq(cfg, i):
    is_ctx, s, _, _ = _pair_pos(cfg, i)
    return jnp.where(is_ctx, 0, jnp.minimum(s, cfg.bl // NSLOT - 1))


def _pair_ctx_seq(cfg, i):
    is_ctx, s, _, _ = _pair_pos(cfg, i)
    return jnp.where(is_ctx, s, cfg.bc // NSLOT - 1)


def _slot_of_tile(cfg, tm, i):
    nc, nl = cfg.m_ctx // tm // NSLOT, (cfg.m - cfg.m_ctx) // tm // NSLOT
    il = jnp.maximum(i - NSLOT * nc, 0)
    is_ctx = i < NSLOT * nc
    return jnp.where(is_ctx, i // nc, il // nl), jnp.where(is_ctx, i % nc, nc + il % nl)


def _tri(reverse, n=CHUNK):
    row = lax.broadcasted_iota(jnp.int32, (n, n), 0)
    col = lax.broadcasted_iota(jnp.int32, (n, n), 1)
    return (row <= col) if reverse else (row >= col)


def _head_norm(x, g):
    return x * lax.rsqrt(jnp.mean(x * x, axis=-1, keepdims=True) + EPS) * g


_NT = (((1,), (1,)), ((), ()))
_TN = (((0,), (0,)), ((), ()))


GLA_LOCKSTEP = 2


def _gla_kernel(*refs, cfg, reverse):
    it = iter(refs)
    blocks = [[next(it) for _ in range(4)] for _ in range(NSLOT)]
    wal_ref, bal_ref = next(it), next(it)
    init_ref = next(it)
    init = [init_ref.at[b] for b in range(NSLOT)]
    if reverse:
        gates = [next(it) for _ in range(NSLOT)]
        of_ref, gn_ref = next(it), next(it)
        epi = [[gates[b], of_ref.at[b]] for b in range(NSLOT)]
    out_ref, fin_ref = next(it), next(it)
    outs = [out_ref.at[b] for b in range(NSLOT)]
    fin = [fin_ref.at[b] for b in range(NSLOT)]
    st_ref, la_ref = next(it), next(it)
    is_ctx, _, j, k = _pair_pos(cfg, pl.program_id(0))
    nh = GLA_HEADS

    @pl.when(jnp.logical_and(j == 0, is_ctx))
    def _():
        st_ref[...] = jnp.zeros_like(st_ref)

    @pl.when(jnp.logical_and(j == 0, jnp.logical_not(is_ctx)))
    def _():
        for b in range(NSLOT):
            st_ref[b * nh:(b + 1) * nh] = init[b][0]

    d = 1 if reverse else 0
    slots = range(NSLOT)
    both = lambda f: [f(b) for b in slots]
    wal = wal_ref[...].astype(BF16)
    for b in slots:
        ra = blocks[b][3][:, SM_RA + d * GLA_RANK:SM_RA + (d + 1) * GLA_RANK].astype(BF16)
        pre = jnp.dot(ra, wal, preferred_element_type=F32) + bal_ref[...]
        la_ref[b] = _log_sigmoid(pre) * (1.0 / GLA_TAU)

    tri = _tri(reverse)
    tri_f = tri.astype(F32)
    last = 0 if reverse else CHUNK - 1
    qscale = GLA_DK ** -0.5
    for c in (range(NCH - 1, -1, -1) if reverse else range(NCH)):
        rows = slice(c * CHUNK, (c + 1) * CHUNK)
        cum = both(lambda b: jnp.dot(tri_f, la_ref[b, rows, :], precision=lax.Precision.HIGHEST,
                                     preferred_element_type=F32))
        tot = both(lambda b: cum[b][last:last + 1, :])
        e_q = both(lambda b: jnp.exp(cum[b]))
        e_k = both(lambda b: jnp.exp(-cum[b]))
        e_end = both(lambda b: jnp.exp(tot[b] - cum[b]))
        e_tot = both(lambda b: jnp.exp(tot[b]))
        for h0 in range(0, GLA_HEADS, GLA_LOCKSTEP):
            units = [(b, h0 + dh) for dh in range(GLA_LOCKSTEP) for b in slots]
            each = lambda f: [f(u, b, h) for u, (b, h) in enumerate(units)]
            kc = [slice(h * GLA_DK, (h + 1) * GLA_DK) for _, h in units]
            vc = [slice(h * GLA_DV, (h + 1) * GLA_DV) for _, h in units]
            sh = [b * nh + h for b, h in units]
            qh = each(lambda u, b, h: blocks[b][0][rows, kc[u]] * qscale)
            kh = each(lambda u, b, h: blocks[b][1][rows, kc[u]])
            vh = each(lambda u, b, h: blocks[b][2][rows, vc[u]].astype(BF16))
            q_in = each(lambda u, b, h: (qh[u] * e_q[b][:, kc[u]]).astype(BF16))
            k_in = each(lambda u, b, h: (kh[u] * e_k[b][:, kc[u]]).astype(BF16))
            att = each(lambda u, b, h: lax.dot_general(q_in[u], k_in[u], _NT, preferred_element_type=F32))
            att = each(lambda u, b, h: jnp.where(tri, att[u], 0.0).astype(BF16))
            st = each(lambda u, b, h: st_ref[sh[u]])
            o = each(lambda u, b, h: jnp.dot(att[u], vh[u], preferred_element_type=F32)
                     + lax.dot_general(q_in[u], st[u].astype(BF16), _NT, preferred_element_type=F32))
            k_end = each(lambda u, b, h: (kh[u] * e_end[b][:, kc[u]]).astype(BF16))
            upd = each(lambda u, b, h: lax.dot_general(vh[u], k_end[u], _TN, preferred_element_type=F32))
            for u, (b, h) in enumerate(units):
                st_ref[sh[u]] = e_tot[b][:, kc[u]] * st[u] + upd[u]
            for u, (b, h) in enumerate(units):
                if reverse:
                    ga_ref, of_ref = epi[b]
                    y = _head_norm(of_ref[rows, vc[u]] + o[u], gn_ref[:, vc[u]]) * _silu(ga_ref[rows, vc[u]])
                    outs[b][rows, vc[u]] = y.astype(outs[b].dtype)
                else:
                    outs[b][rows, vc[u]] = o[u]

    @pl.when(jnp.logical_and(is_ctx, j == k - 1))
    def _():
        for b in range(NSLOT):
            fin[b][0] = st_ref[b * nh:(b + 1) * nh]


def _gla(cfg, z, wal, bal, s0t, reverse, o_fwd=None, gnorm=None):
    m = cfg.m
    nh = GLA_HEADS
    hk, hv = nh * GLA_DK, nh * GLA_DV
    lb = functools.partial(_pair_local_block, cfg, reverse)
    in_specs, args = [], []
    for b in range(NSLOT):
        rb = functools.partial(_pair_row_block, cfg, reverse, b)
        in_specs += [pl.BlockSpec((TB, hk), lambda i, rb=rb: (rb(i), Z_QA // hk)),
                     pl.BlockSpec((TB, hk), lambda i, rb=rb: (rb(i), Z_KA // hk)),
                     pl.BlockSpec((TB, hv), lambda i, rb=rb: (rb(i), Z_VA // hv)),
                     pl.BlockSpec((TB, SM_W), lambda i, rb=rb: (rb(i), Z_SM // SM_W))]
        args += [z, z, z, z]
    in_specs += [pl.BlockSpec((GLA_RANK, hk), lambda i: (0, 0)), pl.BlockSpec((1, hk), lambda i: (0, 0))]
    args += [wal, bal]
    half_c, half_l = cfg.bc // NSLOT, cfg.bl // NSLOT
    in_specs.append(pl.BlockSpec((NSLOT, 1, nh, GLA_DV, GLA_DK), lambda i: (0, _pair_lat_seq(cfg, i), 0, 0, 0)))
    args.append(s0t.reshape(NSLOT, half_l, nh, GLA_DV, GLA_DK))
    if reverse:
        for b in range(NSLOT):
            rb = functools.partial(_pair_row_block, cfg, reverse, b)
            in_specs.append(pl.BlockSpec((TB, hv), lambda i, rb=rb: (rb(i), Z_GA // hv)))
            args.append(z)
        in_specs += [pl.BlockSpec((NSLOT, TB, hv), lambda i: (0, lb(i), 0)), pl.BlockSpec((1, hv), lambda i: (0, 0))]
        args += [o_fwd, gnorm]
    out, fin = pl.pallas_call(
        functools.partial(_gla_kernel, cfg=cfg, reverse=reverse),
        grid=(m // TB // NSLOT,),
        in_specs=in_specs,
        out_specs=[pl.BlockSpec((NSLOT, TB, hv), lambda i: (0, lb(i), 0)),
                   pl.BlockSpec((NSLOT, 1, nh, GLA_DV, GLA_DK), lambda i: (0, _pair_ctx_seq(cfg, i), 0, 0, 0))],
        out_shape=[jax.ShapeDtypeStruct((NSLOT, m // NSLOT, hv), BF16 if reverse else F32),
                   jax.ShapeDtypeStruct((NSLOT, half_c, nh, GLA_DV, GLA_DK), F32)],
        scratch_shapes=[pltpu.VMEM((NSLOT * nh, GLA_DV, GLA_DK), F32), pltpu.VMEM((NSLOT, TB, hk), F32)],
        compiler_params=_params(("arbitrary",)),
        name="gla_rev" if reverse else "gla_fwd",
    )(*args)
    return out, fin.reshape(cfg.bc, nh, GLA_DV, GLA_DK)


MCH = 128
NMC = TB // MCH
MLSTM_LOCKSTEP = 2


def _mlstm_kernel(*refs, cfg, reverse):
    it = iter(refs)
    blocks = [[next(it) for _ in range(6)] for _ in range(NSLOT)]
    bc_ref, br_ref = next(it), next(it)
    init_refs = [next(it) for _ in range(3)]
    init = [[r.at[b] for r in init_refs] for b in range(NSLOT)]
    if reverse:
        gates = [next(it) for _ in range(NSLOT)]
        hf_all, gn_ref = next(it), next(it)
        epi = [[gates[b], hf_all.at[b]] for b in range(NSLOT)]
    out_ref = next(it)
    outs = [out_ref.at[b] for b in range(NSLOT)]
    fin_refs = [next(it) for _ in range(3)]
    fin = [[r.at[b] for r in fin_refs] for b in range(NSLOT)]
    c_ref, n_ref, m_ref = next(it), next(it), next(it)
    is_ctx, _, j, k = _pair_pos(cfg, pl.program_id(0))
    nh = MLSTM_HEADS

    @pl.when(jnp.logical_and(j == 0, is_ctx))
    def _():
        c_ref[...] = jnp.zeros_like(c_ref)
        n_ref[...] = jnp.zeros_like(n_ref)
        m_ref[...] = jnp.zeros_like(m_ref)

    @pl.when(jnp.logical_and(j == 0, jnp.logical_not(is_ctx)))
    def _():
        for b in range(NSLOT):
            c_ref[b * nh:(b + 1) * nh] = init[b][0][0]
            n_ref[b * nh:(b + 1) * nh] = init[b][1][0]
            m_ref[b * nh:(b + 1) * nh] = init[b][2][0]

    d = 1 if reverse else 0
    tri = _tri(reverse, MCH)
    tri_f = tri.astype(F32)
    tri_tf = _tri(not reverse, MCH).astype(F32)
    last = 0 if reverse else MCH - 1
    ng = 2 * MLSTM_HEADS
    kscale = MLSTM_DH ** -0.5
    slots = range(NSLOT)
    both = lambda f: [f(b) for b in slots]
    hp = lax.Precision.HIGHEST
    for c in (range(NMC - 1, -1, -1) if reverse else range(NMC)):
        rows = slice(c * MCH, (c + 1) * MCH)
        g_col = both(lambda b: blocks[b][4][rows, SM_GB:SM_GB + 2 * ng] + bc_ref[...])
        g_row = both(lambda b: blocks[b][5][:, rows] + br_ref[...])
        fcum_col = both(lambda b: jnp.dot(tri_f, _log_sigmoid(g_col[b]), precision=hp, preferred_element_type=F32))
        fcum_row = both(lambda b: jnp.dot(_log_sigmoid(g_row[b]), tri_tf, precision=hp, preferred_element_type=F32))
        for h0 in range(0, MLSTM_HEADS, MLSTM_LOCKSTEP):
            units = [(b, h0 + dh) for dh in range(MLSTM_LOCKSTEP) for b in slots]
            each = lambda f: [f(u, b, h) for u, (b, h) in enumerate(units)]
            ii = [d * ng + h for _, h in units]
            fi = [d * ng + MLSTM_HEADS + h for _, h in units]
            hc = [slice(h * MLSTM_DH, (h + 1) * MLSTM_DH) for _, h in units]
            sh = [b * nh + h for b, h in units]
            f_col = each(lambda u, b, h: fcum_col[b][:, fi[u]:fi[u] + 1])
            f_row = each(lambda u, b, h: fcum_row[b][fi[u]:fi[u] + 1, :])
            i_col = each(lambda u, b, h: g_col[b][:, ii[u]:ii[u] + 1])
            i_row = each(lambda u, b, h: g_row[b][ii[u]:ii[u] + 1, :])
            m_prev = each(lambda u, b, h: m_ref[sh[u]][:, 0:1])
            dlog = each(lambda u, b, h: jnp.where(tri, f_col[u] - f_row[u] + i_row[u], -jnp.inf))
            prev = each(lambda u, b, h: f_col[u] + m_prev[u])
            mj = each(lambda u, b, h: jnp.maximum(prev[u], jnp.max(dlog[u], axis=-1, keepdims=True)))
            w = each(lambda u, b, h: jnp.exp(dlog[u] - mj[u]))
            wp = each(lambda u, b, h: jnp.exp(prev[u] - mj[u]))
            qh = each(lambda u, b, h: blocks[b][0][rows, hc[u]])
            kh = each(lambda u, b, h: blocks[b][1][rows, hc[u]] * kscale)
            vh = each(lambda u, b, h: blocks[b][2][rows, hc[u]].astype(BF16))
            s = each(lambda u, b, h: lax.dot_general(qh[u].astype(BF16), kh[u].astype(BF16), _NT,
                                                     preferred_element_type=F32) * w[u])
            qp = each(lambda u, b, h: qh[u] * wp[u])
            cm = each(lambda u, b, h: c_ref[sh[u]])
            nv = each(lambda u, b, h: n_ref[sh[u]])
            m_new = each(lambda u, b, h: mj[u][last:last + 1, :])
            tot = each(lambda u, b, h: f_col[u][last:last + 1, :])
            wl_col = each(lambda u, b, h: jnp.exp(tot[u] - f_col[u] + i_col[u] - m_new[u]))
            wl_row = each(lambda u, b, h: jnp.exp(tot[u] - f_row[u] + i_row[u] - m_new[u]))
            decay = each(lambda u, b, h: jnp.exp(tot[u] + m_prev[u] - m_new[u]))
            kw_t = each(lambda u, b, h: (blocks[b][3][hc[u], rows] * kscale) * wl_row[u])
            sv = each(lambda u, b, h: jnp.dot(jnp.concatenate([s[u].astype(BF16), kw_t[u].astype(BF16)], axis=0),
                                              vh[u], preferred_element_type=F32))
            num = each(lambda u, b, h: sv[u][:MCH]
                       + jnp.dot(qp[u].astype(BF16), cm[u].astype(BF16), preferred_element_type=F32))
            upd = each(lambda u, b, h: sv[u][MCH:])
            den = each(lambda u, b, h: jnp.sum(s[u], axis=-1, keepdims=True)
                       + jnp.sum(qp[u] * nv[u], axis=-1, keepdims=True))
            hh = each(lambda u, b, h: num[u] / jnp.maximum(jnp.abs(den[u]), jnp.exp(-mj[u])))
            for u, (b, h) in enumerate(units):
                c_ref[sh[u]] = decay[u] * cm[u] + upd[u]
                n_ref[sh[u]] = decay[u] * nv[u] + jnp.sum(kh[u] * wl_col[u], axis=0, keepdims=True)
                m_ref[sh[u]] = jnp.broadcast_to(m_new[u], (1, 128))
            for u, (b, h) in enumerate(units):
                if reverse:
                    ob_ref, hf_ref = epi[b]
                    y = jax.nn.sigmoid(ob_ref[rows, hc[u]]) * _head_norm(hf_ref[rows, hc[u]] + hh[u], gn_ref[:, hc[u]])
                    outs[b][rows, hc[u]] = y.astype(outs[b].dtype)
                else:
                    outs[b][rows, hc[u]] = hh[u]

    @pl.when(jnp.logical_and(is_ctx, j == k - 1))
    def _():
        for b in range(NSLOT):
            fin[b][0][0] = c_ref[b * nh:(b + 1) * nh]
            fin[b][1][0] = n_ref[b * nh:(b + 1) * nh]
            fin[b][2][0] = m_ref[b * nh:(b + 1) * nh]


def _mlstm(cfg, z, k_t, g_rows, b_col, b_row, c0, n0, m0, reverse, h_fwd=None, gnorm=None):
    m = cfg.m
    hd = MLSTM_HEADS * MLSTM_DH
    nh = MLSTM_HEADS
    lb = functools.partial(_pair_local_block, cfg, reverse)
    ctx = lambda i: (0, _pair_ctx_seq(cfg, i), 0, 0, 0)
    in_specs, args = [], []
    for b in range(NSLOT):
        rb = functools.partial(_pair_row_block, cfg, reverse, b)
        in_specs += [pl.BlockSpec((TB, hd), lambda i, rb=rb: (rb(i), Z_QB // hd)),
                     pl.BlockSpec((TB, hd), lambda i, rb=rb: (rb(i), Z_KB // hd)),
                     pl.BlockSpec((TB, hd), lambda i, rb=rb: (rb(i), Z_VB // hd)),
                     pl.BlockSpec((hd, TB), lambda i, rb=rb: (0, rb(i))),
                     pl.BlockSpec((TB, SM_W), lambda i, rb=rb: (rb(i), Z_SM // SM_W)),
                     pl.BlockSpec((4 * nh, TB), lambda i, rb=rb: (0, rb(i)))]
        args += [z, z, z, k_t, z, g_rows]
    in_specs += [pl.BlockSpec((1, 4 * nh), lambda i: (0, 0)), pl.BlockSpec((4 * nh, 1), lambda i: (0, 0))]
    args += [b_col, b_row]
    half_c, half_l = cfg.bc // NSLOT, cfg.bl // NSLOT
    lat = lambda i: (0, _pair_lat_seq(cfg, i), 0, 0, 0)
    state_dims = ((nh, MLSTM_DH, MLSTM_DH), (nh, 1, MLSTM_DH), (nh, 1, 128))
    in_specs += [pl.BlockSpec((NSLOT, 1) + sd, lat) for sd in state_dims]
    args += [s.reshape((NSLOT, half_l) + sd) for s, sd in zip((c0, n0, m0), state_dims)]
    if reverse:
        for b in range(NSLOT):
            rb = functools.partial(_pair_row_block, cfg, reverse, b)
            in_specs.append(pl.BlockSpec((TB, hd), lambda i, rb=rb: (rb(i), Z_OB // hd)))
            args.append(z)
        in_specs += [pl.BlockSpec((NSLOT, TB, hd), lambda i: (0, lb(i), 0)), pl.BlockSpec((1, hd), lambda i: (0, 0))]
        args += [h_fwd, gnorm]
    res = pl.pallas_call(
        functools.partial(_mlstm_kernel, cfg=cfg, reverse=reverse),
        grid=(m // TB // NSLOT,),
        in_specs=in_specs,
        out_specs=[pl.BlockSpec((NSLOT, TB, hd), lambda i: (0, lb(i), 0))]
                  + [pl.BlockSpec((NSLOT, 1) + sd, ctx) for sd in state_dims],
        out_shape=[jax.ShapeDtypeStruct((NSLOT, m // NSLOT, hd), BF16 if reverse else F32)]
                  + [jax.ShapeDtypeStruct((NSLOT, half_c) + sd, F32) for sd in state_dims],
        scratch_shapes=[pltpu.VMEM((NSLOT * nh, MLSTM_DH, MLSTM_DH), F32),
                        pltpu.VMEM((NSLOT * nh, 1, MLSTM_DH), F32),
                        pltpu.VMEM((NSLOT * nh, 1, 128), F32)],
        compiler_params=_params(("arbitrary",)),
        name="mlstm_rev" if reverse else "mlstm_fwd",
    )(*args)
    return (res[0],) + tuple(r.reshape((cfg.bc,) + sd) for r, sd in zip(res[1:], state_dims))


LRU_ROWS = 256
LRU_PAD = 8


def _rglru_kernel(*refs, t, aliased):
    if aliased:
        refs = refs[1:]
    (xr_ref, yr_ref, cw_ref, cb_ref, gw_ref, gb_ref, lam_ref, h0_ref, y_ref, he_ref,
     pad_ref, a_ref, b_ref, hf_ref, hr_ref) = refs
    nsteps = t // LRU_ROWS
    zeros = jnp.zeros((LRU_PAD, LRU_BLOCK), F32)
    pad_ref[0:LRU_PAD, :] = zeros
    pad_ref[LRU_PAD + t:2 * LRU_PAD + t, :] = zeros

    def fill(ci, _):
        r0 = pl.multiple_of(ci * LRU_ROWS, LRU_ROWS)
        pad_ref[pl.ds(r0 + LRU_PAD, LRU_ROWS), :] = xr_ref[pl.ds(r0, LRU_ROWS), :]
        return 0

    lax.fori_loop(0, nsteps, fill, 0)

    sp = _softplus(-lam_ref[0])
    left = LRU_CONV // 2

    def gates(ci, _):
        r0 = pl.multiple_of(ci * LRU_ROWS, LRU_ROWS)
        xc = cb_ref[...]
        for tap in range(LRU_CONV):
            xc = xc + cw_ref[tap:tap + 1, :] * pad_ref[pl.ds(r0 + LRU_PAD - left + tap, LRU_ROWS), :]
        pre = jnp.dot(xc.astype(BF16), gw_ref[0], preferred_element_type=F32) + gb_ref[0]
        both = lambda f: [f(d) for d in range(N_DIR)]
        r = both(lambda d: jax.nn.sigmoid(pre[:, d * 2 * LRU_BLOCK:d * 2 * LRU_BLOCK + LRU_BLOCK]))
        ig = both(lambda d: jax.nn.sigmoid(pre[:, d * 2 * LRU_BLOCK + LRU_BLOCK:(d + 1) * 2 * LRU_BLOCK]))
        log_a = both(lambda d: (-LRU_C * r[d]) * sp[:, d * LRU_BLOCK:(d + 1) * LRU_BLOCK])
        a = both(lambda d: jnp.exp(log_a[d]))
        for d in range(N_DIR):
            a_ref[d, pl.ds(r0, LRU_ROWS), :] = a[d]
            b_ref[d, pl.ds(r0, LRU_ROWS), :] = jnp.sqrt(1.0 - a[d] * a[d]) * (ig[d] * xc)
        return 0

    lax.fori_loop(0, nsteps, gates, 0)

    ngrp = t // 8
    row = lax.broadcasted_iota(jnp.int32, (8, LRU_BLOCK), 0)

    def scan(g, carry):
        starts = (pl.multiple_of(g * 8, 8), pl.multiple_of((ngrp - 1 - g) * 8, 8))
        both = lambda f: [f(d) for d in range(N_DIR)]
        a = both(lambda d: a_ref[d, pl.ds(starts[d], 8), :])
        b = both(lambda d: b_ref[d, pl.ds(starts[d], 8), :])
        for s in (1, 2, 4):
            keep = (row >= s, row < 8 - s)
            shift = (s, 8 - s)
            b = both(lambda d: a[d] * jnp.where(keep[d], pltpu.roll(b[d], shift[d], axis=0), 0.0) + b[d])
            a = both(lambda d: a[d] * jnp.where(keep[d], pltpu.roll(a[d], shift[d], axis=0), 1.0))
        h = both(lambda d: a[d] * carry[d] + b[d])
        hf_ref[pl.ds(starts[0], 8), :] = h[0]
        hr_ref[pl.ds(starts[1], 8), :] = h[1]
        return (jnp.broadcast_to(h[0][7:8, :], (8, LRU_BLOCK)), jnp.broadcast_to(h[1][0:1, :], (8, LRU_BLOCK)))

    cf0 = jnp.broadcast_to(h0_ref[0, 0:1, :], (8, LRU_BLOCK))
    cr0 = jnp.broadcast_to(h0_ref[0, 1:2, :], (8, LRU_BLOCK))
    cf, cr = lax.fori_loop(0, ngrp, scan, (cf0, cr0), unroll=2)
    he_ref[0, 0:1, :] = cf[0:1, :]
    he_ref[0, 1:2, :] = cr[0:1, :]

    def finish(ci, _):
        r0 = pl.multiple_of(ci * LRU_ROWS, LRU_ROWS)
        y = (hf_ref[pl.ds(r0, LRU_ROWS), :] + hr_ref[pl.ds(r0, LRU_ROWS), :]) * _gelu_tanh(yr_ref[pl.ds(r0, LRU_ROWS), :])
        y_ref[pl.ds(r0, LRU_ROWS), :] = y.astype(y_ref.dtype)
        return 0

    lax.fori_loop(0, nsteps, finish, 0)


def _rglru(cfg, z, cw, cb, gw, gb, lam, h0, nseq, t, row0, y_prev=None):
    m = cfg.m
    assert row0 % t == 0
    sb = row0 // t
    aliased = y_prev is not None
    in_specs = [pl.BlockSpec((t, LRU_BLOCK), lambda b, n: (sb + b, Z_XR // LRU_BLOCK + n)),
                pl.BlockSpec((t, LRU_BLOCK), lambda b, n: (sb + b, Z_YR // LRU_BLOCK + n)),
                pl.BlockSpec((LRU_CONV, LRU_BLOCK), lambda b, n: (0, n)),
                pl.BlockSpec((1, LRU_BLOCK), lambda b, n: (0, n)),
                pl.BlockSpec((1, LRU_BLOCK, 4 * LRU_BLOCK), lambda b, n: (n, 0, 0)),
                pl.BlockSpec((1, 1, 4 * LRU_BLOCK), lambda b, n: (n, 0, 0)),
                pl.BlockSpec((1, 1, 2 * LRU_BLOCK), lambda b, n: (n, 0, 0)),
                pl.BlockSpec((1, N_DIR, LRU_BLOCK), lambda b, n: (b, 0, n))]
    args = [z, z, cw, cb, gw, gb, lam, h0]
    if aliased:
        in_specs = [pl.BlockSpec(memory_space=pl.ANY)] + in_specs
        args = [y_prev] + args
    return pl.pallas_call(
        functools.partial(_rglru_kernel, t=t, aliased=aliased),
        grid=(nseq, LRU_BLOCKS),
        in_specs=in_specs,
        out_specs=[pl.BlockSpec((t, LRU_BLOCK), lambda b, n: (sb + b, n)),
                   pl.BlockSpec((1, N_DIR, LRU_BLOCK), lambda b, n: (b, 0, n))],
        out_shape=[jax.ShapeDtypeStruct((m, LRU_WIDTH), BF16),
                   jax.ShapeDtypeStruct((nseq, N_DIR, LRU_WIDTH), F32)],
        scratch_shapes=[pltpu.VMEM((t + 2 * LRU_PAD, LRU_BLOCK), F32),
                        pltpu.VMEM((N_DIR, t, LRU_BLOCK), F32),
                        pltpu.VMEM((N_DIR, t, LRU_BLOCK), F32),
                        pltpu.VMEM((t, LRU_BLOCK), F32),
                        pltpu.VMEM((t, LRU_BLOCK), F32)],
        input_output_aliases={0: 0} if aliased else {},
        compiler_params=_params(("parallel", "parallel")),
        name="rglru",
    )(*args)


def _merge_kernel(ya_ref, yb_ref, yc_ref, w_ref, ma_ref, mb_ref, mc_ref, bm_ref, o_ref):
    acc = None
    for n, (y_ref, mg_ref) in enumerate(((ya_ref, ma_ref), (yb_ref, mb_ref), (yc_ref, mc_ref))):
        g = jax.nn.sigmoid(mg_ref[...] + bm_ref[n])
        term = g * jnp.dot(y_ref[...], w_ref[n], preferred_element_type=F32)
        acc = term if acc is None else acc + term
    o_ref[...] = acc.astype(o_ref.dtype)


def _merge(cfg, ya, yb, yc, wbr, z, bm, tm, tn):
    m = cfg.m
    nj = D_MODEL // tn
    assert (cfg.m_ctx // NSLOT) % tm == 0 and ((m - cfg.m_ctx) // NSLOT) % tm == 0
    slot_spec = pl.BlockSpec((None, tm, BRANCH_W), lambda j, i: _slot_of_tile(cfg, tm, i) + (0,))
    y_spec = pl.BlockSpec((tm, BRANCH_W), lambda j, i: (i, 0))
    mg_spec = lambda n: pl.BlockSpec((tm, tn), lambda j, i: (i, Z_MG // tn + n * nj + j))
    return pl.pallas_call(
        _merge_kernel,
        grid=(nj, m // tm),
        in_specs=[slot_spec, slot_spec, y_spec,
                  pl.BlockSpec((N_BRANCH, BRANCH_W, tn), lambda j, i: (0, 0, j)),
                  mg_spec(0), mg_spec(1), mg_spec(2),
                  pl.BlockSpec((N_BRANCH, 1, tn), lambda j, i: (0, 0, j))],
        out_specs=pl.BlockSpec((tm, tn), lambda j, i: (i, j)),
        out_shape=jax.ShapeDtypeStruct((m, D_MODEL), BF16),
        compiler_params=_params(("parallel", "parallel")),
        name="merge",
    )(ya, yb, yc, wbr, z, z, z, bm)


def _matmul_res_kernel(a_ref, w_ref, x_ref, g_ref, o_ref):
    o_ref[...] = x_ref[...] + g_ref[0] * jnp.dot(a_ref[...], w_ref[...], preferred_element_type=F32)


def _matmul_res(cfg, a, w, x, gate, tm, tn):
    m, kdim = a.shape
    n = w.shape[1]
    return pl.pallas_call(
        _matmul_res_kernel,
        grid=(n // tn, m // tm),
        in_specs=[pl.BlockSpec((tm, kdim), lambda j, i: (i, 0)),
                  pl.BlockSpec((kdim, tn), lambda j, i: (0, j)),
                  pl.BlockSpec((tm, tn), lambda j, i: (i, j)),
                  pl.BlockSpec((1, 1, tn), lambda j, i: (_mod_row(cfg, i * tm), 0, j))],
        out_specs=pl.BlockSpec((tm, tn), lambda j, i: (i, j)),
        out_shape=jax.ShapeDtypeStruct((m, n), F32),
        compiler_params=_params(("parallel", "parallel")),
        name="matmul_res",
    )(a, w, x, gate)


FFN_PAD = 72
FFN_STEP_ELEMS = 128 * 128


def _conv_act_kernel(*refs, t, taps, ct, aliased):
    if aliased:
        refs = refs[1:]
    hg_ref, hu_ref, w_ref, b_ref, o_ref, pad_ref = refs
    rows = FFN_STEP_ELEMS // ct
    nsteps = t // rows
    zeros = jnp.zeros((FFN_PAD, ct), F32)
    pad_ref[0:FFN_PAD, :] = zeros
    pad_ref[FFN_PAD + t:2 * FFN_PAD + t, :] = zeros

    def fill(ci, _):
        r0 = pl.multiple_of(ci * rows, rows)
        pad_ref[pl.ds(r0 + FFN_PAD, rows), :] = hg_ref[pl.ds(r0, rows), :]
        return 0

    lax.fori_loop(0, nsteps, fill, 0)
    col = lax.broadcasted_iota(jnp.int32, (rows, ct), 0) % GRID_W

    def step(ci, _):
        r0 = ci * rows if isinstance(ci, int) else pl.multiple_of(ci * rows, rows)
        acc = b_ref[...]
        for shift, widx, dx in taps:
            xs = pad_ref[pl.ds(r0 + FFN_PAD + shift, rows), :]
            if dx < 0:
                xs = jnp.where(col >= -dx, xs, 0.0)
            elif dx > 0:
                xs = jnp.where(col < GRID_W - dx, xs, 0.0)
            acc = acc + xs * w_ref[widx:widx + 1, :]
        o_ref[pl.ds(r0, rows), :] = (_silu(acc) * hu_ref[pl.ds(r0, rows), :]).astype(o_ref.dtype)
        return 0

    if ct > 128:
        for ci in range(nsteps):
            step(ci, 0)
    else:
        lax.fori_loop(0, nsteps, step, 0)


def _conv_act(cfg, h, w9, bias, nseq, t, row0, on_grid, ct, prev=None):
    m = cfg.m
    rows = FFN_STEP_ELEMS // ct
    assert row0 % t == 0 and t % rows == 0 and D_FF % ct == 0 and rows % 16 == 0
    assert not on_grid or rows % GRID_W == 0
    sb = row0 // t
    p = FFN_CONV // 2
    if on_grid:
        taps = tuple(((dy - p) * GRID_W + (dx - p), dy * FFN_CONV + dx, dx - p)
                     for dy in range(FFN_CONV) for dx in range(FFN_CONV))
    else:
        taps = tuple((dx - p, p * FFN_CONV + dx, 0) for dx in range(FFN_CONV))
    aliased = prev is not None
    nct = D_FF // ct
    in_specs = [pl.BlockSpec((t, ct), lambda b, j: (sb + b, j)),
                pl.BlockSpec((t, ct), lambda b, j: (sb + b, nct + j)),
                pl.BlockSpec((FFN_CONV * FFN_CONV, ct), lambda b, j: (0, j)),
                pl.BlockSpec((1, ct), lambda b, j: (0, j))]
    args = [h, h, w9, bias]
    if aliased:
        in_specs = [pl.BlockSpec(memory_space=pl.ANY)] + in_specs
        args = [prev] + args
    return pl.pallas_call(
        functools.partial(_conv_act_kernel, t=t, taps=taps, ct=ct, aliased=aliased),
        grid=(nseq, nct),
        in_specs=in_specs,
        out_specs=pl.BlockSpec((t, ct), lambda b, j: (sb + b, j)),
        out_shape=jax.ShapeDtypeStruct((m, D_FF), BF16),
        scratch_shapes=[pltpu.VMEM((t + 2 * FFN_PAD, ct), F32)],
        input_output_aliases={0: 0} if aliased else {},
        compiler_params=_params(("parallel", "parallel")),
        name="conv_act",
    )(*args)


FD_TM = 512
FD_TK = 512
FD_HALO = GRID_W
FD_TOP = 8
FD_VMEM = 56 * 1024 * 1024


def _ffn_down_kernel(*refs, cfg, on_grid, seq_len, final_norm, aliased):
    refs = list(refs[1:] if aliased else refs)
    hg_ref, hu_ref = refs[:2]
    refs = refs[2:]
    if on_grid:
        hp_ref, hn_ref = refs[:2]
        refs = refs[2:]
    w9_ref, cb_ref, wd_ref, x_ref, g_ref = refs[:5]
    refs = refs[5:]
    if final_norm:
        nf_ref = refs[0]
        refs = refs[1:]
    o_ref, pad_ref, even_ref, odd_ref = refs
    i, k = pl.program_id(0), pl.program_id(1)
    nk = pl.num_programs(1) - 1
    tm, tk = FD_TM, FD_TK
    nslab = tk // 128
    base = FD_TOP + FD_HALO

    @pl.when(k == 0)
    def _():
        o_ref[...] = jnp.zeros_like(o_ref)
        edge = jnp.zeros((nslab, FD_TOP, 128), F32)
        pad_ref[:, 0:FD_TOP, :] = edge
        pad_ref[:, base + tm + FD_HALO:base + tm + FD_HALO + FD_TOP, :] = edge
        if not on_grid:
            halo = jnp.zeros((nslab, FD_HALO, 128), F32)
            pad_ref[:, FD_TOP:base, :] = halo
            pad_ref[:, base + tm:base + tm + FD_HALO, :] = halo

    if on_grid:
        tiles_per_seq = seq_len // tm
        first = i % tiles_per_seq == 0
        last = i % tiles_per_seq == tiles_per_seq - 1

        @pl.when(first)
        def _():
            pad_ref[:, FD_TOP:base, :] = jnp.zeros((nslab, FD_HALO, 128), F32)

        @pl.when(jnp.logical_not(first))
        def _():
            for s in range(nslab):
                pad_ref[s, FD_TOP:base, :] = hp_ref[:, s * 128:(s + 1) * 128]

        @pl.when(last)
        def _():
            pad_ref[:, base + tm:base + tm + FD_HALO, :] = jnp.zeros((nslab, FD_HALO, 128), F32)

        @pl.when(jnp.logical_not(last))
        def _():
            for s in range(nslab):
                pad_ref[s, base + tm:base + tm + FD_HALO, :] = hn_ref[:, s * 128:(s + 1) * 128]

    for s in range(nslab):
        pad_ref[s, base:base + tm, :] = hg_ref[:, s * 128:(s + 1) * 128]

    p = FFN_CONV // 2
    if on_grid:
        taps = tuple(((dy - p) * GRID_W + (dx - p), dy * FFN_CONV + dx, dx - p)
                     for dy in range(FFN_CONV) for dx in range(FFN_CONV))
        period = GRID_W
    else:
        taps = tuple((dx - p, p * FFN_CONV + dx, dx - p) for dx in range(FFN_CONV))
        period = seq_len
    kk = pl.multiple_of(jnp.maximum(k - 1, 0) * tk, tk)
    d_out = o_ref.shape[1]
    col_tile = 256
    n_col = d_out // col_tile
    rows = 64
    w9 = w9_ref[0]
    cb = cb_ref[0]
    conv_blocks = [(r, s) for r in range(tm // rows) for s in range(nslab)]
    per_tile = -(-len(conv_blocks) // n_col)

    def step(cur_ref, nxt_ref):
        for t in range(n_col):
            cols = slice(t * col_tile, (t + 1) * col_tile)
            if cur_ref is not None:
                o_ref[:, cols] += jnp.dot(cur_ref[...], wd_ref[pl.ds(kk, tk), cols], preferred_element_type=F32)
            for r, s in (conv_blocks[t * per_tile:(t + 1) * per_tile] if nxt_ref is not None else ()):
                pos = (lax.broadcasted_iota(jnp.int32, (rows, 128), 0) + r * rows) % period
                lanes = slice(s * 128, (s + 1) * 128)
                acc = cb[:, lanes]
                for shift, widx, dx in taps:
                    xs = pad_ref[s, base + r * rows + shift:base + (r + 1) * rows + shift, :]
                    if dx < 0:
                        xs = jnp.where(pos >= -dx, xs, 0.0)
                    elif dx > 0:
                        xs = jnp.where(pos < period - dx, xs, 0.0)
                    acc = acc + xs * w9[widx:widx + 1, lanes]
                act = _silu(acc) * hu_ref[r * rows:(r + 1) * rows, lanes]
                nxt_ref[r * rows:(r + 1) * rows, lanes] = act.astype(BF16)

    nk_static = D_FF // tk
    bufs = (even_ref, odd_ref)

    @pl.when(k == 0)
    def _():
        step(None, even_ref)

    @pl.when(jnp.logical_and(jnp.logical_and(k > 0, k < nk), k % 2 == 0))
    def _():
        step(odd_ref, even_ref)

    @pl.when(jnp.logical_and(k < nk, k % 2 == 1))
    def _():
        step(even_ref, odd_ref)

    @pl.when(k == nk)
    def _():
        step(bufs[(nk_static - 1) % 2], None)

        def slab(si, _):
            r0 = pl.multiple_of(si * NORM_SLAB, NORM_SLAB)
            res = x_ref[pl.ds(r0, NORM_SLAB), :] + g_ref[0] * o_ref[pl.ds(r0, NORM_SLAB), :]
            if final_norm:
                res = res * lax.rsqrt(jnp.mean(res * res, axis=-1, keepdims=True) + EPS) * nf_ref[...]
            o_ref[pl.ds(r0, NORM_SLAB), :] = res
            return 0

        lax.fori_loop(0, tm // NORM_SLAB, slab, 0)


def _ffn_down(cfg, h, w9, bias, wd, x, gate, nseq, t, row0, on_grid, prev=None, norm_gain=None):
    m, d = x.shape
    tm, tk = FD_TM, FD_TK
    rows = nseq * t
    nk = D_FF // tk
    assert D_FF % tk == 0 and row0 % tm == 0 and rows % tm == 0
    assert (t % tm == 0 and tm % GRID_W == 0) if on_grid else tm % t == 0
    final_norm = norm_gain is not None
    aliased = prev is not None
    assert not (final_norm and aliased)
    rt0 = row0 // tm
    kc = lambda k: jnp.minimum(k, nk - 1)
    hpb = tm // FD_HALO
    in_specs = [pl.BlockSpec((tm, tk), lambda i, k: (rt0 + i, kc(k))),
                pl.BlockSpec((tm, tk), lambda i, k: (rt0 + i, nk + kc(k)))]
    args = [h, h]
    if on_grid:
        in_specs += [pl.BlockSpec((FD_HALO, tk), lambda i, k: (jnp.maximum((rt0 + i) * hpb - 1, 0), kc(k))),
                     pl.BlockSpec((FD_HALO, tk), lambda i, k: (jnp.minimum((rt0 + i + 1) * hpb, m // FD_HALO - 1), kc(k)))]
        args += [h, h]
    w9c = jnp.transpose(w9.reshape(FFN_CONV * FFN_CONV, nk, tk), (1, 0, 2))
    in_specs += [pl.BlockSpec((1, FFN_CONV * FFN_CONV, tk), lambda i, k: (kc(k), 0, 0)),
                 pl.BlockSpec((1, 1, tk), lambda i, k: (kc(k), 0, 0)),
                 pl.BlockSpec((D_FF, d), lambda i, k: (0, 0), pipeline_mode=pl.Buffered(1)),
                 pl.BlockSpec((tm, d), lambda i, k: (rt0 + i, 0)),
                 pl.BlockSpec((1, 1, d), lambda i, k: (_mod_row(cfg, (rt0 + i) * tm), 0, 0))]
    args += [w9c, bias.reshape(nk, 1, tk), wd, x, gate]
    if final_norm:
        in_specs.append(pl.BlockSpec((1, d), lambda i, k: (0, 0)))
        args.append(norm_gain)
    if aliased:
        in_specs = [pl.BlockSpec(memory_space=pl.ANY)] + in_specs
        args = [prev] + args
    out_rows, out_rt0 = (rows, 0) if final_norm else (m, rt0)
    return pl.pallas_call(
        functools.partial(_ffn_down_kernel, cfg=cfg, on_grid=on_grid, seq_len=t, final_norm=final_norm, aliased=aliased),
        grid=(rows // tm, nk + 1),
        in_specs=in_specs,
        out_specs=pl.BlockSpec((tm, d), lambda i, k: (out_rt0 + i, 0)),
        out_shape=jax.ShapeDtypeStruct((out_rows, d), F32),
        scratch_shapes=[pltpu.VMEM((tk // 128, tm + 2 * (FD_HALO + FD_TOP), 128), F32),
                        pltpu.VMEM((tm, tk), BF16),
                        pltpu.VMEM((tm, tk), BF16)],
        input_output_aliases={0: 0} if aliased else {},
        compiler_params=pltpu.CompilerParams(dimension_semantics=("parallel", "arbitrary"), vmem_limit_bytes=FD_VMEM),
        name="ffn_down",
    )(*args)


def _rmsnorm_kernel(x_ref, g_ref, o_ref):
    x = x_ref[...]
    o_ref[...] = x * lax.rsqrt(jnp.mean(x * x, axis=-1, keepdims=True) + EPS) * g_ref[...]


def _final_norm(x, gain, row0, rows, tm):
    d = x.shape[1]
    assert row0 % tm == 0 and rows % tm == 0
    return pl.pallas_call(
        _rmsnorm_kernel,
        grid=(rows // tm,),
        in_specs=[pl.BlockSpec((tm, d), lambda i: (row0 // tm + i, 0)),
                  pl.BlockSpec((1, d), lambda i: (0, 0))],
        out_specs=pl.BlockSpec((tm, d), lambda i: (i, 0)),
        out_shape=jax.ShapeDtypeStruct((rows, d), F32),
        compiler_params=_params(("parallel",)),
        name="final_norm",
    )(x, gain)


def _pack_w_in(w):
    s = _SRC
    cols = [w[:, s["mg"]:s["end"]], w[:, s["va"]:s["ga"]], w[:, s["ga"]:s["ra"]], w[:, s["qb"]:s["kb"]],
            w[:, s["kb"]:s["vb"]], w[:, s["vb"]:s["ob"]], w[:, s["ob"]:s["gb"]], w[:, s["xr"]:s["yr"]],
            w[:, s["yr"]:s["mg"]], w[:, s["qa"]:s["ka"]], w[:, s["ka"]:s["va"]], w[:, s["ra"]:s["qb"]],
            w[:, s["gb"]:s["xr"]]]
    used = sum(c.shape[1] for c in cols)
    cols.append(jnp.zeros((w.shape[0], NZ - used), w.dtype))
    return jnp.concatenate(cols, axis=1).astype(BF16)


def _pack_gate_w(gw, gb, lam):
    nb, blk = LRU_BLOCKS, LRU_BLOCK
    gw_p = jnp.transpose(gw, (2, 3, 0, 1, 4)).reshape(nb, blk, 4 * blk).astype(BF16)
    gb_p = jnp.transpose(gb.reshape(2, 2, nb, blk), (2, 0, 1, 3)).reshape(nb, 1, 4 * blk)
    lam_p = jnp.transpose(lam.reshape(2, nb, blk), (1, 0, 2)).reshape(nb, 1, 2 * blk)
    return gw_p, gb_p, lam_p


def _layer(cfg, x, mod, lp, states, final_gain=None):
    (n1, n2, w_in, gwa, gba, gng, mbif, mng, lcw, lcb, lgw, lgb, llam, wbr, bmg, wout, fup, fcw, fcb, fdown) = lp
    s_gla, s_mc, s_mn, s_mm, s_lru = states
    m = cfg.m
    sh1, sc1, g1, sh2, sc2, g2 = (mod[:, k * D_MODEL:(k + 1) * D_MODEL].reshape(MOD_ROWS, 1, D_MODEL) for k in range(6))

    proj_tm = PROJ_TM if (cfg.m_ctx % PROJ_TM == 0 and cfg.tl % PROJ_TM == 0) else 512
    z = _norm_matmul(cfg, x, n1.reshape(1, -1), sc1, sh1, _pack_w_in(w_in), proj_tm, 512)

    s0t = jnp.swapaxes(s_gla, -1, -2)
    o_f, sa_f = _gla(cfg, z, gwa[0], gba[0].reshape(1, -1), s0t[:, 0], False)
    y_a, sa_r = _gla(cfg, z, gwa[1], gba[1].reshape(1, -1), s0t[:, 1], True, o_f, gng.reshape(1, -1))
    new_gla =jnp.swapaxes(jnp.stack([sa_f, sa_r], axis=1), -1, -2)

    g_rows = z[:, Z_SM + SM_GB:Z_SM + SM_GB + 4 * MLSTM_HEADS].T
    k_t = z[:, Z_KB:Z_KB + MLSTM_HEADS * MLSTM_DH].T
    b_col = mbif.reshape(1, -1)
    b_row = mbif.reshape(-1, 1)
    n0 = s_mn[:, :, :, None, :]
    m0 = jnp.broadcast_to(s_mm[:, :, :, None, None], s_mm.shape + (1, 128))
    h_f, cb_f, nb_f, mb_f = _mlstm(cfg, z, k_t, g_rows, b_col, b_row, s_mc[:, 0], n0[:, 0], m0[:, 0], False)
    y_b, cb_r, nb_r, mb_r = _mlstm(cfg, z, k_t, g_rows, b_col, b_row, s_mc[:, 1], n0[:, 1], m0[:, 1], True,
                                   h_f, mng.reshape(1, -1))
    new_mc =jnp.stack([cb_f, cb_r], axis=1)
    new_mn = jnp.stack([nb_f[:, :, 0], nb_r[:, :, 0]], axis=1)
    new_mm = jnp.stack([mb_f[:, :, 0, 0], mb_r[:, :, 0, 0]], axis=1)

    gw_p, gb_p, lam_p = _pack_gate_w(lgw, lgb, llam)
    cbias = lcb.reshape(1, -1)
    y_c, new_lru = _rglru(cfg, z, lcw, cbias, gw_p, gb_p, lam_p, jnp.zeros((cfg.bc, N_DIR, LRU_WIDTH), F32),
                          cfg.bc, cfg.tc, 0)
    y_c, _ = _rglru(cfg, z, lcw, cbias, gw_p, gb_p, lam_p, s_lru, cfg.bl, cfg.tl, cfg.m_ctx, y_prev=y_c)

    merged = _merge(cfg, y_a, y_b, y_c, wbr.astype(BF16), z, bmg.reshape(N_BRANCH, 1, D_MODEL), 512, 1024)
    x = _matmul_res(cfg, merged, wout.astype(BF16), x, g1, 512, 1024)

    h = _norm_matmul(cfg, x, n2.reshape(1, -1), sc2, sh2, fup.astype(BF16), proj_tm, 512)
    w9 = fcw.reshape(FFN_CONV * FFN_CONV, D_FF)
    wd = fdown.astype(BF16)
    if final_gain is None:
        x_ctx = _ffn_down(cfg, h, w9, fcb, wd, x, g2, cfg.bc, cfg.tc, 0, False)
        x = _ffn_down(cfg, h, w9, fcb, wd, x, g2, cfg.bl, cfg.tl, cfg.m_ctx, True, prev=x_ctx)
    else:
        x = (_ffn_down(cfg, h, w9, fcb, wd, x, g2, cfg.bc, cfg.tc, 0, False, norm_gain=final_gain),
             _ffn_down(cfg, h, w9, fcb, wd, x, g2, cfg.bl, cfg.tl, cfg.m_ctx, True, norm_gain=final_gain))
    return x, (new_gla, new_mc, new_mn, new_mm, new_lru)


def kernel(x_prompt, x_sample, state_gla, state_mlstm_c, state_mlstm_n, state_mlstm_m, state_rglru, c, c_ctx, norm1_g, norm2_g, w_mod, b_mod, w_in, gla_w_alpha, gla_b_alpha, gla_norm_g, mlstm_b_if, mlstm_norm_g, lru_conv_w, lru_conv_b, lru_gate_w, lru_gate_b, lru_lambda, w_branch, b_merge, w_out, ffn_w_up, ffn_conv_w, ffn_conv_b, ffn_w_down, norm_f_g):
    bc, tc, d = x_prompt.shape
    bl, tl, _ = x_sample.shape
    cfg = Cfg(bc, tc, bl, tl)
    assert tc % TB == 0 and tl % TB == 0 and cfg.m_ctx % tl == 0 and 1 + bl <= MOD_ROWS
    assert bc % NSLOT == 0 and bl % NSLOT == 0
    depth = w_in.shape[0]

    x = jnp.concatenate([x_prompt.reshape(bc * tc, d), x_sample.reshape(bl * tl, d)], axis=0)
    c_all = jnp.concatenate([c_ctx[None, :], c, jnp.zeros((MOD_ROWS - 1 - bl, d), F32)], axis=0)
    mod = _modulation(c_all, w_mod, b_mod)

    new = []
    for l in range(depth):
        lp = (norm1_g[l], norm2_g[l], w_in[l], gla_w_alpha[l], gla_b_alpha[l], gla_norm_g[l], mlstm_b_if[l],
              mlstm_norm_g[l], lru_conv_w[l], lru_conv_b[l], lru_gate_w[l], lru_gate_b[l], lru_lambda[l],
              w_branch[l], b_merge[l], w_out[l], ffn_w_up[l], ffn_conv_w[l], ffn_conv_b[l], ffn_w_down[l])
        states = (state_gla[:, l], state_mlstm_c[:, l], state_mlstm_n[:, l], state_mlstm_m[:, l], state_rglru[:, l])
        x, st = _layer(cfg, x, mod[l], lp, states, norm_f_g.reshape(1, -1) if l == depth - 1 else None)
        new.append(st)

    y_prompt = x[0].reshape(bc, tc, d)
    y_sample = x[1].reshape(bl, tl, d)
    stacked = tuple(jnp.stack([new[l][k] for l in range(depth)], axis=1) for k in range(5))
    return (y_prompt, y_sample) + stacked
```

```python
import functools
from typing import NamedTuple

import numpy as np
import jax
import jax.numpy as jnp
from jax import lax
from jax.experimental import pallas as pl
from jax.experimental.pallas import tpu as pltpu

F32 = jnp.float32
BF16 = jnp.bfloat16

D_MODEL = 2048
DEPTH = 2
GRID_W = 64
N_DIR = 2
N_BRANCH = 3
BRANCH_W = 1024
GLA_HEADS = 4
GLA_DK = 128
GLA_DV = 256
GLA_RANK = 16
GLA_TAU = 16.0
MLSTM_HEADS = 4
MLSTM_DH = 256
CHUNK = 64
LRU_WIDTH = 1024
LRU_BLOCKS = 8
LRU_BLOCK = 128
LRU_CONV = 4
LRU_C = 8.0
D_FF = 5632
FFN_CONV = 3
EPS = 1e-6

_SRC = dict(qa=0, ka=512, va=1024, ga=2048, ra=3072, qb=3104, kb=4128, vb=5152, ob=6176, gb=7200,
            xr=7216, yr=8240, mg=9264, end=15408)
Z_MG, Z_VA, Z_GA, Z_QB, Z_KB, Z_VB, Z_OB, Z_XR, Z_YR, Z_QA, Z_KA, Z_SM = (
    0, 6144, 7168, 8192, 9216, 10240, 11264, 12288, 13312, 14336, 14848, 15360)
SM_W = 128
SM_RA, SM_GB = 0, 32
NZ = 15872
MOD_ROWS = 16

PROJ_TM = 1024
NORM_SLAB = 32
PROJ_SLAB = 64
TB = 256
NCH = TB // CHUNK
VMEM_LIMIT = 48 * 1024 * 1024


class Cfg(NamedTuple):
    bc: int
    tc: int
    bl: int
    tl: int

    @property
    def m_ctx(self):
        return self.bc * self.tc

    @property
    def m(self):
        return self.bc * self.tc + self.bl * self.tl


def _params(sem):
    return pltpu.CompilerParams(dimension_semantics=sem, vmem_limit_bytes=VMEM_LIMIT)


def _softplus(x):
    return jnp.maximum(x, 0.0) + jnp.log1p(jnp.exp(-jnp.abs(x)))


def _log_sigmoid(x):
    return -_softplus(-x)


def _silu(x):
    return x * jax.nn.sigmoid(x)


def _gelu_tanh(x):
    return 0.5 * x * (1.0 + jnp.tanh(np.sqrt(2.0 / np.pi).astype(np.float32) * (x + 0.044715 * (x * x * x))))


def _mod_row(cfg, row):
    return jnp.where(row < cfg.m_ctx, 0, 1 + jnp.maximum(row - cfg.m_ctx, 0) // cfg.tl)


def _mod_kernel(c_ref, w_ref, b_ref, o_ref):
    c = c_ref[...]
    a = _silu(c).astype(BF16)
    o_ref[0] = jnp.dot(a, w_ref[0].astype(BF16), preferred_element_type=F32) + b_ref[0]


def _modulation(c_all, w_mod, b_mod):
    depth, d, n = w_mod.shape
    tn = 512
    return pl.pallas_call(
        _mod_kernel,
        grid=(depth, n // tn),
        in_specs=[pl.BlockSpec((MOD_ROWS, d), lambda l, j: (0, 0)),
                  pl.BlockSpec((1, d, tn), lambda l, j: (l, 0, j)),
                  pl.BlockSpec((1, 1, tn), lambda l, j: (l, 0, j))],
        out_specs=pl.BlockSpec((1, MOD_ROWS, tn), lambda l, j: (l, 0, j)),
        out_shape=jax.ShapeDtypeStruct((depth, MOD_ROWS, n), F32),
        compiler_params=_params(("parallel", "parallel")),
        name="modulation",
    )(c_all, w_mod, b_mod.reshape(depth, 1, n))


def _norm_rows(x_ref, g_ref, sc_ref, sh_ref, u_ref, r0, nrows):
    x = x_ref[pl.ds(r0, nrows), :]
    y = x * lax.rsqrt(jnp.mean(x * x, axis=-1, keepdims=True) + EPS) * g_ref[...]
    u_ref[pl.ds(r0, nrows), :] = (y * (1.0 + sc_ref[0]) + sh_ref[0]).astype(BF16)


def _norm_matmul_kernel(x0_ref, xn_ref, g_ref, sc0_ref, sh0_ref, scn_ref, shn_ref, w_ref, o_ref, ua_ref, ub_ref, *, slab):
    i, j = pl.program_id(0), pl.program_id(1)
    tm = xn_ref.shape[0]

    @pl.when(jnp.logical_and(i == 0, j == 0))
    def _():
        def first(si, _):
            _norm_rows(x0_ref, g_ref, sc0_ref, sh0_ref, ua_ref, pl.multiple_of(si * NORM_SLAB, NORM_SLAB), NORM_SLAB)
            return 0

        lax.fori_loop(0, tm // NORM_SLAB, first, 0)

    r0 = pl.multiple_of(jnp.minimum(j, tm // slab - 1) * slab, slab)

    @pl.when(i % 2 == 0)
    def _():
        _norm_rows(xn_ref, g_ref, scn_ref, shn_ref, ub_ref, r0, slab)
        o_ref[...] = jnp.dot(ua_ref[...], w_ref[...], preferred_element_type=F32)

    @pl.when(i % 2 == 1)
    def _():
        _norm_rows(xn_ref, g_ref, scn_ref, shn_ref, ua_ref, r0, slab)
        o_ref[...] = jnp.dot(ub_ref[...], w_ref[...], preferred_element_type=F32)


def _norm_matmul(cfg, x, gain, sc, sh, w, tm, tn):
    m, d = x.shape
    n = w.shape[1]
    slab = PROJ_SLAB
    assert tm % slab == 0 and n // tn >= tm // slab
    nxt = lambda i: jnp.minimum(i + 1, m // tm - 1)
    mod0 = lambda i, j: (_mod_row(cfg, 0), 0, 0)
    modn = lambda i, j: (_mod_row(cfg, nxt(i) * tm), 0, 0)
    return pl.pallas_call(
        functools.partial(_norm_matmul_kernel, slab=slab),
        grid=(m // tm, n // tn),
        in_specs=[pl.BlockSpec((tm, d), lambda i, j: (0, 0), pipeline_mode=pl.Buffered(1)),
                  pl.BlockSpec((tm, d), lambda i, j: (nxt(i), 0)),
                  pl.BlockSpec((1, d), lambda i, j: (0, 0)),
                  pl.BlockSpec((1, 1, d), mod0),
                  pl.BlockSpec((1, 1, d), mod0),
                  pl.BlockSpec((1, 1, d), modn),
                  pl.BlockSpec((1, 1, d), modn),
                  pl.BlockSpec((d, tn), lambda i, j: (0, j))],
        out_specs=pl.BlockSpec((tm, tn), lambda i, j: (i, j)),
        out_shape=jax.ShapeDtypeStruct((m, n), F32),
        scratch_shapes=[pltpu.VMEM((tm, d), BF16), pltpu.VMEM((tm, d), BF16)],
        compiler_params=_params(("arbitrary", "arbitrary")),
        name="norm_matmul",
    )(x, x, gain, sc, sh, sc, sh, w)


NSLOT = 2


def _pair_pos(cfg, i):
    kc, kl = cfg.tc // TB, cfg.tl // TB
    nc = (cfg.bc // NSLOT) * kc
    is_ctx = i < nc
    il = jnp.maximum(i - nc, 0)
    j = jnp.where(is_ctx, i % kc, il % kl)
    k = jnp.where(is_ctx, kc, kl)
    s = jnp.where(is_ctx, i // kc, il // kl)
    return is_ctx, s, j, k


def _pair_row_block(cfg, reverse, slot, i):
    kc, kl = cfg.tc // TB, cfg.tl // TB
    is_ctx, s, j, k = _pair_pos(cfg, i)
    jj = (k - 1 - j) if reverse else j
    return jnp.where(is_ctx, (s + slot * (cfg.bc // NSLOT)) * kc + jj,
                     cfg.bc * kc + (s + slot * (cfg.bl // NSLOT)) * kl + jj)


def _pair_local_block(cfg, reverse, i):
    kc, kl = cfg.tc // TB, cfg.tl // TB
    is_ctx, s, j, k = _pair_pos(cfg, i)
    jj = (k - 1 - j) if reverse else j
    return jnp.where(is_ctx, s * kc + jj, (cfg.bc // NSLOT) * kc + s * kl + jj)


def _pair_lat_seq(cfg, i):
    is_ctx, s, _, _ = _pair_pos(cfg, i)
    return jnp.where(is_ctx, 0, jnp.minimum(s, cfg.bl // NSLOT - 1))


def _pair_ctx_seq(cfg, i):
    is_ctx, s, _, _ = _pair_pos(cfg, i)
    return jnp.where(is_ctx, s, cfg.bc // NSLOT - 1)


def _slot_of_tile(cfg, tm, i):
    nc, nl = cfg.m_ctx // tm // NSLOT, (cfg.m - cfg.m_ctx) // tm // NSLOT
    il = jnp.maximum(i - NSLOT * nc, 0)
    is_ctx = i < NSLOT * nc
    return jnp.where(is_ctx, i // nc, il // nl), jnp.where(is_ctx, i % nc, nc + il % nl)


def _tri(reverse, n=CHUNK):
    row = lax.broadcasted_iota(jnp.int32, (n, n), 0)
    col = lax.broadcasted_iota(jnp.int32, (n, n), 1)
    return (row <= col) if reverse else (row >= col)


def _head_norm(x, g):
    return x * lax.rsqrt(jnp.mean(x * x, axis=-1, keepdims=True) + EPS) * g


_NT = (((1,), (1,)), ((), ()))
_TN = (((0,), (0,)), ((), ()))


GLA_LOCKSTEP = 2


def _gla_kernel(*refs, cfg, reverse):
    it = iter(refs)
    blocks = [[next(it) for _ in range(4)] for _ in range(NSLOT)]
    wal_ref, bal_ref = next(it), next(it)
    init_ref = next(it)
    init = [init_ref.at[b] for b in range(NSLOT)]
    if reverse:
        gates = [next(it) for _ in range(NSLOT)]
        of_ref, gn_ref = next(it), next(it)
        epi = [[gates[b], of_ref.at[b]] for b in range(NSLOT)]
    out_ref, fin_ref = next(it), next(it)
    outs = [out_ref.at[b] for b in range(NSLOT)]
    fin = [fin_ref.at[b] for b in range(NSLOT)]
    st_ref, la_ref = next(it), next(it)
    is_ctx, _, j, k = _pair_pos(cfg, pl.program_id(0))
    nh = GLA_HEADS

    @pl.when(jnp.logical_and(j == 0, is_ctx))
    def _():
        st_ref[...] = jnp.zeros_like(st_ref)

    @pl.when(jnp.logical_and(j == 0, jnp.logical_not(is_ctx)))
    def _():
        for b in range(NSLOT):
            st_ref[b * nh:(b + 1) * nh] = init[b][0]

    d = 1 if reverse else 0
    slots = range(NSLOT)
    both = lambda f: [f(b) for b in slots]
    wal = wal_ref[...].astype(BF16)
    for b in slots:
        ra = blocks[b][3][:, SM_RA + d * GLA_RANK:SM_RA + (d + 1) * GLA_RANK].astype(BF16)
        pre = jnp.dot(ra, wal, preferred_element_type=F32) + bal_ref[...]
        la_ref[b] = _log_sigmoid(pre) * (1.0 / GLA_TAU)

    tri = _tri(reverse)
    tri_f = tri.astype(F32)
    last = 0 if reverse else CHUNK - 1
    qscale = GLA_DK ** -0.5
    for c in (range(NCH - 1, -1, -1) if reverse else range(NCH)):
        rows = slice(c * CHUNK, (c + 1) * CHUNK)
        cum = both(lambda b: jnp.dot(tri_f, la_ref[b, rows, :], precision=lax.Precision.HIGHEST,
                                     preferred_element_type=F32))
        tot = both(lambda b: cum[b][last:last + 1, :])
        e_q = both(lambda b: jnp.exp(cum[b]))
        e_k = both(lambda b: jnp.exp(-cum[b]))
        e_end = both(lambda b: jnp.exp(tot[b] - cum[b]))
        e_tot = both(lambda b: jnp.exp(tot[b]))
        for h0 in range(0, GLA_HEADS, GLA_LOCKSTEP):
            units = [(b, h0 + dh) for dh in range(GLA_LOCKSTEP) for b in slots]
            each = lambda f: [f(u, b, h) for u, (b, h) in enumerate(units)]
            kc = [slice(h * GLA_DK, (h + 1) * GLA_DK) for _, h in units]
            vc = [slice(h * GLA_DV, (h + 1) * GLA_DV) for _, h in units]
            sh = [b * nh + h for b, h in units]
            qh = each(lambda u, b, h: blocks[b][0][rows, kc[u]] * qscale)
            kh = each(lambda u, b, h: blocks[b][1][rows, kc[u]])
            vh = each(lambda u, b, h: blocks[b][2][rows, vc[u]].astype(BF16))
            q_in = each(lambda u, b, h: (qh[u] * e_q[b][:, kc[u]]).astype(BF16))
            k_in = each(lambda u, b, h: (kh[u] * e_k[b][:, kc[u]]).astype(BF16))
            att = each(lambda u, b, h: lax.dot_general(q_in[u], k_in[u], _NT, preferred_element_type=F32))
            att = each(lambda u, b, h: jnp.where(tri, att[u], 0.0).astype(BF16))
            st = each(lambda u, b, h: st_ref[sh[u]])
            o = each(lambda u, b, h: jnp.dot(att[u], vh[u], preferred_element_type=F32)
                     + lax.dot_general(q_in[u], st[u].astype(BF16), _NT, preferred_element_type=F32))
            k_end = each(lambda u, b, h: (kh[u] * e_end[b][:, kc[u]]).astype(BF16))
            upd = each(lambda u, b, h: lax.dot_general(vh[u], k_end[u], _TN, preferred_element_type=F32))
            for u, (b, h) in enumerate(units):
                st_ref[sh[u]] = e_tot[b][:, kc[u]] * st[u] + upd[u]
            for u, (b, h) in enumerate(units):
                if reverse:
                    ga_ref, of_ref = epi[b]
                    y = _head_norm(of_ref[rows, vc[u]] + o[u], gn_ref[:, vc[u]]) * _silu(ga_ref[rows, vc[u]])
                    outs[b][rows, vc[u]] = y.astype(outs[b].dtype)
                else:
                    outs[b][rows, vc[u]] = o[u]

    @pl.when(jnp.logical_and(is_ctx, j == k - 1))
    def _():
        for b in range(NSLOT):
            fin[b][0] = st_ref[b * nh:(b + 1) * nh]


def _gla(cfg, z, wal, bal, s0t, reverse, o_fwd=None, gnorm=None):
    m = cfg.m
    nh = GLA_HEADS
    hk, hv = nh * GLA_DK, nh * GLA_DV
    lb = functools.partial(_pair_local_block, cfg, reverse)
    in_specs, args = [], []
    for b in range(NSLOT):
        rb = functools.partial(_pair_row_block, cfg, reverse, b)
        in_specs += [pl.BlockSpec((TB, hk), lambda i, rb=rb: (rb(i), Z_QA // hk)),
                     pl.BlockSpec((TB, hk), lambda i, rb=rb: (rb(i), Z_KA // hk)),
                     pl.BlockSpec((TB, hv), lambda i, rb=rb: (rb(i), Z_VA // hv)),
                     pl.BlockSpec((TB, SM_W), lambda i, rb=rb: (rb(i), Z_SM // SM_W))]
        args += [z, z, z, z]
    in_specs += [pl.BlockSpec((GLA_RANK, hk), lambda i: (0, 0)), pl.BlockSpec((1, hk), lambda i: (0, 0))]
    args += [wal, bal]
    half_c, half_l = cfg.bc // NSLOT, cfg.bl // NSLOT
    in_specs.append(pl.BlockSpec((NSLOT, 1, nh, GLA_DV, GLA_DK), lambda i: (0, _pair_lat_seq(cfg, i), 0, 0, 0)))
    args.append(s0t.reshape(NSLOT, half_l, nh, GLA_DV, GLA_DK))
    if reverse:
        for b in range(NSLOT):
            rb = functools.partial(_pair_row_block, cfg, reverse, b)
            in_specs.append(pl.BlockSpec((TB, hv), lambda i, rb=rb: (rb(i), Z_GA // hv)))
            args.append(z)
        in_specs += [pl.BlockSpec((NSLOT, TB, hv), lambda i: (0, lb(i), 0)), pl.BlockSpec((1, hv), lambda i: (0, 0))]
        args += [o_fwd, gnorm]
    out, fin = pl.pallas_call(
        functools.partial(_gla_kernel, cfg=cfg, reverse=reverse),
        grid=(m // TB // NSLOT,),
        in_specs=in_specs,
        out_specs=[pl.BlockSpec((NSLOT, TB, hv), lambda i: (0, lb(i), 0)),
                   pl.BlockSpec((NSLOT, 1, nh, GLA_DV, GLA_DK), lambda i: (0, _pair_ctx_seq(cfg, i), 0, 0, 0))],
        out_shape=[jax.ShapeDtypeStruct((NSLOT, m // NSLOT, hv), BF16 if reverse else F32),
                   jax.ShapeDtypeStruct((NSLOT, half_c, nh, GLA_DV, GLA_DK), F32)],
        scratch_shapes=[pltpu.VMEM((NSLOT * nh, GLA_DV, GLA_DK), F32), pltpu.VMEM((NSLOT, TB, hk), F32)],
        compiler_params=_params(("arbitrary",)),
        name="gla_rev" if reverse else "gla_fwd",
    )(*args)
    return out, fin.reshape(cfg.bc, nh, GLA_DV, GLA_DK)


MCH = 128
NMC = TB // MCH
MLSTM_LOCKSTEP = 2


def _mlstm_kernel(*refs, cfg, reverse):
    it = iter(refs)
    blocks = [[next(it) for _ in range(6)] for _ in range(NSLOT)]
    bc_ref, br_ref = next(it), next(it)
    init_refs = [next(it) for _ in range(3)]
    init = [[r.at[b] for r in init_refs] for b in range(NSLOT)]
    if reverse:
        gates = [next(it) for _ in range(NSLOT)]
        hf_all, gn_ref = next(it), next(it)
        epi = [[gates[b], hf_all.at[b]] for b in range(NSLOT)]
    out_ref = next(it)
    outs = [out_ref.at[b] for b in range(NSLOT)]
    fin_refs = [next(it) for _ in range(3)]
    fin = [[r.at[b] for r in fin_refs] for b in range(NSLOT)]
    c_ref, n_ref, m_ref = next(it), next(it), next(it)
    is_ctx, _, j, k = _pair_pos(cfg, pl.program_id(0))
    nh = MLSTM_HEADS

    @pl.when(jnp.logical_and(j == 0, is_ctx))
    def _():
        c_ref[...] = jnp.zeros_like(c_ref)
        n_ref[...] = jnp.zeros_like(n_ref)
        m_ref[...] = jnp.zeros_like(m_ref)

    @pl.when(jnp.logical_and(j == 0, jnp.logical_not(is_ctx)))
    def _():
        for b in range(NSLOT):
            c_ref[b * nh:(b + 1) * nh] = init[b][0][0]
            n_ref[b * nh:(b + 1) * nh] = init[b][1][0]
            m_ref[b * nh:(b + 1) * nh] = init[b][2][0]

    d = 1 if reverse else 0
    tri = _tri(reverse, MCH)
    tri_f = tri.astype(F32)
    tri_tf = _tri(not reverse, MCH).astype(F32)
    last = 0 if reverse else MCH - 1
    ng = 2 * MLSTM_HEADS
    kscale = MLSTM_DH ** -0.5
    slots = range(NSLOT)
    both = lambda f: [f(b) for b in slots]
    hp = lax.Precision.HIGHEST
    for c in (range(NMC - 1, -1, -1) if reverse else range(NMC)):
        rows = slice(c * MCH, (c + 1) * MCH)
        g_col = both(lambda b: blocks[b][4][rows, SM_GB:SM_GB + 2 * ng] + bc_ref[...])
        g_row = both(lambda b: blocks[b][5][:, rows] + br_ref[...])
        fcum_col = both(lambda b: jnp.dot(tri_f, _log_sigmoid(g_col[b]), precision=hp, preferred_element_type=F32))
        fcum_row = both(lambda b: jnp.dot(_log_sigmoid(g_row[b]), tri_tf, precision=hp, preferred_element_type=F32))
        for h0 in range(0, MLSTM_HEADS, MLSTM_LOCKSTEP):
            units = [(b, h0 + dh) for dh in range(MLSTM_LOCKSTEP) for b in slots]
            each = lambda f: [f(u, b, h) for u, (b, h) in enumerate(units)]
            ii = [d * ng + h for _, h in units]
            fi = [d * ng + MLSTM_HEADS + h for _, h in units]
            hc = [slice(h * MLSTM_DH, (h + 1) * MLSTM_DH) for _, h in units]
            sh = [b * nh + h for b, h in units]
            f_col = each(lambda u, b, h: fcum_col[b][:, fi[u]:fi[u] + 1])
            f_row = each(lambda u, b, h: fcum_row[b][fi[u]:fi[u] + 1, :])
            i_col = each(lambda u, b, h: g_col[b][:, ii[u]:ii[u] + 1])
            i_row = each(lambda u, b, h: g_row[b][ii[u]:ii[u] + 1, :])
            m_prev = each(lambda u, b, h: m_ref[sh[u]][:, 0:1])
            dlog = each(lambda u, b, h: jnp.where(tri, f_col[u] - f_row[u] + i_row[u], -jnp.inf))
            prev = each(lambda u, b, h: f_col[u] + m_prev[u])
            mj = each(lambda u, b, h: jnp.maximum(prev[u], jnp.max(dlog[u], axis=-1, keepdims=True)))
            w = each(lambda u, b, h: jnp.exp(dlog[u] - mj[u]))
            wp = each(lambda u, b, h: jnp.exp(prev[u] - mj[u]))
            qh = each(lambda u, b, h: blocks[b][0][rows, hc[u]])
            kh = each(lambda u, b, h: blocks[b][1][rows, hc[u]] * kscale)
            vh = each(lambda u, b, h: blocks[b][2][rows, hc[u]].astype(BF16))
            s = each(lambda u, b, h: lax.dot_general(qh[u].astype(BF16), kh[u].astype(BF16), _NT,
                                                     preferred_element_type=F32) * w[u])
            qp = each(lambda u, b, h: qh[u] * wp[u])
            cm = each(lambda u, b, h: c_ref[sh[u]])
            nv = each(lambda u, b, h: n_ref[sh[u]])
            m_new = each(lambda u, b, h: mj[u][last:last + 1, :])
            tot = each(lambda u, b, h: f_col[u][last:last + 1, :])
            wl_col = each(lambda u, b, h: jnp.exp(tot[u] - f_col[u] + i_col[u] - m_new[u]))
            wl_row = each(lambda u, b, h: jnp.exp(tot[u] - f_row[u] + i_row[u] - m_new[u]))
            decay = each(lambda u, b, h: jnp.exp(tot[u] + m_prev[u] - m_new[u]))
            kw_t = each(lambda u, b, h: (blocks[b][3][hc[u], rows] * kscale) * wl_row[u])
            sv = each(lambda u, b, h: jnp.dot(jnp.concatenate([s[u].astype(BF16), kw_t[u].astype(BF16)], axis=0),
                                              vh[u], preferred_element_type=F32))
            num = each(lambda u, b, h: sv[u][:MCH]
                       + jnp.dot(qp[u].astype(BF16), cm[u].astype(BF16), preferred_element_type=F32))
            upd = each(lambda u, b, h: sv[u][MCH:])
            den = each(lambda u, b, h: jnp.sum(s[u], axis=-1, keepdims=True)
                       + jnp.sum(qp[u] * nv[u], axis=-1, keepdims=True))
            hh = each(lambda u, b, h: num[u] / jnp.maximum(jnp.abs(den[u]), jnp.exp(-mj[u])))
            for u, (b, h) in enumerate(units):
                c_ref[sh[u]] = decay[u] * cm[u] + upd[u]
                n_ref[sh[u]] = decay[u] * nv[u] + jnp.sum(kh[u] * wl_col[u], axis=0, keepdims=True)
                m_ref[sh[u]] = jnp.broadcast_to(m_new[u], (1, 128))
            for u, (b, h) in enumerate(units):
                if reverse:
                    ob_ref, hf_ref = epi[b]
                    y = jax.nn.sigmoid(ob_ref[rows, hc[u]]) * _head_norm(hf_ref[rows, hc[u]] + hh[u], gn_ref[:, hc[u]])
                    outs[b][rows, hc[u]] = y.astype(outs[b].dtype)
                else:
                    outs[b][rows, hc[u]] = hh[u]

    @pl.when(jnp.logical_and(is_ctx, j == k - 1))
    def _():
        for b in range(NSLOT):
            fin[b][0][0] = c_ref[b * nh:(b + 1) * nh]
            fin[b][1][0] = n_ref[b * nh:(b + 1) * nh]
            fin[b][2][0] = m_ref[b * nh:(b + 1) * nh]


def _mlstm(cfg, z, k_t, g_rows, b_col, b_row, c0, n0, m0, reverse, h_fwd=None, gnorm=None):
    m = cfg.m
    hd = MLSTM_HEADS * MLSTM_DH
    nh = MLSTM_HEADS
    lb = functools.partial(_pair_local_block, cfg, reverse)
    ctx = lambda i: (0, _pair_ctx_seq(cfg, i), 0, 0, 0)
    in_specs, args = [], []
    for b in range(NSLOT):
        rb = functools.partial(_pair_row_block, cfg, reverse, b)
        in_specs += [pl.BlockSpec((TB, hd), lambda i, rb=rb: (rb(i), Z_QB // hd)),
                     pl.BlockSpec((TB, hd), lambda i, rb=rb: (rb(i), Z_KB // hd)),
                     pl.BlockSpec((TB, hd), lambda i, rb=rb: (rb(i), Z_VB // hd)),
                     pl.BlockSpec((hd, TB), lambda i, rb=rb: (0, rb(i))),
                     pl.BlockSpec((TB, SM_W), lambda i, rb=rb: (rb(i), Z_SM // SM_W)),
                     pl.BlockSpec((4 * nh, TB), lambda i, rb=rb: (0, rb(i)))]
        args += [z, z, z, k_t, z, g_rows]
    in_specs += [pl.BlockSpec((1, 4 * nh), lambda i: (0, 0)), pl.BlockSpec((4 * nh, 1), lambda i: (0, 0))]
    args += [b_col, b_row]
    half_c, half_l = cfg.bc // NSLOT, cfg.bl // NSLOT
    lat = lambda i: (0, _pair_lat_seq(cfg, i), 0, 0, 0)
    state_dims = ((nh, MLSTM_DH, MLSTM_DH), (nh, 1, MLSTM_DH), (nh, 1, 128))
    in_specs += [pl.BlockSpec((NSLOT, 1) + sd, lat) for sd in state_dims]
    args += [s.reshape((NSLOT, half_l) + sd) for s, sd in zip((c0, n0, m0), state_dims)]
    if reverse:
        for b in range(NSLOT):
            rb = functools.partial(_pair_row_block, cfg, reverse, b)
            in_specs.append(pl.BlockSpec((TB, hd), lambda i, rb=rb: (rb(i), Z_OB // hd)))
            args.append(z)
        in_specs += [pl.BlockSpec((NSLOT, TB, hd), lambda i: (0, lb(i), 0)), pl.BlockSpec((1, hd), lambda i: (0, 0))]
        args += [h_fwd, gnorm]
    res = pl.pallas_call(
        functools.partial(_mlstm_kernel, cfg=cfg, reverse=reverse),
        grid=(m // TB // NSLOT,),
        in_specs=in_specs,
        out_specs=[pl.BlockSpec((NSLOT, TB, hd), lambda i: (0, lb(i), 0))]
                  + [pl.BlockSpec((NSLOT, 1) + sd, ctx) for sd in state_dims],
        out_shape=[jax.ShapeDtypeStruct((NSLOT, m // NSLOT, hd), BF16 if reverse else F32)]
                  + [jax.ShapeDtypeStruct((NSLOT, half_c) + sd, F32) for sd in state_dims],
        scratch_shapes=[pltpu.VMEM((NSLOT * nh, MLSTM_DH, MLSTM_DH), F32),
                        pltpu.VMEM((NSLOT * nh, 1, MLSTM_DH), F32),
                        pltpu.VMEM((NSLOT * nh, 1, 128), F32)],
        compiler_params=_params(("arbitrary",)),
        name="mlstm_rev" if reverse else "mlstm_fwd",
    )(*args)
    return (res[0],) + tuple(r.reshape((cfg.bc,) + sd) for r, sd in zip(res[1:], state_dims))


LRU_ROWS = 256
LRU_PAD = 8
LRU_SEG = 8


def _rglru_kernel(*refs, t, aliased):
    if aliased:
        refs = refs[1:]
    (xr_ref, yr_ref, cw_ref, cb_ref, gw_ref, gb_ref, lam_ref, h0_ref, y_ref, he_ref,
     pad_ref, af_ref, ar_ref, bf_ref, br_ref, hl_ref, pc_ref, hs_ref) = refs
    a_refs, b_refs = (af_ref, ar_ref), (bf_ref, br_ref)
    nsteps = t // LRU_ROWS
    seg = t // LRU_SEG
    pitch = seg + 8
    zeros = jnp.zeros((LRU_PAD, LRU_BLOCK), F32)
    pad_ref[0:LRU_PAD, :] = zeros
    pad_ref[LRU_PAD + t:2 * LRU_PAD + t, :] = zeros

    def pitched(r0):
        if seg >= LRU_ROWS:
            return [(0, LRU_ROWS, pl.multiple_of((r0 // seg) * pitch + r0 % seg, 8))]
        return [(q * seg, seg, pl.multiple_of((r0 // seg + q) * pitch, 8)) for q in range(LRU_ROWS // seg)]

    def fill(ci, _):
        r0 = pl.multiple_of(ci * LRU_ROWS, LRU_ROWS)
        pad_ref[pl.ds(r0 + LRU_PAD, LRU_ROWS), :] = xr_ref[pl.ds(r0, LRU_ROWS), :]
        return 0

    lax.fori_loop(0, nsteps, fill, 0)

    sp = _softplus(-lam_ref[0])
    left = LRU_CONV // 2

    def gates(ci, _):
        r0 = pl.multiple_of(ci * LRU_ROWS, LRU_ROWS)
        xc = cb_ref[...]
        for tap in range(LRU_CONV):
            xc = xc + cw_ref[tap:tap + 1, :] * pad_ref[pl.ds(r0 + LRU_PAD - left + tap, LRU_ROWS), :]
        pre = jnp.dot(xc.astype(BF16), gw_ref[0], preferred_element_type=F32) + gb_ref[0]
        both = lambda f: [f(d) for d in range(N_DIR)]
        r = both(lambda d: jax.nn.sigmoid(pre[:, d * 2 * LRU_BLOCK:d * 2 * LRU_BLOCK + LRU_BLOCK]))
        ig = both(lambda d: jax.nn.sigmoid(pre[:, d * 2 * LRU_BLOCK + LRU_BLOCK:(d + 1) * 2 * LRU_BLOCK]))
        log_a = both(lambda d: (-LRU_C * r[d]) * sp[:, d * LRU_BLOCK:(d + 1) * LRU_BLOCK])
        a = both(lambda d: jnp.exp(log_a[d]))
        b = both(lambda d: jnp.sqrt(1.0 - a[d] * a[d]) * (ig[d] * xc))
        for off, n, dst in pitched(r0):
            for d in range(N_DIR):
                a_refs[d][pl.ds(dst, n), :] = a[d][off:off + n]
                b_refs[d][pl.ds(dst, n), :] = b[d][off:off + n]
        return 0

    lax.fori_loop(0, nsteps, gates, 0)

    def local_scan(i, carry):
        hf, pf, hr, pr = carry
        rf, rr = i, seg - 1 - i
        a_f = af_ref[pl.ds(rf, LRU_SEG, stride=pitch), :]
        b_f = bf_ref[pl.ds(rf, LRU_SEG, stride=pitch), :]
        a_r = ar_ref[pl.ds(rr, LRU_SEG, stride=pitch), :]
        b_r = br_ref[pl.ds(rr, LRU_SEG, stride=pitch), :]
        hf, pf = a_f * hf + b_f, a_f * pf
        hr, pr = a_r * hr + b_r, a_r * pr
        hl_ref[0, rf] = hf
        pc_ref[0, rf] = pf
        hl_ref[1, rr] = hr
        pc_ref[1, rr] = pr
        return hf, pf, hr, pr

    zero = jnp.zeros((LRU_SEG, LRU_BLOCK), F32)
    one = jnp.ones((LRU_SEG, LRU_BLOCK), F32)
    hf, pf, hr, pr = lax.fori_loop(0, seg, local_scan, (zero, one, zero, one), unroll=4)

    enter_f = [h0_ref[0, 0:1, :]]
    for s in range(LRU_SEG - 1):
        enter_f.append(pf[s:s + 1, :] * enter_f[s] + hf[s:s + 1, :])
    he_ref[0, 0:1, :] = pf[LRU_SEG - 1:LRU_SEG, :] * enter_f[LRU_SEG - 1] + hf[LRU_SEG - 1:LRU_SEG, :]
    enter_r = [None] * LRU_SEG
    enter_r[LRU_SEG - 1] = h0_ref[0, 1:2, :]
    for s in range(LRU_SEG - 1, 0, -1):
        enter_r[s - 1] = pr[s:s + 1, :] * enter_r[s] + hr[s:s + 1, :]
    he_ref[0, 1:2, :] = pr[0:1, :] * enter_r[0] + hr[0:1, :]
    init_f = jnp.concatenate(enter_f, axis=0)
    init_r = jnp.concatenate(enter_r, axis=0)

    def fix_up(i, _):
        h = (hl_ref[0, i] + pc_ref[0, i] * init_f) + (hl_ref[1, i] + pc_ref[1, i] * init_r)
        hs_ref[pl.ds(i, LRU_SEG, stride=pitch), :] = h
        return 0

    lax.fori_loop(0, seg, fix_up, 0, unroll=4)

    def finish(ci, _):
        r0 = pl.multiple_of(ci * LRU_ROWS, LRU_ROWS)
        for off, n, src in pitched(r0):
            y = hs_ref[pl.ds(src, n), :] * _gelu_tanh(yr_ref[pl.ds(r0 + off, n), :])
            y_ref[pl.ds(r0 + off, n), :] = y.astype(y_ref.dtype)
        return 0

    lax.fori_loop(0, nsteps, finish, 0)


def _rglru(cfg, z, cw, cb, gw, gb, lam, h0, nseq, t, row0, y_prev=None):
    m = cfg.m
    seg = t // LRU_SEG
    assert row0 % t == 0 and t % LRU_ROWS == 0 and seg % 16 == 0 and (seg % LRU_ROWS == 0 or LRU_ROWS % seg == 0)
    pitched_rows = LRU_SEG * (seg + 8)
    sb = row0 // t
    aliased = y_prev is not None
    in_specs = [pl.BlockSpec((t, LRU_BLOCK), lambda b, n: (sb + b, Z_XR // LRU_BLOCK + n)),
                pl.BlockSpec((t, LRU_BLOCK), lambda b, n: (sb + b, Z_YR // LRU_BLOCK + n)),
                pl.BlockSpec((LRU_CONV, LRU_BLOCK), lambda b, n: (0, n)),
                pl.BlockSpec((1, LRU_BLOCK), lambda b, n: (0, n)),
                pl.BlockSpec((1, LRU_BLOCK, 4 * LRU_BLOCK), lambda b, n: (n, 0, 0)),
                pl.BlockSpec((1, 1, 4 * LRU_BLOCK), lambda b, n: (n, 0, 0)),
                pl.BlockSpec((1, 1, 2 * LRU_BLOCK), lambda b, n: (n, 0, 0)),
                pl.BlockSpec((1, N_DIR, LRU_BLOCK), lambda b, n: (b, 0, n))]
    args = [z, z, cw, cb, gw, gb, lam, h0]
    if aliased:
        in_specs = [pl.BlockSpec(memory_space=pl.ANY)] + in_specs
        args = [y_prev] + args
    return pl.pallas_call(
        functools.partial(_rglru_kernel, t=t, aliased=aliased),
        grid=(nseq, LRU_BLOCKS),
        in_specs=in_specs,
        out_specs=[pl.BlockSpec((t, LRU_BLOCK), lambda b, n: (sb + b, n)),
                   pl.BlockSpec((1, N_DIR, LRU_BLOCK), lambda b, n: (b, 0, n))],
        out_shape=[jax.ShapeDtypeStruct((m, LRU_WIDTH), BF16),
                   jax.ShapeDtypeStruct((nseq, N_DIR, LRU_WIDTH), F32)],
        scratch_shapes=[pltpu.VMEM((t + 2 * LRU_PAD, LRU_BLOCK), F32)]
                       + [pltpu.VMEM((pitched_rows, LRU_BLOCK), F32) for _ in range(2 * N_DIR)]
                       + [pltpu.VMEM((N_DIR, seg, LRU_SEG, LRU_BLOCK), F32),
                          pltpu.VMEM((N_DIR, seg, LRU_SEG, LRU_BLOCK), F32),
                          pltpu.VMEM((pitched_rows, LRU_BLOCK), F32)],
        input_output_aliases={0: 0} if aliased else {},
        compiler_params=_params(("parallel", "parallel")),
        name="rglru",
    )(*args)


def _merge_kernel(ya_ref, yb_ref, yc_ref, w_ref, ma_ref, mb_ref, mc_ref, bm_ref, o_ref):
    acc = None
    for n, (y_ref, mg_ref) in enumerate(((ya_ref, ma_ref), (yb_ref, mb_ref), (yc_ref, mc_ref))):
        g = jax.nn.sigmoid(mg_ref[...] + bm_ref[n])
        term = g * jnp.dot(y_ref[...], w_ref[n], preferred_element_type=F32)
        acc = term if acc is None else acc + term
    o_ref[...] = acc.astype(o_ref.dtype)


def _merge(cfg, ya, yb, yc, wbr, z, bm, tm, tn):
    m = cfg.m
    nj = D_MODEL // tn
    assert (cfg.m_ctx // NSLOT) % tm == 0 and ((m - cfg.m_ctx) // NSLOT) % tm == 0
    slot_spec = pl.BlockSpec((None, tm, BRANCH_W), lambda j, i: _slot_of_tile(cfg, tm, i) + (0,))
    y_spec = pl.BlockSpec((tm, BRANCH_W), lambda j, i: (i, 0))
    mg_spec = lambda n: pl.BlockSpec((tm, tn), lambda j, i: (i, Z_MG // tn + n * nj + j))
    return pl.pallas_call(
        _merge_kernel,
        grid=(nj, m // tm),
        in_specs=[slot_spec, slot_spec, y_spec,
                  pl.BlockSpec((N_BRANCH, BRANCH_W, tn), lambda j, i: (0, 0, j)),
                  mg_spec(0), mg_spec(1), mg_spec(2),
                  pl.BlockSpec((N_BRANCH, 1, tn), lambda j, i: (0, 0, j))],
        out_specs=pl.BlockSpec((tm, tn), lambda j, i: (i, j)),
        out_shape=jax.ShapeDtypeStruct((m, D_MODEL), BF16),
        compiler_params=_params(("parallel", "parallel")),
        name="merge",
    )(ya, yb, yc, wbr, z, z, z, bm)


def _matmul_res_kernel(a_ref, w_ref, x_ref, g_ref, o_ref):
    o_ref[...] = x_ref[...] + g_ref[0] * jnp.dot(a_ref[...], w_ref[...], preferred_element_type=F32)


def _matmul_res(cfg, a, w, x, gate, tm, tn):
    m, kdim = a.shape
    n = w.shape[1]
    return pl.pallas_call(
        _matmul_res_kernel,
        grid=(n // tn, m // tm),
        in_specs=[pl.BlockSpec((tm, kdim), lambda j, i: (i, 0)),
                  pl.BlockSpec((kdim, tn), lambda j, i: (0, j)),
                  pl.BlockSpec((tm, tn), lambda j, i: (i, j)),
                  pl.BlockSpec((1, 1, tn), lambda j, i: (_mod_row(cfg, i * tm), 0, j))],
        out_specs=pl.BlockSpec((tm, tn), lambda j, i: (i, j)),
        out_shape=jax.ShapeDtypeStruct((m, n), F32),
        compiler_params=_params(("parallel", "parallel")),
        name="matmul_res",
    )(a, w, x, gate)


FFN_PAD = 72
FFN_STEP_ELEMS = 128 * 128


def _conv_act_kernel(*refs, t, taps, ct, aliased):
    if aliased:
        refs = refs[1:]
    hg_ref, hu_ref, w_ref, b_ref, o_ref, pad_ref = refs
    rows = FFN_STEP_ELEMS // ct
    nsteps = t // rows
    zeros = jnp.zeros((FFN_PAD, ct), F32)
    pad_ref[0:FFN_PAD, :] = zeros
    pad_ref[FFN_PAD + t:2 * FFN_PAD + t, :] = zeros

    def fill(ci, _):
        r0 = pl.multiple_of(ci * rows, rows)
        pad_ref[pl.ds(r0 + FFN_PAD, rows), :] = hg_ref[pl.ds(r0, rows), :]
        return 0

    lax.fori_loop(0, nsteps, fill, 0)
    col = lax.broadcasted_iota(jnp.int32, (rows, ct), 0) % GRID_W

    def step(ci, _):
        r0 = ci * rows if isinstance(ci, int) else pl.multiple_of(ci * rows, rows)
        acc = b_ref[...]
        for shift, widx, dx in taps:
            xs = pad_ref[pl.ds(r0 + FFN_PAD + shift, rows), :]
            if dx < 0:
                xs = jnp.where(col >= -dx, xs, 0.0)
            elif dx > 0:
                xs = jnp.where(col < GRID_W - dx, xs, 0.0)
            acc = acc + xs * w_ref[widx:widx + 1, :]
        o_ref[pl.ds(r0, rows), :] = (_silu(acc) * hu_ref[pl.ds(r0, rows), :]).astype(o_ref.dtype)
        return 0

    if ct > 128:
        for ci in range(nsteps):
            step(ci, 0)
    else:
        lax.fori_loop(0, nsteps, step, 0)


def _conv_act(cfg, h, w9, bias, nseq, t, row0, on_grid, ct, prev=None):
    m = cfg.m
    rows = FFN_STEP_ELEMS // ct
    assert row0 % t == 0 and t % rows == 0 and D_FF % ct == 0 and rows % 16 == 0
    assert not on_grid or rows % GRID_W == 0
    sb = row0 // t
    p = FFN_CONV // 2
    if on_grid:
        taps = tuple(((dy - p) * GRID_W + (dx - p), dy * FFN_CONV + dx, dx - p)
                     for dy in range(FFN_CONV) for dx in range(FFN_CONV))
    else:
        taps = tuple((dx - p, p * FFN_CONV + dx, 0) for dx in range(FFN_CONV))
    aliased = prev is not None
    nct = D_FF // ct
    in_specs = [pl.BlockSpec((t, ct), lambda b, j: (sb + b, j)),
                pl.BlockSpec((t, ct), lambda b, j: (sb + b, nct + j)),
                pl.BlockSpec((FFN_CONV * FFN_CONV, ct), lambda b, j: (0, j)),
                pl.BlockSpec((1, ct), lambda b, j: (0, j))]
    args = [h, h, w9, bias]
    if aliased:
        in_specs = [pl.BlockSpec(memory_space=pl.ANY)] + in_specs
        args = [prev] + args
    return pl.pallas_call(
        functools.partial(_conv_act_kernel, t=t, taps=taps, ct=ct, aliased=aliased),
        grid=(nseq, nct),
        in_specs=in_specs,
        out_specs=pl.BlockSpec((t, ct), lambda b, j: (sb + b, j)),
        out_shape=jax.ShapeDtypeStruct((m, D_FF), BF16),
        scratch_shapes=[pltpu.VMEM((t + 2 * FFN_PAD, ct), F32)],
        input_output_aliases={0: 0} if aliased else {},
        compiler_params=_params(("parallel", "parallel")),
        name="conv_act",
    )(*args)


FD_TM = 512
FD_TK = 512
FD_HALO = GRID_W
FD_TOP = 8
FD_VMEM = 56 * 1024 * 1024


def _ffn_down_kernel(*refs, cfg, on_grid, seq_len, final_norm, aliased):
    refs = list(refs[1:] if aliased else refs)
    hg_ref, hu_ref = refs[:2]
    refs = refs[2:]
    if on_grid:
        hp_ref, hn_ref = refs[:2]
        refs = refs[2:]
    w9_ref, cb_ref, wd_ref, x_ref, g_ref = refs[:5]
    refs = refs[5:]
    if final_norm:
        nf_ref = refs[0]
        refs = refs[1:]
    o_ref, pad_ref, even_ref, odd_ref = refs
    i, k = pl.program_id(0), pl.program_id(1)
    nk = pl.num_programs(1) - 1
    tm, tk = FD_TM, FD_TK
    nslab = tk // 128
    base = FD_TOP + FD_HALO

    @pl.when(k == 0)
    def _():
        o_ref[...] = jnp.zeros_like(o_ref)
        edge = jnp.zeros((nslab, FD_TOP, 128), F32)
        pad_ref[:, 0:FD_TOP, :] = edge
        pad_ref[:, base + tm + FD_HALO:base + tm + FD_HALO + FD_TOP, :] = edge
        if not on_grid:
            halo = jnp.zeros((nslab, FD_HALO, 128), F32)
            pad_ref[:, FD_TOP:base, :] = halo
            pad_ref[:, base + tm:base + tm + FD_HALO, :] = halo

    if on_grid:
        tiles_per_seq = seq_len // tm
        first = i % tiles_per_seq == 0
        last = i % tiles_per_seq == tiles_per_seq - 1

        @pl.when(first)
        def _():
            pad_ref[:, FD_TOP:base, :] = jnp.zeros((nslab, FD_HALO, 128), F32)

        @pl.when(jnp.logical_not(first))
        def _():
            for s in range(nslab):
                pad_ref[s, FD_TOP:base, :] = hp_ref[:, s * 128:(s + 1) * 128]

        @pl.when(last)
        def _():
            pad_ref[:, base + tm:base + tm + FD_HALO, :] = jnp.zeros((nslab, FD_HALO, 128), F32)

        @pl.when(jnp.logical_not(last))
        def _():
            for s in range(nslab):
                pad_ref[s, base + tm:base + tm + FD_HALO, :] = hn_ref[:, s * 128:(s + 1) * 128]

    for s in range(nslab):
        pad_ref[s, base:base + tm, :] = hg_ref[:, s * 128:(s + 1) * 128]

    p = FFN_CONV // 2
    if on_grid:
        taps = tuple(((dy - p) * GRID_W + (dx - p), dy * FFN_CONV + dx, dx - p)
                     for dy in range(FFN_CONV) for dx in range(FFN_CONV))
        period = GRID_W
    else:
        taps = tuple((dx - p, p * FFN_CONV + dx, dx - p) for dx in range(FFN_CONV))
        period = seq_len
    kk = pl.multiple_of(jnp.maximum(k - 1, 0) * tk, tk)
    d_out = o_ref.shape[1]
    col_tile = 256
    n_col = d_out // col_tile
    rows = 64
    w9 = w9_ref[0]
    cb = cb_ref[0]
    conv_blocks = [(r, s) for r in range(tm // rows) for s in range(nslab)]
    per_tile = -(-len(conv_blocks) // n_col)

    def step(cur_ref, nxt_ref):
        for t in range(n_col):
            cols = slice(t * col_tile, (t + 1) * col_tile)
            if cur_ref is not None:
                o_ref[:, cols] += jnp.dot(cur_ref[...], wd_ref[pl.ds(kk, tk), cols], preferred_element_type=F32)
            for r, s in (conv_blocks[t * per_tile:(t + 1) * per_tile] if nxt_ref is not None else ()):
                pos = (lax.broadcasted_iota(jnp.int32, (rows, 128), 0) + r * rows) % period
                lanes = slice(s * 128, (s + 1) * 128)
                acc = cb[:, lanes]
                for shift, widx, dx in taps:
                    xs = pad_ref[s, base + r * rows + shift:base + (r + 1) * rows + shift, :]
                    if dx < 0:
                        xs = jnp.where(pos >= -dx, xs, 0.0)
                    elif dx > 0:
                        xs = jnp.where(pos < period - dx, xs, 0.0)
                    acc = acc + xs * w9[widx:widx + 1, lanes]
                act = _silu(acc) * hu_ref[r * rows:(r + 1) * rows, lanes]
                nxt_ref[r * rows:(r + 1) * rows, lanes] = act.astype(BF16)

    nk_static = D_FF // tk
    bufs = (even_ref, odd_ref)

    @pl.when(k == 0)
    def _():
        step(None, even_ref)

    @pl.when(jnp.logical_and(jnp.logical_and(k > 0, k < nk), k % 2 == 0))
    def _():
        step(odd_ref, even_ref)

    @pl.when(jnp.logical_and(k < nk, k % 2 == 1))
    def _():
        step(even_ref, odd_ref)

    @pl.when(k == nk)
    def _():
        step(bufs[(nk_static - 1) % 2], None)

        def slab(si, _):
            r0 = pl.multiple_of(si * NORM_SLAB, NORM_SLAB)
            res = x_ref[pl.ds(r0, NORM_SLAB), :] + g_ref[0] * o_ref[pl.ds(r0, NORM_SLAB), :]
            if final_norm:
                res = res * lax.rsqrt(jnp.mean(res * res, axis=-1, keepdims=True) + EPS) * nf_ref[...]
            o_ref[pl.ds(r0, NORM_SLAB), :] = res
            return 0

        lax.fori_loop(0, tm // NORM_SLAB, slab, 0)


def _ffn_down(cfg, h, w9, bias, wd, x, gate, nseq, t, row0, on_grid, prev=None, norm_gain=None):
    m, d = x.shape
    tm, tk = FD_TM, FD_TK
    rows = nseq * t
    nk = D_FF // tk
    assert D_FF % tk == 0 and row0 % tm == 0 and rows % tm == 0
    assert (t % tm == 0 and tm % GRID_W == 0) if on_grid else tm % t == 0
    final_norm = norm_gain is not None
    aliased = prev is not None
    assert not (final_norm and aliased)
    rt0 = row0 // tm
    kc = lambda k: jnp.minimum(k, nk - 1)
    hpb = tm // FD_HALO
    in_specs = [pl.BlockSpec((tm, tk), lambda i, k: (rt0 + i, kc(k))),
                pl.BlockSpec((tm, tk), lambda i, k: (rt0 + i, nk + kc(k)))]
    args = [h, h]
    if on_grid:
        in_specs += [pl.BlockSpec((FD_HALO, tk), lambda i, k: (jnp.maximum((rt0 + i) * hpb - 1, 0), kc(k))),
                     pl.BlockSpec((FD_HALO, tk), lambda i, k: (jnp.minimum((rt0 + i + 1) * hpb, m // FD_HALO - 1), kc(k)))]
        args += [h, h]
    w9c = jnp.transpose(w9.reshape(FFN_CONV * FFN_CONV, nk, tk), (1, 0, 2))
    in_specs += [pl.BlockSpec((1, FFN_CONV * FFN_CONV, tk), lambda i, k: (kc(k), 0, 0)),
                 pl.BlockSpec((1, 1, tk), lambda i, k: (kc(k), 0, 0)),
                 pl.BlockSpec((D_FF, d), lambda i, k: (0, 0), pipeline_mode=pl.Buffered(1)),
                 pl.BlockSpec((tm, d), lambda i, k: (rt0 + i, 0)),
                 pl.BlockSpec((1, 1, d), lambda i, k: (_mod_row(cfg, (rt0 + i) * tm), 0, 0))]
    args += [w9c, bias.reshape(nk, 1, tk), wd, x, gate]
    if final_norm:
        in_specs.append(pl.BlockSpec((1, d), lambda i, k: (0, 0)))
        args.append(norm_gain)
    if aliased:
        in_specs = [pl.BlockSpec(memory_space=pl.ANY)] + in_specs
        args = [prev] + args
    out_rows, out_rt0 = (rows, 0) if final_norm else (m, rt0)
    return pl.pallas_call(
        functools.partial(_ffn_down_kernel, cfg=cfg, on_grid=on_grid, seq_len=t, final_norm=final_norm, aliased=aliased),
        grid=(rows // tm, nk + 1),
        in_specs=in_specs,
        out_specs=pl.BlockSpec((tm, d), lambda i, k: (out_rt0 + i, 0)),
        out_shape=jax.ShapeDtypeStruct((out_rows, d), F32),
        scratch_shapes=[pltpu.VMEM((tk // 128, tm + 2 * (FD_HALO + FD_TOP), 128), F32),
                        pltpu.VMEM((tm, tk), BF16),
                        pltpu.VMEM((tm, tk), BF16)],
        input_output_aliases={0: 0} if aliased else {},
        compiler_params=pltpu.CompilerParams(dimension_semantics=("parallel", "arbitrary"), vmem_limit_bytes=FD_VMEM),
        name="ffn_down",
    )(*args)


def _rmsnorm_kernel(x_ref, g_ref, o_ref):
    x = x_ref[...]
    o_ref[...] = x * lax.rsqrt(jnp.mean(x * x, axis=-1, keepdims=True) + EPS) * g_ref[...]


def _final_norm(x, gain, row0, rows, tm):
    d = x.shape[1]
    assert row0 % tm == 0 and rows % tm == 0
    return pl.pallas_call(
        _rmsnorm_kernel,
        grid=(rows // tm,),
        in_specs=[pl.BlockSpec((tm, d), lambda i: (row0 // tm + i, 0)),
                  pl.BlockSpec((1, d), lambda i: (0, 0))],
        out_specs=pl.BlockSpec((tm, d), lambda i: (i, 0)),
        out_shape=jax.ShapeDtypeStruct((rows, d), F32),
        compiler_params=_params(("parallel",)),
        name="final_norm",
    )(x, gain)


def _pack_w_in(w):
    s = _SRC
    cols = [w[:, s["mg"]:s["end"]], w[:, s["va"]:s["ga"]], w[:, s["ga"]:s["ra"]], w[:, s["qb"]:s["kb"]],
            w[:, s["kb"]:s["vb"]], w[:, s["vb"]:s["ob"]], w[:, s["ob"]:s["gb"]], w[:, s["xr"]:s["yr"]],
            w[:, s["yr"]:s["mg"]], w[:, s["qa"]:s["ka"]], w[:, s["ka"]:s["va"]], w[:, s["ra"]:s["qb"]],
            w[:, s["gb"]:s["xr"]]]
    used = sum(c.shape[1] for c in cols)
    cols.append(jnp.zeros((w.shape[0], NZ - used), w.dtype))
    return jnp.concatenate(cols, axis=1).astype(BF16)


def _pack_gate_w(gw, gb, lam):
    nb, blk = LRU_BLOCKS, LRU_BLOCK
    gw_p = jnp.transpose(gw, (2, 3, 0, 1, 4)).reshape(nb, blk, 4 * blk).astype(BF16)
    gb_p = jnp.transpose(gb.reshape(2, 2, nb, blk), (2, 0, 1, 3)).reshape(nb, 1, 4 * blk)
    lam_p = jnp.transpose(lam.reshape(2, nb, blk), (1, 0, 2)).reshape(nb, 1, 2 * blk)
    return gw_p, gb_p, lam_p


def _layer(cfg, x, mod, lp, states, final_gain=None):
    (n1, n2, w_in, gwa, gba, gng, mbif, mng, lcw, lcb, lgw, lgb, llam, wbr, bmg, wout, fup, fcw, fcb, fdown) = lp
    s_gla, s_mc, s_mn, s_mm, s_lru = states
    m = cfg.m
    sh1, sc1, g1, sh2, sc2, g2 = (mod[:, k * D_MODEL:(k + 1) * D_MODEL].reshape(MOD_ROWS, 1, D_MODEL) for k in range(6))

    proj_tm = PROJ_TM if (cfg.m_ctx % PROJ_TM == 0 and cfg.tl % PROJ_TM == 0) else 512
    z = _norm_matmul(cfg, x, n1.reshape(1, -1), sc1, sh1, _pack_w_in(w_in), proj_tm, 512)

    s0t = jnp.swapaxes(s_gla, -1, -2)
    o_f, sa_f = _gla(cfg, z, gwa[0], gba[0].reshape(1, -1), s0t[:, 0], False)
    y_a, sa_r = _gla(cfg, z, gwa[1], gba[1].reshape(1, -1), s0t[:, 1], True, o_f, gng.reshape(1, -1))
    new_gla =jnp.swapaxes(jnp.stack([sa_f, sa_r], axis=1), -1, -2)

    g_rows = z[:, Z_SM + SM_GB:Z_SM + SM_GB + 4 * MLSTM_HEADS].T
    k_t = z[:, Z_KB:Z_KB + MLSTM_HEADS * MLSTM_DH].T
    b_col = mbif.reshape(1, -1)
    b_row = mbif.reshape(-1, 1)
    n0 = s_mn[:, :, :, None, :]
    m0 = jnp.broadcast_to(s_mm[:, :, :, None, None], s_mm.shape + (1, 128))
    h_f, cb_f, nb_f, mb_f = _mlstm(cfg, z, k_t, g_rows, b_col, b_row, s_mc[:, 0], n0[:, 0], m0[:, 0], False)
    y_b, cb_r, nb_r, mb_r = _mlstm(cfg, z, k_t, g_rows, b_col, b_row, s_mc[:, 1], n0[:, 1], m0[:, 1], True,
                                   h_f, mng.reshape(1, -1))
    new_mc =jnp.stack([cb_f, cb_r], axis=1)
    new_mn = jnp.stack([nb_f[:, :, 0], nb_r[:, :, 0]], axis=1)
    new_mm = jnp.stack([mb_f[:, :, 0, 0], mb_r[:, :, 0, 0]], axis=1)

    gw_p, gb_p, lam_p = _pack_gate_w(lgw, lgb, llam)
    cbias = lcb.reshape(1, -1)
    y_c, new_lru = _rglru(cfg, z, lcw, cbias, gw_p, gb_p, lam_p, jnp.zeros((cfg.bc, N_DIR, LRU_WIDTH), F32),
                          cfg.bc, cfg.tc, 0)
    y_c, _ = _rglru(cfg, z, lcw, cbias, gw_p, gb_p, lam_p, s_lru, cfg.bl, cfg.tl, cfg.m_ctx, y_prev=y_c)

    merged = _merge(cfg, y_a, y_b, y_c, wbr.astype(BF16), z, bmg.reshape(N_BRANCH, 1, D_MODEL), 512, 1024)
    x = _matmul_res(cfg, merged, wout.astype(BF16), x, g1, 512, 1024)

    h = _norm_matmul(cfg, x, n2.reshape(1, -1), sc2, sh2, fup.astype(BF16), proj_tm, 512)
    w9 = fcw.reshape(FFN_CONV * FFN_CONV, D_FF)
    wd = fdown.astype(BF16)
    if final_gain is None:
        x_ctx = _ffn_down(cfg, h, w9, fcb, wd, x, g2, cfg.bc, cfg.tc, 0, False)
        x = _ffn_down(cfg, h, w9, fcb, wd, x, g2, cfg.bl, cfg.tl, cfg.m_ctx, True, prev=x_ctx)
    else:
        x = (_ffn_down(cfg, h, w9, fcb, wd, x, g2, cfg.bc, cfg.tc, 0, False, norm_gain=final_gain),
             _ffn_down(cfg, h, w9, fcb, wd, x, g2, cfg.bl, cfg.tl, cfg.m_ctx, True, norm_gain=final_gain))
    return x, (new_gla, new_mc, new_mn, new_mm, new_lru)


def kernel(x_prompt, x_sample, state_gla, state_mlstm_c, state_mlstm_n, state_mlstm_m, state_rglru, c, c_ctx, norm1_g, norm2_g, w_mod, b_mod, w_in, gla_w_alpha, gla_b_alpha, gla_norm_g, mlstm_b_if, mlstm_norm_g, lru_conv_w, lru_conv_b, lru_gate_w, lru_gate_b, lru_lambda, w_branch, b_merge, w_out, ffn_w_up, ffn_conv_w, ffn_conv_b, ffn_w_down, norm_f_g):
    bc, tc, d = x_prompt.shape
    bl, tl, _ = x_sample.shape
    cfg = Cfg(bc, tc, bl, tl)
    assert tc % TB == 0 and tl % TB == 0 and cfg.m_ctx % tl == 0 and 1 + bl <= MOD_ROWS
    assert bc % NSLOT == 0 and bl % NSLOT == 0
    depth = w_in.shape[0]

    x = jnp.concatenate([x_prompt.reshape(bc * tc, d), x_sample.reshape(bl * tl, d)], axis=0)
    c_all = jnp.concatenate([c_ctx[None, :], c, jnp.zeros((MOD_ROWS - 1 - bl, d), F32)], axis=0)
    mod = _modulation(c_all, w_mod, b_mod)

    new = []
    for l in range(depth):
        lp = (norm1_g[l], norm2_g[l], w_in[l], gla_w_alpha[l], gla_b_alpha[l], gla_norm_g[l], mlstm_b_if[l],
              mlstm_norm_g[l], lru_conv_w[l], lru_conv_b[l], lru_gate_w[l], lru_gate_b[l], lru_lambda[l],
              w_branch[l], b_merge[l], w_out[l], ffn_w_up[l], ffn_conv_w[l], ffn_conv_b[l], ffn_w_down[l])
        states = (state_gla[:, l], state_mlstm_c[:, l], state_mlstm_n[:, l], state_mlstm_m[:, l], state_rglru[:, l])
        x, st = _layer(cfg, x, mod[l], lp, states, norm_f_g.reshape(1, -1) if l == depth - 1 else None)
        new.append(st)

    y_prompt = x[0].reshape(bc, tc, d)
    y_sample = x[1].reshape(bl, tl, d)
    stacked = tuple(jnp.stack([new[l][k] for l in range(depth)], axis=1) for k in range(5))
    return (y_prompt, y_sample) + stacked
```

```python
import functools
from typing import NamedTuple

import numpy as np
import jax
import jax.numpy as jnp
from jax import lax
from jax.experimental import pallas as pl
from jax.experimental.pallas import tpu as pltpu

F32 = jnp.float32
BF16 = jnp.bfloat16

D_MODEL = 2048
DEPTH = 2
GRID_W = 64
N_DIR = 2
N_BRANCH = 3
BRANCH_W = 1024
GLA_HEADS = 4
GLA_DK = 128
GLA_DV = 256
GLA_RANK = 16
GLA_TAU = 16.0
MLSTM_HEADS = 4
MLSTM_DH = 256
CHUNK = 64
LRU_WIDTH = 1024
LRU_BLOCKS = 8
LRU_BLOCK = 128
LRU_CONV = 4
LRU_C = 8.0
D_FF = 5632
FFN_CONV = 3
EPS = 1e-6

_SRC = dict(qa=0, ka=512, va=1024, ga=2048, ra=3072, qb=3104, kb=4128, vb=5152, ob=6176, gb=7200,
            xr=7216, yr=8240, mg=9264, end=15408)
Z_MG, Z_VA, Z_GA, Z_QB, Z_KB, Z_VB, Z_OB, Z_XR, Z_YR, Z_QA, Z_KA, Z_SM = (
    0, 6144, 7168, 8192, 9216, 10240, 11264, 12288, 13312, 14336, 14848, 15360)
SM_W = 128
SM_RA, SM_GB = 0, 32
NZ = 15872
MOD_ROWS = 16

PROJ_TM = 1024
NORM_SLAB = 32
PROJ_SLAB = 64
TB = 256
NCH = TB // CHUNK
VMEM_LIMIT = 48 * 1024 * 1024
PROJ_SPLIT_VMEM = 54 * 1024 * 1024


class Cfg(NamedTuple):
    bc: int
    tc: int
    bl: int
    tl: int

    @property
    def m_ctx(self):
        return self.bc * self.tc

    @property
    def m(self):
        return self.bc * self.tc + self.bl * self.tl


def _params(sem):
    return pltpu.CompilerParams(dimension_semantics=sem, vmem_limit_bytes=VMEM_LIMIT)


def _softplus(x):
    return jnp.maximum(x, 0.0) + jnp.log1p(jnp.exp(-jnp.abs(x)))


def _log_sigmoid(x):
    return -_softplus(-x)


def _silu(x):
    return x * jax.nn.sigmoid(x)


def _gelu_tanh(x):
    return 0.5 * x * (1.0 + jnp.tanh(np.sqrt(2.0 / np.pi).astype(np.float32) * (x + 0.044715 * (x * x * x))))


def _mod_row(cfg, row):
    return jnp.where(row < cfg.m_ctx, 0, 1 + jnp.maximum(row - cfg.m_ctx, 0) // cfg.tl)


def _mod_kernel(c_ref, w_ref, b_ref, o_ref):
    c = c_ref[...]
    a = _silu(c).astype(BF16)
    o_ref[0] = jnp.dot(a, w_ref[0].astype(BF16), preferred_element_type=F32) + b_ref[0]


def _modulation(c_all, w_mod, b_mod):
    depth, d, n = w_mod.shape
    tn = 512
    return pl.pallas_call(
        _mod_kernel,
        grid=(depth, n // tn),
        in_specs=[pl.BlockSpec((MOD_ROWS, d), lambda l, j: (0, 0)),
                  pl.BlockSpec((1, d, tn), lambda l, j: (l, 0, j)),
                  pl.BlockSpec((1, 1, tn), lambda l, j: (l, 0, j))],
        out_specs=pl.BlockSpec((1, MOD_ROWS, tn), lambda l, j: (l, 0, j)),
        out_shape=jax.ShapeDtypeStruct((depth, MOD_ROWS, n), F32),
        compiler_params=_params(("parallel", "parallel")),
        name="modulation",
    )(c_all, w_mod, b_mod.reshape(depth, 1, n))


def _norm_rows(x_ref, g_ref, sc_ref, sh_ref, u_ref, r0, nrows):
    x = x_ref[pl.ds(r0, nrows), :]
    y = x * lax.rsqrt(jnp.mean(x * x, axis=-1, keepdims=True) + EPS) * g_ref[...]
    u_ref[pl.ds(r0, nrows), :] = (y * (1.0 + sc_ref[0]) + sh_ref[0]).astype(BF16)


def _norm_matmul_kernel(*refs, slab, starts):
    nsrc = len(starts) - 1
    x0_ref = refs[0]
    xn_refs = refs[1:1 + nsrc]
    g_ref, sc0_ref, sh0_ref, scn_ref, shn_ref, w_ref, o_ref, ua_ref, ub_ref = refs[1 + nsrc:]
    i, j = pl.program_id(0), pl.program_id(1)
    tm = x0_ref.shape[0]
    nxt = jnp.minimum(i + 1, starts[-1] - 1)

    @pl.when(jnp.logical_and(i == 0, j == 0))
    def _():
        def first(si, _):
            _norm_rows(x0_ref, g_ref, sc0_ref, sh0_ref, ua_ref, pl.multiple_of(si * NORM_SLAB, NORM_SLAB), NORM_SLAB)
            return 0

        lax.fori_loop(0, tm // NORM_SLAB, first, 0)

    r0 = pl.multiple_of(jnp.minimum(j, tm // slab - 1) * slab, slab)
    for s, xn_ref in enumerate(xn_refs):
        from_s = jnp.logical_and(nxt >= starts[s], nxt < starts[s + 1])

        @pl.when(jnp.logical_and(from_s, i % 2 == 0))
        def _():
            _norm_rows(xn_ref, g_ref, scn_ref, shn_ref, ub_ref, r0, slab)
            o_ref[...] = jnp.dot(ua_ref[...], w_ref[...], preferred_element_type=F32)

        @pl.when(jnp.logical_and(from_s, i % 2 == 1))
        def _():
            _norm_rows(xn_ref, g_ref, scn_ref, shn_ref, ua_ref, r0, slab)
            o_ref[...] = jnp.dot(ub_ref[...], w_ref[...], preferred_element_type=F32)


def _norm_matmul(cfg, xs, gain, sc, sh, w, tm, tn):
    d = xs[0].shape[1]
    n = w.shape[1]
    slab = PROJ_SLAB
    assert tm % slab == 0 and n // tn >= tm // slab and all(x.shape[0] % tm == 0 for x in xs)
    starts = [0]
    for x in xs:
        starts.append(starts[-1] + x.shape[0] // tm)
    ntiles = starts[-1]
    nxt = lambda i: jnp.minimum(i + 1, ntiles - 1)
    mod0 = lambda i, j: (_mod_row(cfg, 0), 0, 0)
    modn = lambda i, j: (_mod_row(cfg, nxt(i) * tm), 0, 0)
    src_specs = [pl.BlockSpec((tm, d), lambda i, j, s=s: (jnp.clip(nxt(i) - starts[s], 0, starts[s + 1] - starts[s] - 1), 0),
                              pipeline_mode=pl.Buffered(1) if len(xs) > 1 and s == 0 else None)
                 for s in range(len(xs))]
    return pl.pallas_call(
        functools.partial(_norm_matmul_kernel, slab=slab, starts=tuple(starts)),
        grid=(ntiles, n // tn),
        in_specs=[pl.BlockSpec((tm, d), lambda i, j: (0, 0), pipeline_mode=pl.Buffered(1))] + src_specs
                 + [pl.BlockSpec((1, d), lambda i, j: (0, 0)),
                    pl.BlockSpec((1, 1, d), mod0),
                    pl.BlockSpec((1, 1, d), mod0),
                    pl.BlockSpec((1, 1, d), modn),
                    pl.BlockSpec((1, 1, d), modn),
                    pl.BlockSpec((d, tn), lambda i, j: (0, j))],
        out_specs=pl.BlockSpec((tm, tn), lambda i, j: (i, j)),
        out_shape=jax.ShapeDtypeStruct((ntiles * tm, n), F32),
        scratch_shapes=[pltpu.VMEM((tm, d), BF16), pltpu.VMEM((tm, d), BF16)],
        compiler_params=pltpu.CompilerParams(dimension_semantics=("arbitrary", "arbitrary"),
                                             vmem_limit_bytes=VMEM_LIMIT if len(xs) == 1 else PROJ_SPLIT_VMEM),
        name="norm_matmul",
    )(xs[0], *xs, gain, sc, sh, sc, sh, w)


NSLOT = 2


def _pair_pos(cfg, i):
    kc, kl = cfg.tc // TB, cfg.tl // TB
    nc = (cfg.bc // NSLOT) * kc
    is_ctx = i < nc
    il = jnp.maximum(i - nc, 0)
    j = jnp.where(is_ctx, i % kc, il % kl)
    k = jnp.where(is_ctx, kc, kl)
    s = jnp.where(is_ctx, i // kc, il // kl)
    return is_ctx, s, j, k


def _pair_row_block(cfg, reverse, slot, i):
    kc, kl = cfg.tc // TB, cfg.tl // TB
    is_ctx, s, j, k = _pair_pos(cfg, i)
    jj = (k - 1 - j) if reverse else j
    return jnp.where(is_ctx, (s + slot * (cfg.bc // NSLOT)) * kc + jj,
                     cfg.bc * kc + (s + slot * (cfg.bl // NSLOT)) * kl + jj)


def _pair_local_block(cfg, reverse, i):
    kc, kl = cfg.tc // TB, cfg.tl // TB
    is_ctx, s, j, k = _pair_pos(cfg, i)
    jj = (k - 1 - j) if reverse else j
    return jnp.where(is_ctx, s * kc + jj, (cfg.bc // NSLOT) * kc + s * kl + jj)


def _pair_lat_seq(cfg, i):
    is_ctx, s, _, _ = _pair_pos(cfg, i)
    return jnp.where(is_ctx, 0, jnp.minimum(s, cfg.bl // NSLOT - 1))


def _pair_ctx_seq(cfg, i):
    is_ctx, s, _, _ = _pair_pos(cfg, i)
    return jnp.where(is_ctx, s, cfg.bc // NSLOT - 1)


def _slot_of_tile(cfg, tm, i):
    nc, nl = cfg.m_ctx // tm // NSLOT, (cfg.m - cfg.m_ctx) // tm // NSLOT
    il = jnp.maximum(i - NSLOT * nc, 0)
    is_ctx = i < NSLOT * nc
    return jnp.where(is_ctx, i // nc, il // nl), jnp.where(is_ctx, i % nc, nc + il % nl)


def _tri(reverse, n=CHUNK):
    row = lax.broadcasted_iota(jnp.int32, (n, n), 0)
    col = lax.broadcasted_iota(jnp.int32, (n, n), 1)
    return (row <= col) if reverse else (row >= col)


def _head_norm(x, g):
    return x * lax.rsqrt(jnp.mean(x * x, axis=-1, keepdims=True) + EPS) * g


_NT = (((1,), (1,)), ((), ()))
_TN = (((0,), (0,)), ((), ()))


GLA_LOCKSTEP = 2


def _gla_kernel(*refs, cfg, reverse):
    it = iter(refs)
    blocks = [[next(it) for _ in range(4)] for _ in range(NSLOT)]
    wal_ref, bal_ref = next(it), next(it)
    init_ref = next(it)
    init = [init_ref.at[b] for b in range(NSLOT)]
    if reverse:
        gates = [next(it) for _ in range(NSLOT)]
        of_ref, gn_ref = next(it), next(it)
        epi = [[gates[b], of_ref.at[b]] for b in range(NSLOT)]
    out_ref, fin_ref = next(it), next(it)
    outs = [out_ref.at[b] for b in range(NSLOT)]
    fin = [fin_ref.at[b] for b in range(NSLOT)]
    st_ref, la_ref = next(it), next(it)
    is_ctx, _, j, k = _pair_pos(cfg, pl.program_id(0))
    nh = GLA_HEADS

    @pl.when(jnp.logical_and(j == 0, is_ctx))
    def _():
        st_ref[...] = jnp.zeros_like(st_ref)

    @pl.when(jnp.logical_and(j == 0, jnp.logical_not(is_ctx)))
    def _():
        for b in range(NSLOT):
            st_ref[b * nh:(b + 1) * nh] = init[b][0]

    d = 1 if reverse else 0
    slots = range(NSLOT)
    both = lambda f: [f(b) for b in slots]
    wal = wal_ref[...].astype(BF16)
    for b in slots:
        ra = blocks[b][3][:, SM_RA + d * GLA_RANK:SM_RA + (d + 1) * GLA_RANK].astype(BF16)
        pre = jnp.dot(ra, wal, preferred_element_type=F32) + bal_ref[...]
        la_ref[b] = _log_sigmoid(pre) * (1.0 / GLA_TAU)

    tri = _tri(reverse)
    tri_f = tri.astype(F32)
    last = 0 if reverse else CHUNK - 1
    qscale = GLA_DK ** -0.5
    for c in (range(NCH - 1, -1, -1) if reverse else range(NCH)):
        rows = slice(c * CHUNK, (c + 1) * CHUNK)
        cum = both(lambda b: jnp.dot(tri_f, la_ref[b, rows, :], precision=lax.Precision.HIGHEST,
                                     preferred_element_type=F32))
        tot = both(lambda b: cum[b][last:last + 1, :])
        e_q = both(lambda b: jnp.exp(cum[b]))
        e_k = both(lambda b: jnp.exp(-cum[b]))
        e_end = both(lambda b: jnp.exp(tot[b] - cum[b]))
        e_tot = both(lambda b: jnp.exp(tot[b]))
        for h0 in range(0, GLA_HEADS, GLA_LOCKSTEP):
            units = [(b, h0 + dh) for dh in range(GLA_LOCKSTEP) for b in slots]
            each = lambda f: [f(u, b, h) for u, (b, h) in enumerate(units)]
            kc = [slice(h * GLA_DK, (h + 1) * GLA_DK) for _, h in units]
            vc = [slice(h * GLA_DV, (h + 1) * GLA_DV) for _, h in units]
            sh = [b * nh + h for b, h in units]
            qh = each(lambda u, b, h: blocks[b][0][rows, kc[u]] * qscale)
            kh = each(lambda u, b, h: blocks[b][1][rows, kc[u]])
            vh = each(lambda u, b, h: blocks[b][2][rows, vc[u]].astype(BF16))
            q_in = each(lambda u, b, h: (qh[u] * e_q[b][:, kc[u]]).astype(BF16))
            k_in = each(lambda u, b, h: (kh[u] * e_k[b][:, kc[u]]).astype(BF16))
            att = each(lambda u, b, h: lax.dot_general(q_in[u], k_in[u], _NT, preferred_element_type=F32))
            att = each(lambda u, b, h: jnp.where(tri, att[u], 0.0).astype(BF16))
            st = each(lambda u, b, h: st_ref[sh[u]])
            o = each(lambda u, b, h: jnp.dot(att[u], vh[u], preferred_element_type=F32)
                     + lax.dot_general(q_in[u], st[u].astype(BF16), _NT, preferred_element_type=F32))
            k_end = each(lambda u, b, h: (kh[u] * e_end[b][:, kc[u]]).astype(BF16))
            upd = each(lambda u, b, h: lax.dot_general(vh[u], k_end[u], _TN, preferred_element_type=F32))
            for u, (b, h) in enumerate(units):
                st_ref[sh[u]] = e_tot[b][:, kc[u]] * st[u] + upd[u]
            for u, (b, h) in enumerate(units):
                if reverse:
                    ga_ref, of_ref = epi[b]
                    y = _head_norm(of_ref[rows, vc[u]] + o[u], gn_ref[:, vc[u]]) * _silu(ga_ref[rows, vc[u]])
                    outs[b][rows, vc[u]] = y.astype(outs[b].dtype)
                else:
                    outs[b][rows, vc[u]] = o[u]

    @pl.when(jnp.logical_and(is_ctx, j == k - 1))
    def _():
        for b in range(NSLOT):
            fin[b][0] = st_ref[b * nh:(b + 1) * nh]


def _gla(cfg, z, wal, bal, s0t, reverse, o_fwd=None, gnorm=None):
    m = cfg.m
    nh = GLA_HEADS
    hk, hv = nh * GLA_DK, nh * GLA_DV
    lb = functools.partial(_pair_local_block, cfg, reverse)
    in_specs, args = [], []
    for b in range(NSLOT):
        rb = functools.partial(_pair_row_block, cfg, reverse, b)
        in_specs += [pl.BlockSpec((TB, hk), lambda i, rb=rb: (rb(i), Z_QA // hk)),
                     pl.BlockSpec((TB, hk), lambda i, rb=rb: (rb(i), Z_KA // hk)),
                     pl.BlockSpec((TB, hv), lambda i, rb=rb: (rb(i), Z_VA // hv)),
                     pl.BlockSpec((TB, SM_W), lambda i, rb=rb: (rb(i), Z_SM // SM_W))]
        args += [z, z, z, z]
    in_specs += [pl.BlockSpec((GLA_RANK, hk), lambda i: (0, 0)), pl.BlockSpec((1, hk), lambda i: (0, 0))]
    args += [wal, bal]
    half_c, half_l = cfg.bc // NSLOT, cfg.bl // NSLOT
    in_specs.append(pl.BlockSpec((NSLOT, 1, nh, GLA_DV, GLA_DK), lambda i: (0, _pair_lat_seq(cfg, i), 0, 0, 0)))
    args.append(s0t.reshape(NSLOT, half_l, nh, GLA_DV, GLA_DK))
    if reverse:
        for b in range(NSLOT):
            rb = functools.partial(_pair_row_block, cfg, reverse, b)
            in_specs.append(pl.BlockSpec((TB, hv), lambda i, rb=rb: (rb(i), Z_GA // hv)))
            args.append(z)
        in_specs += [pl.BlockSpec((NSLOT, TB, hv), lambda i: (0, lb(i), 0)), pl.BlockSpec((1, hv), lambda i: (0, 0))]
        args += [o_fwd, gnorm]
    out, fin = pl.pallas_call(
        functools.partial(_gla_kernel, cfg=cfg, reverse=reverse),
        grid=(m // TB // NSLOT,),
        in_specs=in_specs,
        out_specs=[pl.BlockSpec((NSLOT, TB, hv), lambda i: (0, lb(i), 0)),
                   pl.BlockSpec((NSLOT, 1, nh, GLA_DV, GLA_DK), lambda i: (0, _pair_ctx_seq(cfg, i), 0, 0, 0))],
        out_shape=[jax.ShapeDtypeStruct((NSLOT, m // NSLOT, hv), BF16 if reverse else F32),
                   jax.ShapeDtypeStruct((NSLOT, half_c, nh, GLA_DV, GLA_DK), F32)],
        scratch_shapes=[pltpu.VMEM((NSLOT * nh, GLA_DV, GLA_DK), F32), pltpu.VMEM((NSLOT, TB, hk), F32)],
        compiler_params=_params(("arbitrary",)),
        name="gla_rev" if reverse else "gla_fwd",
    )(*args)
    return out, fin.reshape(cfg.bc, nh, GLA_DV, GLA_DK)


MCH = 128
NMC = TB // MCH
MLSTM_LOCKSTEP = 2


def _mlstm_kernel(*refs, cfg, reverse):
    it = iter(refs)
    blocks = [[next(it) for _ in range(6)] for _ in range(NSLOT)]
    bc_ref, br_ref = next(it), next(it)
    init_refs = [next(it) for _ in range(3)]
    init = [[r.at[b] for r in init_refs] for b in range(NSLOT)]
    if reverse:
        gates = [next(it) for _ in range(NSLOT)]
        hf_all, gn_ref = next(it), next(it)
        epi = [[gates[b], hf_all.at[b]] for b in range(NSLOT)]
    out_ref = next(it)
    outs = [out_ref.at[b] for b in range(NSLOT)]
    fin_refs = [next(it) for _ in range(3)]
    fin = [[r.at[b] for r in fin_refs] for b in range(NSLOT)]
    c_ref, n_ref, m_ref = next(it), next(it), next(it)
    is_ctx, _, j, k = _pair_pos(cfg, pl.program_id(0))
    nh = MLSTM_HEADS

    @pl.when(jnp.logical_and(j == 0, is_ctx))
    def _():
        c_ref[...] = jnp.zeros_like(c_ref)
        n_ref[...] = jnp.zeros_like(n_ref)
        m_ref[...] = jnp.zeros_like(m_ref)

    @pl.when(jnp.logical_and(j == 0, jnp.logical_not(is_ctx)))
    def _():
        for b in range(NSLOT):
            c_ref[b * nh:(b + 1) * nh] = init[b][0][0]
            n_ref[b * nh:(b + 1) * nh] = init[b][1][0]
            m_ref[b * nh:(b + 1) * nh] = init[b][2][0]

    d = 1 if reverse else 0
    tri = _tri(reverse, MCH)
    tri_f = tri.astype(F32)
    tri_tf = _tri(not reverse, MCH).astype(F32)
    last = 0 if reverse else MCH - 1
    ng = 2 * MLSTM_HEADS
    kscale = MLSTM_DH ** -0.5
    slots = range(NSLOT)
    both = lambda f: [f(b) for b in slots]
    hp = lax.Precision.HIGHEST
    for c in (range(NMC - 1, -1, -1) if reverse else range(NMC)):
        rows = slice(c * MCH, (c + 1) * MCH)
        g_col = both(lambda b: blocks[b][4][rows, SM_GB:SM_GB + 2 * ng] + bc_ref[...])
        g_row = both(lambda b: blocks[b][5][:, rows] + br_ref[...])
        fcum_col = both(lambda b: jnp.dot(tri_f, _log_sigmoid(g_col[b]), precision=hp, preferred_element_type=F32))
        fcum_row = both(lambda b: jnp.dot(_log_sigmoid(g_row[b]), tri_tf, precision=hp, preferred_element_type=F32))
        for h0 in range(0, MLSTM_HEADS, MLSTM_LOCKSTEP):
            units = [(b, h0 + dh) for dh in range(MLSTM_LOCKSTEP) for b in slots]
            each = lambda f: [f(u, b, h) for u, (b, h) in enumerate(units)]
            ii = [d * ng + h for _, h in units]
            fi = [d * ng + MLSTM_HEADS + h for _, h in units]
            hc = [slice(h * MLSTM_DH, (h + 1) * MLSTM_DH) for _, h in units]
            sh = [b * nh + h for b, h in units]
            f_col = each(lambda u, b, h: fcum_col[b][:, fi[u]:fi[u] + 1])
            f_row = each(lambda u, b, h: fcum_row[b][fi[u]:fi[u] + 1, :])
            i_col = each(lambda u, b, h: g_col[b][:, ii[u]:ii[u] + 1])
            i_row = each(lambda u, b, h: g_row[b][ii[u]:ii[u] + 1, :])
            m_prev = each(lambda u, b, h: m_ref[sh[u]][:, 0:1])
            dlog = each(lambda u, b, h: jnp.where(tri, f_col[u] - f_row[u] + i_row[u], -jnp.inf))
            prev = each(lambda u, b, h: f_col[u] + m_prev[u])
            mj = each(lambda u, b, h: jnp.maximum(prev[u], jnp.max(dlog[u], axis=-1, keepdims=True)))
            w = each(lambda u, b, h: jnp.exp(dlog[u] - mj[u]))
            wp = each(lambda u, b, h: jnp.exp(prev[u] - mj[u]))
            qh = each(lambda u, b, h: blocks[b][0][rows, hc[u]])
            kh = each(lambda u, b, h: blocks[b][1][rows, hc[u]] * kscale)
            vh = each(lambda u, b, h: blocks[b][2][rows, hc[u]].astype(BF16))
            s = each(lambda u, b, h: lax.dot_general(qh[u].astype(BF16), kh[u].astype(BF16), _NT,
                                                     preferred_element_type=F32) * w[u])
            qp = each(lambda u, b, h: qh[u] * wp[u])
            cm = each(lambda u, b, h: c_ref[sh[u]])
            nv = each(lambda u, b, h: n_ref[sh[u]])
            m_new = each(lambda u, b, h: mj[u][last:last + 1, :])
            tot = each(lambda u, b, h: f_col[u][last:last + 1, :])
            wl_col = each(lambda u, b, h: jnp.exp(tot[u] - f_col[u] + i_col[u] - m_new[u]))
            wl_row = each(lambda u, b, h: jnp.exp(tot[u] - f_row[u] + i_row[u] - m_new[u]))
            decay = each(lambda u, b, h: jnp.exp(tot[u] + m_prev[u] - m_new[u]))
            kw_t = each(lambda u, b, h: (blocks[b][3][hc[u], rows] * kscale) * wl_row[u])
            sv = each(lambda u, b, h: jnp.dot(jnp.concatenate([s[u].astype(BF16), kw_t[u].astype(BF16)], axis=0),
                                              vh[u], preferred_element_type=F32))
            num = each(lambda u, b, h: sv[u][:MCH]
                       + jnp.dot(qp[u].astype(BF16), cm[u].astype(BF16), preferred_element_type=F32))
            upd = each(lambda u, b, h: sv[u][MCH:])
            den = each(lambda u, b, h: jnp.sum(s[u], axis=-1, keepdims=True)
                       + jnp.sum(qp[u] * nv[u], axis=-1, keepdims=True))
            hh = each(lambda u, b, h: num[u] / jnp.maximum(jnp.abs(den[u]), jnp.exp(-mj[u])))
            for u, (b, h) in enumerate(units):
                c_ref[sh[u]] = decay[u] * cm[u] + upd[u]
                n_ref[sh[u]] = decay[u] * nv[u] + jnp.sum(kh[u] * wl_col[u], axis=0, keepdims=True)
                m_ref[sh[u]] = jnp.broadcast_to(m_new[u], (1, 128))
            for u, (b, h) in enumerate(units):
                if reverse:
                    ob_ref, hf_ref = epi[b]
                    y = jax.nn.sigmoid(ob_ref[rows, hc[u]]) * _head_norm(hf_ref[rows, hc[u]] + hh[u], gn_ref[:, hc[u]])
                    outs[b][rows, hc[u]] = y.astype(outs[b].dtype)
                else:
                    outs[b][rows, hc[u]] = hh[u]

    @pl.when(jnp.logical_and(is_ctx, j == k - 1))
    def _():
        for b in range(NSLOT):
            fin[b][0][0] = c_ref[b * nh:(b + 1) * nh]
            fin[b][1][0] = n_ref[b * nh:(b + 1) * nh]
            fin[b][2][0] = m_ref[b * nh:(b + 1) * nh]


def _mlstm(cfg, z, k_t, g_rows, b_col, b_row, c0, n0, m0, reverse, h_fwd=None, gnorm=None):
    m = cfg.m
    hd = MLSTM_HEADS * MLSTM_DH
    nh = MLSTM_HEADS
    lb = functools.partial(_pair_local_block, cfg, reverse)
    ctx = lambda i: (0, _pair_ctx_seq(cfg, i), 0, 0, 0)
    in_specs, args = [], []
    for b in range(NSLOT):
        rb = functools.partial(_pair_row_block, cfg, reverse, b)
        in_specs += [pl.BlockSpec((TB, hd), lambda i, rb=rb: (rb(i), Z_QB // hd)),
                     pl.BlockSpec((TB, hd), lambda i, rb=rb: (rb(i), Z_KB // hd)),
                     pl.BlockSpec((TB, hd), lambda i, rb=rb: (rb(i), Z_VB // hd)),
                     pl.BlockSpec((hd, TB), lambda i, rb=rb: (0, rb(i))),
                     pl.BlockSpec((TB, SM_W), lambda i, rb=rb: (rb(i), Z_SM // SM_W)),
                     pl.BlockSpec((4 * nh, TB), lambda i, rb=rb: (0, rb(i)))]
        args += [z, z, z, k_t, z, g_rows]
    in_specs += [pl.BlockSpec((1, 4 * nh), lambda i: (0, 0)), pl.BlockSpec((4 * nh, 1), lambda i: (0, 0))]
    args += [b_col, b_row]
    half_c, half_l = cfg.bc // NSLOT, cfg.bl // NSLOT
    lat = lambda i: (0, _pair_lat_seq(cfg, i), 0, 0, 0)
    state_dims = ((nh, MLSTM_DH, MLSTM_DH), (nh, 1, MLSTM_DH), (nh, 1, 128))
    in_specs += [pl.BlockSpec((NSLOT, 1) + sd, lat) for sd in state_dims]
    args += [s.reshape((NSLOT, half_l) + sd) for s, sd in zip((c0, n0, m0), state_dims)]
    if reverse:
        for b in range(NSLOT):
            rb = functools.partial(_pair_row_block, cfg, reverse, b)
            in_specs.append(pl.BlockSpec((TB, hd), lambda i, rb=rb: (rb(i), Z_OB // hd)))
            args.append(z)
        in_specs += [pl.BlockSpec((NSLOT, TB, hd), lambda i: (0, lb(i), 0)), pl.BlockSpec((1, hd), lambda i: (0, 0))]
        args += [h_fwd, gnorm]
    res = pl.pallas_call(
        functools.partial(_mlstm_kernel, cfg=cfg, reverse=reverse),
        grid=(m // TB // NSLOT,),
        in_specs=in_specs,
        out_specs=[pl.BlockSpec((NSLOT, TB, hd), lambda i: (0, lb(i), 0))]
                  + [pl.BlockSpec((NSLOT, 1) + sd, ctx) for sd in state_dims],
        out_shape=[jax.ShapeDtypeStruct((NSLOT, m // NSLOT, hd), BF16 if reverse else F32)]
                  + [jax.ShapeDtypeStruct((NSLOT, half_c) + sd, F32) for sd in state_dims],
        scratch_shapes=[pltpu.VMEM((NSLOT * nh, MLSTM_DH, MLSTM_DH), F32),
                        pltpu.VMEM((NSLOT * nh, 1, MLSTM_DH), F32),
                        pltpu.VMEM((NSLOT * nh, 1, 128), F32)],
        compiler_params=_params(("arbitrary",)),
        name="mlstm_rev" if reverse else "mlstm_fwd",
    )(*args)
    return (res[0],) + tuple(r.reshape((cfg.bc,) + sd) for r, sd in zip(res[1:], state_dims))


LRU_ROWS = 256
LRU_PAD = 8
LRU_SEG = 8


def _rglru_kernel(*refs, t, aliased):
    if aliased:
        refs = refs[1:]
    (xr_ref, yr_ref, cw_ref, cb_ref, gw_ref, gb_ref, lam_ref, h0_ref, y_ref, he_ref,
     pad_ref, af_ref, ar_ref, bf_ref, br_ref, hl_ref, pc_ref, hs_ref) = refs
    a_refs, b_refs = (af_ref, ar_ref), (bf_ref, br_ref)
    nsteps = t // LRU_ROWS
    seg = t // LRU_SEG
    pitch = seg + 8
    zeros = jnp.zeros((LRU_PAD, LRU_BLOCK), F32)
    pad_ref[0:LRU_PAD, :] = zeros
    pad_ref[LRU_PAD + t:2 * LRU_PAD + t, :] = zeros

    def pitched(r0):
        if seg >= LRU_ROWS:
            return [(0, LRU_ROWS, pl.multiple_of((r0 // seg) * pitch + r0 % seg, 8))]
        return [(q * seg, seg, pl.multiple_of((r0 // seg + q) * pitch, 8)) for q in range(LRU_ROWS // seg)]

    def fill(ci, _):
        r0 = pl.multiple_of(ci * LRU_ROWS, LRU_ROWS)
        pad_ref[pl.ds(r0 + LRU_PAD, LRU_ROWS), :] = xr_ref[pl.ds(r0, LRU_ROWS), :]
        return 0

    lax.fori_loop(0, nsteps, fill, 0)

    sp = _softplus(-lam_ref[0])
    left = LRU_CONV // 2

    def gates(ci, _):
        r0 = pl.multiple_of(ci * LRU_ROWS, LRU_ROWS)
        xc = cb_ref[...]
        for tap in range(LRU_CONV):
            xc = xc + cw_ref[tap:tap + 1, :] * pad_ref[pl.ds(r0 + LRU_PAD - left + tap, LRU_ROWS), :]
        pre = jnp.dot(xc.astype(BF16), gw_ref[0], preferred_element_type=F32) + gb_ref[0]
        both = lambda f: [f(d) for d in range(N_DIR)]
        r = both(lambda d: jax.nn.sigmoid(pre[:, d * 2 * LRU_BLOCK:d * 2 * LRU_BLOCK + LRU_BLOCK]))
        ig = both(lambda d: jax.nn.sigmoid(pre[:, d * 2 * LRU_BLOCK + LRU_BLOCK:(d + 1) * 2 * LRU_BLOCK]))
        log_a = both(lambda d: (-LRU_C * r[d]) * sp[:, d * LRU_BLOCK:(d + 1) * LRU_BLOCK])
        a = both(lambda d: jnp.exp(log_a[d]))
        b = both(lambda d: jnp.sqrt(1.0 - a[d] * a[d]) * (ig[d] * xc))
        for off, n, dst in pitched(r0):
            for d in range(N_DIR):
                a_refs[d][pl.ds(dst, n), :] = a[d][off:off + n]
                b_refs[d][pl.ds(dst, n), :] = b[d][off:off + n]
        return 0

    lax.fori_loop(0, nsteps, gates, 0)

    def local_scan(i, carry):
        hf, pf, hr, pr = carry
        rf, rr = i, seg - 1 - i
        a_f = af_ref[pl.ds(rf, LRU_SEG, stride=pitch), :]
        b_f = bf_ref[pl.ds(rf, LRU_SEG, stride=pitch), :]
        a_r = ar_ref[pl.ds(rr, LRU_SEG, stride=pitch), :]
        b_r = br_ref[pl.ds(rr, LRU_SEG, stride=pitch), :]
        hf, pf = a_f * hf + b_f, a_f * pf
        hr, pr = a_r * hr + b_r, a_r * pr
        hl_ref[0, rf] = hf
        pc_ref[0, rf] = pf
        hl_ref[1, rr] = hr
        pc_ref[1, rr] = pr
        return hf, pf, hr, pr

    zero = jnp.zeros((LRU_SEG, LRU_BLOCK), F32)
    one = jnp.ones((LRU_SEG, LRU_BLOCK), F32)
    hf, pf, hr, pr = lax.fori_loop(0, seg, local_scan, (zero, one, zero, one), unroll=8)

    enter_f = [h0_ref[0, 0:1, :]]
    for s in range(LRU_SEG - 1):
        enter_f.append(pf[s:s + 1, :] * enter_f[s] + hf[s:s + 1, :])
    he_ref[0, 0:1, :] = pf[LRU_SEG - 1:LRU_SEG, :] * enter_f[LRU_SEG - 1] + hf[LRU_SEG - 1:LRU_SEG, :]
    enter_r = [None] * LRU_SEG
    enter_r[LRU_SEG - 1] = h0_ref[0, 1:2, :]
    for s in range(LRU_SEG - 1, 0, -1):
        enter_r[s - 1] = pr[s:s + 1, :] * enter_r[s] + hr[s:s + 1, :]
    he_ref[0, 1:2, :] = pr[0:1, :] * enter_r[0] + hr[0:1, :]
    init_f = jnp.concatenate(enter_f, axis=0)
    init_r = jnp.concatenate(enter_r, axis=0)

    def fix_up(i, _):
        h = (hl_ref[0, i] + pc_ref[0, i] * init_f) + (hl_ref[1, i] + pc_ref[1, i] * init_r)
        hs_ref[pl.ds(i, LRU_SEG, stride=pitch), :] = h
        return 0

    lax.fori_loop(0, seg, fix_up, 0, unroll=8)

    def finish(ci, _):
        r0 = pl.multiple_of(ci * LRU_ROWS, LRU_ROWS)
        for off, n, src in pitched(r0):
            y = hs_ref[pl.ds(src, n), :] * _gelu_tanh(yr_ref[pl.ds(r0 + off, n), :])
            y_ref[pl.ds(r0 + off, n), :] = y.astype(y_ref.dtype)
        return 0

    lax.fori_loop(0, nsteps, finish, 0)


def _rglru(cfg, z, cw, cb, gw, gb, lam, h0, nseq, t, row0, y_prev=None):
    m = cfg.m
    seg = t // LRU_SEG
    assert row0 % t == 0 and t % LRU_ROWS == 0 and seg % 16 == 0 and (seg % LRU_ROWS == 0 or LRU_ROWS % seg == 0)
    pitched_rows = LRU_SEG * (seg + 8)
    sb = row0 // t
    aliased = y_prev is not None
    in_specs = [pl.BlockSpec((t, LRU_BLOCK), lambda b, n: (sb + b, Z_XR // LRU_BLOCK + n)),
                pl.BlockSpec((t, LRU_BLOCK), lambda b, n: (sb + b, Z_YR // LRU_BLOCK + n)),
                pl.BlockSpec((LRU_CONV, LRU_BLOCK), lambda b, n: (0, n)),
                pl.BlockSpec((1, LRU_BLOCK), lambda b, n: (0, n)),
                pl.BlockSpec((1, LRU_BLOCK, 4 * LRU_BLOCK), lambda b, n: (n, 0, 0)),
                pl.BlockSpec((1, 1, 4 * LRU_BLOCK), lambda b, n: (n, 0, 0)),
                pl.BlockSpec((1, 1, 2 * LRU_BLOCK), lambda b, n: (n, 0, 0)),
                pl.BlockSpec((1, N_DIR, LRU_BLOCK), lambda b, n: (b, 0, n))]
    args = [z, z, cw, cb, gw, gb, lam, h0]
    if aliased:
        in_specs = [pl.BlockSpec(memory_space=pl.ANY)] + in_specs
        args = [y_prev] + args
    return pl.pallas_call(
        functools.partial(_rglru_kernel, t=t, aliased=aliased),
        grid=(nseq, LRU_BLOCKS),
        in_specs=in_specs,
        out_specs=[pl.BlockSpec((t, LRU_BLOCK), lambda b, n: (sb + b, n)),
                   pl.BlockSpec((1, N_DIR, LRU_BLOCK), lambda b, n: (b, 0, n))],
        out_shape=[jax.ShapeDtypeStruct((m, LRU_WIDTH), BF16),
                   jax.ShapeDtypeStruct((nseq, N_DIR, LRU_WIDTH), F32)],
        scratch_shapes=[pltpu.VMEM((t + 2 * LRU_PAD, LRU_BLOCK), F32)]
                       + [pltpu.VMEM((pitched_rows, LRU_BLOCK), F32) for _ in range(2 * N_DIR)]
                       + [pltpu.VMEM((N_DIR, seg, LRU_SEG, LRU_BLOCK), F32),
                          pltpu.VMEM((N_DIR, seg, LRU_SEG, LRU_BLOCK), F32),
                          pltpu.VMEM((pitched_rows, LRU_BLOCK), F32)],
        input_output_aliases={0: 0} if aliased else {},
        compiler_params=_params(("parallel", "parallel")),
        name="rglru",
    )(*args)


def _merge_kernel(ya_ref, yb_ref, yc_ref, w_ref, ma_ref, mb_ref, mc_ref, bm_ref, o_ref):
    acc = None
    for n, (y_ref, mg_ref) in enumerate(((ya_ref, ma_ref), (yb_ref, mb_ref), (yc_ref, mc_ref))):
        g = jax.nn.sigmoid(mg_ref[...] + bm_ref[n])
        term = g * jnp.dot(y_ref[...], w_ref[n], preferred_element_type=F32)
        acc = term if acc is None else acc + term
    o_ref[...] = acc.astype(o_ref.dtype)


def _merge(cfg, ya, yb, yc, wbr, z, bm, tm, tn):
    m = cfg.m
    nj = D_MODEL // tn
    assert (cfg.m_ctx // NSLOT) % tm == 0 and ((m - cfg.m_ctx) // NSLOT) % tm == 0
    slot_spec = pl.BlockSpec((None, tm, BRANCH_W), lambda j, i: _slot_of_tile(cfg, tm, i) + (0,))
    y_spec = pl.BlockSpec((tm, BRANCH_W), lambda j, i: (i, 0))
    mg_spec = lambda n: pl.BlockSpec((tm, tn), lambda j, i: (i, Z_MG // tn + n * nj + j))
    return pl.pallas_call(
        _merge_kernel,
        grid=(nj, m // tm),
        in_specs=[slot_spec, slot_spec, y_spec,
                  pl.BlockSpec((N_BRANCH, BRANCH_W, tn), lambda j, i: (0, 0, j)),
                  mg_spec(0), mg_spec(1), mg_spec(2),
                  pl.BlockSpec((N_BRANCH, 1, tn), lambda j, i: (0, 0, j))],
        out_specs=pl.BlockSpec((tm, tn), lambda j, i: (i, j)),
        out_shape=jax.ShapeDtypeStruct((m, D_MODEL), BF16),
        compiler_params=_params(("parallel", "parallel")),
        name="merge",
    )(ya, yb, yc, wbr, z, z, z, bm)


def _matmul_res_kernel(*refs, starts):
    nsrc = len(starts) - 1
    a_ref, w_ref = refs[:2]
    x_refs = refs[2:2 + nsrc]
    g_ref, o_ref = refs[2 + nsrc:]
    i = pl.program_id(1)
    if nsrc == 1:
        o_ref[...] = x_refs[0][...] + g_ref[0] * jnp.dot(a_ref[...], w_ref[...], preferred_element_type=F32)
        return
    upd = g_ref[0] * jnp.dot(a_ref[...], w_ref[...], preferred_element_type=F32)
    for s, x_ref in enumerate(x_refs):
        @pl.when(jnp.logical_and(i >= starts[s], i < starts[s + 1]))
        def _():
            o_ref[...] = x_ref[...] + upd


def _matmul_res(cfg, a, w, xs, gate, tm, tn):
    m, kdim = a.shape
    n = w.shape[1]
    assert all(x.shape[0] % tm == 0 for x in xs)
    starts = [0]
    for x in xs:
        starts.append(starts[-1] + x.shape[0] // tm)
    assert starts[-1] * tm == m
    x_specs = [pl.BlockSpec((tm, tn), lambda j, i, s=s: (jnp.clip(i - starts[s], 0, starts[s + 1] - starts[s] - 1), j))
               for s in range(len(xs))]
    return pl.pallas_call(
        functools.partial(_matmul_res_kernel, starts=tuple(starts)),
        grid=(n // tn, m // tm),
        in_specs=[pl.BlockSpec((tm, kdim), lambda j, i: (i, 0)),
                  pl.BlockSpec((kdim, tn), lambda j, i: (0, j))] + x_specs
                 + [pl.BlockSpec((1, 1, tn), lambda j, i: (_mod_row(cfg, i * tm), 0, j))],
        out_specs=pl.BlockSpec((tm, tn), lambda j, i: (i, j)),
        out_shape=jax.ShapeDtypeStruct((m, n), F32),
        compiler_params=_params(("parallel", "parallel")),
        name="matmul_res",
    )(a, w, *xs, gate)


FD_TM = 512
FD_TK = 512
FD_HALO = GRID_W
FD_TOP = 8
FD_VMEM = 56 * 1024 * 1024


def _ffn_down_kernel(*refs, cfg, on_grid, seq_len, final_norm, aliased):
    refs = list(refs[1:] if aliased else refs)
    hg_ref, hu_ref = refs[:2]
    refs = refs[2:]
    if on_grid:
        hp_ref, hn_ref = refs[:2]
        refs = refs[2:]
    w9_ref, cb_ref, wd_ref, x_ref, g_ref = refs[:5]
    refs = refs[5:]
    if final_norm:
        nf_ref = refs[0]
        refs = refs[1:]
    o_ref, pad_ref, even_ref, odd_ref = refs
    i, k = pl.program_id(0), pl.program_id(1)
    nk = pl.num_programs(1) - 1
    tm, tk = FD_TM, FD_TK
    nslab = tk // 128
    base = FD_TOP + FD_HALO

    @pl.when(k == 0)
    def _():
        o_ref[...] = jnp.zeros_like(o_ref)
        edge = jnp.zeros((nslab, FD_TOP, 128), F32)
        pad_ref[:, 0:FD_TOP, :] = edge
        pad_ref[:, base + tm + FD_HALO:base + tm + FD_HALO + FD_TOP, :] = edge
        if not on_grid:
            halo = jnp.zeros((nslab, FD_HALO, 128), F32)
            pad_ref[:, FD_TOP:base, :] = halo
            pad_ref[:, base + tm:base + tm + FD_HALO, :] = halo

    if on_grid:
        tiles_per_seq = seq_len // tm
        first = i % tiles_per_seq == 0
        last = i % tiles_per_seq == tiles_per_seq - 1

        @pl.when(first)
        def _():
            pad_ref[:, FD_TOP:base, :] = jnp.zeros((nslab, FD_HALO, 128), F32)

        @pl.when(jnp.logical_not(first))
        def _():
            for s in range(nslab):
                pad_ref[s, FD_TOP:base, :] = hp_ref[:, s * 128:(s + 1) * 128]

        @pl.when(last)
        def _():
            pad_ref[:, base + tm:base + tm + FD_HALO, :] = jnp.zeros((nslab, FD_HALO, 128), F32)

        @pl.when(jnp.logical_not(last))
        def _():
            for s in range(nslab):
                pad_ref[s, base + tm:base + tm + FD_HALO, :] = hn_ref[:, s * 128:(s + 1) * 128]

    for s in range(nslab):
        pad_ref[s, base:base + tm, :] = hg_ref[:, s * 128:(s + 1) * 128]

    p = FFN_CONV // 2
    if on_grid:
        taps = tuple(((dy - p) * GRID_W + (dx - p), dy * FFN_CONV + dx, dx - p)
                     for dy in range(FFN_CONV) for dx in range(FFN_CONV))
        period = GRID_W
    else:
        taps = tuple((dx - p, p * FFN_CONV + dx, dx - p) for dx in range(FFN_CONV))
        period = seq_len
    kk = pl.multiple_of(jnp.maximum(k - 1, 0) * tk, tk)
    d_out = o_ref.shape[1]
    col_tile = 256
    n_col = d_out // col_tile
    rows = 64
    w9 = w9_ref[0]
    cb = cb_ref[0]
    conv_blocks = [(r, s) for r in range(tm // rows) for s in range(nslab)]
    per_tile = -(-len(conv_blocks) // n_col)

    def step(cur_ref, nxt_ref):
        for t in range(n_col):
            cols = slice(t * col_tile, (t + 1) * col_tile)
            if cur_ref is not None:
                o_ref[:, cols] += jnp.dot(cur_ref[...], wd_ref[pl.ds(kk, tk), cols], preferred_element_type=F32)
            for r, s in (conv_blocks[t * per_tile:(t + 1) * per_tile] if nxt_ref is not None else ()):
                pos = (lax.broadcasted_iota(jnp.int32, (rows, 128), 0) + r * rows) % period
                lanes = slice(s * 128, (s + 1) * 128)
                acc = cb[:, lanes]
                for shift, widx, dx in taps:
                    xs = pad_ref[s, base + r * rows + shift:base + (r + 1) * rows + shift, :]
                    if dx < 0:
                        xs = jnp.where(pos >= -dx, xs, 0.0)
                    elif dx > 0:
                        xs = jnp.where(pos < period - dx, xs, 0.0)
                    acc = acc + xs * w9[widx:widx + 1, lanes]
                act = _silu(acc) * hu_ref[r * rows:(r + 1) * rows, lanes]
                nxt_ref[r * rows:(r + 1) * rows, lanes] = act.astype(BF16)

    nk_static = D_FF // tk
    bufs = (even_ref, odd_ref)

    @pl.when(k == 0)
    def _():
        step(None, even_ref)

    @pl.when(jnp.logical_and(jnp.logical_and(k > 0, k < nk), k % 2 == 0))
    def _():
        step(odd_ref, even_ref)

    @pl.when(jnp.logical_and(k < nk, k % 2 == 1))
    def _():
        step(even_ref, odd_ref)

    @pl.when(k == nk)
    def _():
        step(bufs[(nk_static - 1) % 2], None)

        def slab(si, _):
            r0 = pl.multiple_of(si * NORM_SLAB, NORM_SLAB)
            res = x_ref[pl.ds(r0, NORM_SLAB), :] + g_ref[0] * o_ref[pl.ds(r0, NORM_SLAB), :]
            if final_norm:
                res = res * lax.rsqrt(jnp.mean(res * res, axis=-1, keepdims=True) + EPS) * nf_ref[...]
            o_ref[pl.ds(r0, NORM_SLAB), :] = res
            return 0

        lax.fori_loop(0, tm // NORM_SLAB, slab, 0)


def _ffn_down(cfg, h, w9, bias, wd, x, gate, nseq, t, row0, on_grid, prev=None, norm_gain=None):
    m, d = x.shape
    tm, tk = FD_TM, FD_TK
    rows = nseq * t
    nk = D_FF // tk
    assert D_FF % tk == 0 and row0 % tm == 0 and rows % tm == 0
    assert (t % tm == 0 and tm % GRID_W == 0) if on_grid else tm % t == 0
    final_norm = norm_gain is not None
    aliased = prev is not None
    assert not (final_norm and aliased)
    rt0 = row0 // tm
    kc = lambda k: jnp.minimum(k, nk - 1)
    hpb = tm // FD_HALO
    in_specs = [pl.BlockSpec((tm, tk), lambda i, k: (rt0 + i, kc(k))),
                pl.BlockSpec((tm, tk), lambda i, k: (rt0 + i, nk + kc(k)))]
    args = [h, h]
    if on_grid:
        in_specs += [pl.BlockSpec((FD_HALO, tk), lambda i, k: (jnp.maximum((rt0 + i) * hpb - 1, 0), kc(k))),
                     pl.BlockSpec((FD_HALO, tk), lambda i, k: (jnp.minimum((rt0 + i + 1) * hpb, m // FD_HALO - 1), kc(k)))]
        args += [h, h]
    w9c = jnp.transpose(w9.reshape(FFN_CONV * FFN_CONV, nk, tk), (1, 0, 2))
    in_specs += [pl.BlockSpec((1, FFN_CONV * FFN_CONV, tk), lambda i, k: (kc(k), 0, 0)),
                 pl.BlockSpec((1, 1, tk), lambda i, k: (kc(k), 0, 0)),
                 pl.BlockSpec((D_FF, d), lambda i, k: (0, 0), pipeline_mode=pl.Buffered(1)),
                 pl.BlockSpec((tm, d), lambda i, k: (rt0 + i, 0)),
                 pl.BlockSpec((1, 1, d), lambda i, k: (_mod_row(cfg, (rt0 + i) * tm), 0, 0))]
    args += [w9c, bias.reshape(nk, 1, tk), wd, x, gate]
    if final_norm:
        in_specs.append(pl.BlockSpec((1, d), lambda i, k: (0, 0)))
        args.append(norm_gain)
    if aliased:
        in_specs = [pl.BlockSpec(memory_space=pl.ANY)] + in_specs
        args = [prev] + args
    out_rows, out_rt0 = (rows, 0) if final_norm else (m, rt0)
    return pl.pallas_call(
        functools.partial(_ffn_down_kernel, cfg=cfg, on_grid=on_grid, seq_len=t, final_norm=final_norm, aliased=aliased),
        grid=(rows // tm, nk + 1),
        in_specs=in_specs,
        out_specs=pl.BlockSpec((tm, d), lambda i, k: (out_rt0 + i, 0)),
        out_shape=jax.ShapeDtypeStruct((out_rows, d), F32),
        scratch_shapes=[pltpu.VMEM((tk // 128, tm + 2 * (FD_HALO + FD_TOP), 128), F32),
                        pltpu.VMEM((tm, tk), BF16),
                        pltpu.VMEM((tm, tk), BF16)],
        input_output_aliases={0: 0} if aliased else {},
        compiler_params=pltpu.CompilerParams(dimension_semantics=("parallel", "arbitrary"), vmem_limit_bytes=FD_VMEM),
        name="ffn_down",
    )(*args)


def _pack_w_in(w):
    s = _SRC
    cols = [w[:, s["mg"]:s["end"]], w[:, s["va"]:s["ga"]], w[:, s["ga"]:s["ra"]], w[:, s["qb"]:s["kb"]],
            w[:, s["kb"]:s["vb"]], w[:, s["vb"]:s["ob"]], w[:, s["ob"]:s["gb"]], w[:, s["xr"]:s["yr"]],
            w[:, s["yr"]:s["mg"]], w[:, s["qa"]:s["ka"]], w[:, s["ka"]:s["va"]], w[:, s["ra"]:s["qb"]],
            w[:, s["gb"]:s["xr"]]]
    used = sum(c.shape[1] for c in cols)
    cols.append(jnp.zeros((w.shape[0], NZ - used), w.dtype))
    return jnp.concatenate(cols, axis=1).astype(BF16)


def _pack_gate_w(gw, gb, lam):
    nb, blk = LRU_BLOCKS, LRU_BLOCK
    gw_p = jnp.transpose(gw, (2, 3, 0, 1, 4)).reshape(nb, blk, 4 * blk).astype(BF16)
    gb_p = jnp.transpose(gb.reshape(2, 2, nb, blk), (2, 0, 1, 3)).reshape(nb, 1, 4 * blk)
    lam_p = jnp.transpose(lam.reshape(2, nb, blk), (1, 0, 2)).reshape(nb, 1, 2 * blk)
    return gw_p, gb_p, lam_p


def _layer(cfg, x, mod, lp, states, final_gain=None):
    (n1, n2, w_in, gwa, gba, gng, mbif, mng, lcw, lcb, lgw, lgb, llam, wbr, bmg, wout, fup, fcw, fcb, fdown) = lp
    s_gla, s_mc, s_mn, s_mm, s_lru = states
    m = cfg.m
    sh1, sc1, g1, sh2, sc2, g2 = (mod[:, k * D_MODEL:(k + 1) * D_MODEL].reshape(MOD_ROWS, 1, D_MODEL) for k in range(6))

    proj_tm = PROJ_TM if (cfg.m_ctx % PROJ_TM == 0 and cfg.tl % PROJ_TM == 0) else 512
    z = _norm_matmul(cfg, x, n1.reshape(1, -1), sc1, sh1, _pack_w_in(w_in), proj_tm, 512)

    s0t = jnp.swapaxes(s_gla, -1, -2)
    o_f, sa_f = _gla(cfg, z, gwa[0], gba[0].reshape(1, -1), s0t[:, 0], False)
    y_a, sa_r = _gla(cfg, z, gwa[1], gba[1].reshape(1, -1), s0t[:, 1], True, o_f, gng.reshape(1, -1))
    new_gla =jnp.swapaxes(jnp.stack([sa_f, sa_r], axis=1), -1, -2)

    g_rows = z[:, Z_SM + SM_GB:Z_SM + SM_GB + 4 * MLSTM_HEADS].T
    k_t = z[:, Z_KB:Z_KB + MLSTM_HEADS * MLSTM_DH].T
    b_col = mbif.reshape(1, -1)
    b_row = mbif.reshape(-1, 1)
    n0 = s_mn[:, :, :, None, :]
    m0 = jnp.broadcast_to(s_mm[:, :, :, None, None], s_mm.shape + (1, 128))
    h_f, cb_f, nb_f, mb_f = _mlstm(cfg, z, k_t, g_rows, b_col, b_row, s_mc[:, 0], n0[:, 0], m0[:, 0], False)
    y_b, cb_r, nb_r, mb_r = _mlstm(cfg, z, k_t, g_rows, b_col, b_row, s_mc[:, 1], n0[:, 1], m0[:, 1], True,
                                   h_f, mng.reshape(1, -1))
    new_mc =jnp.stack([cb_f, cb_r], axis=1)
    new_mn = jnp.stack([nb_f[:, :, 0], nb_r[:, :, 0]], axis=1)
    new_mm = jnp.stack([mb_f[:, :, 0, 0], mb_r[:, :, 0, 0]], axis=1)

    gw_p, gb_p, lam_p = _pack_gate_w(lgw, lgb, llam)
    cbias = lcb.reshape(1, -1)
    y_c, new_lru = _rglru(cfg, z, lcw, cbias, gw_p, gb_p, lam_p, jnp.zeros((cfg.bc, N_DIR, LRU_WIDTH), F32),
                          cfg.bc, cfg.tc, 0)
    y_c, _ = _rglru(cfg, z, lcw, cbias, gw_p, gb_p, lam_p, s_lru, cfg.bl, cfg.tl, cfg.m_ctx, y_prev=y_c)

    merged = _merge(cfg, y_a, y_b, y_c, wbr.astype(BF16), z, bmg.reshape(N_BRANCH, 1, D_MODEL), 512, 1024)
    x = _matmul_res(cfg, merged, wout.astype(BF16), x, g1, 512, 1024)

    h = _norm_matmul(cfg, (x,), n2.reshape(1, -1), sc2, sh2, fup.astype(BF16), proj_tm, 512)
    w9 = fcw.reshape(FFN_CONV * FFN_CONV, D_FF)
    wd = fdown.astype(BF16)
    if final_gain is None:
        x_ctx = _ffn_down(cfg, h, w9, fcb, wd, x, g2, cfg.bc, cfg.tc, 0, False)
        x = _ffn_down(cfg, h, w9, fcb, wd, x, g2, cfg.bl, cfg.tl, cfg.m_ctx, True, prev=x_ctx)
    else:
        x = (_ffn_down(cfg, h, w9, fcb, wd, x, g2, cfg.bc, cfg.tc, 0, False, norm_gain=final_gain),
             _ffn_down(cfg, h, w9, fcb, wd, x, g2, cfg.bl, cfg.tl, cfg.m_ctx, True, norm_gain=final_gain))
    return x, (new_gla, new_mc, new_mn, new_mm, new_lru)


def kernel(x_prompt, x_sample, state_gla, state_mlstm_c, state_mlstm_n, state_mlstm_m, state_rglru, c, c_ctx, norm1_g, norm2_g, w_mod, b_mod, w_in, gla_w_alpha, gla_b_alpha, gla_norm_g, mlstm_b_if, mlstm_norm_g, lru_conv_w, lru_conv_b, lru_gate_w, lru_gate_b, lru_lambda, w_branch, b_merge, w_out, ffn_w_up, ffn_conv_w, ffn_conv_b, ffn_w_down, norm_f_g):
    bc, tc, d = x_prompt.shape
    bl, tl, _ = x_sample.shape
    cfg = Cfg(bc, tc, bl, tl)
    assert tc % TB == 0 and tl % TB == 0 and cfg.m_ctx % tl == 0 and 1 + bl <= MOD_ROWS
    assert bc % NSLOT == 0 and bl % NSLOT == 0
    depth = w_in.shape[0]

    x = (x_prompt.reshape(bc * tc, d), x_sample.reshape(bl * tl, d))
    c_all = jnp.concatenate([c_ctx[None, :], c, jnp.zeros((MOD_ROWS - 1 - bl, d), F32)], axis=0)
    mod = _modulation(c_all, w_mod, b_mod)

    new = []
    for l in range(depth):
        lp = (norm1_g[l], norm2_g[l], w_in[l], gla_w_alpha[l], gla_b_alpha[l], gla_norm_g[l], mlstm_b_if[l],
              mlstm_norm_g[l], lru_conv_w[l], lru_conv_b[l], lru_gate_w[l], lru_gate_b[l], lru_lambda[l],
              w_branch[l], b_merge[l], w_out[l], ffn_w_up[l], ffn_conv_w[l], ffn_conv_b[l], ffn_w_down[l])
        states = (state_gla[:, l], state_mlstm_c[:, l], state_mlstm_n[:, l], state_mlstm_m[:, l], state_rglru[:, l])
        x, st = _layer(cfg, x, mod[l], lp, states, norm_f_g.reshape(1, -1) if l == depth - 1 else None)
        x = x if l == depth - 1 else (x,)
        new.append(st)

    y_prompt = x[0].reshape(bc, tc, d)
    y_sample = x[1].reshape(bl, tl, d)
    stacked = tuple(jnp.stack([new[l][k] for l in range(depth)], axis=1) for k in range(5))
    return (y_prompt, y_sample) + stacked
```

```python
import functools
from typing import NamedTuple

import numpy as np
import jax
import jax.numpy as jnp
from jax import lax
from jax.experimental import pallas as pl
from jax.experimental.pallas import tpu as pltpu

F32 = jnp.float32
BF16 = jnp.bfloat16

D_MODEL = 2048
DEPTH = 2
GRID_W = 64
N_DIR = 2
N_BRANCH = 3
BRANCH_W = 1024
GLA_HEADS = 4
GLA_DK = 128
GLA_DV = 256
GLA_RANK = 16
GLA_TAU = 16.0
MLSTM_HEADS = 4
MLSTM_DH = 256
CHUNK = 64
LRU_WIDTH = 1024
LRU_BLOCKS = 8
LRU_BLOCK = 128
LRU_CONV = 4
LRU_C = 8.0
D_FF = 5632
FFN_CONV = 3
EPS = 1e-6

_SRC = dict(qa=0, ka=512, va=1024, ga=2048, ra=3072, qb=3104, kb=4128, vb=5152, ob=6176, gb=7200,
            xr=7216, yr=8240, mg=9264, end=15408)
Z_MG, Z_VA, Z_GA, Z_QB, Z_KB, Z_VB, Z_OB, Z_XR, Z_YR, Z_QA, Z_KA, Z_SM = (
    0, 6144, 7168, 8192, 9216, 10240, 11264, 12288, 13312, 14336, 14848, 15360)
SM_W = 128
SM_RA, SM_GB = 0, 32
NZ = 15872
MOD_ROWS = 16

PROJ_TM = 1024
NORM_SLAB = 32
PROJ_SLAB = 64
TB = 256
NCH = TB // CHUNK
VMEM_LIMIT = 48 * 1024 * 1024
PROJ_SPLIT_VMEM = 58 * 1024 * 1024


class Cfg(NamedTuple):
    bc: int
    tc: int
    bl: int
    tl: int

    @property
    def m_ctx(self):
        return self.bc * self.tc

    @property
    def m(self):
        return self.bc * self.tc + self.bl * self.tl


def _params(sem):
    return pltpu.CompilerParams(dimension_semantics=sem, vmem_limit_bytes=VMEM_LIMIT)


def _softplus(x):
    return jnp.maximum(x, 0.0) + jnp.log1p(jnp.exp(-jnp.abs(x)))


def _log_sigmoid(x):
    return -_softplus(-x)


def _silu(x):
    return x * jax.nn.sigmoid(x)


def _gelu_tanh(x):
    return 0.5 * x * (1.0 + jnp.tanh(np.sqrt(2.0 / np.pi).astype(np.float32) * (x + 0.044715 * (x * x * x))))


def _mod_row(cfg, row):
    return jnp.where(row < cfg.m_ctx, 0, 1 + jnp.maximum(row - cfg.m_ctx, 0) // cfg.tl)


def _mod_kernel(c_ref, w_ref, b_ref, o_ref):
    c = c_ref[...]
    a = _silu(c).astype(BF16)
    o_ref[0] = jnp.dot(a, w_ref[0].astype(BF16), preferred_element_type=F32) + b_ref[0]


def _modulation(c_all, w_mod, b_mod):
    depth, d, n = w_mod.shape
    tn = 512
    return pl.pallas_call(
        _mod_kernel,
        grid=(depth, n // tn),
        in_specs=[pl.BlockSpec((MOD_ROWS, d), lambda l, j: (0, 0)),
                  pl.BlockSpec((1, d, tn), lambda l, j: (l, 0, j)),
                  pl.BlockSpec((1, 1, tn), lambda l, j: (l, 0, j))],
        out_specs=pl.BlockSpec((1, MOD_ROWS, tn), lambda l, j: (l, 0, j)),
        out_shape=jax.ShapeDtypeStruct((depth, MOD_ROWS, n), F32),
        compiler_params=_params(("parallel", "parallel")),
        name="modulation",
    )(c_all, w_mod, b_mod.reshape(depth, 1, n))


def _norm_rows(x_ref, g_ref, sc_ref, sh_ref, u_ref, r0, nrows):
    x = x_ref[pl.ds(r0, nrows), :]
    y = x * lax.rsqrt(jnp.mean(x * x, axis=-1, keepdims=True) + EPS) * g_ref[...]
    u_ref[pl.ds(r0, nrows), :] = (y * (1.0 + sc_ref[0]) + sh_ref[0]).astype(BF16)


def _norm_matmul_kernel(*refs, slab, starts):
    nsrc = len(starts) - 1
    x0_ref = refs[0]
    xn_refs = refs[1:1 + nsrc]
    g_ref, sc0_ref, sh0_ref, scn_ref, shn_ref, w_ref, o_ref, ua_ref, ub_ref = refs[1 + nsrc:]
    i, j = pl.program_id(0), pl.program_id(1)
    tm = x0_ref.shape[0]
    nxt = jnp.minimum(i + 1, starts[-1] - 1)

    @pl.when(jnp.logical_and(i == 0, j == 0))
    def _():
        def first(si, _):
            _norm_rows(x0_ref, g_ref, sc0_ref, sh0_ref, ua_ref, pl.multiple_of(si * NORM_SLAB, NORM_SLAB), NORM_SLAB)
            return 0

        lax.fori_loop(0, tm // NORM_SLAB, first, 0)

    r0 = pl.multiple_of(jnp.minimum(j, tm // slab - 1) * slab, slab)
    for s, xn_ref in enumerate(xn_refs):
        from_s = jnp.logical_and(nxt >= starts[s], nxt < starts[s + 1])

        @pl.when(jnp.logical_and(from_s, i % 2 == 0))
        def _():
            _norm_rows(xn_ref, g_ref, scn_ref, shn_ref, ub_ref, r0, slab)
            o_ref[...] = jnp.dot(ua_ref[...], w_ref[...], preferred_element_type=F32)

        @pl.when(jnp.logical_and(from_s, i % 2 == 1))
        def _():
            _norm_rows(xn_ref, g_ref, scn_ref, shn_ref, ua_ref, r0, slab)
            o_ref[...] = jnp.dot(ub_ref[...], w_ref[...], preferred_element_type=F32)


def _norm_matmul(cfg, xs, gain, sc, sh, w, tm, tn):
    d = xs[0].shape[1]
    n = w.shape[1]
    slab = PROJ_SLAB
    assert tm % slab == 0 and n // tn >= tm // slab and all(x.shape[0] % tm == 0 for x in xs)
    starts = [0]
    for x in xs:
        starts.append(starts[-1] + x.shape[0] // tm)
    ntiles = starts[-1]
    nxt = lambda i: jnp.minimum(i + 1, ntiles - 1)
    mod0 = lambda i, j: (_mod_row(cfg, 0), 0, 0)
    modn = lambda i, j: (_mod_row(cfg, nxt(i) * tm), 0, 0)
    src_specs = [pl.BlockSpec((tm, d), lambda i, j, s=s: (jnp.clip(nxt(i) - starts[s], 0, starts[s + 1] - starts[s] - 1), 0))
                 for s in range(len(xs))]
    return pl.pallas_call(
        functools.partial(_norm_matmul_kernel, slab=slab, starts=tuple(starts)),
        grid=(ntiles, n // tn),
        in_specs=[pl.BlockSpec((tm, d), lambda i, j: (0, 0), pipeline_mode=pl.Buffered(1))] + src_specs
                 + [pl.BlockSpec((1, d), lambda i, j: (0, 0)),
                    pl.BlockSpec((1, 1, d), mod0),
                    pl.BlockSpec((1, 1, d), mod0),
                    pl.BlockSpec((1, 1, d), modn),
                    pl.BlockSpec((1, 1, d), modn),
                    pl.BlockSpec((d, tn), lambda i, j: (0, j))],
        out_specs=pl.BlockSpec((tm, tn), lambda i, j: (i, j)),
        out_shape=jax.ShapeDtypeStruct((ntiles * tm, n), F32),
        scratch_shapes=[pltpu.VMEM((tm, d), BF16), pltpu.VMEM((tm, d), BF16)],
        compiler_params=pltpu.CompilerParams(dimension_semantics=("arbitrary", "arbitrary"),
                                             vmem_limit_bytes=VMEM_LIMIT if len(xs) == 1 else PROJ_SPLIT_VMEM),
        name="norm_matmul",
    )(xs[0], *xs, gain, sc, sh, sc, sh, w)


NSLOT = 2


def _pair_pos(cfg, i):
    kc, kl = cfg.tc // TB, cfg.tl // TB
    nc = (cfg.bc // NSLOT) * kc
    is_ctx = i < nc
    il = jnp.maximum(i - nc, 0)
    j = jnp.where(is_ctx, i % kc, il % kl)
    k = jnp.where(is_ctx, kc, kl)
    s = jnp.where(is_ctx, i // kc, il // kl)
    return is_ctx, s, j, k


def _pair_row_block(cfg, reverse, slot, i):
    kc, kl = cfg.tc // TB, cfg.tl // TB
    is_ctx, s, j, k = _pair_pos(cfg, i)
    jj = (k - 1 - j) if reverse else j
    return jnp.where(is_ctx, (s + slot * (cfg.bc // NSLOT)) * kc + jj,
                     cfg.bc * kc + (s + slot * (cfg.bl // NSLOT)) * kl + jj)


def _pair_local_block(cfg, reverse, i):
    kc, kl = cfg.tc // TB, cfg.tl // TB
    is_ctx, s, j, k = _pair_pos(cfg, i)
    jj = (k - 1 - j) if reverse else j
    return jnp.where(is_ctx, s * kc + jj, (cfg.bc // NSLOT) * kc + s * kl + jj)


def _pair_lat_seq(cfg, i):
    is_ctx, s, _, _ = _pair_pos(cfg, i)
    return jnp.where(is_ctx, 0, jnp.minimum(s, cfg.bl // NSLOT - 1))


def _pair_ctx_seq(cfg, i):
    is_ctx, s, _, _ = _pair_pos(cfg, i)
    return jnp.where(is_ctx, s, cfg.bc // NSLOT - 1)


def _slot_of_tile(cfg, tm, i):
    nc, nl = cfg.m_ctx // tm // NSLOT, (cfg.m - cfg.m_ctx) // tm // NSLOT
    il = jnp.maximum(i - NSLOT * nc, 0)
    is_ctx = i < NSLOT * nc
    return jnp.where(is_ctx, i // nc, il // nl), jnp.where(is_ctx, i % nc, nc + il % nl)


def _tri(reverse, n=CHUNK):
    row = lax.broadcasted_iota(jnp.int32, (n, n), 0)
    col = lax.broadcasted_iota(jnp.int32, (n, n), 1)
    return (row <= col) if reverse else (row >= col)


def _head_norm(x, g):
    return x * lax.rsqrt(jnp.mean(x * x, axis=-1, keepdims=True) + EPS) * g


_NT = (((1,), (1,)), ((), ()))
_TN = (((0,), (0,)), ((), ()))


GLA_LOCKSTEP = 2


def _gla_kernel(*refs, cfg, reverse):
    it = iter(refs)
    blocks = [[next(it) for _ in range(4)] for _ in range(NSLOT)]
    wal_ref, bal_ref = next(it), next(it)
    init_ref = next(it)
    init = [init_ref.at[b] for b in range(NSLOT)]
    if reverse:
        gates = [next(it) for _ in range(NSLOT)]
        of_ref, gn_ref = next(it), next(it)
        epi = [[gates[b], of_ref.at[b]] for b in range(NSLOT)]
    out_ref, fin_ref = next(it), next(it)
    outs = [out_ref.at[b] for b in range(NSLOT)]
    fin = [fin_ref.at[b] for b in range(NSLOT)]
    st_ref, la_ref = next(it), next(it)
    is_ctx, _, j, k = _pair_pos(cfg, pl.program_id(0))
    nh = GLA_HEADS

    @pl.when(jnp.logical_and(j == 0, is_ctx))
    def _():
        st_ref[...] = jnp.zeros_like(st_ref)

    @pl.when(jnp.logical_and(j == 0, jnp.logical_not(is_ctx)))
    def _():
        for b in range(NSLOT):
            st_ref[b * nh:(b + 1) * nh] = init[b][0]

    d = 1 if reverse else 0
    slots = range(NSLOT)
    both = lambda f: [f(b) for b in slots]
    wal = wal_ref[...].astype(BF16)
    for b in slots:
        ra = blocks[b][3][:, SM_RA + d * GLA_RANK:SM_RA + (d + 1) * GLA_RANK].astype(BF16)
        pre = jnp.dot(ra, wal, preferred_element_type=F32) + bal_ref[...]
        la_ref[b] = _log_sigmoid(pre) * (1.0 / GLA_TAU)

    tri = _tri(reverse)
    tri_f = tri.astype(F32)
    last = 0 if reverse else CHUNK - 1
    qscale = GLA_DK ** -0.5
    for c in (range(NCH - 1, -1, -1) if reverse else range(NCH)):
        rows = slice(c * CHUNK, (c + 1) * CHUNK)
        cum = both(lambda b: jnp.dot(tri_f, la_ref[b, rows, :], precision=lax.Precision.HIGHEST,
                                     preferred_element_type=F32))
        tot = both(lambda b: cum[b][last:last + 1, :])
        e_q = both(lambda b: jnp.exp(cum[b]))
        e_k = both(lambda b: jnp.exp(-cum[b]))
        e_end = both(lambda b: jnp.exp(tot[b] - cum[b]))
        e_tot = both(lambda b: jnp.exp(tot[b]))
        for h0 in range(0, GLA_HEADS, GLA_LOCKSTEP):
            units = [(b, h0 + dh) for dh in range(GLA_LOCKSTEP) for b in slots]
            each = lambda f: [f(u, b, h) for u, (b, h) in enumerate(units)]
            kc = [slice(h * GLA_DK, (h + 1) * GLA_DK) for _, h in units]
            vc = [slice(h * GLA_DV, (h + 1) * GLA_DV) for _, h in units]
            sh = [b * nh + h for b, h in units]
            qh = each(lambda u, b, h: blocks[b][0][rows, kc[u]] * qscale)
            kh = each(lambda u, b, h: blocks[b][1][rows, kc[u]])
            vh = each(lambda u, b, h: blocks[b][2][rows, vc[u]].astype(BF16))
            q_in = each(lambda u, b, h: (qh[u] * e_q[b][:, kc[u]]).astype(BF16))
            k_in = each(lambda u, b, h: (kh[u] * e_k[b][:, kc[u]]).astype(BF16))
            att = each(lambda u, b, h: lax.dot_general(q_in[u], k_in[u], _NT, preferred_element_type=F32))
            att = each(lambda u, b, h: jnp.where(tri, att[u], 0.0).astype(BF16))
            st = each(lambda u, b, h: st_ref[sh[u]])
            o = each(lambda u, b, h: jnp.dot(att[u], vh[u], preferred_element_type=F32)
                     + lax.dot_general(q_in[u], st[u].astype(BF16), _NT, preferred_element_type=F32))
            k_end = each(lambda u, b, h: (kh[u] * e_end[b][:, kc[u]]).astype(BF16))
            upd = each(lambda u, b, h: lax.dot_general(vh[u], k_end[u], _TN, preferred_element_type=F32))
            for u, (b, h) in enumerate(units):
                st_ref[sh[u]] = e_tot[b][:, kc[u]] * st[u] + upd[u]
            for u, (b, h) in enumerate(units):
                if reverse:
                    ga_ref, of_ref = epi[b]
                    y = _head_norm(of_ref[rows, vc[u]] + o[u], gn_ref[:, vc[u]]) * _silu(ga_ref[rows, vc[u]])
                    outs[b][rows, vc[u]] = y.astype(outs[b].dtype)
                else:
                    outs[b][rows, vc[u]] = o[u]

    @pl.when(jnp.logical_and(is_ctx, j == k - 1))
    def _():
        for b in range(NSLOT):
            fin[b][0] = st_ref[b * nh:(b + 1) * nh]


def _gla(cfg, z, wal, bal, s0t, reverse, o_fwd=None, gnorm=None):
    m = cfg.m
    nh = GLA_HEADS
    hk, hv = nh * GLA_DK, nh * GLA_DV
    lb = functools.partial(_pair_local_block, cfg, reverse)
    in_specs, args = [], []
    for b in range(NSLOT):
        rb = functools.partial(_pair_row_block, cfg, reverse, b)
        in_specs += [pl.BlockSpec((TB, hk), lambda i, rb=rb: (rb(i), Z_QA // hk)),
                     pl.BlockSpec((TB, hk), lambda i, rb=rb: (rb(i), Z_KA // hk)),
                     pl.BlockSpec((TB, hv), lambda i, rb=rb: (rb(i), Z_VA // hv)),
                     pl.BlockSpec((TB, SM_W), lambda i, rb=rb: (rb(i), Z_SM // SM_W))]
        args += [z, z, z, z]
    in_specs += [pl.BlockSpec((GLA_RANK, hk), lambda i: (0, 0)), pl.BlockSpec((1, hk), lambda i: (0, 0))]
    args += [wal, bal]
    half_c, half_l = cfg.bc // NSLOT, cfg.bl // NSLOT
    in_specs.append(pl.BlockSpec((NSLOT, 1, nh, GLA_DV, GLA_DK), lambda i: (0, _pair_lat_seq(cfg, i), 0, 0, 0)))
    args.append(s0t.reshape(NSLOT, half_l, nh, GLA_DV, GLA_DK))
    if reverse:
        for b in range(NSLOT):
            rb = functools.partial(_pair_row_block, cfg, reverse, b)
            in_specs.append(pl.BlockSpec((TB, hv), lambda i, rb=rb: (rb(i), Z_GA // hv)))
            args.append(z)
        in_specs += [pl.BlockSpec((NSLOT, TB, hv), lambda i: (0, lb(i), 0)), pl.BlockSpec((1, hv), lambda i: (0, 0))]
        args += [o_fwd, gnorm]
    out, fin = pl.pallas_call(
        functools.partial(_gla_kernel, cfg=cfg, reverse=reverse),
        grid=(m // TB // NSLOT,),
        in_specs=in_specs,
        out_specs=[pl.BlockSpec((NSLOT, TB, hv), lambda i: (0, lb(i), 0)),
                   pl.BlockSpec((NSLOT, 1, nh, GLA_DV, GLA_DK), lambda i: (0, _pair_ctx_seq(cfg, i), 0, 0, 0))],
        out_shape=[jax.ShapeDtypeStruct((NSLOT, m // NSLOT, hv), BF16 if reverse else F32),
                   jax.ShapeDtypeStruct((NSLOT, half_c, nh, GLA_DV, GLA_DK), F32)],
        scratch_shapes=[pltpu.VMEM((NSLOT * nh, GLA_DV, GLA_DK), F32), pltpu.VMEM((NSLOT, TB, hk), F32)],
        compiler_params=_params(("arbitrary",)),
        name="gla_rev" if reverse else "gla_fwd",
    )(*args)
    return out, fin.reshape(cfg.bc, nh, GLA_DV, GLA_DK)


MCH = 128
NMC = TB // MCH
MLSTM_LOCKSTEP = 2


def _mlstm_kernel(*refs, cfg, reverse):
    it = iter(refs)
    blocks = [[next(it) for _ in range(6)] for _ in range(NSLOT)]
    bc_ref, br_ref = next(it), next(it)
    init_refs = [next(it) for _ in range(3)]
    init = [[r.at[b] for r in init_refs] for b in range(NSLOT)]
    if reverse:
        gates = [next(it) for _ in range(NSLOT)]
        hf_all, gn_ref = next(it), next(it)
        epi = [[gates[b], hf_all.at[b]] for b in range(NSLOT)]
    out_ref = next(it)
    outs = [out_ref.at[b] for b in range(NSLOT)]
    fin_refs = [next(it) for _ in range(3)]
    fin = [[r.at[b] for r in fin_refs] for b in range(NSLOT)]
    c_ref, n_ref, m_ref = next(it), next(it), next(it)
    is_ctx, _, j, k = _pair_pos(cfg, pl.program_id(0))
    nh = MLSTM_HEADS

    @pl.when(jnp.logical_and(j == 0, is_ctx))
    def _():
        c_ref[...] = jnp.zeros_like(c_ref)
        n_ref[...] = jnp.zeros_like(n_ref)
        m_ref[...] = jnp.zeros_like(m_ref)

    @pl.when(jnp.logical_and(j == 0, jnp.logical_not(is_ctx)))
    def _():
        for b in range(NSLOT):
            c_ref[b * nh:(b + 1) * nh] = init[b][0][0]
            n_ref[b * nh:(b + 1) * nh] = init[b][1][0]
            m_ref[b * nh:(b + 1) * nh] = init[b][2][0]

    d = 1 if reverse else 0
    tri = _tri(reverse, MCH)
    tri_f = tri.astype(F32)
    tri_tf = _tri(not reverse, MCH).astype(F32)
    last = 0 if reverse else MCH - 1
    ng = 2 * MLSTM_HEADS
    kscale = MLSTM_DH ** -0.5
    slots = range(NSLOT)
    both = lambda f: [f(b) for b in slots]
    hp = lax.Precision.HIGHEST
    for c in (range(NMC - 1, -1, -1) if reverse else range(NMC)):
        rows = slice(c * MCH, (c + 1) * MCH)
        g_col = both(lambda b: blocks[b][4][rows, SM_GB:SM_GB + 2 * ng] + bc_ref[...])
        g_row = both(lambda b: blocks[b][5][:, rows] + br_ref[...])
        fcum_col = both(lambda b: jnp.dot(tri_f, _log_sigmoid(g_col[b]), precision=hp, preferred_element_type=F32))
        fcum_row = both(lambda b: jnp.dot(_log_sigmoid(g_row[b]), tri_tf, precision=hp, preferred_element_type=F32))
        for h0 in range(0, MLSTM_HEADS, MLSTM_LOCKSTEP):
            units = [(b, h0 + dh) for dh in range(MLSTM_LOCKSTEP) for b in slots]
            each = lambda f: [f(u, b, h) for u, (b, h) in enumerate(units)]
            ii = [d * ng + h for _, h in units]
            fi = [d * ng + MLSTM_HEADS + h for _, h in units]
            hc = [slice(h * MLSTM_DH, (h + 1) * MLSTM_DH) for _, h in units]
            sh = [b * nh + h for b, h in units]
            f_col = each(lambda u, b, h: fcum_col[b][:, fi[u]:fi[u] + 1])
            f_row = each(lambda u, b, h: fcum_row[b][fi[u]:fi[u] + 1, :])
            i_col = each(lambda u, b, h: g_col[b][:, ii[u]:ii[u] + 1])
            i_row = each(lambda u, b, h: g_row[b][ii[u]:ii[u] + 1, :])
            m_prev = each(lambda u, b, h: m_ref[sh[u]][:, 0:1])
            dlog = each(lambda u, b, h: jnp.where(tri, f_col[u] - f_row[u] + i_row[u], -jnp.inf))
            prev = each(lambda u, b, h: f_col[u] + m_prev[u])
            mj = each(lambda u, b, h: jnp.maximum(prev[u], jnp.max(dlog[u], axis=-1, keepdims=True)))
            w = each(lambda u, b, h: jnp.exp(dlog[u] - mj[u]))
            wp = each(lambda u, b, h: jnp.exp(prev[u] - mj[u]))
            qh = each(lambda u, b, h: blocks[b][0][rows, hc[u]])
            kh = each(lambda u, b, h: blocks[b][1][rows, hc[u]] * kscale)
            vh = each(lambda u, b, h: blocks[b][2][rows, hc[u]].astype(BF16))
            s = each(lambda u, b, h: lax.dot_general(qh[u].astype(BF16), kh[u].astype(BF16), _NT,
                                                     preferred_element_type=F32) * w[u])
            qp = each(lambda u, b, h: qh[u] * wp[u])
            cm = each(lambda u, b, h: c_ref[sh[u]])
            nv = each(lambda u, b, h: n_ref[sh[u]])
            m_new = each(lambda u, b, h: mj[u][last:last + 1, :])
            tot = each(lambda u, b, h: f_col[u][last:last + 1, :])
            wl_col = each(lambda u, b, h: jnp.exp(tot[u] - f_col[u] + i_col[u] - m_new[u]))
            wl_row = each(lambda u, b, h: jnp.exp(tot[u] - f_row[u] + i_row[u] - m_new[u]))
            decay = each(lambda u, b, h: jnp.exp(tot[u] + m_prev[u] - m_new[u]))
            kw_t = each(lambda u, b, h: (blocks[b][3][hc[u], rows] * kscale) * wl_row[u])
            sv = each(lambda u, b, h: jnp.dot(jnp.concatenate([s[u].astype(BF16), kw_t[u].astype(BF16)], axis=0),
                                              vh[u], preferred_element_type=F32))
            num = each(lambda u, b, h: sv[u][:MCH]
                       + jnp.dot(qp[u].astype(BF16), cm[u].astype(BF16), preferred_element_type=F32))
            upd = each(lambda u, b, h: sv[u][MCH:])
            den = each(lambda u, b, h: jnp.sum(s[u], axis=-1, keepdims=True)
                       + jnp.sum(qp[u] * nv[u], axis=-1, keepdims=True))
            hh = each(lambda u, b, h: num[u] / jnp.maximum(jnp.abs(den[u]), jnp.exp(-mj[u])))
            for u, (b, h) in enumerate(units):
                c_ref[sh[u]] = decay[u] * cm[u] + upd[u]
                n_ref[sh[u]] = decay[u] * nv[u] + jnp.sum(kh[u] * wl_col[u], axis=0, keepdims=True)
                m_ref[sh[u]] = jnp.broadcast_to(m_new[u], (1, 128))
            for u, (b, h) in enumerate(units):
                if reverse:
                    ob_ref, hf_ref = epi[b]
                    y = jax.nn.sigmoid(ob_ref[rows, hc[u]]) * _head_norm(hf_ref[rows, hc[u]] + hh[u], gn_ref[:, hc[u]])
                    outs[b][rows, hc[u]] = y.astype(outs[b].dtype)
                else:
                    outs[b][rows, hc[u]] = hh[u]

    @pl.when(jnp.logical_and(is_ctx, j == k - 1))
    def _():
        for b in range(NSLOT):
            fin[b][0][0] = c_ref[b * nh:(b + 1) * nh]
            fin[b][1][0] = n_ref[b * nh:(b + 1) * nh]
            fin[b][2][0] = m_ref[b * nh:(b + 1) * nh]


def _mlstm(cfg, z, k_t, g_rows, b_col, b_row, c0, n0, m0, reverse, h_fwd=None, gnorm=None):
    m = cfg.m
    hd = MLSTM_HEADS * MLSTM_DH
    nh = MLSTM_HEADS
    lb = functools.partial(_pair_local_block, cfg, reverse)
    ctx = lambda i: (0, _pair_ctx_seq(cfg, i), 0, 0, 0)
    in_specs, args = [], []
    for b in range(NSLOT):
        rb = functools.partial(_pair_row_block, cfg, reverse, b)
        in_specs += [pl.BlockSpec((TB, hd), lambda i, rb=rb: (rb(i), Z_QB // hd)),
                     pl.BlockSpec((TB, hd), lambda i, rb=rb: (rb(i), Z_KB // hd)),
                     pl.BlockSpec((TB, hd), lambda i, rb=rb: (rb(i), Z_VB // hd)),
                     pl.BlockSpec((hd, TB), lambda i, rb=rb: (0, rb(i))),
                     pl.BlockSpec((TB, SM_W), lambda i, rb=rb: (rb(i), Z_SM // SM_W)),
                     pl.BlockSpec((4 * nh, TB), lambda i, rb=rb: (0, rb(i)))]
        args += [z, z, z, k_t, z, g_rows]
    in_specs += [pl.BlockSpec((1, 4 * nh), lambda i: (0, 0)), pl.BlockSpec((4 * nh, 1), lambda i: (0, 0))]
    args += [b_col, b_row]
    half_c, half_l = cfg.bc // NSLOT, cfg.bl // NSLOT
    lat = lambda i: (0, _pair_lat_seq(cfg, i), 0, 0, 0)
    state_dims = ((nh, MLSTM_DH, MLSTM_DH), (nh, 1, MLSTM_DH), (nh, 1, 128))
    in_specs += [pl.BlockSpec((NSLOT, 1) + sd, lat) for sd in state_dims]
    args += [s.reshape((NSLOT, half_l) + sd) for s, sd in zip((c0, n0, m0), state_dims)]
    if reverse:
        for b in range(NSLOT):
            rb = functools.partial(_pair_row_block, cfg, reverse, b)
            in_specs.append(pl.BlockSpec((TB, hd), lambda i, rb=rb: (rb(i), Z_OB // hd)))
            args.append(z)
        in_specs += [pl.BlockSpec((NSLOT, TB, hd), lambda i: (0, lb(i), 0)), pl.BlockSpec((1, hd), lambda i: (0, 0))]
        args += [h_fwd, gnorm]
    res = pl.pallas_call(
        functools.partial(_mlstm_kernel, cfg=cfg, reverse=reverse),
        grid=(m // TB // NSLOT,),
        in_specs=in_specs,
        out_specs=[pl.BlockSpec((NSLOT, TB, hd), lambda i: (0, lb(i), 0))]
                  + [pl.BlockSpec((NSLOT, 1) + sd, ctx) for sd in state_dims],
        out_shape=[jax.ShapeDtypeStruct((NSLOT, m // NSLOT, hd), BF16 if reverse else F32)]
                  + [jax.ShapeDtypeStruct((NSLOT, half_c) + sd, F32) for sd in state_dims],
        scratch_shapes=[pltpu.VMEM((NSLOT * nh, MLSTM_DH, MLSTM_DH), F32),
                        pltpu.VMEM((NSLOT * nh, 1, MLSTM_DH), F32),
                        pltpu.VMEM((NSLOT * nh, 1, 128), F32)],
        compiler_params=_params(("arbitrary",)),
        name="mlstm_rev" if reverse else "mlstm_fwd",
    )(*args)
    return (res[0],) + tuple(r.reshape((cfg.bc,) + sd) for r, sd in zip(res[1:], state_dims))


LRU_ROWS = 256
LRU_PAD = 8
LRU_SEG = 8


def _rglru_kernel(*refs, t, aliased):
    if aliased:
        refs = refs[1:]
    (xr_ref, yr_ref, cw_ref, cb_ref, gw_ref, gb_ref, lam_ref, h0_ref, y_ref, he_ref,
     pad_ref, af_ref, ar_ref, bf_ref, br_ref, hl_ref, pc_ref, hs_ref) = refs
    a_refs, b_refs = (af_ref, ar_ref), (bf_ref, br_ref)
    nsteps = t // LRU_ROWS
    seg = t // LRU_SEG
    pitch = seg + 8
    zeros = jnp.zeros((LRU_PAD, LRU_BLOCK), F32)
    pad_ref[0:LRU_PAD, :] = zeros
    pad_ref[LRU_PAD + t:2 * LRU_PAD + t, :] = zeros

    def pitched(r0):
        if seg >= LRU_ROWS:
            return [(0, LRU_ROWS, pl.multiple_of((r0 // seg) * pitch + r0 % seg, 8))]
        return [(q * seg, seg, pl.multiple_of((r0 // seg + q) * pitch, 8)) for q in range(LRU_ROWS // seg)]

    def fill(ci, _):
        r0 = pl.multiple_of(ci * LRU_ROWS, LRU_ROWS)
        pad_ref[pl.ds(r0 + LRU_PAD, LRU_ROWS), :] = xr_ref[pl.ds(r0, LRU_ROWS), :]
        return 0

    lax.fori_loop(0, nsteps, fill, 0)

    sp = _softplus(-lam_ref[0])
    left = LRU_CONV // 2

    def gates(ci, _):
        r0 = pl.multiple_of(ci * LRU_ROWS, LRU_ROWS)
        xc = cb_ref[...]
        for tap in range(LRU_CONV):
            xc = xc + cw_ref[tap:tap + 1, :] * pad_ref[pl.ds(r0 + LRU_PAD - left + tap, LRU_ROWS), :]
        pre = jnp.dot(xc.astype(BF16), gw_ref[0], preferred_element_type=F32) + gb_ref[0]
        both = lambda f: [f(d) for d in range(N_DIR)]
        r = both(lambda d: jax.nn.sigmoid(pre[:, d * 2 * LRU_BLOCK:d * 2 * LRU_BLOCK + LRU_BLOCK]))
        ig = both(lambda d: jax.nn.sigmoid(pre[:, d * 2 * LRU_BLOCK + LRU_BLOCK:(d + 1) * 2 * LRU_BLOCK]))
        log_a = both(lambda d: (-LRU_C * r[d]) * sp[:, d * LRU_BLOCK:(d + 1) * LRU_BLOCK])
        a = both(lambda d: jnp.exp(log_a[d]))
        b = both(lambda d: jnp.sqrt(1.0 - a[d] * a[d]) * (ig[d] * xc))
        for off, n, dst in pitched(r0):
            for d in range(N_DIR):
                a_refs[d][pl.ds(dst, n), :] = a[d][off:off + n]
                b_refs[d][pl.ds(dst, n), :] = b[d][off:off + n]
        return 0

    lax.fori_loop(0, nsteps, gates, 0)

    def local_scan(i, carry):
        hf, pf, hr, pr = carry
        rf, rr = i, seg - 1 - i
        a_f = af_ref[pl.ds(rf, LRU_SEG, stride=pitch), :]
        b_f = bf_ref[pl.ds(rf, LRU_SEG, stride=pitch), :]
        a_r = ar_ref[pl.ds(rr, LRU_SEG, stride=pitch), :]
        b_r = br_ref[pl.ds(rr, LRU_SEG, stride=pitch), :]
        hf, pf = a_f * hf + b_f, a_f * pf
        hr, pr = a_r * hr + b_r, a_r * pr
        hl_ref[0, rf] = hf
        pc_ref[0, rf] = pf
        hl_ref[1, rr] = hr
        pc_ref[1, rr] = pr
        return hf, pf, hr, pr

    zero = jnp.zeros((LRU_SEG, LRU_BLOCK), F32)
    one = jnp.ones((LRU_SEG, LRU_BLOCK), F32)
    hf, pf, hr, pr = lax.fori_loop(0, seg, local_scan, (zero, one, zero, one), unroll=8)

    enter_f = [h0_ref[0, 0:1, :]]
    for s in range(LRU_SEG - 1):
        enter_f.append(pf[s:s + 1, :] * enter_f[s] + hf[s:s + 1, :])
    he_ref[0, 0:1, :] = pf[LRU_SEG - 1:LRU_SEG, :] * enter_f[LRU_SEG - 1] + hf[LRU_SEG - 1:LRU_SEG, :]
    enter_r = [None] * LRU_SEG
    enter_r[LRU_SEG - 1] = h0_ref[0, 1:2, :]
    for s in range(LRU_SEG - 1, 0, -1):
        enter_r[s - 1] = pr[s:s + 1, :] * enter_r[s] + hr[s:s + 1, :]
    he_ref[0, 1:2, :] = pr[0:1, :] * enter_r[0] + hr[0:1, :]
    init_f = jnp.concatenate(enter_f, axis=0)
    init_r = jnp.concatenate(enter_r, axis=0)

    def fix_up(i, _):
        h = (hl_ref[0, i] + pc_ref[0, i] * init_f) + (hl_ref[1, i] + pc_ref[1, i] * init_r)
        hs_ref[pl.ds(i, LRU_SEG, stride=pitch), :] = h
        return 0

    lax.fori_loop(0, seg, fix_up, 0, unroll=8)

    def finish(ci, _):
        r0 = pl.multiple_of(ci * LRU_ROWS, LRU_ROWS)
        for off, n, src in pitched(r0):
            y = hs_ref[pl.ds(src, n), :] * _gelu_tanh(yr_ref[pl.ds(r0 + off, n), :])
            y_ref[pl.ds(r0 + off, n), :] = y.astype(y_ref.dtype)
        return 0

    lax.fori_loop(0, nsteps, finish, 0)


def _rglru(cfg, z, cw, cb, gw, gb, lam, h0, nseq, t, row0, y_prev=None):
    m = cfg.m
    seg = t // LRU_SEG
    assert row0 % t == 0 and t % LRU_ROWS == 0 and seg % 16 == 0 and (seg % LRU_ROWS == 0 or LRU_ROWS % seg == 0)
    pitched_rows = LRU_SEG * (seg + 8)
    sb = row0 // t
    aliased = y_prev is not None
    in_specs = [pl.BlockSpec((t, LRU_BLOCK), lambda b, n: (sb + b, Z_XR // LRU_BLOCK + n)),
                pl.BlockSpec((t, LRU_BLOCK), lambda b, n: (sb + b, Z_YR // LRU_BLOCK + n)),
                pl.BlockSpec((LRU_CONV, LRU_BLOCK), lambda b, n: (0, n)),
                pl.BlockSpec((1, LRU_BLOCK), lambda b, n: (0, n)),
                pl.BlockSpec((1, LRU_BLOCK, 4 * LRU_BLOCK), lambda b, n: (n, 0, 0)),
                pl.BlockSpec((1, 1, 4 * LRU_BLOCK), lambda b, n: (n, 0, 0)),
                pl.BlockSpec((1, 1, 2 * LRU_BLOCK), lambda b, n: (n, 0, 0)),
                pl.BlockSpec((1, N_DIR, LRU_BLOCK), lambda b, n: (b, 0, n))]
    args = [z, z, cw, cb, gw, gb, lam, h0]
    if aliased:
        in_specs = [pl.BlockSpec(memory_space=pl.ANY)] + in_specs
        args = [y_prev] + args
    return pl.pallas_call(
        functools.partial(_rglru_kernel, t=t, aliased=aliased),
        grid=(nseq, LRU_BLOCKS),
        in_specs=in_specs,
        out_specs=[pl.BlockSpec((t, LRU_BLOCK), lambda b, n: (sb + b, n)),
                   pl.BlockSpec((1, N_DIR, LRU_BLOCK), lambda b, n: (b, 0, n))],
        out_shape=[jax.ShapeDtypeStruct((m, LRU_WIDTH), BF16),
                   jax.ShapeDtypeStruct((nseq, N_DIR, LRU_WIDTH), F32)],
        scratch_shapes=[pltpu.VMEM((t + 2 * LRU_PAD, LRU_BLOCK), F32)]
                       + [pltpu.VMEM((pitched_rows, LRU_BLOCK), F32) for _ in range(2 * N_DIR)]
                       + [pltpu.VMEM((N_DIR, seg, LRU_SEG, LRU_BLOCK), F32),
                          pltpu.VMEM((N_DIR, seg, LRU_SEG, LRU_BLOCK), F32),
                          pltpu.VMEM((pitched_rows, LRU_BLOCK), F32)],
        input_output_aliases={0: 0} if aliased else {},
        compiler_params=_params(("parallel", "parallel")),
        name="rglru",
    )(*args)


def _merge_kernel(ya_ref, yb_ref, yc_ref, w_ref, ma_ref, mb_ref, mc_ref, bm_ref, o_ref):
    acc = None
    for n, (y_ref, mg_ref) in enumerate(((ya_ref, ma_ref), (yb_ref, mb_ref), (yc_ref, mc_ref))):
        g = jax.nn.sigmoid(mg_ref[...] + bm_ref[n])
        term = g * jnp.dot(y_ref[...], w_ref[n], preferred_element_type=F32)
        acc = term if acc is None else acc + term
    o_ref[...] = acc.astype(o_ref.dtype)


def _merge(cfg, ya, yb, yc, wbr, z, bm, tm, tn):
    m = cfg.m
    nj = D_MODEL // tn
    assert (cfg.m_ctx // NSLOT) % tm == 0 and ((m - cfg.m_ctx) // NSLOT) % tm == 0
    slot_spec = pl.BlockSpec((None, tm, BRANCH_W), lambda j, i: _slot_of_tile(cfg, tm, i) + (0,))
    y_spec = pl.BlockSpec((tm, BRANCH_W), lambda j, i: (i, 0))
    mg_spec = lambda n: pl.BlockSpec((tm, tn), lambda j, i: (i, Z_MG // tn + n * nj + j))
    return pl.pallas_call(
        _merge_kernel,
        grid=(nj, m // tm),
        in_specs=[slot_spec, slot_spec, y_spec,
                  pl.BlockSpec((N_BRANCH, BRANCH_W, tn), lambda j, i: (0, 0, j)),
                  mg_spec(0), mg_spec(1), mg_spec(2),
                  pl.BlockSpec((N_BRANCH, 1, tn), lambda j, i: (0, 0, j))],
        out_specs=pl.BlockSpec((tm, tn), lambda j, i: (i, j)),
        out_shape=jax.ShapeDtypeStruct((m, D_MODEL), BF16),
        compiler_params=_params(("parallel", "parallel")),
        name="merge",
    )(ya, yb, yc, wbr, z, z, z, bm)


def _matmul_res_kernel(*refs, starts):
    nsrc = len(starts) - 1
    a_ref, w_ref = refs[:2]
    x_refs = refs[2:2 + nsrc]
    g_ref, o_ref = refs[2 + nsrc:]
    i = pl.program_id(1)
    if nsrc == 1:
        o_ref[...] = x_refs[0][...] + g_ref[0] * jnp.dot(a_ref[...], w_ref[...], preferred_element_type=F32)
        return
    upd = g_ref[0] * jnp.dot(a_ref[...], w_ref[...], preferred_element_type=F32)
    for s, x_ref in enumerate(x_refs):
        @pl.when(jnp.logical_and(i >= starts[s], i < starts[s + 1]))
        def _():
            o_ref[...] = x_ref[...] + upd


def _matmul_res(cfg, a, w, xs, gate, tm, tn):
    m, kdim = a.shape
    n = w.shape[1]
    assert all(x.shape[0] % tm == 0 for x in xs)
    starts = [0]
    for x in xs:
        starts.append(starts[-1] + x.shape[0] // tm)
    assert starts[-1] * tm == m
    x_specs = [pl.BlockSpec((tm, tn), lambda j, i, s=s: (jnp.clip(i - starts[s], 0, starts[s + 1] - starts[s] - 1), j))
               for s in range(len(xs))]
    return pl.pallas_call(
        functools.partial(_matmul_res_kernel, starts=tuple(starts)),
        grid=(n // tn, m // tm),
        in_specs=[pl.BlockSpec((tm, kdim), lambda j, i: (i, 0)),
                  pl.BlockSpec((kdim, tn), lambda j, i: (0, j))] + x_specs
                 + [pl.BlockSpec((1, 1, tn), lambda j, i: (_mod_row(cfg, i * tm), 0, j))],
        out_specs=pl.BlockSpec((tm, tn), lambda j, i: (i, j)),
        out_shape=jax.ShapeDtypeStruct((m, n), F32),
        compiler_params=_params(("parallel", "parallel")),
        name="matmul_res",
    )(a, w, *xs, gate)


FD_TM = 512
FD_TK = 512
FD_HALO = GRID_W
FD_TOP = 8
FD_VMEM = 56 * 1024 * 1024


def _ffn_down_kernel(*refs, cfg, on_grid, seq_len, final_norm, aliased):
    refs = list(refs[1:] if aliased else refs)
    hg_ref, hu_ref = refs[:2]
    refs = refs[2:]
    if on_grid:
        hp_ref, hn_ref = refs[:2]
        refs = refs[2:]
    w9_ref, cb_ref, wd_ref, x_ref, g_ref = refs[:5]
    refs = refs[5:]
    if final_norm:
        nf_ref = refs[0]
        refs = refs[1:]
    o_ref, pad_ref, even_ref, odd_ref = refs
    i, k = pl.program_id(0), pl.program_id(1)
    nk = pl.num_programs(1) - 1
    tm, tk = FD_TM, FD_TK
    nslab = tk // 128
    base = FD_TOP + FD_HALO

    @pl.when(k == 0)
    def _():
        o_ref[...] = jnp.zeros_like(o_ref)
        edge = jnp.zeros((nslab, FD_TOP, 128), F32)
        pad_ref[:, 0:FD_TOP, :] = edge
        pad_ref[:, base + tm + FD_HALO:base + tm + FD_HALO + FD_TOP, :] = edge
        if not on_grid:
            halo = jnp.zeros((nslab, FD_HALO, 128), F32)
            pad_ref[:, FD_TOP:base, :] = halo
            pad_ref[:, base + tm:base + tm + FD_HALO, :] = halo

    if on_grid:
        tiles_per_seq = seq_len // tm
        first = i % tiles_per_seq == 0
        last = i % tiles_per_seq == tiles_per_seq - 1

        @pl.when(first)
        def _():
            pad_ref[:, FD_TOP:base, :] = jnp.zeros((nslab, FD_HALO, 128), F32)

        @pl.when(jnp.logical_not(first))
        def _():
            for s in range(nslab):
                pad_ref[s, FD_TOP:base, :] = hp_ref[:, s * 128:(s + 1) * 128]

        @pl.when(last)
        def _():
            pad_ref[:, base + tm:base + tm + FD_HALO, :] = jnp.zeros((nslab, FD_HALO, 128), F32)

        @pl.when(jnp.logical_not(last))
        def _():
            for s in range(nslab):
                pad_ref[s, base + tm:base + tm + FD_HALO, :] = hn_ref[:, s * 128:(s + 1) * 128]

    for s in range(nslab):
        pad_ref[s, base:base + tm, :] = hg_ref[:, s * 128:(s + 1) * 128]

    p = FFN_CONV // 2
    if on_grid:
        taps = tuple(((dy - p) * GRID_W + (dx - p), dy * FFN_CONV + dx, dx - p)
                     for dy in range(FFN_CONV) for dx in range(FFN_CONV))
        period = GRID_W
    else:
        taps = tuple((dx - p, p * FFN_CONV + dx, dx - p) for dx in range(FFN_CONV))
        period = seq_len
    kk = pl.multiple_of(jnp.maximum(k - 1, 0) * tk, tk)
    d_out = o_ref.shape[1]
    col_tile = 256
    n_col = d_out // col_tile
    rows = 64
    w9 = w9_ref[0]
    cb = cb_ref[0]
    conv_blocks = [(r, s) for r in range(tm // rows) for s in range(nslab)]
    per_tile = -(-len(conv_blocks) // n_col)

    def step(cur_ref, nxt_ref):
        for t in range(n_col):
            cols = slice(t * col_tile, (t + 1) * col_tile)
            if cur_ref is not None:
                o_ref[:, cols] += jnp.dot(cur_ref[...], wd_ref[pl.ds(kk, tk), cols], preferred_element_type=F32)
            for r, s in (conv_blocks[t * per_tile:(t + 1) * per_tile] if nxt_ref is not None else ()):
                pos = (lax.broadcasted_iota(jnp.int32, (rows, 128), 0) + r * rows) % period
                lanes = slice(s * 128, (s + 1) * 128)
                acc = cb[:, lanes]
                for shift, widx, dx in taps:
                    xs = pad_ref[s, base + r * rows + shift:base + (r + 1) * rows + shift, :]
                    if dx < 0:
                        xs = jnp.where(pos >= -dx, xs, 0.0)
                    elif dx > 0:
                        xs = jnp.where(pos < period - dx, xs, 0.0)
                    acc = acc + xs * w9[widx:widx + 1, lanes]
                act = _silu(acc) * hu_ref[r * rows:(r + 1) * rows, lanes]
                nxt_ref[r * rows:(r + 1) * rows, lanes] = act.astype(BF16)

    nk_static = D_FF // tk
    bufs = (even_ref, odd_ref)

    @pl.when(k == 0)
    def _():
        step(None, even_ref)

    @pl.when(jnp.logical_and(jnp.logical_and(k > 0, k < nk), k % 2 == 0))
    def _():
        step(odd_ref, even_ref)

    @pl.when(jnp.logical_and(k < nk, k % 2 == 1))
    def _():
        step(even_ref, odd_ref)

    @pl.when(k == nk)
    def _():
        step(bufs[(nk_static - 1) % 2], None)

        def slab(si, _):
            r0 = pl.multiple_of(si * NORM_SLAB, NORM_SLAB)
            res = x_ref[pl.ds(r0, NORM_SLAB), :] + g_ref[0] * o_ref[pl.ds(r0, NORM_SLAB), :]
            if final_norm:
                res = res * lax.rsqrt(jnp.mean(res * res, axis=-1, keepdims=True) + EPS) * nf_ref[...]
            o_ref[pl.ds(r0, NORM_SLAB), :] = res
            return 0

        lax.fori_loop(0, tm // NORM_SLAB, slab, 0)


def _ffn_down(cfg, h, w9, bias, wd, x, gate, nseq, t, row0, on_grid, prev=None, norm_gain=None):
    m, d = x.shape
    tm, tk = FD_TM, FD_TK
    rows = nseq * t
    nk = D_FF // tk
    assert D_FF % tk == 0 and row0 % tm == 0 and rows % tm == 0
    assert (t % tm == 0 and tm % GRID_W == 0) if on_grid else tm % t == 0
    final_norm = norm_gain is not None
    aliased = prev is not None
    assert not (final_norm and aliased)
    rt0 = row0 // tm
    kc = lambda k: jnp.minimum(k, nk - 1)
    hpb = tm // FD_HALO
    in_specs = [pl.BlockSpec((tm, tk), lambda i, k: (rt0 + i, kc(k))),
                pl.BlockSpec((tm, tk), lambda i, k: (rt0 + i, nk + kc(k)))]
    args = [h, h]
    if on_grid:
        in_specs += [pl.BlockSpec((FD_HALO, tk), lambda i, k: (jnp.maximum((rt0 + i) * hpb - 1, 0), kc(k))),
                     pl.BlockSpec((FD_HALO, tk), lambda i, k: (jnp.minimum((rt0 + i + 1) * hpb, m // FD_HALO - 1), kc(k)))]
        args += [h, h]
    w9c = jnp.transpose(w9.reshape(FFN_CONV * FFN_CONV, nk, tk), (1, 0, 2))
    in_specs += [pl.BlockSpec((1, FFN_CONV * FFN_CONV, tk), lambda i, k: (kc(k), 0, 0)),
                 pl.BlockSpec((1, 1, tk), lambda i, k: (kc(k), 0, 0)),
                 pl.BlockSpec((D_FF, d), lambda i, k: (0, 0), pipeline_mode=pl.Buffered(1)),
                 pl.BlockSpec((tm, d), lambda i, k: (rt0 + i, 0)),
                 pl.BlockSpec((1, 1, d), lambda i, k: (_mod_row(cfg, (rt0 + i) * tm), 0, 0))]
    args += [w9c, bias.reshape(nk, 1, tk), wd, x, gate]
    if final_norm:
        in_specs.append(pl.BlockSpec((1, d), lambda i, k: (0, 0)))
        args.append(norm_gain)
    if aliased:
        in_specs = [pl.BlockSpec(memory_space=pl.ANY)] + in_specs
        args = [prev] + args
    out_rows, out_rt0 = (rows, 0) if final_norm else (m, rt0)
    return pl.pallas_call(
        functools.partial(_ffn_down_kernel, cfg=cfg, on_grid=on_grid, seq_len=t, final_norm=final_norm, aliased=aliased),
        grid=(rows // tm, nk + 1),
        in_specs=in_specs,
        out_specs=pl.BlockSpec((tm, d), lambda i, k: (out_rt0 + i, 0)),
        out_shape=jax.ShapeDtypeStruct((out_rows, d), F32),
        scratch_shapes=[pltpu.VMEM((tk // 128, tm + 2 * (FD_HALO + FD_TOP), 128), F32),
                        pltpu.VMEM((tm, tk), BF16),
                        pltpu.VMEM((tm, tk), BF16)],
        input_output_aliases={0: 0} if aliased else {},
        compiler_params=pltpu.CompilerParams(dimension_semantics=("parallel", "arbitrary"), vmem_limit_bytes=FD_VMEM),
        name="ffn_down",
    )(*args)


def _pack_w_in(w):
    s = _SRC
    cols = [w[:, s["mg"]:s["end"]], w[:, s["va"]:s["ga"]], w[:, s["ga"]:s["ra"]], w[:, s["qb"]:s["kb"]],
            w[:, s["kb"]:s["vb"]], w[:, s["vb"]:s["ob"]], w[:, s["ob"]:s["gb"]], w[:, s["xr"]:s["yr"]],
            w[:, s["yr"]:s["mg"]], w[:, s["qa"]:s["ka"]], w[:, s["ka"]:s["va"]], w[:, s["ra"]:s["qb"]],
            w[:, s["gb"]:s["xr"]]]
    used = sum(c.shape[1] for c in cols)
    cols.append(jnp.zeros((w.shape[0], NZ - used), w.dtype))
    return jnp.concatenate(cols, axis=1).astype(BF16)


def _pack_gate_w(gw, gb, lam):
    nb, blk = LRU_BLOCKS, LRU_BLOCK
    gw_p = jnp.transpose(gw, (2, 3, 0, 1, 4)).reshape(nb, blk, 4 * blk).astype(BF16)
    gb_p = jnp.transpose(gb.reshape(2, 2, nb, blk), (2, 0, 1, 3)).reshape(nb, 1, 4 * blk)
    lam_p = jnp.transpose(lam.reshape(2, nb, blk), (1, 0, 2)).reshape(nb, 1, 2 * blk)
    return gw_p, gb_p, lam_p


def _layer(cfg, x, mod, lp, states, final_gain=None):
    (n1, n2, w_in, gwa, gba, gng, mbif, mng, lcw, lcb, lgw, lgb, llam, wbr, bmg, wout, fup, fcw, fcb, fdown) = lp
    s_gla, s_mc, s_mn, s_mm, s_lru = states
    m = cfg.m
    sh1, sc1, g1, sh2, sc2, g2 = (mod[:, k * D_MODEL:(k + 1) * D_MODEL].reshape(MOD_ROWS, 1, D_MODEL) for k in range(6))

    proj_tm = PROJ_TM if (cfg.m_ctx % PROJ_TM == 0 and cfg.tl % PROJ_TM == 0) else 512
    z = _norm_matmul(cfg, x, n1.reshape(1, -1), sc1, sh1, _pack_w_in(w_in), proj_tm, 512)

    s0t = jnp.swapaxes(s_gla, -1, -2)
    o_f, sa_f = _gla(cfg, z, gwa[0], gba[0].reshape(1, -1), s0t[:, 0], False)
    y_a, sa_r = _gla(cfg, z, gwa[1], gba[1].reshape(1, -1), s0t[:, 1], True, o_f, gng.reshape(1, -1))
    new_gla =jnp.swapaxes(jnp.stack([sa_f, sa_r], axis=1), -1, -2)

    g_rows = z[:, Z_SM + SM_GB:Z_SM + SM_GB + 4 * MLSTM_HEADS].T
    k_t = z[:, Z_KB:Z_KB + MLSTM_HEADS * MLSTM_DH].T
    b_col = mbif.reshape(1, -1)
    b_row = mbif.reshape(-1, 1)
    n0 = s_mn[:, :, :, None, :]
    m0 = jnp.broadcast_to(s_mm[:, :, :, None, None], s_mm.shape + (1, 128))
    h_f, cb_f, nb_f, mb_f = _mlstm(cfg, z, k_t, g_rows, b_col, b_row, s_mc[:, 0], n0[:, 0], m0[:, 0], False)
    y_b, cb_r, nb_r, mb_r = _mlstm(cfg, z, k_t, g_rows, b_col, b_row, s_mc[:, 1], n0[:, 1], m0[:, 1], True,
                                   h_f, mng.reshape(1, -1))
    new_mc =jnp.stack([cb_f, cb_r], axis=1)
    new_mn = jnp.stack([nb_f[:, :, 0], nb_r[:, :, 0]], axis=1)
    new_mm = jnp.stack([mb_f[:, :, 0, 0], mb_r[:, :, 0, 0]], axis=1)

    gw_p, gb_p, lam_p = _pack_gate_w(lgw, lgb, llam)
    cbias = lcb.reshape(1, -1)
    y_c, new_lru = _rglru(cfg, z, lcw, cbias, gw_p, gb_p, lam_p, jnp.zeros((cfg.bc, N_DIR, LRU_WIDTH), F32),
                          cfg.bc, cfg.tc, 0)
    y_c, _ = _rglru(cfg, z, lcw, cbias, gw_p, gb_p, lam_p, s_lru, cfg.bl, cfg.tl, cfg.m_ctx, y_prev=y_c)

    merged = _merge(cfg, y_a, y_b, y_c, wbr.astype(BF16), z, bmg.reshape(N_BRANCH, 1, D_MODEL), 512, 1024)
    x = _matmul_res(cfg, merged, wout.astype(BF16), x, g1, 512, 1024)

    h = _norm_matmul(cfg, (x,), n2.reshape(1, -1), sc2, sh2, fup.astype(BF16), proj_tm, 512)
    w9 = fcw.reshape(FFN_CONV * FFN_CONV, D_FF)
    wd = fdown.astype(BF16)
    if final_gain is None:
        x_ctx = _ffn_down(cfg, h, w9, fcb, wd, x, g2, cfg.bc, cfg.tc, 0, False)
        x = _ffn_down(cfg, h, w9, fcb, wd, x, g2, cfg.bl, cfg.tl, cfg.m_ctx, True, prev=x_ctx)
    else:
        x = (_ffn_down(cfg, h, w9, fcb, wd, x, g2, cfg.bc, cfg.tc, 0, False, norm_gain=final_gain),
             _ffn_down(cfg, h, w9, fcb, wd, x, g2, cfg.bl, cfg.tl, cfg.m_ctx, True, norm_gain=final_gain))
    return x, (new_gla, new_mc, new_mn, new_mm, new_lru)


def kernel(x_prompt, x_sample, state_gla, state_mlstm_c, state_mlstm_n, state_mlstm_m, state_rglru, c, c_ctx, norm1_g, norm2_g, w_mod, b_mod, w_in, gla_w_alpha, gla_b_alpha, gla_norm_g, mlstm_b_if, mlstm_norm_g, lru_conv_w, lru_conv_b, lru_gate_w, lru_gate_b, lru_lambda, w_branch, b_merge, w_out, ffn_w_up, ffn_conv_w, ffn_conv_b, ffn_w_down, norm_f_g):
    bc, tc, d = x_prompt.shape
    bl, tl, _ = x_sample.shape
    cfg = Cfg(bc, tc, bl, tl)
    assert tc % TB == 0 and tl % TB == 0 and cfg.m_ctx % tl == 0 and 1 + bl <= MOD_ROWS
    assert bc % NSLOT == 0 and bl % NSLOT == 0
    depth = w_in.shape[0]

    x = (x_prompt.reshape(bc * tc, d), x_sample.reshape(bl * tl, d))
    c_all = jnp.concatenate([c_ctx[None, :], c, jnp.zeros((MOD_ROWS - 1 - bl, d), F32)], axis=0)
    mod = _modulation(c_all, w_mod, b_mod)

    new = []
    for l in range(depth):
        lp = (norm1_g[l], norm2_g[l], w_in[l], gla_w_alpha[l], gla_b_alpha[l], gla_norm_g[l], mlstm_b_if[l],
              mlstm_norm_g[l], lru_conv_w[l], lru_conv_b[l], lru_gate_w[l], lru_gate_b[l], lru_lambda[l],
              w_branch[l], b_merge[l], w_out[l], ffn_w_up[l], ffn_conv_w[l], ffn_conv_b[l], ffn_w_down[l])
        states = (state_gla[:, l], state_mlstm_c[:, l], state_mlstm_n[:, l], state_mlstm_m[:, l], state_rglru[:, l])
        x, st = _layer(cfg, x, mod[l], lp, states, norm_f_g.reshape(1, -1) if l == depth - 1 else None)
        x = x if l == depth - 1 else (x,)
        new.append(st)

    y_prompt = x[0].reshape(bc, tc, d)
    y_sample = x[1].reshape(bl, tl, d)
    stacked = tuple(jnp.stack([new[l][k] for l in range(depth)], axis=1) for k in range(5))
    return (y_prompt, y_sample) + stacked
```

```python
import functools
from typing import NamedTuple

import numpy as np
import jax
import jax.numpy as jnp
from jax import lax
from jax.experimental import pallas as pl
from jax.experimental.pallas import tpu as pltpu

F32 = jnp.float32
BF16 = jnp.bfloat16

D_MODEL = 2048
DEPTH = 2
GRID_W = 64
N_DIR = 2
N_BRANCH = 3
BRANCH_W = 1024
GLA_HEADS = 4
GLA_DK = 128
GLA_DV = 256
GLA_RANK = 16
GLA_TAU = 16.0
MLSTM_HEADS = 4
MLSTM_DH = 256
CHUNK = 64
LRU_WIDTH = 1024
LRU_BLOCKS = 8
LRU_BLOCK = 128
LRU_CONV = 4
LRU_C = 8.0
D_FF = 5632
FFN_CONV = 3
EPS = 1e-6

_SRC = dict(qa=0, ka=512, va=1024, ga=2048, ra=3072, qb=3104, kb=4128, vb=5152, ob=6176, gb=7200,
            xr=7216, yr=8240, mg=9264, end=15408)
Z_MG, Z_VA, Z_GA, Z_QB, Z_KB, Z_VB, Z_OB, Z_XR, Z_YR, Z_QA, Z_KA, Z_SM = (
    0, 6144, 7168, 8192, 9216, 10240, 11264, 12288, 13312, 14336, 14848, 15360)
SM_W = 128
SM_RA, SM_GB = 0, 32
NZ = 15872
MOD_ROWS = 16

PROJ_TM = 1024
NORM_SLAB = 32
PROJ_SLAB = 64
TB = 256
NCH = TB // CHUNK
VMEM_LIMIT = 48 * 1024 * 1024
PROJ_SPLIT_VMEM = 54 * 1024 * 1024


class Cfg(NamedTuple):
    bc: int
    tc: int
    bl: int
    tl: int

    @property
    def m_ctx(self):
        return self.bc * self.tc

    @property
    def m(self):
        return self.bc * self.tc + self.bl * self.tl


def _params(sem):
    return pltpu.CompilerParams(dimension_semantics=sem, vmem_limit_bytes=VMEM_LIMIT)


def _softplus(x):
    return jnp.maximum(x, 0.0) + jnp.log1p(jnp.exp(-jnp.abs(x)))


def _log_sigmoid(x):
    return -_softplus(-x)


def _silu(x):
    return x * jax.nn.sigmoid(x)


def _gelu_tanh(x):
    return 0.5 * x * (1.0 + jnp.tanh(np.sqrt(2.0 / np.pi).astype(np.float32) * (x + 0.044715 * (x * x * x))))


def _mod_row(cfg, row):
    return jnp.where(row < cfg.m_ctx, 0, 1 + jnp.maximum(row - cfg.m_ctx, 0) // cfg.tl)


def _mod_kernel(c_ref, w_ref, b_ref, o_ref):
    c = c_ref[...]
    a = _silu(c).astype(BF16)
    o_ref[0] = jnp.dot(a, w_ref[0].astype(BF16), preferred_element_type=F32) + b_ref[0]


def _modulation(c_all, w_mod, b_mod):
    depth, d, n = w_mod.shape
    tn = 512
    return pl.pallas_call(
        _mod_kernel,
        grid=(depth, n // tn),
        in_specs=[pl.BlockSpec((MOD_ROWS, d), lambda l, j: (0, 0)),
                  pl.BlockSpec((1, d, tn), lambda l, j: (l, 0, j)),
                  pl.BlockSpec((1, 1, tn), lambda l, j: (l, 0, j))],
        out_specs=pl.BlockSpec((1, MOD_ROWS, tn), lambda l, j: (l, 0, j)),
        out_shape=jax.ShapeDtypeStruct((depth, MOD_ROWS, n), F32),
        compiler_params=_params(("parallel", "parallel")),
        name="modulation",
    )(c_all, w_mod, b_mod.reshape(depth, 1, n))


def _norm_rows(x_ref, g_ref, sc_ref, sh_ref, u_ref, r0, nrows):
    x = x_ref[pl.ds(r0, nrows), :]
    y = x * lax.rsqrt(jnp.mean(x * x, axis=-1, keepdims=True) + EPS) * g_ref[...]
    u_ref[pl.ds(r0, nrows), :] = (y * (1.0 + sc_ref[0]) + sh_ref[0]).astype(BF16)


def _norm_matmul_kernel(*refs, slab, starts):
    nsrc = len(starts) - 1
    x0_ref = refs[0]
    xn_refs = refs[1:1 + nsrc]
    g_ref, sc0_ref, sh0_ref, scn_ref, shn_ref, w_ref, o_ref, ua_ref, ub_ref = refs[1 + nsrc:]
    i, j = pl.program_id(0), pl.program_id(1)
    tm = x0_ref.shape[0]
    nxt = jnp.minimum(i + 1, starts[-1] - 1)

    @pl.when(jnp.logical_and(i == 0, j == 0))
    def _():
        def first(si, _):
            _norm_rows(x0_ref, g_ref, sc0_ref, sh0_ref, ua_ref, pl.multiple_of(si * NORM_SLAB, NORM_SLAB), NORM_SLAB)
            return 0

        lax.fori_loop(0, tm // NORM_SLAB, first, 0)

    r0 = pl.multiple_of(jnp.minimum(j, tm // slab - 1) * slab, slab)
    for s, xn_ref in enumerate(xn_refs):
        from_s = jnp.logical_and(nxt >= starts[s], nxt < starts[s + 1])

        @pl.when(jnp.logical_and(from_s, i % 2 == 0))
        def _():
            _norm_rows(xn_ref, g_ref, scn_ref, shn_ref, ub_ref, r0, slab)
            o_ref[...] = jnp.dot(ua_ref[...], w_ref[...], preferred_element_type=F32)

        @pl.when(jnp.logical_and(from_s, i % 2 == 1))
        def _():
            _norm_rows(xn_ref, g_ref, scn_ref, shn_ref, ua_ref, r0, slab)
            o_ref[...] = jnp.dot(ub_ref[...], w_ref[...], preferred_element_type=F32)


def _norm_matmul(cfg, xs, gain, sc, sh, w, tm, tn):
    d = xs[0].shape[1]
    n = w.shape[1]
    slab = PROJ_SLAB
    while n // tn < tm // slab:
        slab *= 2
    assert tm % slab == 0 and n // tn >= tm // slab and all(x.shape[0] % tm == 0 for x in xs)
    starts = [0]
    for x in xs:
        starts.append(starts[-1] + x.shape[0] // tm)
    ntiles = starts[-1]
    nxt = lambda i: jnp.minimum(i + 1, ntiles - 1)
    mod0 = lambda i, j: (_mod_row(cfg, 0), 0, 0)
    modn = lambda i, j: (_mod_row(cfg, nxt(i) * tm), 0, 0)
    src_specs = [pl.BlockSpec((tm, d), lambda i, j, s=s: (jnp.clip(nxt(i) - starts[s], 0, starts[s + 1] - starts[s] - 1), 0),
                              pipeline_mode=pl.Buffered(1) if len(xs) > 1 and s == 0 else None)
                 for s in range(len(xs))]
    return pl.pallas_call(
        functools.partial(_norm_matmul_kernel, slab=slab, starts=tuple(starts)),
        grid=(ntiles, n // tn),
        in_specs=[pl.BlockSpec((tm, d), lambda i, j: (0, 0), pipeline_mode=pl.Buffered(1))] + src_specs
                 + [pl.BlockSpec((1, d), lambda i, j: (0, 0)),
                    pl.BlockSpec((1, 1, d), mod0),
                    pl.BlockSpec((1, 1, d), mod0),
                    pl.BlockSpec((1, 1, d), modn),
                    pl.BlockSpec((1, 1, d), modn),
                    pl.BlockSpec((d, tn), lambda i, j: (0, j))],
        out_specs=pl.BlockSpec((tm, tn), lambda i, j: (i, j)),
        out_shape=jax.ShapeDtypeStruct((ntiles * tm, n), F32),
        scratch_shapes=[pltpu.VMEM((tm, d), BF16), pltpu.VMEM((tm, d), BF16)],
        compiler_params=pltpu.CompilerParams(dimension_semantics=("arbitrary", "arbitrary"),
                                             vmem_limit_bytes=VMEM_LIMIT if len(xs) == 1 and tn <= 512 else PROJ_SPLIT_VMEM),
        name="norm_matmul",
    )(xs[0], *xs, gain, sc, sh, sc, sh, w)


NSLOT = 2


def _pair_pos(cfg, i):
    kc, kl = cfg.tc // TB, cfg.tl // TB
    nc = (cfg.bc // NSLOT) * kc
    is_ctx = i < nc
    il = jnp.maximum(i - nc, 0)
    j = jnp.where(is_ctx, i % kc, il % kl)
    k = jnp.where(is_ctx, kc, kl)
    s = jnp.where(is_ctx, i // kc, il // kl)
    return is_ctx, s, j, k


def _pair_row_block(cfg, reverse, slot, i):
    kc, kl = cfg.tc // TB, cfg.tl // TB
    is_ctx, s, j, k = _pair_pos(cfg, i)
    jj = (k - 1 - j) if reverse else j
    return jnp.where(is_ctx, (s + slot * (cfg.bc // NSLOT)) * kc + jj,
                     cfg.bc * kc + (s + slot * (cfg.bl // NSLOT)) * kl + jj)


def _pair_local_block(cfg, reverse, i):
    kc, kl = cfg.tc // TB, cfg.tl // TB
    is_ctx, s, j, k = _pair_pos(cfg, i)
    jj = (k - 1 - j) if reverse else j
    return jnp.where(is_ctx, s * kc + jj, (cfg.bc // NSLOT) * kc + s * kl + jj)


def _pair_lat_seq(cfg, i):
    is_ctx, s, _, _ = _pair_pos(cfg, i)
    return jnp.where(is_ctx, 0, jnp.minimum(s, cfg.bl // NSLOT - 1))


def _pair_ctx_seq(cfg, i):
    is_ctx, s, _, _ = _pair_pos(cfg, i)
    return jnp.where(is_ctx, s, cfg.bc // NSLOT - 1)


def _slot_of_tile(cfg, tm, i):
    nc, nl = cfg.m_ctx // tm // NSLOT, (cfg.m - cfg.m_ctx) // tm // NSLOT
    il = jnp.maximum(i - NSLOT * nc, 0)
    is_ctx = i < NSLOT * nc
    return jnp.where(is_ctx, i // nc, il // nl), jnp.where(is_ctx, i % nc, nc + il % nl)


def _tri(reverse, n=CHUNK):
    row = lax.broadcasted_iota(jnp.int32, (n, n), 0)
    col = lax.broadcasted_iota(jnp.int32, (n, n), 1)
    return (row <= col) if reverse else (row >= col)


def _head_norm(x, g):
    return x * lax.rsqrt(jnp.mean(x * x, axis=-1, keepdims=True) + EPS) * g


_NT = (((1,), (1,)), ((), ()))
_TN = (((0,), (0,)), ((), ()))


GLA_LOCKSTEP = 2


def _gla_kernel(*refs, cfg, reverse):
    it = iter(refs)
    blocks = [[next(it) for _ in range(4)] for _ in range(NSLOT)]
    wal_ref, bal_ref = next(it), next(it)
    init_ref = next(it)
    init = [init_ref.at[b] for b in range(NSLOT)]
    if reverse:
        gates = [next(it) for _ in range(NSLOT)]
        of_ref, gn_ref = next(it), next(it)
        epi = [[gates[b], of_ref.at[b]] for b in range(NSLOT)]
    out_ref, fin_ref = next(it), next(it)
    outs = [out_ref.at[b] for b in range(NSLOT)]
    fin = [fin_ref.at[b] for b in range(NSLOT)]
    st_ref, la_ref = next(it), next(it)
    is_ctx, _, j, k = _pair_pos(cfg, pl.program_id(0))
    nh = GLA_HEADS

    @pl.when(jnp.logical_and(j == 0, is_ctx))
    def _():
        st_ref[...] = jnp.zeros_like(st_ref)

    @pl.when(jnp.logical_and(j == 0, jnp.logical_not(is_ctx)))
    def _():
        for b in range(NSLOT):
            st_ref[b * nh:(b + 1) * nh] = init[b][0]

    d = 1 if reverse else 0
    slots = range(NSLOT)
    both = lambda f: [f(b) for b in slots]
    wal = wal_ref[...].astype(BF16)
    for b in slots:
        ra = blocks[b][3][:, SM_RA + d * GLA_RANK:SM_RA + (d + 1) * GLA_RANK].astype(BF16)
        pre = jnp.dot(ra, wal, preferred_element_type=F32) + bal_ref[...]
        la_ref[b] = _log_sigmoid(pre) * (1.0 / GLA_TAU)

    tri = _tri(reverse)
    tri_f = tri.astype(F32)
    last = 0 if reverse else CHUNK - 1
    qscale = GLA_DK ** -0.5
    for c in (range(NCH - 1, -1, -1) if reverse else range(NCH)):
        rows = slice(c * CHUNK, (c + 1) * CHUNK)
        cum = both(lambda b: jnp.dot(tri_f, la_ref[b, rows, :], precision=lax.Precision.HIGHEST,
                                     preferred_element_type=F32))
        tot = both(lambda b: cum[b][last:last + 1, :])
        e_q = both(lambda b: jnp.exp(cum[b]))
        e_k = both(lambda b: jnp.exp(-cum[b]))
        e_end = both(lambda b: jnp.exp(tot[b] - cum[b]))
        e_tot = both(lambda b: jnp.exp(tot[b]))
        for h0 in range(0, GLA_HEADS, GLA_LOCKSTEP):
            units = [(b, h0 + dh) for dh in range(GLA_LOCKSTEP) for b in slots]
            each = lambda f: [f(u, b, h) for u, (b, h) in enumerate(units)]
            kc = [slice(h * GLA_DK, (h + 1) * GLA_DK) for _, h in units]
            vc = [slice(h * GLA_DV, (h + 1) * GLA_DV) for _, h in units]
            sh = [b * nh + h for b, h in units]
            qh = each(lambda u, b, h: blocks[b][0][rows, kc[u]] * qscale)
            kh = each(lambda u, b, h: blocks[b][1][rows, kc[u]])
            vh = each(lambda u, b, h: blocks[b][2][rows, vc[u]].astype(BF16))
            q_in = each(lambda u, b, h: (qh[u] * e_q[b][:, kc[u]]).astype(BF16))
            k_in = each(lambda u, b, h: (kh[u] * e_k[b][:, kc[u]]).astype(BF16))
            att = each(lambda u, b, h: lax.dot_general(q_in[u], k_in[u], _NT, preferred_element_type=F32))
            att = each(lambda u, b, h: jnp.where(tri, att[u], 0.0).astype(BF16))
            st = each(lambda u, b, h: st_ref[sh[u]])
            o = each(lambda u, b, h: jnp.dot(att[u], vh[u], preferred_element_type=F32)
                     + lax.dot_general(q_in[u], st[u].astype(BF16), _NT, preferred_element_type=F32))
            k_end = each(lambda u, b, h: (kh[u] * e_end[b][:, kc[u]]).astype(BF16))
            upd = each(lambda u, b, h: lax.dot_general(vh[u], k_end[u], _TN, preferred_element_type=F32))
            for u, (b, h) in enumerate(units):
                st_ref[sh[u]] = e_tot[b][:, kc[u]] * st[u] + upd[u]
            for u, (b, h) in enumerate(units):
                if reverse:
                    ga_ref, of_ref = epi[b]
                    y = _head_norm(of_ref[rows, vc[u]] + o[u], gn_ref[:, vc[u]]) * _silu(ga_ref[rows, vc[u]])
                    outs[b][rows, vc[u]] = y.astype(outs[b].dtype)
                else:
                    outs[b][rows, vc[u]] = o[u]

    @pl.when(jnp.logical_and(is_ctx, j == k - 1))
    def _():
        for b in range(NSLOT):
            fin[b][0] = st_ref[b * nh:(b + 1) * nh]


def _gla(cfg, z, wal, bal, s0t, reverse, o_fwd=None, gnorm=None):
    m = cfg.m
    nh = GLA_HEADS
    hk, hv = nh * GLA_DK, nh * GLA_DV
    lb = functools.partial(_pair_local_block, cfg, reverse)
    in_specs, args = [], []
    for b in range(NSLOT):
        rb = functools.partial(_pair_row_block, cfg, reverse, b)
        in_specs += [pl.BlockSpec((TB, hk), lambda i, rb=rb: (rb(i), Z_QA // hk)),
                     pl.BlockSpec((TB, hk), lambda i, rb=rb: (rb(i), Z_KA // hk)),
                     pl.BlockSpec((TB, hv), lambda i, rb=rb: (rb(i), Z_VA // hv)),
                     pl.BlockSpec((TB, SM_W), lambda i, rb=rb: (rb(i), Z_SM // SM_W))]
        args += [z, z, z, z]
    in_specs += [pl.BlockSpec((GLA_RANK, hk), lambda i: (0, 0)), pl.BlockSpec((1, hk), lambda i: (0, 0))]
    args += [wal, bal]
    half_c, half_l = cfg.bc // NSLOT, cfg.bl // NSLOT
    in_specs.append(pl.BlockSpec((NSLOT, 1, nh, GLA_DV, GLA_DK), lambda i: (0, _pair_lat_seq(cfg, i), 0, 0, 0)))
    args.append(s0t.reshape(NSLOT, half_l, nh, GLA_DV, GLA_DK))
    if reverse:
        for b in range(NSLOT):
            rb = functools.partial(_pair_row_block, cfg, reverse, b)
            in_specs.append(pl.BlockSpec((TB, hv), lambda i, rb=rb: (rb(i), Z_GA // hv)))
            args.append(z)
        in_specs += [pl.BlockSpec((NSLOT, TB, hv), lambda i: (0, lb(i), 0)), pl.BlockSpec((1, hv), lambda i: (0, 0))]
        args += [o_fwd, gnorm]
    out, fin = pl.pallas_call(
        functools.partial(_gla_kernel, cfg=cfg, reverse=reverse),
        grid=(m // TB // NSLOT,),
        in_specs=in_specs,
        out_specs=[pl.BlockSpec((NSLOT, TB, hv), lambda i: (0, lb(i), 0)),
                   pl.BlockSpec((NSLOT, 1, nh, GLA_DV, GLA_DK), lambda i: (0, _pair_ctx_seq(cfg, i), 0, 0, 0))],
        out_shape=[jax.ShapeDtypeStruct((NSLOT, m // NSLOT, hv), BF16 if reverse else F32),
                   jax.ShapeDtypeStruct((NSLOT, half_c, nh, GLA_DV, GLA_DK), F32)],
        scratch_shapes=[pltpu.VMEM((NSLOT * nh, GLA_DV, GLA_DK), F32), pltpu.VMEM((NSLOT, TB, hk), F32)],
        compiler_params=_params(("arbitrary",)),
        name="gla_rev" if reverse else "gla_fwd",
    )(*args)
    return out, fin.reshape(cfg.bc, nh, GLA_DV, GLA_DK)


MCH = 128
NMC = TB // MCH
MLSTM_LOCKSTEP = 2


def _mlstm_kernel(*refs, cfg, reverse):
    it = iter(refs)
    blocks = [[next(it) for _ in range(6)] for _ in range(NSLOT)]
    bc_ref, br_ref = next(it), next(it)
    init_refs = [next(it) for _ in range(3)]
    init = [[r.at[b] for r in init_refs] for b in range(NSLOT)]
    if reverse:
        gates = [next(it) for _ in range(NSLOT)]
        hf_all, gn_ref = next(it), next(it)
        epi = [[gates[b], hf_all.at[b]] for b in range(NSLOT)]
    out_ref = next(it)
    outs = [out_ref.at[b] for b in range(NSLOT)]
    fin_refs = [next(it) for _ in range(3)]
    fin = [[r.at[b] for r in fin_refs] for b in range(NSLOT)]
    c_ref, n_ref, m_ref = next(it), next(it), next(it)
    is_ctx, _, j, k = _pair_pos(cfg, pl.program_id(0))
    nh = MLSTM_HEADS

    @pl.when(jnp.logical_and(j == 0, is_ctx))
    def _():
        c_ref[...] = jnp.zeros_like(c_ref)
        n_ref[...] = jnp.zeros_like(n_ref)
        m_ref[...] = jnp.zeros_like(m_ref)

    @pl.when(jnp.logical_and(j == 0, jnp.logical_not(is_ctx)))
    def _():
        for b in range(NSLOT):
            c_ref[b * nh:(b + 1) * nh] = init[b][0][0]
            n_ref[b * nh:(b + 1) * nh] = init[b][1][0]
            m_ref[b * nh:(b + 1) * nh] = init[b][2][0]

    d = 1 if reverse else 0
    tri = _tri(reverse, MCH)
    tri_f = tri.astype(F32)
    tri_tf = _tri(not reverse, MCH).astype(F32)
    last = 0 if reverse else MCH - 1
    ng = 2 * MLSTM_HEADS
    kscale = MLSTM_DH ** -0.5
    slots = range(NSLOT)
    both = lambda f: [f(b) for b in slots]
    hp = lax.Precision.HIGHEST
    for c in (range(NMC - 1, -1, -1) if reverse else range(NMC)):
        rows = slice(c * MCH, (c + 1) * MCH)
        g_col = both(lambda b: blocks[b][4][rows, SM_GB:SM_GB + 2 * ng] + bc_ref[...])
        g_row = both(lambda b: blocks[b][5][:, rows] + br_ref[...])
        fcum_col = both(lambda b: jnp.dot(tri_f, _log_sigmoid(g_col[b]), precision=hp, preferred_element_type=F32))
        fcum_row = both(lambda b: jnp.dot(_log_sigmoid(g_row[b]), tri_tf, precision=hp, preferred_element_type=F32))
        for h0 in range(0, MLSTM_HEADS, MLSTM_LOCKSTEP):
            units = [(b, h0 + dh) for dh in range(MLSTM_LOCKSTEP) for b in slots]
            each = lambda f: [f(u, b, h) for u, (b, h) in enumerate(units)]
            ii = [d * ng + h for _, h in units]
            fi = [d * ng + MLSTM_HEADS + h for _, h in units]
            hc = [slice(h * MLSTM_DH, (h + 1) * MLSTM_DH) for _, h in units]
            sh = [b * nh + h for b, h in units]
            f_col = each(lambda u, b, h: fcum_col[b][:, fi[u]:fi[u] + 1])
            f_row = each(lambda u, b, h: fcum_row[b][fi[u]:fi[u] + 1, :])
            i_col = each(lambda u, b, h: g_col[b][:, ii[u]:ii[u] + 1])
            i_row = each(lambda u, b, h: g_row[b][ii[u]:ii[u] + 1, :])
            m_prev = each(lambda u, b, h: m_ref[sh[u]][:, 0:1])
            dlog = each(lambda u, b, h: jnp.where(tri, f_col[u] - f_row[u] + i_row[u], -jnp.inf))
            prev = each(lambda u, b, h: f_col[u] + m_prev[u])
            mj = each(lambda u, b, h: jnp.maximum(prev[u], jnp.max(dlog[u], axis=-1, keepdims=True)))
            w = each(lambda u, b, h: jnp.exp(dlog[u] - mj[u]))
            wp = each(lambda u, b, h: jnp.exp(prev[u] - mj[u]))
            qh = each(lambda u, b, h: blocks[b][0][rows, hc[u]])
            kh = each(lambda u, b, h: blocks[b][1][rows, hc[u]] * kscale)
            vh = each(lambda u, b, h: blocks[b][2][rows, hc[u]].astype(BF16))
            s = each(lambda u, b, h: lax.dot_general(qh[u].astype(BF16), kh[u].astype(BF16), _NT,
                                                     preferred_element_type=F32) * w[u])
            qp = each(lambda u, b, h: qh[u] * wp[u])
            cm = each(lambda u, b, h: c_ref[sh[u]])
            nv = each(lambda u, b, h: n_ref[sh[u]])
            m_new = each(lambda u, b, h: mj[u][last:last + 1, :])
            tot = each(lambda u, b, h: f_col[u][last:last + 1, :])
            wl_col = each(lambda u, b, h: jnp.exp(tot[u] - f_col[u] + i_col[u] - m_new[u]))
            wl_row = each(lambda u, b, h: jnp.exp(tot[u] - f_row[u] + i_row[u] - m_new[u]))
            decay = each(lambda u, b, h: jnp.exp(tot[u] + m_prev[u] - m_new[u]))
            kw_t = each(lambda u, b, h: (blocks[b][3][hc[u], rows] * kscale) * wl_row[u])
            sv = each(lambda u, b, h: jnp.dot(jnp.concatenate([s[u].astype(BF16), kw_t[u].astype(BF16)], axis=0),
                                              vh[u], preferred_element_type=F32))
            num = each(lambda u, b, h: sv[u][:MCH]
                       + jnp.dot(qp[u].astype(BF16), cm[u].astype(BF16), preferred_element_type=F32))
            upd = each(lambda u, b, h: sv[u][MCH:])
            den = each(lambda u, b, h: jnp.sum(s[u], axis=-1, keepdims=True)
                       + jnp.sum(qp[u] * nv[u], axis=-1, keepdims=True))
            hh = each(lambda u, b, h: num[u] / jnp.maximum(jnp.abs(den[u]), jnp.exp(-mj[u])))
            for u, (b, h) in enumerate(units):
                c_ref[sh[u]] = decay[u] * cm[u] + upd[u]
                n_ref[sh[u]] = decay[u] * nv[u] + jnp.sum(kh[u] * wl_col[u], axis=0, keepdims=True)
                m_ref[sh[u]] = jnp.broadcast_to(m_new[u], (1, 128))
            for u, (b, h) in enumerate(units):
                if reverse:
                    ob_ref, hf_ref = epi[b]
                    y = jax.nn.sigmoid(ob_ref[rows, hc[u]]) * _head_norm(hf_ref[rows, hc[u]] + hh[u], gn_ref[:, hc[u]])
                    outs[b][rows, hc[u]] = y.astype(outs[b].dtype)
                else:
                    outs[b][rows, hc[u]] = hh[u]

    @pl.when(jnp.logical_and(is_ctx, j == k - 1))
    def _():
        for b in range(NSLOT):
            fin[b][0][0] = c_ref[b * nh:(b + 1) * nh]
            fin[b][1][0] = n_ref[b * nh:(b + 1) * nh]
            fin[b][2][0] = m_ref[b * nh:(b + 1) * nh]


def _mlstm(cfg, z, k_t, g_rows, b_col, b_row, c0, n0, m0, reverse, h_fwd=None, gnorm=None):
    m = cfg.m
    hd = MLSTM_HEADS * MLSTM_DH
    nh = MLSTM_HEADS
    lb = functools.partial(_pair_local_block, cfg, reverse)
    ctx = lambda i: (0, _pair_ctx_seq(cfg, i), 0, 0, 0)
    in_specs, args = [], []
    for b in range(NSLOT):
        rb = functools.partial(_pair_row_block, cfg, reverse, b)
        in_specs += [pl.BlockSpec((TB, hd), lambda i, rb=rb: (rb(i), Z_QB // hd)),
                     pl.BlockSpec((TB, hd), lambda i, rb=rb: (rb(i), Z_KB // hd)),
                     pl.BlockSpec((TB, hd), lambda i, rb=rb: (rb(i), Z_VB // hd)),
                     pl.BlockSpec((hd, TB), lambda i, rb=rb: (0, rb(i))),
                     pl.BlockSpec((TB, SM_W), lambda i, rb=rb: (rb(i), Z_SM // SM_W)),
                     pl.BlockSpec((4 * nh, TB), lambda i, rb=rb: (0, rb(i)))]
        args += [z, z, z, k_t, z, g_rows]
    in_specs += [pl.BlockSpec((1, 4 * nh), lambda i: (0, 0)), pl.BlockSpec((4 * nh, 1), lambda i: (0, 0))]
    args += [b_col, b_row]
    half_c, half_l = cfg.bc // NSLOT, cfg.bl // NSLOT
    lat = lambda i: (0, _pair_lat_seq(cfg, i), 0, 0, 0)
    state_dims = ((nh, MLSTM_DH, MLSTM_DH), (nh, 1, MLSTM_DH), (nh, 1, 128))
    in_specs += [pl.BlockSpec((NSLOT, 1) + sd, lat) for sd in state_dims]
    args += [s.reshape((NSLOT, half_l) + sd) for s, sd in zip((c0, n0, m0), state_dims)]
    if reverse:
        for b in range(NSLOT):
            rb = functools.partial(_pair_row_block, cfg, reverse, b)
            in_specs.append(pl.BlockSpec((TB, hd), lambda i, rb=rb: (rb(i), Z_OB // hd)))
            args.append(z)
        in_specs += [pl.BlockSpec((NSLOT, TB, hd), lambda i: (0, lb(i), 0)), pl.BlockSpec((1, hd), lambda i: (0, 0))]
        args += [h_fwd, gnorm]
    res = pl.pallas_call(
        functools.partial(_mlstm_kernel, cfg=cfg, reverse=reverse),
        grid=(m // TB // NSLOT,),
        in_specs=in_specs,
        out_specs=[pl.BlockSpec((NSLOT, TB, hd), lambda i: (0, lb(i), 0))]
                  + [pl.BlockSpec((NSLOT, 1) + sd, ctx) for sd in state_dims],
        out_shape=[jax.ShapeDtypeStruct((NSLOT, m // NSLOT, hd), BF16 if reverse else F32)]
                  + [jax.ShapeDtypeStruct((NSLOT, half_c) + sd, F32) for sd in state_dims],
        scratch_shapes=[pltpu.VMEM((NSLOT * nh, MLSTM_DH, MLSTM_DH), F32),
                        pltpu.VMEM((NSLOT * nh, 1, MLSTM_DH), F32),
                        pltpu.VMEM((NSLOT * nh, 1, 128), F32)],
        compiler_params=_params(("arbitrary",)),
        name="mlstm_rev" if reverse else "mlstm_fwd",
    )(*args)
    return (res[0],) + tuple(r.reshape((cfg.bc,) + sd) for r, sd in zip(res[1:], state_dims))


LRU_ROWS = 256
LRU_PAD = 8
LRU_SEG = 8


def _rglru_kernel(*refs, t, aliased):
    if aliased:
        refs = refs[1:]
    (xr_ref, yr_ref, cw_ref, cb_ref, gw_ref, gb_ref, lam_ref, h0_ref, y_ref, he_ref,
     pad_ref, af_ref, ar_ref, bf_ref, br_ref, hl_ref, pc_ref, hs_ref) = refs
    a_refs, b_refs = (af_ref, ar_ref), (bf_ref, br_ref)
    nsteps = t // LRU_ROWS
    seg = t // LRU_SEG
    pitch = seg + 8
    zeros = jnp.zeros((LRU_PAD, LRU_BLOCK), F32)
    pad_ref[0:LRU_PAD, :] = zeros
    pad_ref[LRU_PAD + t:2 * LRU_PAD + t, :] = zeros

    def pitched(r0):
        if seg >= LRU_ROWS:
            return [(0, LRU_ROWS, pl.multiple_of((r0 // seg) * pitch + r0 % seg, 8))]
        return [(q * seg, seg, pl.multiple_of((r0 // seg + q) * pitch, 8)) for q in range(LRU_ROWS // seg)]

    def fill(ci, _):
        r0 = pl.multiple_of(ci * LRU_ROWS, LRU_ROWS)
        pad_ref[pl.ds(r0 + LRU_PAD, LRU_ROWS), :] = xr_ref[pl.ds(r0, LRU_ROWS), :]
        return 0

    lax.fori_loop(0, nsteps, fill, 0)

    sp = _softplus(-lam_ref[0])
    left = LRU_CONV // 2

    def gates(ci, _):
        r0 = pl.multiple_of(ci * LRU_ROWS, LRU_ROWS)
        xc = cb_ref[...]
        for tap in range(LRU_CONV):
            xc = xc + cw_ref[tap:tap + 1, :] * pad_ref[pl.ds(r0 + LRU_PAD - left + tap, LRU_ROWS), :]
        pre = jnp.dot(xc.astype(BF16), gw_ref[0], preferred_element_type=F32) + gb_ref[0]
        both = lambda f: [f(d) for d in range(N_DIR)]
        r = both(lambda d: jax.nn.sigmoid(pre[:, d * 2 * LRU_BLOCK:d * 2 * LRU_BLOCK + LRU_BLOCK]))
        ig = both(lambda d: jax.nn.sigmoid(pre[:, d * 2 * LRU_BLOCK + LRU_BLOCK:(d + 1) * 2 * LRU_BLOCK]))
        log_a = both(lambda d: (-LRU_C * r[d]) * sp[:, d * LRU_BLOCK:(d + 1) * LRU_BLOCK])
        a = both(lambda d: jnp.exp(log_a[d]))
        b = both(lambda d: jnp.sqrt(1.0 - a[d] * a[d]) * (ig[d] * xc))
        for off, n, dst in pitched(r0):
            for d in range(N_DIR):
                a_refs[d][pl.ds(dst, n), :] = a[d][off:off + n]
                b_refs[d][pl.ds(dst, n), :] = b[d][off:off + n]
        return 0

    lax.fori_loop(0, nsteps, gates, 0)

    def local_scan(i, carry):
        hf, pf, hr, pr = carry
        rf, rr = i, seg - 1 - i
        a_f = af_ref[pl.ds(rf, LRU_SEG, stride=pitch), :]
        b_f = bf_ref[pl.ds(rf, LRU_SEG, stride=pitch), :]
        a_r = ar_ref[pl.ds(rr, LRU_SEG, stride=pitch), :]
        b_r = br_ref[pl.ds(rr, LRU_SEG, stride=pitch), :]
        hf, pf = a_f * hf + b_f, a_f * pf
        hr, pr = a_r * hr + b_r, a_r * pr
        hl_ref[0, rf] = hf
        pc_ref[0, rf] = pf
        hl_ref[1, rr] = hr
        pc_ref[1, rr] = pr
        return hf, pf, hr, pr

    zero = jnp.zeros((LRU_SEG, LRU_BLOCK), F32)
    one = jnp.ones((LRU_SEG, LRU_BLOCK), F32)
    hf, pf, hr, pr = lax.fori_loop(0, seg, local_scan, (zero, one, zero, one), unroll=8)

    enter_f = [h0_ref[0, 0:1, :]]
    for s in range(LRU_SEG - 1):
        enter_f.append(pf[s:s + 1, :] * enter_f[s] + hf[s:s + 1, :])
    he_ref[0, 0:1, :] = pf[LRU_SEG - 1:LRU_SEG, :] * enter_f[LRU_SEG - 1] + hf[LRU_SEG - 1:LRU_SEG, :]
    enter_r = [None] * LRU_SEG
    enter_r[LRU_SEG - 1] = h0_ref[0, 1:2, :]
    for s in range(LRU_SEG - 1, 0, -1):
        enter_r[s - 1] = pr[s:s + 1, :] * enter_r[s] + hr[s:s + 1, :]
    he_ref[0, 1:2, :] = pr[0:1, :] * enter_r[0] + hr[0:1, :]
    init_f = jnp.concatenate(enter_f, axis=0)
    init_r = jnp.concatenate(enter_r, axis=0)

    def fix_up(i, _):
        h = (hl_ref[0, i] + pc_ref[0, i] * init_f) + (hl_ref[1, i] + pc_ref[1, i] * init_r)
        hs_ref[pl.ds(i, LRU_SEG, stride=pitch), :] = h
        return 0

    lax.fori_loop(0, seg, fix_up, 0, unroll=8)

    def finish(ci, _):
        r0 = pl.multiple_of(ci * LRU_ROWS, LRU_ROWS)
        for off, n, src in pitched(r0):
            y = hs_ref[pl.ds(src, n), :] * _gelu_tanh(yr_ref[pl.ds(r0 + off, n), :])
            y_ref[pl.ds(r0 + off, n), :] = y.astype(y_ref.dtype)
        return 0

    lax.fori_loop(0, nsteps, finish, 0)


def _rglru(cfg, z, cw, cb, gw, gb, lam, h0, nseq, t, row0, y_prev=None):
    m = cfg.m
    seg = t // LRU_SEG
    assert row0 % t == 0 and t % LRU_ROWS == 0 and seg % 16 == 0 and (seg % LRU_ROWS == 0 or LRU_ROWS % seg == 0)
    pitched_rows = LRU_SEG * (seg + 8)
    sb = row0 // t
    aliased = y_prev is not None
    in_specs = [pl.BlockSpec((t, LRU_BLOCK), lambda b, n: (sb + b, Z_XR // LRU_BLOCK + n)),
                pl.BlockSpec((t, LRU_BLOCK), lambda b, n: (sb + b, Z_YR // LRU_BLOCK + n)),
                pl.BlockSpec((LRU_CONV, LRU_BLOCK), lambda b, n: (0, n)),
                pl.BlockSpec((1, LRU_BLOCK), lambda b, n: (0, n)),
                pl.BlockSpec((1, LRU_BLOCK, 4 * LRU_BLOCK), lambda b, n: (n, 0, 0)),
                pl.BlockSpec((1, 1, 4 * LRU_BLOCK), lambda b, n: (n, 0, 0)),
                pl.BlockSpec((1, 1, 2 * LRU_BLOCK), lambda b, n: (n, 0, 0)),
                pl.BlockSpec((1, N_DIR, LRU_BLOCK), lambda b, n: (b, 0, n))]
    args = [z, z, cw, cb, gw, gb, lam, h0]
    if aliased:
        in_specs = [pl.BlockSpec(memory_space=pl.ANY)] + in_specs
        args = [y_prev] + args
    return pl.pallas_call(
        functools.partial(_rglru_kernel, t=t, aliased=aliased),
        grid=(nseq, LRU_BLOCKS),
        in_specs=in_specs,
        out_specs=[pl.BlockSpec((t, LRU_BLOCK), lambda b, n: (sb + b, n)),
                   pl.BlockSpec((1, N_DIR, LRU_BLOCK), lambda b, n: (b, 0, n))],
        out_shape=[jax.ShapeDtypeStruct((m, LRU_WIDTH), BF16),
                   jax.ShapeDtypeStruct((nseq, N_DIR, LRU_WIDTH), F32)],
        scratch_shapes=[pltpu.VMEM((t + 2 * LRU_PAD, LRU_BLOCK), F32)]
                       + [pltpu.VMEM((pitched_rows, LRU_BLOCK), F32) for _ in range(2 * N_DIR)]
                       + [pltpu.VMEM((N_DIR, seg, LRU_SEG, LRU_BLOCK), F32),
                          pltpu.VMEM((N_DIR, seg, LRU_SEG, LRU_BLOCK), F32),
                          pltpu.VMEM((pitched_rows, LRU_BLOCK), F32)],
        input_output_aliases={0: 0} if aliased else {},
        compiler_params=_params(("parallel", "parallel")),
        name="rglru",
    )(*args)


def _merge_kernel(ya_ref, yb_ref, yc_ref, w_ref, ma_ref, mb_ref, mc_ref, bm_ref, o_ref):
    acc = None
    for n, (y_ref, mg_ref) in enumerate(((ya_ref, ma_ref), (yb_ref, mb_ref), (yc_ref, mc_ref))):
        g = jax.nn.sigmoid(mg_ref[...] + bm_ref[n])
        term = g * jnp.dot(y_ref[...], w_ref[n], preferred_element_type=F32)
        acc = term if acc is None else acc + term
    o_ref[...] = acc.astype(o_ref.dtype)


def _merge(cfg, ya, yb, yc, wbr, z, bm, tm, tn):
    m = cfg.m
    nj = D_MODEL // tn
    assert (cfg.m_ctx // NSLOT) % tm == 0 and ((m - cfg.m_ctx) // NSLOT) % tm == 0
    slot_spec = pl.BlockSpec((None, tm, BRANCH_W), lambda j, i: _slot_of_tile(cfg, tm, i) + (0,))
    y_spec = pl.BlockSpec((tm, BRANCH_W), lambda j, i: (i, 0))
    mg_spec = lambda n: pl.BlockSpec((tm, tn), lambda j, i: (i, Z_MG // tn + n * nj + j))
    return pl.pallas_call(
        _merge_kernel,
        grid=(nj, m // tm),
        in_specs=[slot_spec, slot_spec, y_spec,
                  pl.BlockSpec((N_BRANCH, BRANCH_W, tn), lambda j, i: (0, 0, j)),
                  mg_spec(0), mg_spec(1), mg_spec(2),
                  pl.BlockSpec((N_BRANCH, 1, tn), lambda j, i: (0, 0, j))],
        out_specs=pl.BlockSpec((tm, tn), lambda j, i: (i, j)),
        out_shape=jax.ShapeDtypeStruct((m, D_MODEL), BF16),
        compiler_params=_params(("parallel", "parallel")),
        name="merge",
    )(ya, yb, yc, wbr, z, z, z, bm)


def _matmul_res_kernel(*refs, starts):
    nsrc = len(starts) - 1
    a_ref, w_ref = refs[:2]
    x_refs = refs[2:2 + nsrc]
    g_ref, o_ref = refs[2 + nsrc:]
    i = pl.program_id(1)
    if nsrc == 1:
        o_ref[...] = x_refs[0][...] + g_ref[0] * jnp.dot(a_ref[...], w_ref[...], preferred_element_type=F32)
        return
    upd = g_ref[0] * jnp.dot(a_ref[...], w_ref[...], preferred_element_type=F32)
    for s, x_ref in enumerate(x_refs):
        @pl.when(jnp.logical_and(i >= starts[s], i < starts[s + 1]))
        def _():
            o_ref[...] = x_ref[...] + upd


def _matmul_res(cfg, a, w, xs, gate, tm, tn):
    m, kdim = a.shape
    n = w.shape[1]
    assert all(x.shape[0] % tm == 0 for x in xs)
    starts = [0]
    for x in xs:
        starts.append(starts[-1] + x.shape[0] // tm)
    assert starts[-1] * tm == m
    x_specs = [pl.BlockSpec((tm, tn), lambda j, i, s=s: (jnp.clip(i - starts[s], 0, starts[s + 1] - starts[s] - 1), j))
               for s in range(len(xs))]
    return pl.pallas_call(
        functools.partial(_matmul_res_kernel, starts=tuple(starts)),
        grid=(n // tn, m // tm),
        in_specs=[pl.BlockSpec((tm, kdim), lambda j, i: (i, 0)),
                  pl.BlockSpec((kdim, tn), lambda j, i: (0, j))] + x_specs
                 + [pl.BlockSpec((1, 1, tn), lambda j, i: (_mod_row(cfg, i * tm), 0, j))],
        out_specs=pl.BlockSpec((tm, tn), lambda j, i: (i, j)),
        out_shape=jax.ShapeDtypeStruct((m, n), F32),
        compiler_params=_params(("parallel", "parallel")),
        name="matmul_res",
    )(a, w, *xs, gate)


FD_TM = 512
FD_TK = 512
FD_HALO = GRID_W
FD_TOP = 8
FD_VMEM = 56 * 1024 * 1024


def _ffn_down_kernel(*refs, cfg, on_grid, seq_len, final_norm, aliased):
    refs = list(refs[1:] if aliased else refs)
    hg_ref, hu_ref = refs[:2]
    refs = refs[2:]
    if on_grid:
        hp_ref, hn_ref = refs[:2]
        refs = refs[2:]
    w9_ref, cb_ref, wd_ref, x_ref, g_ref = refs[:5]
    refs = refs[5:]
    if final_norm:
        nf_ref = refs[0]
        refs = refs[1:]
    o_ref, pad_ref, even_ref, odd_ref = refs
    i, k = pl.program_id(0), pl.program_id(1)
    nk = pl.num_programs(1) - 1
    tm, tk = FD_TM, FD_TK
    nslab = tk // 128
    base = FD_TOP + FD_HALO

    @pl.when(k == 0)
    def _():
        o_ref[...] = jnp.zeros_like(o_ref)
        edge = jnp.zeros((nslab, FD_TOP, 128), F32)
        pad_ref[:, 0:FD_TOP, :] = edge
        pad_ref[:, base + tm + FD_HALO:base + tm + FD_HALO + FD_TOP, :] = edge
        if not on_grid:
            halo = jnp.zeros((nslab, FD_HALO, 128), F32)
            pad_ref[:, FD_TOP:base, :] = halo
            pad_ref[:, base + tm:base + tm + FD_HALO, :] = halo

    if on_grid:
        tiles_per_seq = seq_len // tm
        first = i % tiles_per_seq == 0
        last = i % tiles_per_seq == tiles_per_seq - 1

        @pl.when(first)
        def _():
            pad_ref[:, FD_TOP:base, :] = jnp.zeros((nslab, FD_HALO, 128), F32)

        @pl.when(jnp.logical_not(first))
        def _():
            for s in range(nslab):
                pad_ref[s, FD_TOP:base, :] = hp_ref[:, s * 128:(s + 1) * 128]

        @pl.when(last)
        def _():
            pad_ref[:, base + tm:base + tm + FD_HALO, :] = jnp.zeros((nslab, FD_HALO, 128), F32)

        @pl.when(jnp.logical_not(last))
        def _():
            for s in range(nslab):
                pad_ref[s, base + tm:base + tm + FD_HALO, :] = hn_ref[:, s * 128:(s + 1) * 128]

    for s in range(nslab):
        pad_ref[s, base:base + tm, :] = hg_ref[:, s * 128:(s + 1) * 128]

    p = FFN_CONV // 2
    if on_grid:
        taps = tuple(((dy - p) * GRID_W + (dx - p), dy * FFN_CONV + dx, dx - p)
                     for dy in range(FFN_CONV) for dx in range(FFN_CONV))
        period = GRID_W
    else:
        taps = tuple((dx - p, p * FFN_CONV + dx, dx - p) for dx in range(FFN_CONV))
        period = seq_len
    kk = pl.multiple_of(jnp.maximum(k - 1, 0) * tk, tk)
    d_out = o_ref.shape[1]
    col_tile = 256
    n_col = d_out // col_tile
    rows = 64
    w9 = w9_ref[0]
    cb = cb_ref[0]
    conv_blocks = [(r, s) for r in range(tm // rows) for s in range(nslab)]
    per_tile = -(-len(conv_blocks) // n_col)

    def step(cur_ref, nxt_ref):
        for t in range(n_col):
            cols = slice(t * col_tile, (t + 1) * col_tile)
            if cur_ref is not None:
                o_ref[:, cols] += jnp.dot(cur_ref[...], wd_ref[pl.ds(kk, tk), cols], preferred_element_type=F32)
            for r, s in (conv_blocks[t * per_tile:(t + 1) * per_tile] if nxt_ref is not None else ()):
                pos = (lax.broadcasted_iota(jnp.int32, (rows, 128), 0) + r * rows) % period
                lanes = slice(s * 128, (s + 1) * 128)
                acc = cb[:, lanes]
                for shift, widx, dx in taps:
                    xs = pad_ref[s, base + r * rows + shift:base + (r + 1) * rows + shift, :]
                    if dx < 0:
                        xs = jnp.where(pos >= -dx, xs, 0.0)
                    elif dx > 0:
                        xs = jnp.where(pos < period - dx, xs, 0.0)
                    acc = acc + xs * w9[widx:widx + 1, lanes]
                act = _silu(acc) * hu_ref[r * rows:(r + 1) * rows, lanes]
                nxt_ref[r * rows:(r + 1) * rows, lanes] = act.astype(BF16)

    nk_static = D_FF // tk
    bufs = (even_ref, odd_ref)

    @pl.when(k == 0)
    def _():
        step(None, even_ref)

    @pl.when(jnp.logical_and(jnp.logical_and(k > 0, k < nk), k % 2 == 0))
    def _():
        step(odd_ref, even_ref)

    @pl.when(jnp.logical_and(k < nk, k % 2 == 1))
    def _():
        step(even_ref, odd_ref)

    @pl.when(k == nk)
    def _():
        step(bufs[(nk_static - 1) % 2], None)

        def slab(si, _):
            r0 = pl.multiple_of(si * NORM_SLAB, NORM_SLAB)
            res = x_ref[pl.ds(r0, NORM_SLAB), :] + g_ref[0] * o_ref[pl.ds(r0, NORM_SLAB), :]
            if final_norm:
                res = res * lax.rsqrt(jnp.mean(res * res, axis=-1, keepdims=True) + EPS) * nf_ref[...]
            o_ref[pl.ds(r0, NORM_SLAB), :] = res
            return 0

        lax.fori_loop(0, tm // NORM_SLAB, slab, 0)


def _ffn_down(cfg, h, w9, bias, wd, x, gate, nseq, t, row0, on_grid, prev=None, norm_gain=None):
    m, d = x.shape
    tm, tk = FD_TM, FD_TK
    rows = nseq * t
    nk = D_FF // tk
    assert D_FF % tk == 0 and row0 % tm == 0 and rows % tm == 0
    assert (t % tm == 0 and tm % GRID_W == 0) if on_grid else tm % t == 0
    final_norm = norm_gain is not None
    aliased = prev is not None
    assert not (final_norm and aliased)
    rt0 = row0 // tm
    kc = lambda k: jnp.minimum(k, nk - 1)
    hpb = tm // FD_HALO
    in_specs = [pl.BlockSpec((tm, tk), lambda i, k: (rt0 + i, kc(k))),
                pl.BlockSpec((tm, tk), lambda i, k: (rt0 + i, nk + kc(k)))]
    args = [h, h]
    if on_grid:
        in_specs += [pl.BlockSpec((FD_HALO, tk), lambda i, k: (jnp.maximum((rt0 + i) * hpb - 1, 0), kc(k))),
                     pl.BlockSpec((FD_HALO, tk), lambda i, k: (jnp.minimum((rt0 + i + 1) * hpb, m // FD_HALO - 1), kc(k)))]
        args += [h, h]
    w9c = jnp.transpose(w9.reshape(FFN_CONV * FFN_CONV, nk, tk), (1, 0, 2))
    in_specs += [pl.BlockSpec((1, FFN_CONV * FFN_CONV, tk), lambda i, k: (kc(k), 0, 0)),
                 pl.BlockSpec((1, 1, tk), lambda i, k: (kc(k), 0, 0)),
                 pl.BlockSpec((D_FF, d), lambda i, k: (0, 0), pipeline_mode=pl.Buffered(1)),
                 pl.BlockSpec((tm, d), lambda i, k: (rt0 + i, 0)),
                 pl.BlockSpec((1, 1, d), lambda i, k: (_mod_row(cfg, (rt0 + i) * tm), 0, 0))]
    args += [w9c, bias.reshape(nk, 1, tk), wd, x, gate]
    if final_norm:
        in_specs.append(pl.BlockSpec((1, d), lambda i, k: (0, 0)))
        args.append(norm_gain)
    if aliased:
        in_specs = [pl.BlockSpec(memory_space=pl.ANY)] + in_specs
        args = [prev] + args
    out_rows, out_rt0 = (rows, 0) if final_norm else (m, rt0)
    return pl.pallas_call(
        functools.partial(_ffn_down_kernel, cfg=cfg, on_grid=on_grid, seq_len=t, final_norm=final_norm, aliased=aliased),
        grid=(rows // tm, nk + 1),
        in_specs=in_specs,
        out_specs=pl.BlockSpec((tm, d), lambda i, k: (out_rt0 + i, 0)),
        out_shape=jax.ShapeDtypeStruct((out_rows, d), F32),
        scratch_shapes=[pltpu.VMEM((tk // 128, tm + 2 * (FD_HALO + FD_TOP), 128), F32),
                        pltpu.VMEM((tm, tk), BF16),
                        pltpu.VMEM((tm, tk), BF16)],
        input_output_aliases={0: 0} if aliased else {},
        compiler_params=pltpu.CompilerParams(dimension_semantics=("parallel", "arbitrary"), vmem_limit_bytes=FD_VMEM),
        name="ffn_down",
    )(*args)


def _pack_w_in(w):
    s = _SRC
    w = w.astype(BF16)
    cols = [w[:, s["mg"]:s["end"]], w[:, s["va"]:s["ga"]], w[:, s["ga"]:s["ra"]], w[:, s["qb"]:s["kb"]],
            w[:, s["kb"]:s["vb"]], w[:, s["vb"]:s["ob"]], w[:, s["ob"]:s["gb"]], w[:, s["xr"]:s["yr"]],
            w[:, s["yr"]:s["mg"]], w[:, s["qa"]:s["ka"]], w[:, s["ka"]:s["va"]], w[:, s["ra"]:s["qb"]],
            w[:, s["gb"]:s["xr"]]]
    used = sum(c.shape[1] for c in cols)
    cols.append(jnp.zeros((w.shape[0], NZ - used), w.dtype))
    return jnp.concatenate(cols, axis=1)


def _pack_gate_w(gw, gb, lam):
    nb, blk = LRU_BLOCKS, LRU_BLOCK
    gw_p = jnp.transpose(gw, (2, 3, 0, 1, 4)).reshape(nb, blk, 4 * blk).astype(BF16)
    gb_p = jnp.transpose(gb.reshape(2, 2, nb, blk), (2, 0, 1, 3)).reshape(nb, 1, 4 * blk)
    lam_p = jnp.transpose(lam.reshape(2, nb, blk), (1, 0, 2)).reshape(nb, 1, 2 * blk)
    return gw_p, gb_p, lam_p


def _layer(cfg, x, mod, lp, states, final_gain=None):
    (n1, n2, w_in, gwa, gba, gng, mbif, mng, lcw, lcb, lgw, lgb, llam, wbr, bmg, wout, fup, fcw, fcb, fdown) = lp
    s_gla, s_mc, s_mn, s_mm, s_lru = states
    m = cfg.m
    sh1, sc1, g1, sh2, sc2, g2 = (mod[:, k * D_MODEL:(k + 1) * D_MODEL].reshape(MOD_ROWS, 1, D_MODEL) for k in range(6))

    proj_tm = PROJ_TM if (cfg.m_ctx % PROJ_TM == 0 and cfg.tl % PROJ_TM == 0) else 512
    z = _norm_matmul(cfg, x, n1.reshape(1, -1), sc1, sh1, _pack_w_in(w_in), proj_tm, 512)

    s0t = jnp.swapaxes(s_gla, -1, -2)
    o_f, sa_f = _gla(cfg, z, gwa[0], gba[0].reshape(1, -1), s0t[:, 0], False)
    y_a, sa_r = _gla(cfg, z, gwa[1], gba[1].reshape(1, -1), s0t[:, 1], True, o_f, gng.reshape(1, -1))
    new_gla =jnp.swapaxes(jnp.stack([sa_f, sa_r], axis=1), -1, -2)

    g_rows = z[:, Z_SM + SM_GB:Z_SM + SM_GB + 4 * MLSTM_HEADS].T
    k_t = z[:, Z_KB:Z_KB + MLSTM_HEADS * MLSTM_DH].T
    b_col = mbif.reshape(1, -1)
    b_row = mbif.reshape(-1, 1)
    n0 = s_mn[:, :, :, None, :]
    m0 = jnp.broadcast_to(s_mm[:, :, :, None, None], s_mm.shape + (1, 128))
    h_f, cb_f, nb_f, mb_f = _mlstm(cfg, z, k_t, g_rows, b_col, b_row, s_mc[:, 0], n0[:, 0], m0[:, 0], False)
    y_b, cb_r, nb_r, mb_r = _mlstm(cfg, z, k_t, g_rows, b_col, b_row, s_mc[:, 1], n0[:, 1], m0[:, 1], True,
                                   h_f, mng.reshape(1, -1))
    new_mc =jnp.stack([cb_f, cb_r], axis=1)
    new_mn = jnp.stack([nb_f[:, :, 0], nb_r[:, :, 0]], axis=1)
    new_mm = jnp.stack([mb_f[:, :, 0, 0], mb_r[:, :, 0, 0]], axis=1)

    gw_p, gb_p, lam_p = _pack_gate_w(lgw, lgb, llam)
    cbias = lcb.reshape(1, -1)
    y_c, new_lru = _rglru(cfg, z, lcw, cbias, gw_p, gb_p, lam_p, jnp.zeros((cfg.bc, N_DIR, LRU_WIDTH), F32),
                          cfg.bc, cfg.tc, 0)
    y_c, _ = _rglru(cfg, z, lcw, cbias, gw_p, gb_p, lam_p, s_lru, cfg.bl, cfg.tl, cfg.m_ctx, y_prev=y_c)

    merged = _merge(cfg, y_a, y_b, y_c, wbr.astype(BF16), z, bmg.reshape(N_BRANCH, 1, D_MODEL), 512, 1024)
    x = _matmul_res(cfg, merged, wout.astype(BF16), x, g1, 512, 1024)

    h = _norm_matmul(cfg, (x,), n2.reshape(1, -1), sc2, sh2, fup.astype(BF16), proj_tm, 1024)
    w9 = fcw.reshape(FFN_CONV * FFN_CONV, D_FF)
    wd = fdown.astype(BF16)
    if final_gain is None:
        x_ctx = _ffn_down(cfg, h, w9, fcb, wd, x, g2, cfg.bc, cfg.tc, 0, False)
        x = _ffn_down(cfg, h, w9, fcb, wd, x, g2, cfg.bl, cfg.tl, cfg.m_ctx, True, prev=x_ctx)
    else:
        x = (_ffn_down(cfg, h, w9, fcb, wd, x, g2, cfg.bc, cfg.tc, 0, False, norm_gain=final_gain),
             _ffn_down(cfg, h, w9, fcb, wd, x, g2, cfg.bl, cfg.tl, cfg.m_ctx, True, norm_gain=final_gain))
    return x, (new_gla, new_mc, new_mn, new_mm, new_lru)


def kernel(x_prompt, x_sample, state_gla, state_mlstm_c, state_mlstm_n, state_mlstm_m, state_rglru, c, c_ctx, norm1_g, norm2_g, w_mod, b_mod, w_in, gla_w_alpha, gla_b_alpha, gla_norm_g, mlstm_b_if, mlstm_norm_g, lru_conv_w, lru_conv_b, lru_gate_w, lru_gate_b, lru_lambda, w_branch, b_merge, w_out, ffn_w_up, ffn_conv_w, ffn_conv_b, ffn_w_down, norm_f_g):
    bc, tc, d = x_prompt.shape
    bl, tl, _ = x_sample.shape
    cfg = Cfg(bc, tc, bl, tl)
    assert tc % TB == 0 and tl % TB == 0 and cfg.m_ctx % tl == 0 and 1 + bl <= MOD_ROWS
    assert bc % NSLOT == 0 and bl % NSLOT == 0
    depth = w_in.shape[0]

    x = (x_prompt.reshape(bc * tc, d), x_sample.reshape(bl * tl, d))
    c_all = jnp.concatenate([c_ctx[None, :], c, jnp.zeros((MOD_ROWS - 1 - bl, d), F32)], axis=0)
    mod = _modulation(c_all, w_mod, b_mod)

    new = []
    for l in range(depth):
        lp = (norm1_g[l], norm2_g[l], w_in[l], gla_w_alpha[l], gla_b_alpha[l], gla_norm_g[l], mlstm_b_if[l],
              mlstm_norm_g[l], lru_conv_w[l], lru_conv_b[l], lru_gate_w[l], lru_gate_b[l], lru_lambda[l],
              w_branch[l], b_merge[l], w_out[l], ffn_w_up[l], ffn_conv_w[l], ffn_conv_b[l], ffn_w_down[l])
        states = (state_gla[:, l], state_mlstm_c[:, l], state_mlstm_n[:, l], state_mlstm_m[:, l], state_rglru[:, l])
        x, st = _layer(cfg, x, mod[l], lp, states, norm_f_g.reshape(1, -1) if l == depth - 1 else None)
        x = x if l == depth - 1 else (x,)
        new.append(st)

    y_prompt = x[0].reshape(bc, tc, d)
    y_sample = x[1].reshape(bl, tl, d)
    stacked = tuple(jnp.stack([new[l][k] for l in range(depth)], axis=1) for k in range(5))
    return (y_prompt, y_sample) + stacked
```

```python
import functools
from typing import NamedTuple

import numpy as np
import jax
import jax.numpy as jnp
from jax import lax
from jax.experimental import pallas as pl
from jax.experimental.pallas import tpu as pltpu

F32 = jnp.float32
BF16 = jnp.bfloat16

D_MODEL = 2048
DEPTH = 2
GRID_W = 64
N_DIR = 2
N_BRANCH = 3
BRANCH_W = 1024
GLA_HEADS = 4
GLA_DK = 128
GLA_DV = 256
GLA_RANK = 16
GLA_TAU = 16.0
MLSTM_HEADS = 4
MLSTM_DH = 256
CHUNK = 64
LRU_WIDTH = 1024
LRU_BLOCKS = 8
LRU_BLOCK = 128
LRU_CONV = 4
LRU_C = 8.0
D_FF = 5632
FFN_CONV = 3
EPS = 1e-6

_SRC = dict(qa=0, ka=512, va=1024, ga=2048, ra=3072, qb=3104, kb=4128, vb=5152, ob=6176, gb=7200,
            xr=7216, yr=8240, mg=9264, end=15408)
Z_MG, Z_VA, Z_GA, Z_QB, Z_KB, Z_VB, Z_OB, Z_XR, Z_YR, Z_QA, Z_KA, Z_SM = (
    0, 6144, 7168, 8192, 9216, 10240, 11264, 12288, 13312, 14336, 14848, 15360)
SM_W = 128
SM_RA, SM_GB = 0, 32
NZ = 16384
MOD_ROWS = 16

PROJ_TM = 1024
NORM_SLAB = 32
PROJ_SLAB = 64
TB = 256
NCH = TB // CHUNK
VMEM_LIMIT = 48 * 1024 * 1024
PROJ_SPLIT_VMEM = 54 * 1024 * 1024


class Cfg(NamedTuple):
    bc: int
    tc: int
    bl: int
    tl: int

    @property
    def m_ctx(self):
        return self.bc * self.tc

    @property
    def m(self):
        return self.bc * self.tc + self.bl * self.tl


def _params(sem):
    return pltpu.CompilerParams(dimension_semantics=sem, vmem_limit_bytes=VMEM_LIMIT)


def _softplus(x):
    return jnp.maximum(x, 0.0) + jnp.log1p(jnp.exp(-jnp.abs(x)))


def _log_sigmoid(x):
    return -_softplus(-x)


def _silu(x):
    return x * jax.nn.sigmoid(x)


def _gelu_tanh(x):
    return 0.5 * x * (1.0 + jnp.tanh(np.sqrt(2.0 / np.pi).astype(np.float32) * (x + 0.044715 * (x * x * x))))


def _mod_row(cfg, row):
    return jnp.where(row < cfg.m_ctx, 0, 1 + jnp.maximum(row - cfg.m_ctx, 0) // cfg.tl)


def _mod_kernel(c_ref, w_ref, b_ref, o_ref):
    c = c_ref[...]
    a = _silu(c).astype(BF16)
    o_ref[0] = jnp.dot(a, w_ref[0].astype(BF16), preferred_element_type=F32) + b_ref[0]


def _modulation(c_all, w_mod, b_mod):
    depth, d, n = w_mod.shape
    tn = 512
    return pl.pallas_call(
        _mod_kernel,
        grid=(depth, n // tn),
        in_specs=[pl.BlockSpec((MOD_ROWS, d), lambda l, j: (0, 0)),
                  pl.BlockSpec((1, d, tn), lambda l, j: (l, 0, j)),
                  pl.BlockSpec((1, 1, tn), lambda l, j: (l, 0, j))],
        out_specs=pl.BlockSpec((1, MOD_ROWS, tn), lambda l, j: (l, 0, j)),
        out_shape=jax.ShapeDtypeStruct((depth, MOD_ROWS, n), F32),
        compiler_params=_params(("parallel", "parallel")),
        name="modulation",
    )(c_all, w_mod, b_mod.reshape(depth, 1, n))


def _norm_rows(x_ref, g_ref, sc_ref, sh_ref, u_ref, r0, nrows):
    x = x_ref[pl.ds(r0, nrows), :]
    y = x * lax.rsqrt(jnp.mean(x * x, axis=-1, keepdims=True) + EPS) * g_ref[...]
    u_ref[pl.ds(r0, nrows), :] = (y * (1.0 + sc_ref[0]) + sh_ref[0]).astype(BF16)


def _norm_matmul_kernel(*refs, slab, starts):
    nsrc = len(starts) - 1
    x0_ref = refs[0]
    xn_refs = refs[1:1 + nsrc]
    g_ref, sc0_ref, sh0_ref, scn_ref, shn_ref, w_ref, o_ref, ua_ref, ub_ref = refs[1 + nsrc:]
    i, j = pl.program_id(0), pl.program_id(1)
    tm = x0_ref.shape[0]
    nxt = jnp.minimum(i + 1, starts[-1] - 1)

    @pl.when(jnp.logical_and(i == 0, j == 0))
    def _():
        def first(si, _):
            _norm_rows(x0_ref, g_ref, sc0_ref, sh0_ref, ua_ref, pl.multiple_of(si * NORM_SLAB, NORM_SLAB), NORM_SLAB)
            return 0

        lax.fori_loop(0, tm // NORM_SLAB, first, 0)

    r0 = pl.multiple_of(jnp.minimum(j, tm // slab - 1) * slab, slab)
    for s, xn_ref in enumerate(xn_refs):
        from_s = jnp.logical_and(nxt >= starts[s], nxt < starts[s + 1])

        @pl.when(jnp.logical_and(from_s, i % 2 == 0))
        def _():
            _norm_rows(xn_ref, g_ref, scn_ref, shn_ref, ub_ref, r0, slab)
            o_ref[...] = jnp.dot(ua_ref[...], w_ref[...], preferred_element_type=F32)

        @pl.when(jnp.logical_and(from_s, i % 2 == 1))
        def _():
            _norm_rows(xn_ref, g_ref, scn_ref, shn_ref, ua_ref, r0, slab)
            o_ref[...] = jnp.dot(ub_ref[...], w_ref[...], preferred_element_type=F32)


def _norm_matmul(cfg, xs, gain, sc, sh, w, tm, tn):
    d = xs[0].shape[1]
    n = w.shape[1]
    slab = PROJ_SLAB
    while n // tn < tm // slab:
        slab *= 2
    assert tm % slab == 0 and n // tn >= tm // slab and all(x.shape[0] % tm == 0 for x in xs)
    starts = [0]
    for x in xs:
        starts.append(starts[-1] + x.shape[0] // tm)
    ntiles = starts[-1]
    nxt = lambda i: jnp.minimum(i + 1, ntiles - 1)
    mod0 = lambda i, j: (_mod_row(cfg, 0), 0, 0)
    modn = lambda i, j: (_mod_row(cfg, nxt(i) * tm), 0, 0)
    src_specs = [pl.BlockSpec((tm, d), lambda i, j, s=s: (jnp.clip(nxt(i) - starts[s], 0, starts[s + 1] - starts[s] - 1), 0),
                              pipeline_mode=pl.Buffered(1) if len(xs) > 1 and s == 0 else None)
                 for s in range(len(xs))]
    return pl.pallas_call(
        functools.partial(_norm_matmul_kernel, slab=slab, starts=tuple(starts)),
        grid=(ntiles, n // tn),
        in_specs=[pl.BlockSpec((tm, d), lambda i, j: (0, 0), pipeline_mode=pl.Buffered(1))] + src_specs
                 + [pl.BlockSpec((1, d), lambda i, j: (0, 0)),
                    pl.BlockSpec((1, 1, d), mod0),
                    pl.BlockSpec((1, 1, d), mod0),
                    pl.BlockSpec((1, 1, d), modn),
                    pl.BlockSpec((1, 1, d), modn),
                    pl.BlockSpec((d, tn), lambda i, j: (0, j))],
        out_specs=pl.BlockSpec((tm, tn), lambda i, j: (i, j)),
        out_shape=jax.ShapeDtypeStruct((ntiles * tm, n), F32),
        scratch_shapes=[pltpu.VMEM((tm, d), BF16), pltpu.VMEM((tm, d), BF16)],
        compiler_params=pltpu.CompilerParams(dimension_semantics=("arbitrary", "arbitrary"),
                                             vmem_limit_bytes=VMEM_LIMIT if len(xs) == 1 and tn <= 512 else PROJ_SPLIT_VMEM),
        name="norm_matmul",
    )(xs[0], *xs, gain, sc, sh, sc, sh, w)


NSLOT = 2


def _pair_pos(cfg, i):
    kc, kl = cfg.tc // TB, cfg.tl // TB
    nc = (cfg.bc // NSLOT) * kc
    is_ctx = i < nc
    il = jnp.maximum(i - nc, 0)
    j = jnp.where(is_ctx, i % kc, il % kl)
    k = jnp.where(is_ctx, kc, kl)
    s = jnp.where(is_ctx, i // kc, il // kl)
    return is_ctx, s, j, k


def _pair_row_block(cfg, reverse, slot, i):
    kc, kl = cfg.tc // TB, cfg.tl // TB
    is_ctx, s, j, k = _pair_pos(cfg, i)
    jj = (k - 1 - j) if reverse else j
    return jnp.where(is_ctx, (s + slot * (cfg.bc // NSLOT)) * kc + jj,
                     cfg.bc * kc + (s + slot * (cfg.bl // NSLOT)) * kl + jj)


def _pair_local_block(cfg, reverse, i):
    kc, kl = cfg.tc // TB, cfg.tl // TB
    is_ctx, s, j, k = _pair_pos(cfg, i)
    jj = (k - 1 - j) if reverse else j
    return jnp.where(is_ctx, s * kc + jj, (cfg.bc // NSLOT) * kc + s * kl + jj)


def _pair_lat_seq(cfg, i):
    is_ctx, s, _, _ = _pair_pos(cfg, i)
    return jnp.where(is_ctx, 0, jnp.minimum(s, cfg.bl // NSLOT - 1))


def _pair_ctx_seq(cfg, i):
    is_ctx, s, _, _ = _pair_pos(cfg, i)
    return jnp.where(is_ctx, s, cfg.bc // NSLOT - 1)


def _slot_of_tile(cfg, tm, i):
    nc, nl = cfg.m_ctx // tm // NSLOT, (cfg.m - cfg.m_ctx) // tm // NSLOT
    il = jnp.maximum(i - NSLOT * nc, 0)
    is_ctx = i < NSLOT * nc
    return jnp.where(is_ctx, i // nc, il // nl), jnp.where(is_ctx, i % nc, nc + il % nl)


def _tri(reverse, n=CHUNK):
    row = lax.broadcasted_iota(jnp.int32, (n, n), 0)
    col = lax.broadcasted_iota(jnp.int32, (n, n), 1)
    return (row <= col) if reverse else (row >= col)


def _head_norm(x, g):
    return x * lax.rsqrt(jnp.mean(x * x, axis=-1, keepdims=True) + EPS) * g


_NT = (((1,), (1,)), ((), ()))
_TN = (((0,), (0,)), ((), ()))


GLA_LOCKSTEP = 2


def _gla_kernel(*refs, cfg, reverse):
    it = iter(refs)
    blocks = [[next(it) for _ in range(4)] for _ in range(NSLOT)]
    wal_ref, bal_ref = next(it), next(it)
    init_ref = next(it)
    init = [init_ref.at[b] for b in range(NSLOT)]
    if reverse:
        gates = [next(it) for _ in range(NSLOT)]
        of_ref, gn_ref = next(it), next(it)
        epi = [[gates[b], of_ref.at[b]] for b in range(NSLOT)]
    out_ref, fin_ref = next(it), next(it)
    outs = [out_ref.at[b] for b in range(NSLOT)]
    fin = [fin_ref.at[b] for b in range(NSLOT)]
    st_ref, la_ref = next(it), next(it)
    is_ctx, _, j, k = _pair_pos(cfg, pl.program_id(0))
    nh = GLA_HEADS

    @pl.when(jnp.logical_and(j == 0, is_ctx))
    def _():
        st_ref[...] = jnp.zeros_like(st_ref)

    @pl.when(jnp.logical_and(j == 0, jnp.logical_not(is_ctx)))
    def _():
        for b in range(NSLOT):
            st_ref[b * nh:(b + 1) * nh] = init[b][0]

    d = 1 if reverse else 0
    slots = range(NSLOT)
    both = lambda f: [f(b) for b in slots]
    wal = wal_ref[...].astype(BF16)
    for b in slots:
        ra = blocks[b][3][:, SM_RA + d * GLA_RANK:SM_RA + (d + 1) * GLA_RANK].astype(BF16)
        pre = jnp.dot(ra, wal, preferred_element_type=F32) + bal_ref[...]
        la_ref[b] = _log_sigmoid(pre) * (1.0 / GLA_TAU)

    tri = _tri(reverse)
    tri_f = tri.astype(F32)
    last = 0 if reverse else CHUNK - 1
    qscale = GLA_DK ** -0.5
    for c in (range(NCH - 1, -1, -1) if reverse else range(NCH)):
        rows = slice(c * CHUNK, (c + 1) * CHUNK)
        cum = both(lambda b: jnp.dot(tri_f, la_ref[b, rows, :], precision=lax.Precision.HIGHEST,
                                     preferred_element_type=F32))
        tot = both(lambda b: cum[b][last:last + 1, :])
        e_q = both(lambda b: jnp.exp(cum[b]))
        e_k = both(lambda b: jnp.exp(-cum[b]))
        e_end = both(lambda b: jnp.exp(tot[b] - cum[b]))
        e_tot = both(lambda b: jnp.exp(tot[b]))
        for h0 in range(0, GLA_HEADS, GLA_LOCKSTEP):
            units = [(b, h0 + dh) for dh in range(GLA_LOCKSTEP) for b in slots]
            each = lambda f: [f(u, b, h) for u, (b, h) in enumerate(units)]
            kc = [slice(h * GLA_DK, (h + 1) * GLA_DK) for _, h in units]
            vc = [slice(h * GLA_DV, (h + 1) * GLA_DV) for _, h in units]
            sh = [b * nh + h for b, h in units]
            qh = each(lambda u, b, h: blocks[b][0][rows, kc[u]] * qscale)
            kh = each(lambda u, b, h: blocks[b][1][rows, kc[u]])
            vh = each(lambda u, b, h: blocks[b][2][rows, vc[u]].astype(BF16))
            q_in = each(lambda u, b, h: (qh[u] * e_q[b][:, kc[u]]).astype(BF16))
            k_in = each(lambda u, b, h: (kh[u] * e_k[b][:, kc[u]]).astype(BF16))
            att = each(lambda u, b, h: lax.dot_general(q_in[u], k_in[u], _NT, preferred_element_type=F32))
            att = each(lambda u, b, h: jnp.where(tri, att[u], 0.0).astype(BF16))
            st = each(lambda u, b, h: st_ref[sh[u]])
            o = each(lambda u, b, h: jnp.dot(att[u], vh[u], preferred_element_type=F32)
                     + lax.dot_general(q_in[u], st[u].astype(BF16), _NT, preferred_element_type=F32))
            k_end = each(lambda u, b, h: (kh[u] * e_end[b][:, kc[u]]).astype(BF16))
            upd = each(lambda u, b, h: lax.dot_general(vh[u], k_end[u], _TN, preferred_element_type=F32))
            for u, (b, h) in enumerate(units):
                st_ref[sh[u]] = e_tot[b][:, kc[u]] * st[u] + upd[u]
            for u, (b, h) in enumerate(units):
                if reverse:
                    ga_ref, of_ref = epi[b]
                    y = _head_norm(of_ref[rows, vc[u]] + o[u], gn_ref[:, vc[u]]) * _silu(ga_ref[rows, vc[u]])
                    outs[b][rows, vc[u]] = y.astype(outs[b].dtype)
                else:
                    outs[b][rows, vc[u]] = o[u]

    @pl.when(jnp.logical_and(is_ctx, j == k - 1))
    def _():
        for b in range(NSLOT):
            fin[b][0] = st_ref[b * nh:(b + 1) * nh]


def _gla(cfg, z, wal, bal, s0t, reverse, o_fwd=None, gnorm=None):
    m = cfg.m
    nh = GLA_HEADS
    hk, hv = nh * GLA_DK, nh * GLA_DV
    lb = functools.partial(_pair_local_block, cfg, reverse)
    in_specs, args = [], []
    for b in range(NSLOT):
        rb = functools.partial(_pair_row_block, cfg, reverse, b)
        in_specs += [pl.BlockSpec((TB, hk), lambda i, rb=rb: (rb(i), Z_QA // hk)),
                     pl.BlockSpec((TB, hk), lambda i, rb=rb: (rb(i), Z_KA // hk)),
                     pl.BlockSpec((TB, hv), lambda i, rb=rb: (rb(i), Z_VA // hv)),
                     pl.BlockSpec((TB, SM_W), lambda i, rb=rb: (rb(i), Z_SM // SM_W))]
        args += [z, z, z, z]
    in_specs += [pl.BlockSpec((GLA_RANK, hk), lambda i: (0, 0)), pl.BlockSpec((1, hk), lambda i: (0, 0))]
    args += [wal, bal]
    half_c, half_l = cfg.bc // NSLOT, cfg.bl // NSLOT
    in_specs.append(pl.BlockSpec((NSLOT, 1, nh, GLA_DV, GLA_DK), lambda i: (0, _pair_lat_seq(cfg, i), 0, 0, 0)))
    args.append(s0t.reshape(NSLOT, half_l, nh, GLA_DV, GLA_DK))
    if reverse:
        for b in range(NSLOT):
            rb = functools.partial(_pair_row_block, cfg, reverse, b)
            in_specs.append(pl.BlockSpec((TB, hv), lambda i, rb=rb: (rb(i), Z_GA // hv)))
            args.append(z)
        in_specs += [pl.BlockSpec((NSLOT, TB, hv), lambda i: (0, lb(i), 0)), pl.BlockSpec((1, hv), lambda i: (0, 0))]
        args += [o_fwd, gnorm]
    out, fin = pl.pallas_call(
        functools.partial(_gla_kernel, cfg=cfg, reverse=reverse),
        grid=(m // TB // NSLOT,),
        in_specs=in_specs,
        out_specs=[pl.BlockSpec((NSLOT, TB, hv), lambda i: (0, lb(i), 0)),
                   pl.BlockSpec((NSLOT, 1, nh, GLA_DV, GLA_DK), lambda i: (0, _pair_ctx_seq(cfg, i), 0, 0, 0))],
        out_shape=[jax.ShapeDtypeStruct((NSLOT, m // NSLOT, hv), BF16 if reverse else F32),
                   jax.ShapeDtypeStruct((NSLOT, half_c, nh, GLA_DV, GLA_DK), F32)],
        scratch_shapes=[pltpu.VMEM((NSLOT * nh, GLA_DV, GLA_DK), F32), pltpu.VMEM((NSLOT, TB, hk), F32)],
        compiler_params=_params(("arbitrary",)),
        name="gla_rev" if reverse else "gla_fwd",
    )(*args)
    return out, fin.reshape(cfg.bc, nh, GLA_DV, GLA_DK)


MCH = 128
NMC = TB // MCH
MLSTM_LOCKSTEP = 2


def _mlstm_kernel(*refs, cfg, reverse):
    it = iter(refs)
    blocks = [[next(it) for _ in range(6)] for _ in range(NSLOT)]
    bc_ref, br_ref = next(it), next(it)
    init_refs = [next(it) for _ in range(3)]
    init = [[r.at[b] for r in init_refs] for b in range(NSLOT)]
    if reverse:
        gates = [next(it) for _ in range(NSLOT)]
        hf_all, gn_ref = next(it), next(it)
        epi = [[gates[b], hf_all.at[b]] for b in range(NSLOT)]
    out_ref = next(it)
    outs = [out_ref.at[b] for b in range(NSLOT)]
    fin_refs = [next(it) for _ in range(3)]
    fin = [[r.at[b] for r in fin_refs] for b in range(NSLOT)]
    c_ref, n_ref, m_ref = next(it), next(it), next(it)
    is_ctx, _, j, k = _pair_pos(cfg, pl.program_id(0))
    nh = MLSTM_HEADS

    @pl.when(jnp.logical_and(j == 0, is_ctx))
    def _():
        c_ref[...] = jnp.zeros_like(c_ref)
        n_ref[...] = jnp.zeros_like(n_ref)
        m_ref[...] = jnp.zeros_like(m_ref)

    @pl.when(jnp.logical_and(j == 0, jnp.logical_not(is_ctx)))
    def _():
        for b in range(NSLOT):
            c_ref[b * nh:(b + 1) * nh] = init[b][0][0]
            n_ref[b * nh:(b + 1) * nh] = init[b][1][0]
            m_ref[b * nh:(b + 1) * nh] = init[b][2][0]

    d = 1 if reverse else 0
    tri = _tri(reverse, MCH)
    tri_f = tri.astype(F32)
    tri_tf = _tri(not reverse, MCH).astype(F32)
    last = 0 if reverse else MCH - 1
    ng = 2 * MLSTM_HEADS
    kscale = MLSTM_DH ** -0.5
    slots = range(NSLOT)
    both = lambda f: [f(b) for b in slots]
    hp = lax.Precision.HIGHEST
    for c in (range(NMC - 1, -1, -1) if reverse else range(NMC)):
        rows = slice(c * MCH, (c + 1) * MCH)
        g_col = both(lambda b: blocks[b][4][rows, SM_GB:SM_GB + 2 * ng] + bc_ref[...])
        g_row = both(lambda b: blocks[b][5][:, rows] + br_ref[...])
        fcum_col = both(lambda b: jnp.dot(tri_f, _log_sigmoid(g_col[b]), precision=hp, preferred_element_type=F32))
        fcum_row = both(lambda b: jnp.dot(_log_sigmoid(g_row[b]), tri_tf, precision=hp, preferred_element_type=F32))
        for h0 in range(0, MLSTM_HEADS, MLSTM_LOCKSTEP):
            units = [(b, h0 + dh) for dh in range(MLSTM_LOCKSTEP) for b in slots]
            each = lambda f: [f(u, b, h) for u, (b, h) in enumerate(units)]
            ii = [d * ng + h for _, h in units]
            fi = [d * ng + MLSTM_HEADS + h for _, h in units]
            hc = [slice(h * MLSTM_DH, (h + 1) * MLSTM_DH) for _, h in units]
            sh = [b * nh + h for b, h in units]
            f_col = each(lambda u, b, h: fcum_col[b][:, fi[u]:fi[u] + 1])
            f_row = each(lambda u, b, h: fcum_row[b][fi[u]:fi[u] + 1, :])
            i_col = each(lambda u, b, h: g_col[b][:, ii[u]:ii[u] + 1])
            i_row = each(lambda u, b, h: g_row[b][ii[u]:ii[u] + 1, :])
            m_prev = each(lambda u, b, h: m_ref[sh[u]][:, 0:1])
            dlog = each(lambda u, b, h: jnp.where(tri, f_col[u] - f_row[u] + i_row[u], -jnp.inf))
            prev = each(lambda u, b, h: f_col[u] + m_prev[u])
            mj = each(lambda u, b, h: jnp.maximum(prev[u], jnp.max(dlog[u], axis=-1, keepdims=True)))
            w = each(lambda u, b, h: jnp.exp(dlog[u] - mj[u]))
            wp = each(lambda u, b, h: jnp.exp(prev[u] - mj[u]))
            qh = each(lambda u, b, h: blocks[b][0][rows, hc[u]])
            kh = each(lambda u, b, h: blocks[b][1][rows, hc[u]] * kscale)
            vh = each(lambda u, b, h: blocks[b][2][rows, hc[u]].astype(BF16))
            s = each(lambda u, b, h: lax.dot_general(qh[u].astype(BF16), kh[u].astype(BF16), _NT,
                                                     preferred_element_type=F32) * w[u])
            qp = each(lambda u, b, h: qh[u] * wp[u])
            cm = each(lambda u, b, h: c_ref[sh[u]])
            nv = each(lambda u, b, h: n_ref[sh[u]])
            m_new = each(lambda u, b, h: mj[u][last:last + 1, :])
            tot = each(lambda u, b, h: f_col[u][last:last + 1, :])
            wl_col = each(lambda u, b, h: jnp.exp(tot[u] - f_col[u] + i_col[u] - m_new[u]))
            wl_row = each(lambda u, b, h: jnp.exp(tot[u] - f_row[u] + i_row[u] - m_new[u]))
            decay = each(lambda u, b, h: jnp.exp(tot[u] + m_prev[u] - m_new[u]))
            kw_t = each(lambda u, b, h: (blocks[b][3][hc[u], rows] * kscale) * wl_row[u])
            sv = each(lambda u, b, h: jnp.dot(jnp.concatenate([s[u].astype(BF16), kw_t[u].astype(BF16)], axis=0),
                                              vh[u], preferred_element_type=F32))
            num = each(lambda u, b, h: sv[u][:MCH]
                       + jnp.dot(qp[u].astype(BF16), cm[u].astype(BF16), preferred_element_type=F32))
            upd = each(lambda u, b, h: sv[u][MCH:])
            den = each(lambda u, b, h: jnp.sum(s[u], axis=-1, keepdims=True)
                       + jnp.sum(qp[u] * nv[u], axis=-1, keepdims=True))
            hh = each(lambda u, b, h: num[u] / jnp.maximum(jnp.abs(den[u]), jnp.exp(-mj[u])))
            for u, (b, h) in enumerate(units):
                c_ref[sh[u]] = decay[u] * cm[u] + upd[u]
                n_ref[sh[u]] = decay[u] * nv[u] + jnp.sum(kh[u] * wl_col[u], axis=0, keepdims=True)
                m_ref[sh[u]] = jnp.broadcast_to(m_new[u], (1, 128))
            for u, (b, h) in enumerate(units):
                if reverse:
                    ob_ref, hf_ref = epi[b]
                    y = jax.nn.sigmoid(ob_ref[rows, hc[u]]) * _head_norm(hf_ref[rows, hc[u]] + hh[u], gn_ref[:, hc[u]])
                    outs[b][rows, hc[u]] = y.astype(outs[b].dtype)
                else:
                    outs[b][rows, hc[u]] = hh[u]

    @pl.when(jnp.logical_and(is_ctx, j == k - 1))
    def _():
        for b in range(NSLOT):
            fin[b][0][0] = c_ref[b * nh:(b + 1) * nh]
            fin[b][1][0] = n_ref[b * nh:(b + 1) * nh]
            fin[b][2][0] = m_ref[b * nh:(b + 1) * nh]


def _mlstm(cfg, z, k_t, g_rows, b_col, b_row, c0, n0, m0, reverse, h_fwd=None, gnorm=None):
    m = cfg.m
    hd = MLSTM_HEADS * MLSTM_DH
    nh = MLSTM_HEADS
    lb = functools.partial(_pair_local_block, cfg, reverse)
    ctx = lambda i: (0, _pair_ctx_seq(cfg, i), 0, 0, 0)
    in_specs, args = [], []
    for b in range(NSLOT):
        rb = functools.partial(_pair_row_block, cfg, reverse, b)
        in_specs += [pl.BlockSpec((TB, hd), lambda i, rb=rb: (rb(i), Z_QB // hd)),
                     pl.BlockSpec((TB, hd), lambda i, rb=rb: (rb(i), Z_KB // hd)),
                     pl.BlockSpec((TB, hd), lambda i, rb=rb: (rb(i), Z_VB // hd)),
                     pl.BlockSpec((hd, TB), lambda i, rb=rb: (0, rb(i))),
                     pl.BlockSpec((TB, SM_W), lambda i, rb=rb: (rb(i), Z_SM // SM_W)),
                     pl.BlockSpec((4 * nh, TB), lambda i, rb=rb: (0, rb(i)))]
        args += [z, z, z, k_t, z, g_rows]
    in_specs += [pl.BlockSpec((1, 4 * nh), lambda i: (0, 0)), pl.BlockSpec((4 * nh, 1), lambda i: (0, 0))]
    args += [b_col, b_row]
    half_c, half_l = cfg.bc // NSLOT, cfg.bl // NSLOT
    lat = lambda i: (0, _pair_lat_seq(cfg, i), 0, 0, 0)
    state_dims = ((nh, MLSTM_DH, MLSTM_DH), (nh, 1, MLSTM_DH), (nh, 1, 128))
    in_specs += [pl.BlockSpec((NSLOT, 1) + sd, lat) for sd in state_dims]
    args += [s.reshape((NSLOT, half_l) + sd) for s, sd in zip((c0, n0, m0), state_dims)]
    if reverse:
        for b in range(NSLOT):
            rb = functools.partial(_pair_row_block, cfg, reverse, b)
            in_specs.append(pl.BlockSpec((TB, hd), lambda i, rb=rb: (rb(i), Z_OB // hd)))
            args.append(z)
        in_specs += [pl.BlockSpec((NSLOT, TB, hd), lambda i: (0, lb(i), 0)), pl.BlockSpec((1, hd), lambda i: (0, 0))]
        args += [h_fwd, gnorm]
    res = pl.pallas_call(
        functools.partial(_mlstm_kernel, cfg=cfg, reverse=reverse),
        grid=(m // TB // NSLOT,),
        in_specs=in_specs,
        out_specs=[pl.BlockSpec((NSLOT, TB, hd), lambda i: (0, lb(i), 0))]
                  + [pl.BlockSpec((NSLOT, 1) + sd, ctx) for sd in state_dims],
        out_shape=[jax.ShapeDtypeStruct((NSLOT, m // NSLOT, hd), BF16 if reverse else F32)]
                  + [jax.ShapeDtypeStruct((NSLOT, half_c) + sd, F32) for sd in state_dims],
        scratch_shapes=[pltpu.VMEM((NSLOT * nh, MLSTM_DH, MLSTM_DH), F32),
                        pltpu.VMEM((NSLOT * nh, 1, MLSTM_DH), F32),
                        pltpu.VMEM((NSLOT * nh, 1, 128), F32)],
        compiler_params=_params(("arbitrary",)),
        name="mlstm_rev" if reverse else "mlstm_fwd",
    )(*args)
    return (res[0],) + tuple(r.reshape((cfg.bc,) + sd) for r, sd in zip(res[1:], state_dims))


LRU_ROWS = 256
LRU_PAD = 8
LRU_SEG = 8


def _rglru_kernel(*refs, t, aliased):
    if aliased:
        refs = refs[1:]
    (xr_ref, yr_ref, cw_ref, cb_ref, gw_ref, gb_ref, lam_ref, h0_ref, y_ref, he_ref,
     pad_ref, af_ref, ar_ref, bf_ref, br_ref, hl_ref, pc_ref, hs_ref) = refs
    a_refs, b_refs = (af_ref, ar_ref), (bf_ref, br_ref)
    nsteps = t // LRU_ROWS
    seg = t // LRU_SEG
    pitch = seg + 8
    zeros = jnp.zeros((LRU_PAD, LRU_BLOCK), F32)
    pad_ref[0:LRU_PAD, :] = zeros
    pad_ref[LRU_PAD + t:2 * LRU_PAD + t, :] = zeros

    def pitched(r0):
        if seg >= LRU_ROWS:
            return [(0, LRU_ROWS, pl.multiple_of((r0 // seg) * pitch + r0 % seg, 8))]
        return [(q * seg, seg, pl.multiple_of((r0 // seg + q) * pitch, 8)) for q in range(LRU_ROWS // seg)]

    def fill(ci, _):
        r0 = pl.multiple_of(ci * LRU_ROWS, LRU_ROWS)
        pad_ref[pl.ds(r0 + LRU_PAD, LRU_ROWS), :] = xr_ref[pl.ds(r0, LRU_ROWS), :]
        return 0

    lax.fori_loop(0, nsteps, fill, 0)

    sp = _softplus(-lam_ref[0])
    left = LRU_CONV // 2

    def gates(ci, _):
        r0 = pl.multiple_of(ci * LRU_ROWS, LRU_ROWS)
        xc = cb_ref[...]
        for tap in range(LRU_CONV):
            xc = xc + cw_ref[tap:tap + 1, :] * pad_ref[pl.ds(r0 + LRU_PAD - left + tap, LRU_ROWS), :]
        pre = jnp.dot(xc.astype(BF16), gw_ref[0], preferred_element_type=F32) + gb_ref[0]
        both = lambda f: [f(d) for d in range(N_DIR)]
        r = both(lambda d: jax.nn.sigmoid(pre[:, d * 2 * LRU_BLOCK:d * 2 * LRU_BLOCK + LRU_BLOCK]))
        ig = both(lambda d: jax.nn.sigmoid(pre[:, d * 2 * LRU_BLOCK + LRU_BLOCK:(d + 1) * 2 * LRU_BLOCK]))
        log_a = both(lambda d: (-LRU_C * r[d]) * sp[:, d * LRU_BLOCK:(d + 1) * LRU_BLOCK])
        a = both(lambda d: jnp.exp(log_a[d]))
        b = both(lambda d: jnp.sqrt(1.0 - a[d] * a[d]) * (ig[d] * xc))
        for off, n, dst in pitched(r0):
            for d in range(N_DIR):
                a_refs[d][pl.ds(dst, n), :] = a[d][off:off + n]
                b_refs[d][pl.ds(dst, n), :] = b[d][off:off + n]
        return 0

    lax.fori_loop(0, nsteps, gates, 0)

    def local_scan(i, carry):
        hf, pf, hr, pr = carry
        rf, rr = i, seg - 1 - i
        a_f = af_ref[pl.ds(rf, LRU_SEG, stride=pitch), :]
        b_f = bf_ref[pl.ds(rf, LRU_SEG, stride=pitch), :]
        a_r = ar_ref[pl.ds(rr, LRU_SEG, stride=pitch), :]
        b_r = br_ref[pl.ds(rr, LRU_SEG, stride=pitch), :]
        hf, pf = a_f * hf + b_f, a_f * pf
        hr, pr = a_r * hr + b_r, a_r * pr
        hl_ref[0, rf] = hf
        pc_ref[0, rf] = pf
        hl_ref[1, rr] = hr
        pc_ref[1, rr] = pr
        return hf, pf, hr, pr

    zero = jnp.zeros((LRU_SEG, LRU_BLOCK), F32)
    one = jnp.ones((LRU_SEG, LRU_BLOCK), F32)
    hf, pf, hr, pr = lax.fori_loop(0, seg, local_scan, (zero, one, zero, one), unroll=8)

    enter_f = [h0_ref[0, 0:1, :]]
    for s in range(LRU_SEG - 1):
        enter_f.append(pf[s:s + 1, :] * enter_f[s] + hf[s:s + 1, :])
    he_ref[0, 0:1, :] = pf[LRU_SEG - 1:LRU_SEG, :] * enter_f[LRU_SEG - 1] + hf[LRU_SEG - 1:LRU_SEG, :]
    enter_r = [None] * LRU_SEG
    enter_r[LRU_SEG - 1] = h0_ref[0, 1:2, :]
    for s in range(LRU_SEG - 1, 0, -1):
        enter_r[s - 1] = pr[s:s + 1, :] * enter_r[s] + hr[s:s + 1, :]
    he_ref[0, 1:2, :] = pr[0:1, :] * enter_r[0] + hr[0:1, :]
    init_f = jnp.concatenate(enter_f, axis=0)
    init_r = jnp.concatenate(enter_r, axis=0)

    def fix_up(i, _):
        h = (hl_ref[0, i] + pc_ref[0, i] * init_f) + (hl_ref[1, i] + pc_ref[1, i] * init_r)
        hs_ref[pl.ds(i, LRU_SEG, stride=pitch), :] = h
        return 0

    lax.fori_loop(0, seg, fix_up, 0, unroll=8)

    def finish(ci, _):
        r0 = pl.multiple_of(ci * LRU_ROWS, LRU_ROWS)
        for off, n, src in pitched(r0):
            y = hs_ref[pl.ds(src, n), :] * _gelu_tanh(yr_ref[pl.ds(r0 + off, n), :])
            y_ref[pl.ds(r0 + off, n), :] = y.astype(y_ref.dtype)
        return 0

    lax.fori_loop(0, nsteps, finish, 0)


def _rglru(cfg, z, cw, cb, gw, gb, lam, h0, nseq, t, row0, y_prev=None):
    m = cfg.m
    seg = t // LRU_SEG
    assert row0 % t == 0 and t % LRU_ROWS == 0 and seg % 16 == 0 and (seg % LRU_ROWS == 0 or LRU_ROWS % seg == 0)
    pitched_rows = LRU_SEG * (seg + 8)
    sb = row0 // t
    aliased = y_prev is not None
    in_specs = [pl.BlockSpec((t, LRU_BLOCK), lambda b, n: (sb + b, Z_XR // LRU_BLOCK + n)),
                pl.BlockSpec((t, LRU_BLOCK), lambda b, n: (sb + b, Z_YR // LRU_BLOCK + n)),
                pl.BlockSpec((LRU_CONV, LRU_BLOCK), lambda b, n: (0, n)),
                pl.BlockSpec((1, LRU_BLOCK), lambda b, n: (0, n)),
                pl.BlockSpec((1, LRU_BLOCK, 4 * LRU_BLOCK), lambda b, n: (n, 0, 0)),
                pl.BlockSpec((1, 1, 4 * LRU_BLOCK), lambda b, n: (n, 0, 0)),
                pl.BlockSpec((1, 1, 2 * LRU_BLOCK), lambda b, n: (n, 0, 0)),
                pl.BlockSpec((1, N_DIR, LRU_BLOCK), lambda b, n: (b, 0, n))]
    args = [z, z, cw, cb, gw, gb, lam, h0]
    if aliased:
        in_specs = [pl.BlockSpec(memory_space=pl.ANY)] + in_specs
        args = [y_prev] + args
    return pl.pallas_call(
        functools.partial(_rglru_kernel, t=t, aliased=aliased),
        grid=(nseq, LRU_BLOCKS),
        in_specs=in_specs,
        out_specs=[pl.BlockSpec((t, LRU_BLOCK), lambda b, n: (sb + b, n)),
                   pl.BlockSpec((1, N_DIR, LRU_BLOCK), lambda b, n: (b, 0, n))],
        out_shape=[jax.ShapeDtypeStruct((m, LRU_WIDTH), BF16),
                   jax.ShapeDtypeStruct((nseq, N_DIR, LRU_WIDTH), F32)],
        scratch_shapes=[pltpu.VMEM((t + 2 * LRU_PAD, LRU_BLOCK), F32)]
                       + [pltpu.VMEM((pitched_rows, LRU_BLOCK), F32) for _ in range(2 * N_DIR)]
                       + [pltpu.VMEM((N_DIR, seg, LRU_SEG, LRU_BLOCK), F32),
                          pltpu.VMEM((N_DIR, seg, LRU_SEG, LRU_BLOCK), F32),
                          pltpu.VMEM((pitched_rows, LRU_BLOCK), F32)],
        input_output_aliases={0: 0} if aliased else {},
        compiler_params=_params(("parallel", "parallel")),
        name="rglru",
    )(*args)


def _merge_kernel(ya_ref, yb_ref, yc_ref, w_ref, ma_ref, mb_ref, mc_ref, bm_ref, o_ref):
    acc = None
    for n, (y_ref, mg_ref) in enumerate(((ya_ref, ma_ref), (yb_ref, mb_ref), (yc_ref, mc_ref))):
        g = jax.nn.sigmoid(mg_ref[...] + bm_ref[n])
        term = g * jnp.dot(y_ref[...], w_ref[n], preferred_element_type=F32)
        acc = term if acc is None else acc + term
    o_ref[...] = acc.astype(o_ref.dtype)


def _merge(cfg, ya, yb, yc, wbr, z, bm, tm, tn):
    m = cfg.m
    nj = D_MODEL // tn
    assert (cfg.m_ctx // NSLOT) % tm == 0 and ((m - cfg.m_ctx) // NSLOT) % tm == 0
    slot_spec = pl.BlockSpec((None, tm, BRANCH_W), lambda j, i: _slot_of_tile(cfg, tm, i) + (0,))
    y_spec = pl.BlockSpec((tm, BRANCH_W), lambda j, i: (i, 0))
    mg_spec = lambda n: pl.BlockSpec((tm, tn), lambda j, i: (i, Z_MG // tn + n * nj + j))
    return pl.pallas_call(
        _merge_kernel,
        grid=(nj, m // tm),
        in_specs=[slot_spec, slot_spec, y_spec,
                  pl.BlockSpec((N_BRANCH, BRANCH_W, tn), lambda j, i: (0, 0, j)),
                  mg_spec(0), mg_spec(1), mg_spec(2),
                  pl.BlockSpec((N_BRANCH, 1, tn), lambda j, i: (0, 0, j))],
        out_specs=pl.BlockSpec((tm, tn), lambda j, i: (i, j)),
        out_shape=jax.ShapeDtypeStruct((m, D_MODEL), BF16),
        compiler_params=_params(("parallel", "parallel")),
        name="merge",
    )(ya, yb, yc, wbr, z, z, z, bm)


def _matmul_res_kernel(*refs, starts):
    nsrc = len(starts) - 1
    a_ref, w_ref = refs[:2]
    x_refs = refs[2:2 + nsrc]
    g_ref, o_ref = refs[2 + nsrc:]
    i = pl.program_id(1)
    if nsrc == 1:
        o_ref[...] = x_refs[0][...] + g_ref[0] * jnp.dot(a_ref[...], w_ref[...], preferred_element_type=F32)
        return
    upd = g_ref[0] * jnp.dot(a_ref[...], w_ref[...], preferred_element_type=F32)
    for s, x_ref in enumerate(x_refs):
        @pl.when(jnp.logical_and(i >= starts[s], i < starts[s + 1]))
        def _():
            o_ref[...] = x_ref[...] + upd


def _matmul_res(cfg, a, w, xs, gate, tm, tn):
    m, kdim = a.shape
    n = w.shape[1]
    assert all(x.shape[0] % tm == 0 for x in xs)
    starts = [0]
    for x in xs:
        starts.append(starts[-1] + x.shape[0] // tm)
    assert starts[-1] * tm == m
    x_specs = [pl.BlockSpec((tm, tn), lambda j, i, s=s: (jnp.clip(i - starts[s], 0, starts[s + 1] - starts[s] - 1), j))
               for s in range(len(xs))]
    return pl.pallas_call(
        functools.partial(_matmul_res_kernel, starts=tuple(starts)),
        grid=(n // tn, m // tm),
        in_specs=[pl.BlockSpec((tm, kdim), lambda j, i: (i, 0)),
                  pl.BlockSpec((kdim, tn), lambda j, i: (0, j))] + x_specs
                 + [pl.BlockSpec((1, 1, tn), lambda j, i: (_mod_row(cfg, i * tm), 0, j))],
        out_specs=pl.BlockSpec((tm, tn), lambda j, i: (i, j)),
        out_shape=jax.ShapeDtypeStruct((m, n), F32),
        compiler_params=_params(("parallel", "parallel")),
        name="matmul_res",
    )(a, w, *xs, gate)


FD_TM = 512
FD_TK = 512
FD_HALO = GRID_W
FD_TOP = 8
FD_VMEM = 56 * 1024 * 1024


def _ffn_down_kernel(*refs, cfg, on_grid, seq_len, final_norm, aliased):
    refs = list(refs[1:] if aliased else refs)
    hg_ref, hu_ref = refs[:2]
    refs = refs[2:]
    if on_grid:
        hp_ref, hn_ref = refs[:2]
        refs = refs[2:]
    w9_ref, cb_ref, wd_ref, x_ref, g_ref = refs[:5]
    refs = refs[5:]
    if final_norm:
        nf_ref = refs[0]
        refs = refs[1:]
    o_ref, pad_ref, even_ref, odd_ref = refs
    i, k = pl.program_id(0), pl.program_id(1)
    nk = pl.num_programs(1) - 1
    tm, tk = FD_TM, FD_TK
    nslab = tk // 128
    base = FD_TOP + FD_HALO

    @pl.when(k == 0)
    def _():
        o_ref[...] = jnp.zeros_like(o_ref)
        edge = jnp.zeros((nslab, FD_TOP, 128), F32)
        pad_ref[:, 0:FD_TOP, :] = edge
        pad_ref[:, base + tm + FD_HALO:base + tm + FD_HALO + FD_TOP, :] = edge
        if not on_grid:
            halo = jnp.zeros((nslab, FD_HALO, 128), F32)
            pad_ref[:, FD_TOP:base, :] = halo
            pad_ref[:, base + tm:base + tm + FD_HALO, :] = halo

    if on_grid:
        tiles_per_seq = seq_len // tm
        first = i % tiles_per_seq == 0
        last = i % tiles_per_seq == tiles_per_seq - 1

        @pl.when(first)
        def _():
            pad_ref[:, FD_TOP:base, :] = jnp.zeros((nslab, FD_HALO, 128), F32)

        @pl.when(jnp.logical_not(first))
        def _():
            for s in range(nslab):
                pad_ref[s, FD_TOP:base, :] = hp_ref[:, s * 128:(s + 1) * 128]

        @pl.when(last)
        def _():
            pad_ref[:, base + tm:base + tm + FD_HALO, :] = jnp.zeros((nslab, FD_HALO, 128), F32)

        @pl.when(jnp.logical_not(last))
        def _():
            for s in range(nslab):
                pad_ref[s, base + tm:base + tm + FD_HALO, :] = hn_ref[:, s * 128:(s + 1) * 128]

    for s in range(nslab):
        pad_ref[s, base:base + tm, :] = hg_ref[:, s * 128:(s + 1) * 128]

    p = FFN_CONV // 2
    if on_grid:
        taps = tuple(((dy - p) * GRID_W + (dx - p), dy * FFN_CONV + dx, dx - p)
                     for dy in range(FFN_CONV) for dx in range(FFN_CONV))
        period = GRID_W
    else:
        taps = tuple((dx - p, p * FFN_CONV + dx, dx - p) for dx in range(FFN_CONV))
        period = seq_len
    kk = pl.multiple_of(jnp.maximum(k - 1, 0) * tk, tk)
    d_out = o_ref.shape[1]
    col_tile = 256
    n_col = d_out // col_tile
    rows = 64
    w9 = w9_ref[0]
    cb = cb_ref[0]
    conv_blocks = [(r, s) for r in range(tm // rows) for s in range(nslab)]
    per_tile = -(-len(conv_blocks) // n_col)

    def step(cur_ref, nxt_ref):
        for t in range(n_col):
            cols = slice(t * col_tile, (t + 1) * col_tile)
            if cur_ref is not None:
                o_ref[:, cols] += jnp.dot(cur_ref[...], wd_ref[pl.ds(kk, tk), cols], preferred_element_type=F32)
            for r, s in (conv_blocks[t * per_tile:(t + 1) * per_tile] if nxt_ref is not None else ()):
                pos = (lax.broadcasted_iota(jnp.int32, (rows, 128), 0) + r * rows) % period
                lanes = slice(s * 128, (s + 1) * 128)
                acc = cb[:, lanes]
                for shift, widx, dx in taps:
                    xs = pad_ref[s, base + r * rows + shift:base + (r + 1) * rows + shift, :]
                    if dx < 0:
                        xs = jnp.where(pos >= -dx, xs, 0.0)
                    elif dx > 0:
                        xs = jnp.where(pos < period - dx, xs, 0.0)
                    acc = acc + xs * w9[widx:widx + 1, lanes]
                act = _silu(acc) * hu_ref[r * rows:(r + 1) * rows, lanes]
                nxt_ref[r * rows:(r + 1) * rows, lanes] = act.astype(BF16)

    nk_static = D_FF // tk
    bufs = (even_ref, odd_ref)

    @pl.when(k == 0)
    def _():
        step(None, even_ref)

    @pl.when(jnp.logical_and(jnp.logical_and(k > 0, k < nk), k % 2 == 0))
    def _():
        step(odd_ref, even_ref)

    @pl.when(jnp.logical_and(k < nk, k % 2 == 1))
    def _():
        step(even_ref, odd_ref)

    @pl.when(k == nk)
    def _():
        step(bufs[(nk_static - 1) % 2], None)

        def slab(si, _):
            r0 = pl.multiple_of(si * NORM_SLAB, NORM_SLAB)
            res = x_ref[pl.ds(r0, NORM_SLAB), :] + g_ref[0] * o_ref[pl.ds(r0, NORM_SLAB), :]
            if final_norm:
                res = res * lax.rsqrt(jnp.mean(res * res, axis=-1, keepdims=True) + EPS) * nf_ref[...]
            o_ref[pl.ds(r0, NORM_SLAB), :] = res
            return 0

        lax.fori_loop(0, tm // NORM_SLAB, slab, 0)


def _ffn_down(cfg, h, w9, bias, wd, x, gate, nseq, t, row0, on_grid, prev=None, norm_gain=None):
    m, d = x.shape
    tm, tk = FD_TM, FD_TK
    rows = nseq * t
    nk = D_FF // tk
    assert D_FF % tk == 0 and row0 % tm == 0 and rows % tm == 0
    assert (t % tm == 0 and tm % GRID_W == 0) if on_grid else tm % t == 0
    final_norm = norm_gain is not None
    aliased = prev is not None
    assert not (final_norm and aliased)
    rt0 = row0 // tm
    kc = lambda k: jnp.minimum(k, nk - 1)
    hpb = tm // FD_HALO
    in_specs = [pl.BlockSpec((tm, tk), lambda i, k: (rt0 + i, kc(k))),
                pl.BlockSpec((tm, tk), lambda i, k: (rt0 + i, nk + kc(k)))]
    args = [h, h]
    if on_grid:
        in_specs += [pl.BlockSpec((FD_HALO, tk), lambda i, k: (jnp.maximum((rt0 + i) * hpb - 1, 0), kc(k))),
                     pl.BlockSpec((FD_HALO, tk), lambda i, k: (jnp.minimum((rt0 + i + 1) * hpb, m // FD_HALO - 1), kc(k)))]
        args += [h, h]
    w9c = jnp.transpose(w9.reshape(FFN_CONV * FFN_CONV, nk, tk), (1, 0, 2))
    in_specs += [pl.BlockSpec((1, FFN_CONV * FFN_CONV, tk), lambda i, k: (kc(k), 0, 0)),
                 pl.BlockSpec((1, 1, tk), lambda i, k: (kc(k), 0, 0)),
                 pl.BlockSpec((D_FF, d), lambda i, k: (0, 0), pipeline_mode=pl.Buffered(1)),
                 pl.BlockSpec((tm, d), lambda i, k: (rt0 + i, 0)),
                 pl.BlockSpec((1, 1, d), lambda i, k: (_mod_row(cfg, (rt0 + i) * tm), 0, 0))]
    args += [w9c, bias.reshape(nk, 1, tk), wd, x, gate]
    if final_norm:
        in_specs.append(pl.BlockSpec((1, d), lambda i, k: (0, 0)))
        args.append(norm_gain)
    if aliased:
        in_specs = [pl.BlockSpec(memory_space=pl.ANY)] + in_specs
        args = [prev] + args
    out_rows, out_rt0 = (rows, 0) if final_norm else (m, rt0)
    return pl.pallas_call(
        functools.partial(_ffn_down_kernel, cfg=cfg, on_grid=on_grid, seq_len=t, final_norm=final_norm, aliased=aliased),
        grid=(rows // tm, nk + 1),
        in_specs=in_specs,
        out_specs=pl.BlockSpec((tm, d), lambda i, k: (out_rt0 + i, 0)),
        out_shape=jax.ShapeDtypeStruct((out_rows, d), F32),
        scratch_shapes=[pltpu.VMEM((tk // 128, tm + 2 * (FD_HALO + FD_TOP), 128), F32),
                        pltpu.VMEM((tm, tk), BF16),
                        pltpu.VMEM((tm, tk), BF16)],
        input_output_aliases={0: 0} if aliased else {},
        compiler_params=pltpu.CompilerParams(dimension_semantics=("parallel", "arbitrary"), vmem_limit_bytes=FD_VMEM),
        name="ffn_down",
    )(*args)


def _pack_w_in(w):
    s = _SRC
    w = w.astype(BF16)
    cols = [w[:, s["mg"]:s["end"]], w[:, s["va"]:s["ga"]], w[:, s["ga"]:s["ra"]], w[:, s["qb"]:s["kb"]],
            w[:, s["kb"]:s["vb"]], w[:, s["vb"]:s["ob"]], w[:, s["ob"]:s["gb"]], w[:, s["xr"]:s["yr"]],
            w[:, s["yr"]:s["mg"]], w[:, s["qa"]:s["ka"]], w[:, s["ka"]:s["va"]], w[:, s["ra"]:s["qb"]],
            w[:, s["gb"]:s["xr"]]]
    used = sum(c.shape[1] for c in cols)
    cols.append(jnp.zeros((w.shape[0], NZ - used), w.dtype))
    return jnp.concatenate(cols, axis=1)


def _pack_gate_w(gw, gb, lam):
    nb, blk = LRU_BLOCKS, LRU_BLOCK
    gw_p = jnp.transpose(gw, (2, 3, 0, 1, 4)).reshape(nb, blk, 4 * blk).astype(BF16)
    gb_p = jnp.transpose(gb.reshape(2, 2, nb, blk), (2, 0, 1, 3)).reshape(nb, 1, 4 * blk)
    lam_p = jnp.transpose(lam.reshape(2, nb, blk), (1, 0, 2)).reshape(nb, 1, 2 * blk)
    return gw_p, gb_p, lam_p


def _layer(cfg, x, mod, lp, states, final_gain=None):
    (n1, n2, w_in, gwa, gba, gng, mbif, mng, lcw, lcb, lgw, lgb, llam, wbr, bmg, wout, fup, fcw, fcb, fdown) = lp
    s_gla, s_mc, s_mn, s_mm, s_lru = states
    m = cfg.m
    sh1, sc1, g1, sh2, sc2, g2 = (mod[:, k * D_MODEL:(k + 1) * D_MODEL].reshape(MOD_ROWS, 1, D_MODEL) for k in range(6))

    proj_tm = PROJ_TM if (cfg.m_ctx % PROJ_TM == 0 and cfg.tl % PROJ_TM == 0) else 512
    z = _norm_matmul(cfg, x, n1.reshape(1, -1), sc1, sh1, _pack_w_in(w_in), proj_tm, 1024 if len(x) == 1 else 512)

    s0t = jnp.swapaxes(s_gla, -1, -2)
    o_f, sa_f = _gla(cfg, z, gwa[0], gba[0].reshape(1, -1), s0t[:, 0], False)
    y_a, sa_r = _gla(cfg, z, gwa[1], gba[1].reshape(1, -1), s0t[:, 1], True, o_f, gng.reshape(1, -1))
    new_gla =jnp.swapaxes(jnp.stack([sa_f, sa_r], axis=1), -1, -2)

    g_rows = z[:, Z_SM + SM_GB:Z_SM + SM_GB + 4 * MLSTM_HEADS].T
    k_t = z[:, Z_KB:Z_KB + MLSTM_HEADS * MLSTM_DH].T
    b_col = mbif.reshape(1, -1)
    b_row = mbif.reshape(-1, 1)
    n0 = s_mn[:, :, :, None, :]
    m0 = jnp.broadcast_to(s_mm[:, :, :, None, None], s_mm.shape + (1, 128))
    h_f, cb_f, nb_f, mb_f = _mlstm(cfg, z, k_t, g_rows, b_col, b_row, s_mc[:, 0], n0[:, 0], m0[:, 0], False)
    y_b, cb_r, nb_r, mb_r = _mlstm(cfg, z, k_t, g_rows, b_col, b_row, s_mc[:, 1], n0[:, 1], m0[:, 1], True,
                                   h_f, mng.reshape(1, -1))
    new_mc =jnp.stack([cb_f, cb_r], axis=1)
    new_mn = jnp.stack([nb_f[:, :, 0], nb_r[:, :, 0]], axis=1)
    new_mm = jnp.stack([mb_f[:, :, 0, 0], mb_r[:, :, 0, 0]], axis=1)

    gw_p, gb_p, lam_p = _pack_gate_w(lgw, lgb, llam)
    cbias = lcb.reshape(1, -1)
    y_c, new_lru = _rglru(cfg, z, lcw, cbias, gw_p, gb_p, lam_p, jnp.zeros((cfg.bc, N_DIR, LRU_WIDTH), F32),
                          cfg.bc, cfg.tc, 0)
    y_c, _ = _rglru(cfg, z, lcw, cbias, gw_p, gb_p, lam_p, s_lru, cfg.bl, cfg.tl, cfg.m_ctx, y_prev=y_c)

    merged = _merge(cfg, y_a, y_b, y_c, wbr.astype(BF16), z, bmg.reshape(N_BRANCH, 1, D_MODEL), 512, 1024)
    x = _matmul_res(cfg, merged, wout.astype(BF16), x, g1, 512, 1024)

    h = _norm_matmul(cfg, (x,), n2.reshape(1, -1), sc2, sh2, fup.astype(BF16), proj_tm, 1024)
    w9 = fcw.reshape(FFN_CONV * FFN_CONV, D_FF)
    wd = fdown.astype(BF16)
    if final_gain is None:
        x_ctx = _ffn_down(cfg, h, w9, fcb, wd, x, g2, cfg.bc, cfg.tc, 0, False)
        x = _ffn_down(cfg, h, w9, fcb, wd, x, g2, cfg.bl, cfg.tl, cfg.m_ctx, True, prev=x_ctx)
    else:
        x = (_ffn_down(cfg, h, w9, fcb, wd, x, g2, cfg.bc, cfg.tc, 0, False, norm_gain=final_gain),
             _ffn_down(cfg, h, w9, fcb, wd, x, g2, cfg.bl, cfg.tl, cfg.m_ctx, True, norm_gain=final_gain))
    return x, (new_gla, new_mc, new_mn, new_mm, new_lru)


def kernel(x_prompt, x_sample, state_gla, state_mlstm_c, state_mlstm_n, state_mlstm_m, state_rglru, c, c_ctx, norm1_g, norm2_g, w_mod, b_mod, w_in, gla_w_alpha, gla_b_alpha, gla_norm_g, mlstm_b_if, mlstm_norm_g, lru_conv_w, lru_conv_b, lru_gate_w, lru_gate_b, lru_lambda, w_branch, b_merge, w_out, ffn_w_up, ffn_conv_w, ffn_conv_b, ffn_w_down, norm_f_g):
    bc, tc, d = x_prompt.shape
    bl, tl, _ = x_sample.shape
    cfg = Cfg(bc, tc, bl, tl)
    assert tc % TB == 0 and tl % TB == 0 and cfg.m_ctx % tl == 0 and 1 + bl <= MOD_ROWS
    assert bc % NSLOT == 0 and bl % NSLOT == 0
    depth = w_in.shape[0]

    x = (x_prompt.reshape(bc * tc, d), x_sample.reshape(bl * tl, d))
    c_all = jnp.concatenate([c_ctx[None, :], c, jnp.zeros((MOD_ROWS - 1 - bl, d), F32)], axis=0)
    mod = _modulation(c_all, w_mod, b_mod)

    new = []
    for l in range(depth):
        lp = (norm1_g[l], norm2_g[l], w_in[l], gla_w_alpha[l], gla_b_alpha[l], gla_norm_g[l], mlstm_b_if[l],
              mlstm_norm_g[l], lru_conv_w[l], lru_conv_b[l], lru_gate_w[l], lru_gate_b[l], lru_lambda[l],
              w_branch[l], b_merge[l], w_out[l], ffn_w_up[l], ffn_conv_w[l], ffn_conv_b[l], ffn_w_down[l])
        states = (state_gla[:, l], state_mlstm_c[:, l], state_mlstm_n[:, l], state_mlstm_m[:, l], state_rglru[:, l])
        x, st = _layer(cfg, x, mod[l], lp, states, norm_f_g.reshape(1, -1) if l == depth - 1 else None)
        x = x if l == depth - 1 else (x,)
        new.append(st)

    y_prompt = x[0].reshape(bc, tc, d)
    y_sample = x[1].reshape(bl, tl, d)
    stacked = tuple(jnp.stack([new[l][k] for l in range(depth)], axis=1) for k in range(5))
    return (y_prompt, y_sample) + stacked
```
